```python
import math
import jax, jax.numpy as jnp
from jax import lax
import numpy as np


D_MODEL = 1024
BATCH = 16
SEQ = 2048
DEPTH = 2

HEAD_DIM = 64
FOX_HEADS = 4
FOX_W = FOX_HEADS * HEAD_DIM
CONV_W = 256
CONV_K = 3
MLSTM_HEADS = 4
MLSTM_W = MLSTM_HEADS * HEAD_DIM
MLSTM_CHUNK = 64
SWA_Q_HEADS = 8
SWA_KV_HEADS = 2
SWA_W = SWA_Q_HEADS * HEAD_DIM
SWA_KV_W = SWA_KV_HEADS * HEAD_DIM
WINDOW = 128
Q_BLOCK = 128
REL_BUCKETS = 32
REL_MAX_DIST = 128
N_BRANCH = 4
D_FF = -(-8 * D_MODEL // (3 * 256)) * 256
EPS = 1e-6

SPLIT_SIZES = (FOX_W, FOX_W, FOX_W, FOX_HEADS,
               CONV_W, CONV_W, CONV_W,
               MLSTM_W, MLSTM_W, MLSTM_W, MLSTM_HEADS, MLSTM_HEADS, MLSTM_W,
               SWA_W, SWA_KV_W, SWA_KV_W,
               N_BRANCH * D_MODEL)
IN_COLS = sum(SPLIT_SIZES)

kernel_name = 'hybrid_fox_conv_mlstm_swa_block'


def rms_norm(x, gain):
    xf = x.astype(jnp.float32)
    y = xf * lax.rsqrt(jnp.mean(xf * xf, axis=-1, keepdims=True) + EPS)
    return (y * gain.astype(jnp.float32)).astype(x.dtype)


def t5_bucket(n):
    max_exact = REL_BUCKETS // 2
    nf = jnp.maximum(n, 1).astype(jnp.float32)
    large = max_exact + (jnp.log(nf / max_exact) / math.log(REL_MAX_DIST / max_exact)
                         * (REL_BUCKETS - max_exact)).astype(jnp.int32)
    large = jnp.minimum(large, REL_BUCKETS - 1)
    return jnp.where(n < max_exact, n, large)


def fox_attention(q, k, v, f_logit):
    B, S, H, d = q.shape
    nblk = S // Q_BLOCK
    c = jnp.cumsum(jax.nn.log_sigmoid(f_logit.astype(jnp.float32)), axis=1)
    c_t = c.transpose(0, 2, 1)
    qb = q.reshape(B, nblk, Q_BLOCK, H, d).transpose(1, 0, 2, 3, 4)
    cb = c.reshape(B, nblk, Q_BLOCK, H).transpose(1, 0, 3, 2)
    key_pos = jnp.arange(S)
    scale = d ** -0.5

    def block(args):
        qi, ci, bi = args
        s = jnp.einsum('bqhd,bkhd->bhqk', qi, k).astype(jnp.float32) * scale
        s = s + (ci[..., :, None] - c_t[..., None, :])
        q_pos = bi * Q_BLOCK + jnp.arange(Q_BLOCK)
        s = jnp.where(key_pos[None, :] <= q_pos[:, None], s, -jnp.inf)
        p = jax.nn.softmax(s, axis=-1).astype(v.dtype)
        return jnp.einsum('bhqk,bkhd->bqhd', p, v)

    out = lax.map(block, (qb, cb, jnp.arange(nblk)))
    return out.transpose(1, 0, 2, 3, 4).reshape(B, S, H * d)


def short_conv_mixer(u, b_gate, c_gate, w):
    z = c_gate * u
    y = lax.conv_general_dilated(z, w[:, None, :], window_strides=(1,),
                                 padding=[(CONV_K - 1, 0)],
                                 dimension_numbers=('NWC', 'WIO', 'NWC'),
                                 feature_group_count=z.shape[-1])
    return b_gate * y


def mlstm_chunkwise(q, k, v, i_pre, f_pre):
    out_dtype = q.dtype
    f32 = jnp.float32
    B, S, H, d = q.shape
    L = MLSTM_CHUNK
    nc = S // L
    q = q.astype(f32).reshape(B, nc, L, H, d) * d ** -0.5
    k = k.astype(f32).reshape(B, nc, L, H, d)
    v = v.astype(f32).reshape(B, nc, L, H, d)
    log_i = i_pre.astype(f32).reshape(B, nc, L, H)
    log_f = jax.nn.log_sigmoid(f_pre.astype(f32)).reshape(B, nc, L, H)
    b = jnp.cumsum(log_f, axis=2)
    b_last = b[:, :, -1]

    a = b_last[:, :, None] - b + log_i
    m_loc = a.max(axis=2)
    w_state = jnp.exp(a - m_loc[:, :, None])
    c_loc = jnp.einsum('bclh,bclhd,bclhe->bchde', w_state, k, v)
    n_loc = jnp.einsum('bclh,bclhd->bchd', w_state, k)

    def step(carry, xs):
        c_st, n_st, m_st = carry
        bl, ml, cl, nl = xs
        m_new = jnp.maximum(bl + m_st, ml)
        decay = jnp.exp(bl + m_st - m_new)
        wl = jnp.exp(ml - m_new)
        c_new = decay[..., None, None] * c_st + wl[..., None, None] * cl
        n_new = decay[..., None] * n_st + wl[..., None] * nl
        return (c_new, n_new, m_new), (c_st, n_st, m_st)

    init = (jnp.zeros((B, H, d, d), f32), jnp.zeros((B, H, d), f32), jnp.zeros((B, H), f32))
    xs = (jnp.moveaxis(b_last, 1, 0), jnp.moveaxis(m_loc, 1, 0),
          jnp.moveaxis(c_loc, 1, 0), jnp.moveaxis(n_loc, 1, 0))
    _, (c_prev, n_prev, m_prev) = lax.scan(step, init, xs)
    c_prev = jnp.moveaxis(c_prev, 0, 1)
    n_prev = jnp.moveaxis(n_prev, 0, 1)
    m_prev = jnp.moveaxis(m_prev, 0, 1)

    g = b + m_prev[:, :, None, :]
    dmat = b[:, :, :, None, :] - b[:, :, None, :, :] + log_i[:, :, None, :, :]
    causal = jnp.tril(jnp.ones((L, L), dtype=bool))
    dmat = jnp.where(causal[None, None, :, :, None], dmat, -jnp.inf)
    m_t = jnp.maximum(g, dmat.max(axis=3))
    s_w = jnp.einsum('bcthd,bcshd->bctsh', q, k) * jnp.exp(dmat - m_t[:, :, :, None, :])
    inter = jnp.exp(g - m_t)
    num = (jnp.einsum('bctsh,bcshe->bcthe', s_w, v)
           + inter[..., None] * jnp.einsum('bcthd,bchde->bcthe', q, c_prev))
    den = s_w.sum(axis=3) + inter * jnp.einsum('bcthd,bchd->bcth', q, n_prev)
    h = num / jnp.maximum(jnp.abs(den), jnp.exp(-m_t))[..., None]
    return h.reshape(B, S, H, d).astype(out_dtype)


def swa_attention(q, k, v, sinks, bias):
    B, S, Hq, d = q.shape
    Hkv = k.shape[2]
    G = Hq // Hkv
    W = WINDOW
    nb = S // W
    qb = q.reshape(B, nb, W, Hkv, G, d)

    def band(t):
        tp = jnp.pad(t, ((0, 0), (W, 0), (0, 0), (0, 0))).reshape(B, nb + 1, W, Hkv, d)
        return jnp.concatenate([tp[:, :-1], tp[:, 1:]], axis=2)

    kb, vb = band(k), band(v)
    s = jnp.einsum('bnqhgd,bnjhd->bnhgqj', qb, kb).astype(jnp.float32) * d ** -0.5
    s = s + bias.astype(jnp.float32).reshape(Hkv, G, W, 2 * W)
    qi = jnp.arange(W)[:, None]
    kj = jnp.arange(2 * W)[None, :]
    dist = qi + W - kj
    in_window = (dist >= 0) & (dist < WINDOW)
    key_pos = jnp.arange(nb)[:, None, None] * W - W + kj[None]
    valid = in_window[None] & (key_pos >= 0)
    s = jnp.where(valid[None, :, None, None], s, -jnp.inf)
    sink = jnp.broadcast_to(sinks.astype(jnp.float32).reshape(1, 1, Hkv, G, 1, 1),
                            s.shape[:-1] + (1,))
    p = jax.nn.softmax(jnp.concatenate([s, sink], axis=-1), axis=-1)[..., :-1]
    out = jnp.einsum('bnhgqj,bnjhd->bnqhgd', p.astype(v.dtype), vb)
    return out.reshape(B, S, Hq * d)


def setup_inputs(seed: int = 0) -> dict:
    key = jax.random.key(seed)
    ks = jax.random.split(key, 24)

    def nrm(k, shape, scale):
        return jax.random.normal(k, shape, jnp.float32) * scale

    return {
        'x': nrm(ks[0], (BATCH, SEQ, D_MODEL), 1.0),
        'rel_bias': nrm(ks[1], (REL_BUCKETS, SWA_Q_HEADS), 0.5),
        'attn_norm': 1.0 + nrm(ks[2], (DEPTH, D_MODEL), 0.05),
        'w_in': nrm(ks[3], (DEPTH, D_MODEL, IN_COLS), D_MODEL ** -0.5),
        'fox_f_bias': 3.0 + nrm(ks[4], (DEPTH, FOX_HEADS), 1.0),
        'fox_q_gain': 1.0 + nrm(ks[5], (DEPTH, HEAD_DIM), 0.05),
        'fox_k_gain': 1.0 + nrm(ks[6], (DEPTH, HEAD_DIM), 0.05),
        'conv_w': nrm(ks[7], (DEPTH, CONV_K, CONV_W), CONV_K ** -0.5),
        'mlstm_i_bias': nrm(ks[8], (DEPTH, MLSTM_HEADS), 0.1),
        'mlstm_f_bias': 4.0 + nrm(ks[9], (DEPTH, MLSTM_HEADS), 1.0),
        'mlstm_h_gain': 1.0 + nrm(ks[10], (DEPTH, MLSTM_W), 0.05),
        'swa_q_gain': 1.0 + nrm(ks[11], (DEPTH, HEAD_DIM), 0.05),
        'swa_k_gain': 1.0 + nrm(ks[12], (DEPTH, HEAD_DIM), 0.05),
        'swa_sinks': nrm(ks[13], (DEPTH, SWA_Q_HEADS), 1.0),
        'w_fox_out': nrm(ks[14], (DEPTH, FOX_W, D_MODEL), FOX_W ** -0.5),
        'w_conv_out': nrm(ks[15], (DEPTH, CONV_W, D_MODEL), CONV_W ** -0.5),
        'w_mlstm_out': nrm(ks[16], (DEPTH, MLSTM_W, D_MODEL), MLSTM_W ** -0.5),
        'w_swa_out': nrm(ks[17], (DEPTH, SWA_W, D_MODEL), SWA_W ** -0.5),
        'w_merge_out': nrm(ks[18], (DEPTH, D_MODEL, D_MODEL), D_MODEL ** -0.5),
        'ffn_norm': 1.0 + nrm(ks[19], (DEPTH, D_MODEL), 0.05),
        'w_gate': nrm(ks[20], (DEPTH, D_MODEL, D_FF), D_MODEL ** -0.5),
        'w_up': nrm(ks[21], (DEPTH, D_MODEL, D_FF), D_MODEL ** -0.5),
        'w_down': nrm(ks[22], (DEPTH, D_FF, D_MODEL), D_FF ** -0.5),
    }


def reference(x, rel_bias, attn_norm, w_in, fox_f_bias, fox_q_gain, fox_k_gain, conv_w,
              mlstm_i_bias, mlstm_f_bias, mlstm_h_gain, swa_q_gain, swa_k_gain, swa_sinks,
              w_fox_out, w_conv_out, w_mlstm_out, w_swa_out, w_merge_out, ffn_norm,
              w_gate, w_up, w_down):
    B, S, _ = x.shape
    split_points = np.cumsum(SPLIT_SIZES)[:-1].tolist()

    dist = jnp.arange(WINDOW)[:, None] + WINDOW - jnp.arange(2 * WINDOW)[None, :]
    swa_bias = rel_bias[t5_bucket(jnp.maximum(dist, 0))].transpose(2, 0, 1)

    for l in range(DEPTH):
        h = rms_norm(x, attn_norm[l])
        proj = jnp.einsum('bsd,dc->bsc', h, w_in[l])
        (fq, fk, fv, ff, cu, cb, cc, mq, mk, mv, mi, mf, mo, sq, sk, sv,
         gates) = jnp.split(proj, split_points, axis=-1)

        fq = rms_norm(fq.reshape(B, S, FOX_HEADS, HEAD_DIM), fox_q_gain[l])
        fk = rms_norm(fk.reshape(B, S, FOX_HEADS, HEAD_DIM), fox_k_gain[l])
        fv = fv.reshape(B, S, FOX_HEADS, HEAD_DIM)
        y_fox = fox_attention(fq, fk, fv, ff + fox_f_bias[l])

        y_conv = short_conv_mixer(cu, cb, cc, conv_w[l])

        hm = mlstm_chunkwise(mq.reshape(B, S, MLSTM_HEADS, HEAD_DIM),
                             mk.reshape(B, S, MLSTM_HEADS, HEAD_DIM),
                             mv.reshape(B, S, MLSTM_HEADS, HEAD_DIM),
                             mi + mlstm_i_bias[l], mf + mlstm_f_bias[l])
        hm = rms_norm(hm, mlstm_h_gain[l].reshape(MLSTM_HEADS, HEAD_DIM))
        y_mlstm = jax.nn.sigmoid(mo) * hm.reshape(B, S, MLSTM_W)

        sq = rms_norm(sq.reshape(B, S, SWA_Q_HEADS, HEAD_DIM), swa_q_gain[l])
        sk = rms_norm(sk.reshape(B, S, SWA_KV_HEADS, HEAD_DIM), swa_k_gain[l])
        sv = sv.reshape(B, S, SWA_KV_HEADS, HEAD_DIM)
        y_swa = swa_attention(sq, sk, sv, swa_sinks[l], swa_bias)

        g = jax.nn.sigmoid(gates.reshape(B, S, N_BRANCH, D_MODEL))
        merged = (g[:, :, 0] * jnp.einsum('bsw,wd->bsd', y_fox, w_fox_out[l])
                  + g[:, :, 1] * jnp.einsum('bsw,wd->bsd', y_conv, w_conv_out[l])
                  + g[:, :, 2] * jnp.einsum('bsw,wd->bsd', y_mlstm, w_mlstm_out[l])
                  + g[:, :, 3] * jnp.einsum('bsw,wd->bsd', y_swa, w_swa_out[l]))
        x = x + jnp.einsum('bsd,de->bse', merged, w_merge_out[l])

        h = rms_norm(x, ffn_norm[l])
        act = jax.nn.silu(jnp.einsum('bsd,df->bsf', h, w_gate[l])) * jnp.einsum('bsd,df->bsf', h, w_up[l])
        x = x + jnp.einsum('bsf,fd->bsd', act, w_down[l])
    return x
```

```python
import functools

import numpy as np
import jax
import jax.numpy as jnp
from jax import lax
from jax.experimental import pallas as pl
from jax.experimental.pallas import tpu as pltpu

F32 = jnp.float32
BF16 = jnp.bfloat16

D_MODEL = 1024
HEAD_DIM = 64
FOX_HEADS = 4
MLSTM_HEADS = 4
SWA_Q_HEADS = 8
SWA_KV_HEADS = 2
SWA_GROUP = SWA_Q_HEADS // SWA_KV_HEADS
WINDOW = 128
REL_BUCKETS = 32
REL_MAX_DIST = 128
D_FF = 2816
EPS = 1e-6
NEG = -1e30
QK_SCALE = HEAD_DIM ** -0.5

ROW_FQ, ROW_FK, ROW_FV = 0, 256, 512
ROW_MQ, ROW_MV, ROW_MO = 768, 1024, 1280
ROW_SQ, ROW_SK, ROW_SV = 1536, 2048, 2176
N_FEAT = 2304
N_SMALL = 16
COL_CU, COL_CB, COL_CC, COL_MK, COL_GATES = 0, 256, 512, 768, 1024
N_TOK = 5120

TM_IN = 1024
TN_IN = 512
FEAT_CHUNK = 576
TQ_FOX = 256
L_MLSTM = 256
TM_MERGE = 512
TM_FFN = 512
TF_FFN = 1408
VMEM_LIMIT = 56 * 1024 * 1024


def _dot(a, b):
    return jnp.dot(a, b, preferred_element_type=F32)


def _dot_nt(a, b):
    return lax.dot_general(a, b, (((1,), (1,)), ((), ())), preferred_element_type=F32)


def _dot_tn(a, b):
    return lax.dot_general(a, b, (((0,), (0,)), ((), ())), preferred_element_type=F32)


def _sigmoid(x):
    return 0.5 * jnp.tanh(0.5 * x) + 0.5


def _log_sigmoid(x):
    return jnp.minimum(x, 0.0) - jnp.log(1.0 + jnp.exp(-jnp.abs(x)))


def _cumsum_lanes(x):
    n = x.shape[-1]
    lane = lax.broadcasted_iota(jnp.int32, x.shape, x.ndim - 1)
    k = 1
    while k < n:
        x = x + jnp.where(lane >= k, pltpu.roll(x, k, x.ndim - 1), 0.0)
        k *= 2
    return x


def _row_select(rows, n_rows, width):
    rid = lax.broadcasted_iota(jnp.int32, (n_rows, width), 0)
    out = jnp.zeros((n_rows, width), F32)
    for r, v in enumerate(rows):
        out = jnp.where(rid == r, v, out)
    return out


def _split3(c):
    hi = c.astype(BF16).astype(F32)
    r = c - hi
    mid = r.astype(BF16).astype(F32)
    lo = (r - mid).astype(BF16).astype(F32)
    return hi, mid, lo


def _rms_rows(xt, gain_col):
    ms = jnp.mean(xt * xt, axis=0, keepdims=True)
    return xt * lax.rsqrt(ms + EPS) * gain_col


def _bias_kernel(rb_ref, idx_ref, out_ref):
    kv = pl.program_id(0)
    g = pl.program_id(1)
    head = kv * SWA_GROUP + g
    idx = idx_ref[...]
    acc = jnp.full(idx.shape, NEG, F32)
    for b in range(REL_BUCKETS):
        acc = jnp.where(idx == b, rb_ref[b * SWA_Q_HEADS + head], acc)
    out_ref[0] = acc


def _bucket_table():
    j = np.arange(2 * WINDOW)[:, None]
    i = np.arange(WINDOW)[None, :]
    dist = i + WINDOW - j
    n = np.maximum(dist, 0).astype(np.int32)
    max_exact = REL_BUCKETS // 2
    nf = np.maximum(n, 1).astype(np.float32)
    large = max_exact + (np.log(nf / np.float32(max_exact)) / np.float32(np.log(REL_MAX_DIST / max_exact))
                         * np.float32(REL_BUCKETS - max_exact)).astype(np.int32)
    large = np.minimum(large, REL_BUCKETS - 1)
    bucket = np.where(n < max_exact, n, large)
    return np.where((dist >= 0) & (dist < WINDOW), bucket, -1).astype(np.int32)


def _swa_bias_table(rel_bias):
    idx = jnp.asarray(_bucket_table())
    return pl.pallas_call(
        _bias_kernel,
        grid=(SWA_KV_HEADS, SWA_GROUP),
        in_specs=[
            pl.BlockSpec(memory_space=pltpu.SMEM),
            pl.BlockSpec((2 * WINDOW, WINDOW), lambda kv, g: (0, 0)),
        ],
        out_specs=pl.BlockSpec((1, 2 * WINDOW, WINDOW), lambda kv, g: (kv, 0, g)),
        out_shape=jax.ShapeDtypeStruct((SWA_KV_HEADS, 2 * WINDOW, SWA_GROUP * WINDOW), F32),
        name="swa_bias_table",
    )(rel_bias.reshape(-1), idx)


def _inproj_kernel(x_ref, g_ref, wt_ref, wf_ref, tok_ref, feat_ref, small_ref, xn_ref, *, n_tok):
    j = pl.program_id(1)

    @pl.when(j == 0)
    def _():
        x = x_ref[...]
        ms = jnp.mean(x * x, axis=-1, keepdims=True)
        xn_ref[...] = (x * lax.rsqrt(ms + EPS) * g_ref[...]).astype(BF16)

    @pl.when(j < n_tok)
    def _():
        tok_ref[...] = _dot(xn_ref[...], wt_ref[...]).astype(BF16)

    @pl.when(j == n_tok)
    def _():
        xn = xn_ref[...]
        n_chunks = N_FEAT // FEAT_CHUNK
        for c in range(n_chunks):
            lo = c * FEAT_CHUNK
            hi = lo + FEAT_CHUNK + (N_SMALL if c == n_chunks - 1 else 0)
            r = _dot_nt(wf_ref[lo:hi, :], xn)
            feat_ref[lo:lo + FEAT_CHUNK, :] = r[:FEAT_CHUNK].astype(BF16)
            if c == n_chunks - 1:
                small_ref[...] = r[FEAT_CHUNK:]


def _inproj(x2, gain, w_tok, w_feat_t):
    t = x2.shape[0]
    tm = min(TM_IN, t)
    n_tok = N_TOK // TN_IN
    return pl.pallas_call(
        functools.partial(_inproj_kernel, n_tok=n_tok),
        grid=(t // tm, n_tok + 1),
        in_specs=[
            pl.BlockSpec((tm, D_MODEL), lambda i, j: (i, 0)),
            pl.BlockSpec((1, D_MODEL), lambda i, j: (0, 0)),
            pl.BlockSpec((D_MODEL, TN_IN), lambda i, j: (0, jnp.minimum(j, n_tok - 1))),
            pl.BlockSpec((N_FEAT + N_SMALL, D_MODEL), lambda i, j: (0, 0)),
        ],
        out_specs=[
            pl.BlockSpec((tm, TN_IN), lambda i, j: (i, jnp.minimum(j, n_tok - 1))),
            pl.BlockSpec((N_FEAT, tm), lambda i, j: (0, i)),
            pl.BlockSpec((N_SMALL, tm), lambda i, j: (0, i)),
        ],
        out_shape=[
            jax.ShapeDtypeStruct((t, N_TOK), BF16),
            jax.ShapeDtypeStruct((N_FEAT, t), BF16),
            jax.ShapeDtypeStruct((N_SMALL, t), F32),
        ],
        scratch_shapes=[pltpu.VMEM((tm, D_MODEL), BF16)],
        compiler_params=pltpu.CompilerParams(
            dimension_semantics=("parallel", "arbitrary"), vmem_limit_bytes=VMEM_LIMIT),
        name="inproj",
    )(x2, gain, w_tok, w_feat_t)


def _fox_kernel(fq_ref, fk_ref, fv_ref, small_ref, fb_ref, gq_ref, gk_ref, out_ref,
                qa_ref, ka_ref, c_ref, *, seq, tq):
    h = pl.program_id(1)
    i = pl.program_id(2)

    @pl.when(i == 0)
    def _():
        c_ref[...] = _cumsum_lanes(_log_sigmoid(small_ref[...] + fb_ref[...]))
        c = c_ref[pl.ds(h, 1), :]
        hi, mid, lo = _split3(c)
        one = jnp.ones_like(c)
        qn = _rms_rows(fq_ref[...].astype(F32), gq_ref[...]) * QK_SCALE
        qa_ref[0:HEAD_DIM, :] = qn.astype(BF16)
        qa_ref[HEAD_DIM:, :] = _row_select([hi, mid, lo, one, one, one], HEAD_DIM, seq).astype(BF16)
        kn = _rms_rows(fk_ref[...].astype(F32), gk_ref[...])
        k_aug = _row_select([one, one, one, -hi, -mid, -lo], HEAD_DIM, seq)
        ka_ref[...] = jnp.concatenate([kn, k_aug], axis=0).T.astype(BF16)

    qa = qa_ref[:, pl.ds(pl.multiple_of(i * tq, tq), tq)]

    def step(j, carry, masked):
        m, l, acc = carry
        off = pl.multiple_of(j * tq, tq)
        s = _dot(ka_ref[pl.ds(off, tq), :], qa)
        if masked:
            key = lax.broadcasted_iota(jnp.int32, s.shape, 0)
            qry = lax.broadcasted_iota(jnp.int32, s.shape, 1)
            s = jnp.where(key <= qry, s, NEG)
        m_new = jnp.maximum(m, jnp.max(s, axis=0, keepdims=True))
        p = jnp.exp(s - m_new)
        alpha = jnp.exp(m - m_new)
        l = alpha * l + jnp.sum(p, axis=0, keepdims=True)
        acc = alpha * acc + _dot(fv_ref[:, pl.ds(off, tq)], p.astype(BF16))
        return m_new, l, acc

    init = (jnp.full((1, tq), NEG, F32), jnp.zeros((1, tq), F32), jnp.zeros((HEAD_DIM, tq), F32))
    carry = lax.fori_loop(0, i, lambda j, c: step(j, c, False), init)
    _, l, acc = step(i, carry, True)
    out_ref[...] = (acc / l).astype(BF16)


def _fox(feat, small, f_bias_col, q_gain_col, k_gain_col, batch, seq):
    tq = min(TQ_FOX, seq)
    nq = seq // tq
    t = batch * seq
    blk = lambda row: pl.BlockSpec((HEAD_DIM, seq), lambda b, h, i: (row // HEAD_DIM + h, b))
    col = lambda n: pl.BlockSpec((n, 1), lambda b, h, i: (0, 0))
    return pl.pallas_call(
        functools.partial(_fox_kernel, seq=seq, tq=tq),
        grid=(batch, FOX_HEADS, nq),
        in_specs=[blk(ROW_FQ), blk(ROW_FK), blk(ROW_FV),
                  pl.BlockSpec((N_SMALL, seq), lambda b, h, i: (0, b)),
                  col(N_SMALL), col(HEAD_DIM), col(HEAD_DIM)],
        out_specs=pl.BlockSpec((HEAD_DIM, tq), lambda b, h, i: (h, b * nq + i)),
        out_shape=jax.ShapeDtypeStruct((FOX_HEADS * HEAD_DIM, t), BF16),
        scratch_shapes=[pltpu.VMEM((2 * HEAD_DIM, seq), BF16),
                        pltpu.VMEM((seq, 2 * HEAD_DIM), BF16),
                        pltpu.VMEM((N_SMALL, seq), F32)],
        compiler_params=pltpu.CompilerParams(
            dimension_semantics=("parallel", "parallel", "arbitrary"), vmem_limit_bytes=VMEM_LIMIT),
        name="fox_attention",
    )(feat, feat, feat, small, f_bias_col, q_gain_col, k_gain_col)


def _conv_kernel(u_ref, b_ref, c_ref, w_ref, out_ref):
    z = c_ref[...].astype(F32) * u_ref[...].astype(F32)
    row = lax.broadcasted_iota(jnp.int32, z.shape, 0)
    z1 = jnp.where(row >= 1, pltpu.roll(z, 1, 0), 0.0)
    z2 = jnp.where(row >= 2, pltpu.roll(z, 2, 0), 0.0)
    w = w_ref[...]
    y = w[0:1, :] * z2 + w[1:2, :] * z1 + w[2:3, :] * z
    out_ref[...] = (b_ref[...].astype(F32) * y).astype(BF16)


def _conv(tok, conv_w, batch, seq):
    width = conv_w.shape[1]
    blk = lambda c: pl.BlockSpec((seq, width), lambda b: (b, c // width))
    return pl.pallas_call(
        _conv_kernel,
        grid=(batch,),
        in_specs=[blk(COL_CU), blk(COL_CB), blk(COL_CC),
                  pl.BlockSpec(conv_w.shape, lambda b: (0, 0))],
        out_specs=pl.BlockSpec((seq, width), lambda b: (b, 0)),
        out_shape=jax.ShapeDtypeStruct((batch * seq, width), BF16),
        compiler_params=pltpu.CompilerParams(
            dimension_semantics=("parallel",), vmem_limit_bytes=VMEM_LIMIT),
        name="short_conv",
    )(tok, tok, tok, conv_w)


def _mlstm_kernel(q_ref, v_ref, o_ref, k_ref, small_ref, bias_ref, gain_ref, out_ref,
                  c_ref, m_ref, *, chunk):
    @pl.when(pl.program_id(1) == 0)
    def _():
        c_ref[...] = jnp.zeros_like(c_ref)
        m_ref[...] = jnp.zeros_like(m_ref)

    g = small_ref[...] + bias_ref[...]
    bcum = _cumsum_lanes(_log_sigmoid(g))
    rows = []
    for h in range(MLSTM_HEADS):
        rows.append(g[4 + h:5 + h, :] - bcum[8 + h:9 + h, :])
    u_rows = _row_select(rows, 8, chunk)
    u_cols = jnp.concatenate([u_rows, jnp.zeros((120, chunk), F32)], axis=0).T

    src = lax.broadcasted_iota(jnp.int32, (chunk, chunk), 0)
    tgt = lax.broadcasted_iota(jnp.int32, (chunk, chunk), 1)
    causal = src <= tgt
    ones_rows = (lax.broadcasted_iota(jnp.int32, (HEAD_DIM, chunk), 0) == 0).astype(BF16)

    for h in range(MLSTM_HEADS):
        sl = slice(h * HEAD_DIM, (h + 1) * HEAD_DIM)
        b_row = bcum[8 + h:9 + h, :]
        b_last = b_row[:, chunk - 1:chunk]
        u_row = rows[h]
        u_col = u_cols[:, h:h + 1]
        m_prev = m_ref[h][0:1, 0:1]

        dmat = jnp.where(causal, b_row + u_col, NEG)
        g_row = b_row + m_prev
        m_t = jnp.maximum(g_row, jnp.max(dmat, axis=0, keepdims=True))
        qs = (q_ref[sl, :].astype(F32) * QK_SCALE).astype(BF16)
        k_h = k_ref[:, sl]
        sw = (_dot(k_h, qs) * jnp.exp(dmat - m_t)).astype(BF16)
        v_aug = jnp.concatenate([v_ref[sl, :], ones_rows], axis=0)
        c_prev = c_ref[h]
        tot = _dot(v_aug, sw) + jnp.exp(g_row - m_t) * _dot(c_prev.astype(BF16), qs)
        num = tot[0:HEAD_DIM]
        den = tot[HEAD_DIM:HEAD_DIM + 1]
        ht = num / jnp.maximum(jnp.abs(den), jnp.exp(-m_t))
        hn = _rms_rows(ht, gain_ref[sl, :])
        out_ref[sl, :] = (_sigmoid(o_ref[sl, :].astype(F32)) * hn).astype(BF16)

        m_loc = jnp.max(u_row + b_last, axis=1, keepdims=True)
        m_new = jnp.maximum(b_last + m_prev, m_loc)
        decay = jnp.exp(b_last + m_prev - m_new)
        w_col = jnp.exp(u_col + b_last - m_new)
        kw = (k_h.astype(F32) * w_col).astype(BF16)
        c_ref[h] = decay * c_prev + _dot(v_aug, kw)
        m_ref[h] = jnp.broadcast_to(m_new, m_ref.shape[1:])


def _mlstm(feat, tok, small, bias_col, gain_col, batch, seq):
    chunk = min(L_MLSTM, seq)
    nc = seq // chunk
    t = batch * seq
    width = MLSTM_HEADS * HEAD_DIM
    blk = lambda row: pl.BlockSpec((width, chunk), lambda b, c: (row // width, b * nc + c))
    return pl.pallas_call(
        functools.partial(_mlstm_kernel, chunk=chunk),
        grid=(batch, nc),
        in_specs=[blk(ROW_MQ), blk(ROW_MV), blk(ROW_MO),
                  pl.BlockSpec((chunk, width), lambda b, c: (b * nc + c, COL_MK // width)),
                  pl.BlockSpec((N_SMALL, chunk), lambda b, c: (0, b * nc + c)),
                  pl.BlockSpec((N_SMALL, 1), lambda b, c: (0, 0)),
                  pl.BlockSpec((width, 1), lambda b, c: (0, 0))],
        out_specs=pl.BlockSpec((width, chunk), lambda b, c: (0, b * nc + c)),
        out_shape=jax.ShapeDtypeStruct((width, t), BF16),
        scratch_shapes=[pltpu.VMEM((MLSTM_HEADS, 2 * HEAD_DIM, HEAD_DIM), F32),
                        pltpu.VMEM((MLSTM_HEADS, 8, 128), F32)],
        compiler_params=pltpu.CompilerParams(
            dimension_semantics=("parallel", "arbitrary"), vmem_limit_bytes=VMEM_LIMIT),
        name="mlstm",
    )(feat, feat, feat, tok, small, bias_col, gain_col)


def _swa_kernel(q_ref, kc_ref, kp_ref, vc_ref, vp_ref, bias_ref, sink_ref, gq_ref, gk_ref, out_ref):
    n = pl.program_id(1)
    w = WINDOW
    key = lax.broadcasted_iota(jnp.int32, (2 * w, SWA_GROUP * w), 0)
    first_block_pad = jnp.logical_and(n == 0, key < w)
    for kv in range(SWA_KV_HEADS):
        ks = slice(kv * HEAD_DIM, (kv + 1) * HEAD_DIM)
        kt = jnp.concatenate([_rms_rows(kp_ref[ks, :].astype(F32), gk_ref[...]),
                              _rms_rows(kc_ref[ks, :].astype(F32), gk_ref[...])], axis=1).astype(BF16)
        qs = []
        for g in range(SWA_GROUP):
            hs = slice((kv * SWA_GROUP + g) * HEAD_DIM, (kv * SWA_GROUP + g + 1) * HEAD_DIM)
            qs.append((_rms_rows(q_ref[hs, :].astype(F32), gq_ref[...]) * QK_SCALE).astype(BF16))
        qt = jnp.concatenate(qs, axis=1)
        s = _dot_tn(kt, qt) + bias_ref[kv]
        s = jnp.where(first_block_pad, NEG, s)
        sink = sink_ref[kv]
        m = jnp.maximum(jnp.max(s, axis=0, keepdims=True), sink)
        p = jnp.exp(s - m)
        denom = jnp.sum(p, axis=0, keepdims=True) + jnp.exp(sink - m)
        vt = jnp.concatenate([vp_ref[ks, :], vc_ref[ks, :]], axis=1)
        o = _dot(vt, p.astype(BF16)) / denom
        for g in range(SWA_GROUP):
            hs = slice((kv * SWA_GROUP + g) * HEAD_DIM, (kv * SWA_GROUP + g + 1) * HEAD_DIM)
            out_ref[hs, :] = o[:, g * w:(g + 1) * w].astype(BF16)


def _swa(feat, bias_t, sink_rows, q_gain_col, k_gain_col, batch, seq):
    w = WINDOW
    nb = seq // w
    t = batch * seq
    qw = SWA_Q_HEADS * HEAD_DIM
    kw = SWA_KV_HEADS * HEAD_DIM
    cur = lambda row: pl.BlockSpec((kw, w), lambda b, n: (row // kw, b * nb + n))
    prev = lambda row: pl.BlockSpec((kw, w), lambda b, n: (row // kw, b * nb + jnp.maximum(n - 1, 0)))
    return pl.pallas_call(
        _swa_kernel,
        grid=(batch, nb),
        in_specs=[pl.BlockSpec((qw, w), lambda b, n: (ROW_SQ // qw, b * nb + n)),
                  cur(ROW_SK), prev(ROW_SK), cur(ROW_SV), prev(ROW_SV),
                  pl.BlockSpec(bias_t.shape, lambda b, n: (0, 0, 0)),
                  pl.BlockSpec(sink_rows.shape, lambda b, n: (0, 0, 0)),
                  pl.BlockSpec((HEAD_DIM, 1), lambda b, n: (0, 0)),
                  pl.BlockSpec((HEAD_DIM, 1), lambda b, n: (0, 0))],
        out_specs=pl.BlockSpec((qw, w), lambda b, n: (0, b * nb + n)),
        out_shape=jax.ShapeDtypeStruct((qw, t), BF16),
        compiler_params=pltpu.CompilerParams(
            dimension_semantics=("parallel", "arbitrary"), vmem_limit_bytes=VMEM_LIMIT),
        name="swa_attention",
    )(feat, feat, feat, feat, feat, bias_t, sink_rows, q_gain_col, k_gain_col)


def _merge_kernel(x_ref, yf_ref, yc_ref, ym_ref, ys_ref, g0_ref, g1_ref, g2_ref, g3_ref,
                  wf_ref, wc_ref, wm_ref, ws_ref, wo_ref, out_ref):
    merged = _sigmoid(g0_ref[...].astype(F32)) * _dot_tn(yf_ref[...], wf_ref[...])
    merged += _sigmoid(g1_ref[...].astype(F32)) * _dot(yc_ref[...], wc_ref[...])
    merged += _sigmoid(g2_ref[...].astype(F32)) * _dot_tn(ym_ref[...], wm_ref[...])
    merged += _sigmoid(g3_ref[...].astype(F32)) * _dot_tn(ys_ref[...], ws_ref[...])
    out_ref[...] = x_ref[...] + _dot(merged.astype(BF16), wo_ref[...])


def _merge(x2, y_fox_t, y_conv, y_mlstm_t, y_swa_t, tok, w_fox, w_conv, w_mlstm, w_swa, w_out):
    t = x2.shape[0]
    tm = min(TM_MERGE, t)
    feat_blk = lambda a: pl.BlockSpec((a.shape[0], tm), lambda i: (0, i))
    full = lambda a: pl.BlockSpec(a.shape, lambda i: (0, 0))
    gate = lambda k: pl.BlockSpec((tm, D_MODEL), lambda i: (i, COL_GATES // D_MODEL + k))
    return pl.pallas_call(
        _merge_kernel,
        grid=(t // tm,),
        in_specs=[pl.BlockSpec((tm, D_MODEL), lambda i: (i, 0)),
                  feat_blk(y_fox_t),
                  pl.BlockSpec((tm, y_conv.shape[1]), lambda i: (i, 0)),
                  feat_blk(y_mlstm_t), feat_blk(y_swa_t),
                  gate(0), gate(1), gate(2), gate(3),
                  full(w_fox), full(w_conv), full(w_mlstm), full(w_swa), full(w_out)],
        out_specs=pl.BlockSpec((tm, D_MODEL), lambda i: (i, 0)),
        out_shape=jax.ShapeDtypeStruct((t, D_MODEL), F32),
        compiler_params=pltpu.CompilerParams(
            dimension_semantics=("parallel",), vmem_limit_bytes=VMEM_LIMIT),
        name="gated_merge",
    )(x2, y_fox_t, y_conv, y_mlstm_t, y_swa_t, tok, tok, tok, tok,
      w_fox, w_conv, w_mlstm, w_swa, w_out)


def _ffn_kernel(x_ref, g_ref, wg_ref, wu_ref, wd_ref, out_ref, hn_ref, acc_ref):
    j = pl.program_id(1)

    @pl.when(j == 0)
    def _():
        x = x_ref[...]
        ms = jnp.mean(x * x, axis=-1, keepdims=True)
        hn_ref[...] = (x * lax.rsqrt(ms + EPS) * g_ref[...]).astype(BF16)
        acc_ref[...] = x

    hn = hn_ref[...]
    gate = _dot(hn, wg_ref[...])
    act = (gate * _sigmoid(gate) * _dot(hn, wu_ref[...])).astype(BF16)
    acc_ref[...] += _dot(act, wd_ref[...])

    @pl.when(j == pl.num_programs(1) - 1)
    def _():
        out_ref[...] = acc_ref[...]


def _ffn(x2, gain, w_gate, w_up, w_down):
    t = x2.shape[0]
    tm = min(TM_FFN, t)
    nf = D_FF // TF_FFN
    return pl.pallas_call(
        _ffn_kernel,
        grid=(t // tm, nf),
        in_specs=[pl.BlockSpec((tm, D_MODEL), lambda i, j: (i, 0)),
                  pl.BlockSpec((1, D_MODEL), lambda i, j: (0, 0)),
                  pl.BlockSpec((D_MODEL, TF_FFN), lambda i, j: (0, j)),
                  pl.BlockSpec((D_MODEL, TF_FFN), lambda i, j: (0, j)),
                  pl.BlockSpec((TF_FFN, D_MODEL), lambda i, j: (j, 0))],
        out_specs=pl.BlockSpec((tm, D_MODEL), lambda i, j: (i, 0)),
        out_shape=jax.ShapeDtypeStruct((t, D_MODEL), F32),
        scratch_shapes=[pltpu.VMEM((tm, D_MODEL), BF16), pltpu.VMEM((tm, D_MODEL), F32)],
        compiler_params=pltpu.CompilerParams(
            dimension_semantics=("parallel", "arbitrary"), vmem_limit_bytes=VMEM_LIMIT),
        name="swiglu_ffn",
    )(x2, gain, w_gate, w_up, w_down)


def _split_w_in(w):
    sizes = (256, 256, 256, 4, 256, 256, 256, 256, 256, 256, 4, 4, 256, 512, 128, 128, 4096)
    parts, off = [], 0
    for s in sizes:
        parts.append(w[:, off:off + s])
        off += s
    (fq, fk, fv, ff, cu, cb, cc, mq, mk, mv, mi, mf, mo, sq, sk, sv, gates) = parts
    w_tok = jnp.concatenate([cu, cb, cc, mk, gates], axis=1).astype(BF16)
    pad = jnp.zeros((w.shape[0], N_SMALL - 12), w.dtype)
    w_feat_t = jnp.concatenate([fq, fk, fv, mq, mv, mo, sq, sk, sv, ff, mi, mf, pad], axis=1).T.astype(BF16)
    return w_tok, w_feat_t


def _col(v, n=None):
    v = v.astype(F32).reshape(-1, 1)
    if n is not None and v.shape[0] < n:
        v = jnp.concatenate([v, jnp.zeros((n - v.shape[0], 1), F32)], axis=0)
    return v


def kernel(x, rel_bias, attn_norm, w_in, fox_f_bias, fox_q_gain, fox_k_gain, conv_w, mlstm_i_bias, mlstm_f_bias, mlstm_h_gain, swa_q_gain, swa_k_gain, swa_sinks, w_fox_out, w_conv_out, w_mlstm_out, w_swa_out, w_merge_out, ffn_norm, w_gate, w_up, w_down):
    batch, seq, _ = x.shape
    depth = w_in.shape[0]
    x2 = x.reshape(batch * seq, D_MODEL)
    bias_t = _swa_bias_table(rel_bias)

    for l in range(depth):
        w_tok, w_feat_t = _split_w_in(w_in[l])
        tok, feat, small = _inproj(x2, attn_norm[l].reshape(1, -1), w_tok, w_feat_t)

        y_fox_t = _fox(feat, small, _col(fox_f_bias[l], N_SMALL), _col(fox_q_gain[l]), _col(fox_k_gain[l]),
                       batch, seq)
        y_conv = _conv(tok, conv_w[l], batch, seq)
        gate_bias = jnp.concatenate([jnp.zeros((4,), F32), mlstm_i_bias[l], mlstm_f_bias[l]])
        y_mlstm_t = _mlstm(feat, tok, small, _col(gate_bias, N_SMALL), _col(mlstm_h_gain[l]), batch, seq)
        sink_rows = jnp.broadcast_to(
            swa_sinks[l].astype(F32).reshape(SWA_KV_HEADS, 1, SWA_GROUP, 1),
            (SWA_KV_HEADS, 1, SWA_GROUP, WINDOW)).reshape(SWA_KV_HEADS, 1, SWA_GROUP * WINDOW)
        y_swa_t = _swa(feat, bias_t, sink_rows, _col(swa_q_gain[l]), _col(swa_k_gain[l]), batch, seq)

        x2 = _merge(x2, y_fox_t, y_conv, y_mlstm_t, y_swa_t, tok,
                    w_fox_out[l].astype(BF16), w_conv_out[l].astype(BF16), w_mlstm_out[l].astype(BF16),
                    w_swa_out[l].astype(BF16), w_merge_out[l].astype(BF16))
        x2 = _ffn(x2, ffn_norm[l].reshape(1, -1), w_gate[l].astype(BF16), w_up[l].astype(BF16),
                  w_down[l].astype(BF16))
    return x2.reshape(batch, seq, D_MODEL)
```

```python
import functools

import numpy as np
import jax
import jax.numpy as jnp
from jax import lax
from jax.experimental import pallas as pl
from jax.experimental.pallas import tpu as pltpu

F32 = jnp.float32
BF16 = jnp.bfloat16

D_MODEL = 1024
HEAD_DIM = 64
FOX_HEADS = 4
MLSTM_HEADS = 4
SWA_Q_HEADS = 8
SWA_KV_HEADS = 2
SWA_GROUP = SWA_Q_HEADS // SWA_KV_HEADS
WINDOW = 128
REL_BUCKETS = 32
REL_MAX_DIST = 128
D_FF = 2816
EPS = 1e-6
NEG = -1e30
QK_SCALE = HEAD_DIM ** -0.5
LOG2E = 1.4426950408889634

ROW_FQ, ROW_FK, ROW_FV = 0, 256, 512
ROW_MQ, ROW_MV, ROW_MO = 768, 1024, 1280
ROW_SQ, ROW_SK, ROW_SV = 1536, 2048, 2176
N_FEAT = 2304
N_SMALL = 16
COL_CU, COL_CB, COL_CC, COL_MK, COL_GATES = 0, 256, 512, 768, 1024
N_TOK = 5120

TM_IN = 512
TN_IN = 512
FEAT_CHUNK = 576
TQ_FOX = 256
L_MLSTM = 256
TM_MERGE = 512
TM_FFN = 512
TF_FFN = 1408
VMEM_LIMIT = 56 * 1024 * 1024


def _dot(a, b):
    return jnp.dot(a, b, preferred_element_type=F32)


def _dot_nt(a, b):
    return lax.dot_general(a, b, (((1,), (1,)), ((), ())), preferred_element_type=F32)


def _dot_tn(a, b):
    return lax.dot_general(a, b, (((0,), (0,)), ((), ())), preferred_element_type=F32)


def _sigmoid(x):
    return 0.5 * jnp.tanh(0.5 * x) + 0.5


def _log_sigmoid(x):
    return jnp.minimum(x, 0.0) - jnp.log(1.0 + jnp.exp(-jnp.abs(x)))


def _cumsum_lanes(x):
    n = x.shape[-1]
    lane = lax.broadcasted_iota(jnp.int32, x.shape, x.ndim - 1)
    k = 1
    while k < n:
        x = x + jnp.where(lane >= k, pltpu.roll(x, k, x.ndim - 1), 0.0)
        k *= 2
    return x


def _row_select(rows, n_rows, width):
    rid = lax.broadcasted_iota(jnp.int32, (n_rows, width), 0)
    out = jnp.zeros((n_rows, width), F32)
    for r, v in enumerate(rows):
        out = jnp.where(rid == r, v, out)
    return out


def _split3(c):
    hi = c.astype(BF16).astype(F32)
    r = c - hi
    mid = r.astype(BF16).astype(F32)
    lo = (r - mid).astype(BF16).astype(F32)
    return hi, mid, lo


def _rms_rows(xt, gain_col):
    ms = jnp.mean(xt * xt, axis=0, keepdims=True)
    return xt * lax.rsqrt(ms + EPS) * gain_col


def _bias_kernel(rb_ref, idx_ref, out_ref):
    kv = pl.program_id(0)
    g = pl.program_id(1)
    head = kv * SWA_GROUP + g
    idx = idx_ref[...]
    acc = jnp.full(idx.shape, NEG, F32)
    for b in range(REL_BUCKETS):
        acc = jnp.where(idx == b, rb_ref[b * SWA_Q_HEADS + head], acc)
    out_ref[0] = acc


def _bucket_table():
    j = np.arange(2 * WINDOW)[:, None]
    i = np.arange(WINDOW)[None, :]
    dist = i + WINDOW - j
    n = np.maximum(dist, 0).astype(np.int32)
    max_exact = REL_BUCKETS // 2
    nf = np.maximum(n, 1).astype(np.float32)
    large = max_exact + (np.log(nf / np.float32(max_exact)) / np.float32(np.log(REL_MAX_DIST / max_exact))
                         * np.float32(REL_BUCKETS - max_exact)).astype(np.int32)
    large = np.minimum(large, REL_BUCKETS - 1)
    bucket = np.where(n < max_exact, n, large)
    return np.where((dist >= 0) & (dist < WINDOW), bucket, -1).astype(np.int32)


def _swa_bias_table(rel_bias):
    idx = jnp.asarray(_bucket_table())
    return pl.pallas_call(
        _bias_kernel,
        grid=(SWA_KV_HEADS, SWA_GROUP),
        in_specs=[
            pl.BlockSpec(memory_space=pltpu.SMEM),
            pl.BlockSpec((2 * WINDOW, WINDOW), lambda kv, g: (0, 0)),
        ],
        out_specs=pl.BlockSpec((1, 2 * WINDOW, WINDOW), lambda kv, g: (kv, 0, g)),
        out_shape=jax.ShapeDtypeStruct((SWA_KV_HEADS, 2 * WINDOW, SWA_GROUP * WINDOW), F32),
        name="swa_bias_table",
    )(rel_bias.reshape(-1), idx)


def _inproj_kernel(x_ref, g_ref, wt_ref, wf_ref, tok_ref, feat_ref, small_ref):
    x = x_ref[...]
    ms = jnp.mean(x * x, axis=-1, keepdims=True)
    xn = (x * lax.rsqrt(ms + EPS) * g_ref[...]).astype(BF16)
    for c in range(N_TOK // TN_IN):
        cols = slice(c * TN_IN, (c + 1) * TN_IN)
        tok_ref[:, cols] = _dot(xn, wt_ref[:, cols]).astype(BF16)
    n_chunks = N_FEAT // FEAT_CHUNK
    for c in range(n_chunks):
        lo = c * FEAT_CHUNK
        hi = lo + FEAT_CHUNK + (N_SMALL if c == n_chunks - 1 else 0)
        r = _dot_nt(wf_ref[lo:hi, :], xn)
        feat_ref[lo:lo + FEAT_CHUNK, :] = r[:FEAT_CHUNK].astype(BF16)
        if c == n_chunks - 1:
            small_ref[...] = r[FEAT_CHUNK:]


def _resident(shape):
    return pl.BlockSpec(shape, lambda *_: (0,) * len(shape), pipeline_mode=pl.Buffered(1))


def _inproj(x2, gain, w_tok, w_feat_t):
    t = x2.shape[0]
    tm = min(TM_IN, t)
    return pl.pallas_call(
        _inproj_kernel,
        grid=(t // tm,),
        in_specs=[
            pl.BlockSpec((tm, D_MODEL), lambda i: (i, 0)),
            _resident((1, D_MODEL)),
            _resident((D_MODEL, N_TOK)),
            _resident((N_FEAT + N_SMALL, D_MODEL)),
        ],
        out_specs=[
            pl.BlockSpec((tm, N_TOK), lambda i: (i, 0)),
            pl.BlockSpec((N_FEAT, tm), lambda i: (0, i)),
            pl.BlockSpec((N_SMALL, tm), lambda i: (0, i)),
        ],
        out_shape=[
            jax.ShapeDtypeStruct((t, N_TOK), BF16),
            jax.ShapeDtypeStruct((N_FEAT, t), BF16),
            jax.ShapeDtypeStruct((N_SMALL, t), F32),
        ],
        compiler_params=pltpu.CompilerParams(
            dimension_semantics=("parallel",), vmem_limit_bytes=VMEM_LIMIT),
        name="inproj",
    )(x2, gain, w_tok, w_feat_t)


def _fox_kernel(fq_ref, fk_ref, fv_ref, small_ref, fb_ref, gq_ref, gk_ref, out_ref,
                qa_ref, ka_ref, s_ref, p_ref, acc_ref, m_ref, l_ref, a_ref, *, seq, tq):
    i = pl.program_id(1)
    heads = range(FOX_HEADS)

    @pl.when(i == 0)
    def _():
        c_all = _cumsum_lanes(_log_sigmoid(small_ref[...] + fb_ref[...]))
        for h in heads:
            sl = slice(h * HEAD_DIM, (h + 1) * HEAD_DIM)
            c = c_all[h:h + 1, :] * LOG2E
            hi, mid, lo = _split3(c)
            one = jnp.ones_like(c)
            qn = _rms_rows(fq_ref[sl, :].astype(F32), gq_ref[...]) * (QK_SCALE * LOG2E)
            qa_ref[h, 0:HEAD_DIM, :] = qn.astype(BF16)
            qa_ref[h, HEAD_DIM:, :] = _row_select([hi, mid, lo, one, one, one], HEAD_DIM, seq).astype(BF16)
            kn = _rms_rows(fk_ref[sl, :].astype(F32), gk_ref[...])
            k_aug = _row_select([one, one, one, -hi, -mid, -lo], HEAD_DIM, seq)
            ka_ref[h] = jnp.concatenate([kn, k_aug], axis=0).T.astype(BF16)

    q_off = pl.multiple_of(i * tq, tq)
    qa = [qa_ref[h, :, pl.ds(q_off, tq)] for h in heads]

    m_ref[...] = jnp.full(m_ref.shape, NEG, F32)
    l_ref[...] = jnp.zeros_like(l_ref)
    acc_ref[...] = jnp.zeros_like(acc_ref)
    a_ref[...] = jnp.ones_like(a_ref)
    p_ref[...] = jnp.zeros_like(p_ref)

    def qk(j, slot):
        off = pl.multiple_of(j * tq, tq)
        for h in heads:
            s_ref[slot, h] = _dot(ka_ref[h, pl.ds(off, tq), :], qa[h])

    def pv(j):
        off = pl.multiple_of(j * tq, tq)
        for h in heads:
            v = fv_ref[h * HEAD_DIM:(h + 1) * HEAD_DIM, pl.ds(off, tq)]
            acc_ref[h] = a_ref[h] * acc_ref[h] + _dot(v, p_ref[h])

    def softmax(slot, masked):
        for h in heads:
            s = s_ref[slot, h]
            if masked:
                key = lax.broadcasted_iota(jnp.int32, s.shape, 0)
                qry = lax.broadcasted_iota(jnp.int32, s.shape, 1)
                s = jnp.where(key <= qry, s, NEG)
            m = m_ref[h]
            m_new = jnp.maximum(m, jnp.max(s, axis=0, keepdims=True))
            p = jnp.exp2(s - m_new)
            alpha = jnp.exp2(m - m_new)
            l_ref[h] = alpha * l_ref[h] + jnp.sum(p, axis=0, keepdims=True)
            m_ref[h] = m_new
            a_ref[h] = alpha
            p_ref[h] = p.astype(BF16)

    def stage(j, slot):
        qk(j + 1, 1 - slot)
        pv(jnp.maximum(j - 1, 0))
        softmax(slot, False)

    qk(0, 0)

    @pl.loop(0, i // 2)
    def _(k):
        stage(2 * k, 0)
        stage(2 * k + 1, 1)

    odd = lax.rem(i, 2) == 1

    @pl.when(odd)
    def _():
        stage(i - 1, 0)
        pv(i - 1)
        softmax(1, True)

    @pl.when(jnp.logical_not(odd))
    def _():
        pv(jnp.maximum(i - 1, 0))
        softmax(0, True)

    pv(i)
    for h in heads:
        out_ref[h * HEAD_DIM:(h + 1) * HEAD_DIM, :] = (acc_ref[h] / l_ref[h]).astype(BF16)


def _fox(feat, small, f_bias_col, q_gain_col, k_gain_col, batch, seq):
    tq = min(TQ_FOX, seq)
    nq = seq // tq
    t = batch * seq
    width = FOX_HEADS * HEAD_DIM
    blk = lambda row: pl.BlockSpec((width, seq), lambda b, i: (row // width, b))
    col = lambda n: pl.BlockSpec((n, 1), lambda b, i: (0, 0))
    return pl.pallas_call(
        functools.partial(_fox_kernel, seq=seq, tq=tq),
        grid=(batch, nq),
        in_specs=[blk(ROW_FQ), blk(ROW_FK), blk(ROW_FV),
                  pl.BlockSpec((N_SMALL, seq), lambda b, i: (0, b)),
                  col(N_SMALL), col(HEAD_DIM), col(HEAD_DIM)],
        out_specs=pl.BlockSpec((width, tq), lambda b, i: (0, b * nq + i)),
        out_shape=jax.ShapeDtypeStruct((width, t), BF16),
        scratch_shapes=[pltpu.VMEM((FOX_HEADS, 2 * HEAD_DIM, seq), BF16),
                        pltpu.VMEM((FOX_HEADS, seq, 2 * HEAD_DIM), BF16),
                        pltpu.VMEM((2, FOX_HEADS, tq, tq), F32),
                        pltpu.VMEM((FOX_HEADS, tq, tq), BF16),
                        pltpu.VMEM((FOX_HEADS, HEAD_DIM, tq), F32),
                        pltpu.VMEM((FOX_HEADS, 1, tq), F32),
                        pltpu.VMEM((FOX_HEADS, 1, tq), F32),
                        pltpu.VMEM((FOX_HEADS, 1, tq), F32)],
        compiler_params=pltpu.CompilerParams(
            dimension_semantics=("parallel", "arbitrary"), vmem_limit_bytes=VMEM_LIMIT),
        name="fox_attention",
    )(feat, feat, feat, small, f_bias_col, q_gain_col, k_gain_col)


def _conv_kernel(u_ref, b_ref, c_ref, w_ref, out_ref):
    z = c_ref[...].astype(F32) * u_ref[...].astype(F32)
    row = lax.broadcasted_iota(jnp.int32, z.shape, 0)
    z1 = jnp.where(row >= 1, pltpu.roll(z, 1, 0), 0.0)
    z2 = jnp.where(row >= 2, pltpu.roll(z, 2, 0), 0.0)
    w = w_ref[...]
    y = w[0:1, :] * z2 + w[1:2, :] * z1 + w[2:3, :] * z
    out_ref[...] = (b_ref[...].astype(F32) * y).astype(BF16)


def _conv(tok, conv_w, batch, seq):
    width = conv_w.shape[1]
    blk = lambda c: pl.BlockSpec((seq, width), lambda b: (b, c // width))
    return pl.pallas_call(
        _conv_kernel,
        grid=(batch,),
        in_specs=[blk(COL_CU), blk(COL_CB), blk(COL_CC),
                  pl.BlockSpec(conv_w.shape, lambda b: (0, 0))],
        out_specs=pl.BlockSpec((seq, width), lambda b: (b, 0)),
        out_shape=jax.ShapeDtypeStruct((batch * seq, width), BF16),
        compiler_params=pltpu.CompilerParams(
            dimension_semantics=("parallel",), vmem_limit_bytes=VMEM_LIMIT),
        name="short_conv",
    )(tok, tok, tok, conv_w)


def _mlstm_kernel(q_ref, v_ref, o_ref, k_ref, small_ref, bias_ref, gain_ref, out_ref,
                  c_ref, m_ref, *, chunk):
    @pl.when(pl.program_id(1) == 0)
    def _():
        c_ref[...] = jnp.zeros_like(c_ref)
        m_ref[...] = jnp.zeros_like(m_ref)

    g = small_ref[...] + bias_ref[...]
    bcum = _cumsum_lanes(_log_sigmoid(g))
    rows = []
    for h in range(MLSTM_HEADS):
        rows.append(g[4 + h:5 + h, :] - bcum[8 + h:9 + h, :])
    u_rows = _row_select(rows, 8, chunk)
    u_cols = jnp.concatenate([u_rows, jnp.zeros((120, chunk), F32)], axis=0).T

    src = lax.broadcasted_iota(jnp.int32, (chunk, chunk), 0)
    tgt = lax.broadcasted_iota(jnp.int32, (chunk, chunk), 1)
    causal = src <= tgt
    ones_rows = (lax.broadcasted_iota(jnp.int32, (HEAD_DIM, chunk), 0) == 0).astype(BF16)

    for h in range(MLSTM_HEADS):
        sl = slice(h * HEAD_DIM, (h + 1) * HEAD_DIM)
        b_row = bcum[8 + h:9 + h, :]
        b_last = b_row[:, chunk - 1:chunk]
        u_row = rows[h]
        u_col = u_cols[:, h:h + 1]
        m_prev = m_ref[h][0:1, 0:1]

        dmat = jnp.where(causal, b_row + u_col, NEG)
        g_row = b_row + m_prev
        m_t = jnp.maximum(g_row, jnp.max(dmat, axis=0, keepdims=True))
        qs = (q_ref[sl, :].astype(F32) * QK_SCALE).astype(BF16)
        k_h = k_ref[:, sl]
        sw = (_dot(k_h, qs) * jnp.exp(dmat - m_t)).astype(BF16)
        v_aug = jnp.concatenate([v_ref[sl, :], ones_rows], axis=0)
        c_prev = c_ref[h]
        tot = _dot(v_aug, sw) + jnp.exp(g_row - m_t) * _dot(c_prev.astype(BF16), qs)
        num = tot[0:HEAD_DIM]
        den = tot[HEAD_DIM:HEAD_DIM + 1]
        ht = num / jnp.maximum(jnp.abs(den), jnp.exp(-m_t))
        hn = _rms_rows(ht, gain_ref[sl, :])
        out_ref[sl, :] = (_sigmoid(o_ref[sl, :].astype(F32)) * hn).astype(BF16)

        m_loc = jnp.max(u_row + b_last, axis=1, keepdims=True)
        m_new = jnp.maximum(b_last + m_prev, m_loc)
        decay = jnp.exp(b_last + m_prev - m_new)
        w_col = jnp.exp(u_col + b_last - m_new)
        kw = (k_h.astype(F32) * w_col).astype(BF16)
        c_ref[h] = decay * c_prev + _dot(v_aug, kw)
        m_ref[h] = jnp.broadcast_to(m_new, m_ref.shape[1:])


def _mlstm(feat, tok, small, bias_col, gain_col, batch, seq):
    chunk = min(L_MLSTM, seq)
    nc = seq // chunk
    t = batch * seq
    width = MLSTM_HEADS * HEAD_DIM
    blk = lambda row: pl.BlockSpec((width, chunk), lambda b, c: (row // width, b * nc + c))
    return pl.pallas_call(
        functools.partial(_mlstm_kernel, chunk=chunk),
        grid=(batch, nc),
        in_specs=[blk(ROW_MQ), blk(ROW_MV), blk(ROW_MO),
                  pl.BlockSpec((chunk, width), lambda b, c: (b * nc + c, COL_MK // width)),
                  pl.BlockSpec((N_SMALL, chunk), lambda b, c: (0, b * nc + c)),
                  pl.BlockSpec((N_SMALL, 1), lambda b, c: (0, 0)),
                  pl.BlockSpec((width, 1), lambda b, c: (0, 0))],
        out_specs=pl.BlockSpec((width, chunk), lambda b, c: (0, b * nc + c)),
        out_shape=jax.ShapeDtypeStruct((width, t), BF16),
        scratch_shapes=[pltpu.VMEM((MLSTM_HEADS, 2 * HEAD_DIM, HEAD_DIM), F32),
                        pltpu.VMEM((MLSTM_HEADS, 8, 128), F32)],
        compiler_params=pltpu.CompilerParams(
            dimension_semantics=("parallel", "arbitrary"), vmem_limit_bytes=VMEM_LIMIT),
        name="mlstm",
    )(feat, feat, feat, tok, small, bias_col, gain_col)


def _swa_kernel(q_ref, kc_ref, kp_ref, vc_ref, vp_ref, bias_ref, sink_ref, gq_ref, gk_ref, out_ref):
    n = pl.program_id(1)
    w = WINDOW
    key = lax.broadcasted_iota(jnp.int32, (2 * w, SWA_GROUP * w), 0)
    first_block_pad = jnp.logical_and(n == 0, key < w)
    for kv in range(SWA_KV_HEADS):
        ks = slice(kv * HEAD_DIM, (kv + 1) * HEAD_DIM)
        kt = jnp.concatenate([_rms_rows(kp_ref[ks, :].astype(F32), gk_ref[...]),
                              _rms_rows(kc_ref[ks, :].astype(F32), gk_ref[...])], axis=1).astype(BF16)
        qs = []
        for g in range(SWA_GROUP):
            hs = slice((kv * SWA_GROUP + g) * HEAD_DIM, (kv * SWA_GROUP + g + 1) * HEAD_DIM)
            qs.append((_rms_rows(q_ref[hs, :].astype(F32), gq_ref[...]) * QK_SCALE).astype(BF16))
        qt = jnp.concatenate(qs, axis=1)
        s = _dot_tn(kt, qt) + bias_ref[kv]
        s = jnp.where(first_block_pad, NEG, s)
        sink = sink_ref[kv]
        m = jnp.maximum(jnp.max(s, axis=0, keepdims=True), sink)
        p = jnp.exp(s - m)
        denom = jnp.sum(p, axis=0, keepdims=True) + jnp.exp(sink - m)
        vt = jnp.concatenate([vp_ref[ks, :], vc_ref[ks, :]], axis=1)
        o = _dot(vt, p.astype(BF16)) / denom
        for g in range(SWA_GROUP):
            hs = slice((kv * SWA_GROUP + g) * HEAD_DIM, (kv * SWA_GROUP + g + 1) * HEAD_DIM)
            out_ref[hs, :] = o[:, g * w:(g + 1) * w].astype(BF16)


def _swa(feat, bias_t, sink_rows, q_gain_col, k_gain_col, batch, seq):
    w = WINDOW
    nb = seq // w
    t = batch * seq
    qw = SWA_Q_HEADS * HEAD_DIM
    kw = SWA_KV_HEADS * HEAD_DIM
    cur = lambda row: pl.BlockSpec((kw, w), lambda b, n: (row // kw, b * nb + n))
    prev = lambda row: pl.BlockSpec((kw, w), lambda b, n: (row // kw, b * nb + jnp.maximum(n - 1, 0)))
    return pl.pallas_call(
        _swa_kernel,
        grid=(batch, nb),
        in_specs=[pl.BlockSpec((qw, w), lambda b, n: (ROW_SQ // qw, b * nb + n)),
                  cur(ROW_SK), prev(ROW_SK), cur(ROW_SV), prev(ROW_SV),
                  pl.BlockSpec(bias_t.shape, lambda b, n: (0, 0, 0)),
                  pl.BlockSpec(sink_rows.shape, lambda b, n: (0, 0, 0)),
                  pl.BlockSpec((HEAD_DIM, 1), lambda b, n: (0, 0)),
                  pl.BlockSpec((HEAD_DIM, 1), lambda b, n: (0, 0))],
        out_specs=pl.BlockSpec((qw, w), lambda b, n: (0, b * nb + n)),
        out_shape=jax.ShapeDtypeStruct((qw, t), BF16),
        compiler_params=pltpu.CompilerParams(
            dimension_semantics=("parallel", "arbitrary"), vmem_limit_bytes=VMEM_LIMIT),
        name="swa_attention",
    )(feat, feat, feat, feat, feat, bias_t, sink_rows, q_gain_col, k_gain_col)


def _merge_kernel(x_ref, yf_ref, yc_ref, ym_ref, ys_ref, g0_ref, g1_ref, g2_ref, g3_ref,
                  wf_ref, wc_ref, wm_ref, ws_ref, wo_ref, out_ref):
    merged = _sigmoid(g0_ref[...].astype(F32)) * _dot_tn(yf_ref[...], wf_ref[...])
    merged += _sigmoid(g1_ref[...].astype(F32)) * _dot(yc_ref[...], wc_ref[...])
    merged += _sigmoid(g2_ref[...].astype(F32)) * _dot_tn(ym_ref[...], wm_ref[...])
    merged += _sigmoid(g3_ref[...].astype(F32)) * _dot_tn(ys_ref[...], ws_ref[...])
    out_ref[...] = x_ref[...] + _dot(merged.astype(BF16), wo_ref[...])


def _merge(x2, y_fox_t, y_conv, y_mlstm_t, y_swa_t, tok, w_fox, w_conv, w_mlstm, w_swa, w_out):
    t = x2.shape[0]
    tm = min(TM_MERGE, t)
    feat_blk = lambda a: pl.BlockSpec((a.shape[0], tm), lambda i: (0, i))
    full = lambda a: pl.BlockSpec(a.shape, lambda i: (0, 0))
    gate = lambda k: pl.BlockSpec((tm, D_MODEL), lambda i: (i, COL_GATES // D_MODEL + k))
    return pl.pallas_call(
        _merge_kernel,
        grid=(t // tm,),
        in_specs=[pl.BlockSpec((tm, D_MODEL), lambda i: (i, 0)),
                  feat_blk(y_fox_t),
                  pl.BlockSpec((tm, y_conv.shape[1]), lambda i: (i, 0)),
                  feat_blk(y_mlstm_t), feat_blk(y_swa_t),
                  gate(0), gate(1), gate(2), gate(3),
                  full(w_fox), full(w_conv), full(w_mlstm), full(w_swa), full(w_out)],
        out_specs=pl.BlockSpec((tm, D_MODEL), lambda i: (i, 0)),
        out_shape=jax.ShapeDtypeStruct((t, D_MODEL), F32),
        compiler_params=pltpu.CompilerParams(
            dimension_semantics=("parallel",), vmem_limit_bytes=VMEM_LIMIT),
        name="gated_merge",
    )(x2, y_fox_t, y_conv, y_mlstm_t, y_swa_t, tok, tok, tok, tok,
      w_fox, w_conv, w_mlstm, w_swa, w_out)


def _ffn_kernel(x_ref, g_ref, wg_ref, wu_ref, wd_ref, out_ref, hn_ref, acc_ref):
    j = pl.program_id(1)

    @pl.when(j == 0)
    def _():
        x = x_ref[...]
        ms = jnp.mean(x * x, axis=-1, keepdims=True)
        hn_ref[...] = (x * lax.rsqrt(ms + EPS) * g_ref[...]).astype(BF16)
        acc_ref[...] = x

    hn = hn_ref[...]
    gate = _dot(hn, wg_ref[...])
    act = (gate * _sigmoid(gate) * _dot(hn, wu_ref[...])).astype(BF16)
    acc_ref[...] += _dot(act, wd_ref[...])

    @pl.when(j == pl.num_programs(1) - 1)
    def _():
        out_ref[...] = acc_ref[...]


def _ffn(x2, gain, w_gate, w_up, w_down):
    t = x2.shape[0]
    tm = min(TM_FFN, t)
    nf = D_FF // TF_FFN
    return pl.pallas_call(
        _ffn_kernel,
        grid=(t // tm, nf),
        in_specs=[pl.BlockSpec((tm, D_MODEL), lambda i, j: (i, 0)),
                  pl.BlockSpec((1, D_MODEL), lambda i, j: (0, 0)),
                  pl.BlockSpec((D_MODEL, TF_FFN), lambda i, j: (0, j)),
                  pl.BlockSpec((D_MODEL, TF_FFN), lambda i, j: (0, j)),
                  pl.BlockSpec((TF_FFN, D_MODEL), lambda i, j: (j, 0))],
        out_specs=pl.BlockSpec((tm, D_MODEL), lambda i, j: (i, 0)),
        out_shape=jax.ShapeDtypeStruct((t, D_MODEL), F32),
        scratch_shapes=[pltpu.VMEM((tm, D_MODEL), BF16), pltpu.VMEM((tm, D_MODEL), F32)],
        compiler_params=pltpu.CompilerParams(
            dimension_semantics=("parallel", "arbitrary"), vmem_limit_bytes=VMEM_LIMIT),
        name="swiglu_ffn",
    )(x2, gain, w_gate, w_up, w_down)


def _split_w_in(w):
    sizes = (256, 256, 256, 4, 256, 256, 256, 256, 256, 256, 4, 4, 256, 512, 128, 128, 4096)
    parts, off = [], 0
    for s in sizes:
        parts.append(w[:, off:off + s])
        off += s
    (fq, fk, fv, ff, cu, cb, cc, mq, mk, mv, mi, mf, mo, sq, sk, sv, gates) = parts
    w_tok = jnp.concatenate([cu, cb, cc, mk, gates], axis=1).astype(BF16)
    pad = jnp.zeros((w.shape[0], N_SMALL - 12), w.dtype)
    w_feat_t = jnp.concatenate([fq, fk, fv, mq, mv, mo, sq, sk, sv, ff, mi, mf, pad], axis=1).T.astype(BF16)
    return w_tok, w_feat_t


def _col(v, n=None):
    v = v.astype(F32).reshape(-1, 1)
    if n is not None and v.shape[0] < n:
        v = jnp.concatenate([v, jnp.zeros((n - v.shape[0], 1), F32)], axis=0)
    return v


def kernel(x, rel_bias, attn_norm, w_in, fox_f_bias, fox_q_gain, fox_k_gain, conv_w, mlstm_i_bias, mlstm_f_bias, mlstm_h_gain, swa_q_gain, swa_k_gain, swa_sinks, w_fox_out, w_conv_out, w_mlstm_out, w_swa_out, w_merge_out, ffn_norm, w_gate, w_up, w_down):
    batch, seq, _ = x.shape
    depth = w_in.shape[0]
    x2 = x.reshape(batch * seq, D_MODEL)
    bias_t = _swa_bias_table(rel_bias)

    for l in range(depth):
        w_tok, w_feat_t = _split_w_in(w_in[l])
        tok, feat, small = _inproj(x2, attn_norm[l].reshape(1, -1), w_tok, w_feat_t)

        y_fox_t = _fox(feat, small, _col(fox_f_bias[l], N_SMALL), _col(fox_q_gain[l]), _col(fox_k_gain[l]),
                       batch, seq)
        y_conv = _conv(tok, conv_w[l], batch, seq)
        gate_bias = jnp.concatenate([jnp.zeros((4,), F32), mlstm_i_bias[l], mlstm_f_bias[l]])
        y_mlstm_t = _mlstm(feat, tok, small, _col(gate_bias, N_SMALL), _col(mlstm_h_gain[l]), batch, seq)
        sink_rows = jnp.broadcast_to(
            swa_sinks[l].astype(F32).reshape(SWA_KV_HEADS, 1, SWA_GROUP, 1),
            (SWA_KV_HEADS, 1, SWA_GROUP, WINDOW)).reshape(SWA_KV_HEADS, 1, SWA_GROUP * WINDOW)
        y_swa_t = _swa(feat, bias_t, sink_rows, _col(swa_q_gain[l]), _col(swa_k_gain[l]), batch, seq)

        x2 = _merge(x2, y_fox_t, y_conv, y_mlstm_t, y_swa_t, tok,
                    w_fox_out[l].astype(BF16), w_conv_out[l].astype(BF16), w_mlstm_out[l].astype(BF16),
                    w_swa_out[l].astype(BF16), w_merge_out[l].astype(BF16))
        x2 = _ffn(x2, ffn_norm[l].reshape(1, -1), w_gate[l].astype(BF16), w_up[l].astype(BF16),
                  w_down[l].astype(BF16))
    return x2.reshape(batch, seq, D_MODEL)
```

```python
import functools

import numpy as np
import jax
import jax.numpy as jnp
from jax import lax
from jax.experimental import pallas as pl
from jax.experimental.pallas import tpu as pltpu

F32 = jnp.float32
BF16 = jnp.bfloat16

D_MODEL = 1024
HEAD_DIM = 64
FOX_HEADS = 4
MLSTM_HEADS = 4
SWA_Q_HEADS = 8
SWA_KV_HEADS = 2
SWA_GROUP = SWA_Q_HEADS // SWA_KV_HEADS
WINDOW = 128
REL_BUCKETS = 32
REL_MAX_DIST = 128
D_FF = 2816
EPS = 1e-6
NEG = -1e30
QK_SCALE = HEAD_DIM ** -0.5
LOG2E = 1.4426950408889634

ROW_FQ, ROW_FK, ROW_FV = 0, 256, 512
ROW_MQ, ROW_MV, ROW_MO = 768, 1024, 1280
ROW_SQ, ROW_SK, ROW_SV = 1536, 2048, 2176
N_FEAT = 2304
N_SMALL = 16
COL_CU, COL_CB, COL_CC, COL_MK, COL_GATES = 0, 256, 512, 768, 1024
N_TOK = 5120

TM_IN = 512
TN_IN = 512
FEAT_CHUNK = 576
TQ_FOX = 256
L_MLSTM = 256
SWA_BLOCKS = 2
TM_MERGE = 512
TM_FFN = 512
TF_FFN = 1408
VMEM_LIMIT = 56 * 1024 * 1024


def _dot(a, b):
    return jnp.dot(a, b, preferred_element_type=F32)


def _dot_nt(a, b):
    return lax.dot_general(a, b, (((1,), (1,)), ((), ())), preferred_element_type=F32)


def _dot_tn(a, b):
    return lax.dot_general(a, b, (((0,), (0,)), ((), ())), preferred_element_type=F32)


def _sigmoid(x):
    return 0.5 * jnp.tanh(0.5 * x) + 0.5


def _log_sigmoid(x):
    return jnp.minimum(x, 0.0) - jnp.log(1.0 + jnp.exp(-jnp.abs(x)))


def _cumsum_lanes(x, segment=None):
    n = segment or x.shape[-1]
    lane = lax.broadcasted_iota(jnp.int32, x.shape, x.ndim - 1) & (n - 1)
    k = 1
    while k < n:
        x = x + jnp.where(lane >= k, pltpu.roll(x, k, x.ndim - 1), 0.0)
        k *= 2
    return x


def _row_select(rows, n_rows, width):
    rid = lax.broadcasted_iota(jnp.int32, (n_rows, width), 0)
    out = jnp.zeros((n_rows, width), F32)
    for r, v in enumerate(rows):
        out = jnp.where(rid == r, v, out)
    return out


def _split3(c):
    hi = c.astype(BF16).astype(F32)
    r = c - hi
    mid = r.astype(BF16).astype(F32)
    lo = (r - mid).astype(BF16).astype(F32)
    return hi, mid, lo


def _rms_rows(xt, gain_col):
    ms = jnp.mean(xt * xt, axis=0, keepdims=True)
    return xt * lax.rsqrt(ms + EPS) * gain_col


def _bias_kernel(rb_ref, idx_ref, out_ref):
    kv = pl.program_id(0)
    g = pl.program_id(1)
    head = kv * SWA_GROUP + g
    idx = idx_ref[...]
    acc = jnp.full(idx.shape, NEG, F32)
    for b in range(REL_BUCKETS):
        acc = jnp.where(idx == b, rb_ref[b * SWA_Q_HEADS + head], acc)
    out_ref[0] = acc


def _bucket_table():
    j = np.arange(2 * WINDOW)[:, None]
    i = np.arange(WINDOW)[None, :]
    dist = i + WINDOW - j
    n = np.maximum(dist, 0).astype(np.int32)
    max_exact = REL_BUCKETS // 2
    nf = np.maximum(n, 1).astype(np.float32)
    large = max_exact + (np.log(nf / np.float32(max_exact)) / np.float32(np.log(REL_MAX_DIST / max_exact))
                         * np.float32(REL_BUCKETS - max_exact)).astype(np.int32)
    large = np.minimum(large, REL_BUCKETS - 1)
    bucket = np.where(n < max_exact, n, large)
    return np.where((dist >= 0) & (dist < WINDOW), bucket, -1).astype(np.int32)


def _swa_bias_table(rel_bias):
    idx = jnp.asarray(_bucket_table())
    return pl.pallas_call(
        _bias_kernel,
        grid=(SWA_KV_HEADS, SWA_GROUP),
        in_specs=[
            pl.BlockSpec(memory_space=pltpu.SMEM),
            pl.BlockSpec((2 * WINDOW, WINDOW), lambda kv, g: (0, 0)),
        ],
        out_specs=pl.BlockSpec((1, 2 * WINDOW, WINDOW), lambda kv, g: (kv, 0, g)),
        out_shape=jax.ShapeDtypeStruct((SWA_KV_HEADS, 2 * WINDOW, SWA_GROUP * WINDOW), F32),
        name="swa_bias_table",
    )(rel_bias.reshape(-1), idx)


def _inproj_kernel(x_ref, g_ref, wt_ref, wf_ref, tok_ref, feat_ref, small_ref):
    x = x_ref[...]
    ms = jnp.mean(x * x, axis=-1, keepdims=True)
    xn = (x * lax.rsqrt(ms + EPS) * g_ref[...]).astype(BF16)
    for c in range(N_TOK // TN_IN):
        cols = slice(c * TN_IN, (c + 1) * TN_IN)
        tok_ref[:, cols] = _dot(xn, wt_ref[:, cols]).astype(BF16)
    n_chunks = N_FEAT // FEAT_CHUNK
    for c in range(n_chunks):
        lo = c * FEAT_CHUNK
        hi = lo + FEAT_CHUNK + (N_SMALL if c == n_chunks - 1 else 0)
        r = _dot_nt(wf_ref[lo:hi, :], xn)
        feat_ref[lo:lo + FEAT_CHUNK, :] = r[:FEAT_CHUNK].astype(BF16)
        if c == n_chunks - 1:
            small_ref[...] = r[FEAT_CHUNK:]


def _resident(shape):
    return pl.BlockSpec(shape, lambda *_: (0,) * len(shape), pipeline_mode=pl.Buffered(1))


def _inproj(x2, gain, w_tok, w_feat_t):
    t = x2.shape[0]
    tm = min(TM_IN, t)
    return pl.pallas_call(
        _inproj_kernel,
        grid=(t // tm,),
        in_specs=[
            pl.BlockSpec((tm, D_MODEL), lambda i: (i, 0)),
            _resident((1, D_MODEL)),
            _resident((D_MODEL, N_TOK)),
            _resident((N_FEAT + N_SMALL, D_MODEL)),
        ],
        out_specs=[
            pl.BlockSpec((tm, N_TOK), lambda i: (i, 0)),
            pl.BlockSpec((N_FEAT, tm), lambda i: (0, i)),
            pl.BlockSpec((N_SMALL, tm), lambda i: (0, i)),
        ],
        out_shape=[
            jax.ShapeDtypeStruct((t, N_TOK), BF16),
            jax.ShapeDtypeStruct((N_FEAT, t), BF16),
            jax.ShapeDtypeStruct((N_SMALL, t), F32),
        ],
        compiler_params=pltpu.CompilerParams(
            dimension_semantics=("parallel",), vmem_limit_bytes=VMEM_LIMIT),
        name="inproj",
    )(x2, gain, w_tok, w_feat_t)


def _fox_kernel(fq_ref, fk_ref, fv_ref, small_ref, fb_ref, gq_ref, gk_ref, out_ref,
                qa_ref, ka_ref, s_ref, p_ref, acc_ref, m_ref, l_ref, a_ref, *, seq, tq):
    i = pl.program_id(1)
    heads = range(FOX_HEADS)

    @pl.when(i == 0)
    def _():
        c_all = _cumsum_lanes(_log_sigmoid(small_ref[...] + fb_ref[...]))
        for h in heads:
            sl = slice(h * HEAD_DIM, (h + 1) * HEAD_DIM)
            c = c_all[h:h + 1, :] * LOG2E
            hi, mid, lo = _split3(c)
            one = jnp.ones_like(c)
            qn = _rms_rows(fq_ref[sl, :].astype(F32), gq_ref[...]) * (QK_SCALE * LOG2E)
            qa_ref[h, 0:HEAD_DIM, :] = qn.astype(BF16)
            qa_ref[h, HEAD_DIM:, :] = _row_select([hi, mid, lo, one, one, one], HEAD_DIM, seq).astype(BF16)
            kn = _rms_rows(fk_ref[sl, :].astype(F32), gk_ref[...])
            k_aug = _row_select([one, one, one, -hi, -mid, -lo], HEAD_DIM, seq)
            ka_ref[h] = jnp.concatenate([kn, k_aug], axis=0).T.astype(BF16)

    q_off = pl.multiple_of(i * tq, tq)
    qa = [qa_ref[h, :, pl.ds(q_off, tq)] for h in heads]

    m_ref[...] = jnp.full(m_ref.shape, NEG, F32)
    l_ref[...] = jnp.zeros_like(l_ref)
    acc_ref[...] = jnp.zeros_like(acc_ref)
    a_ref[...] = jnp.ones_like(a_ref)
    p_ref[...] = jnp.zeros_like(p_ref)

    def qk(j, slot):
        off = pl.multiple_of(j * tq, tq)
        for h in heads:
            s_ref[slot, h] = _dot(ka_ref[h, pl.ds(off, tq), :], qa[h])

    def pv(j):
        off = pl.multiple_of(j * tq, tq)
        for h in heads:
            v = fv_ref[h * HEAD_DIM:(h + 1) * HEAD_DIM, pl.ds(off, tq)]
            acc_ref[h] = a_ref[h] * acc_ref[h] + _dot(v, p_ref[h])

    def softmax(slot, masked):
        for h in heads:
            s = s_ref[slot, h]
            if masked:
                key = lax.broadcasted_iota(jnp.int32, s.shape, 0)
                qry = lax.broadcasted_iota(jnp.int32, s.shape, 1)
                s = jnp.where(key <= qry, s, NEG)
            m = m_ref[h]
            m_new = jnp.maximum(m, jnp.max(s, axis=0, keepdims=True))
            p = jnp.exp2(s - m_new)
            alpha = jnp.exp2(m - m_new)
            l_ref[h] = alpha * l_ref[h] + jnp.sum(p, axis=0, keepdims=True)
            m_ref[h] = m_new
            a_ref[h] = alpha
            p_ref[h] = p.astype(BF16)

    def stage(j, slot):
        qk(j + 1, 1 - slot)
        pv(jnp.maximum(j - 1, 0))
        softmax(slot, False)

    qk(0, 0)

    @pl.loop(0, i // 2)
    def _(k):
        stage(2 * k, 0)
        stage(2 * k + 1, 1)

    odd = lax.rem(i, 2) == 1

    @pl.when(odd)
    def _():
        stage(i - 1, 0)
        pv(i - 1)
        softmax(1, True)

    @pl.when(jnp.logical_not(odd))
    def _():
        pv(jnp.maximum(i - 1, 0))
        softmax(0, True)

    pv(i)
    for h in heads:
        out_ref[h * HEAD_DIM:(h + 1) * HEAD_DIM, :] = (acc_ref[h] / l_ref[h]).astype(BF16)


def _fox(feat, small, f_bias_col, q_gain_col, k_gain_col, batch, seq):
    tq = min(TQ_FOX, seq)
    nq = seq // tq
    t = batch * seq
    width = FOX_HEADS * HEAD_DIM
    blk = lambda row: pl.BlockSpec((width, seq), lambda b, i: (row // width, b))
    col = lambda n: pl.BlockSpec((n, 1), lambda b, i: (0, 0))
    return pl.pallas_call(
        functools.partial(_fox_kernel, seq=seq, tq=tq),
        grid=(batch, nq),
        in_specs=[blk(ROW_FQ), blk(ROW_FK), blk(ROW_FV),
                  pl.BlockSpec((N_SMALL, seq), lambda b, i: (0, b)),
                  col(N_SMALL), col(HEAD_DIM), col(HEAD_DIM)],
        out_specs=pl.BlockSpec((width, tq), lambda b, i: (0, b * nq + i)),
        out_shape=jax.ShapeDtypeStruct((width, t), BF16),
        scratch_shapes=[pltpu.VMEM((FOX_HEADS, 2 * HEAD_DIM, seq), BF16),
                        pltpu.VMEM((FOX_HEADS, seq, 2 * HEAD_DIM), BF16),
                        pltpu.VMEM((2, FOX_HEADS, tq, tq), F32),
                        pltpu.VMEM((FOX_HEADS, tq, tq), BF16),
                        pltpu.VMEM((FOX_HEADS, HEAD_DIM, tq), F32),
                        pltpu.VMEM((FOX_HEADS, 1, tq), F32),
                        pltpu.VMEM((FOX_HEADS, 1, tq), F32),
                        pltpu.VMEM((FOX_HEADS, 1, tq), F32)],
        compiler_params=pltpu.CompilerParams(
            dimension_semantics=("parallel", "arbitrary"), vmem_limit_bytes=VMEM_LIMIT),
        name="fox_attention",
    )(feat, feat, feat, small, f_bias_col, q_gain_col, k_gain_col)


def _conv_kernel(u_ref, b_ref, c_ref, w_ref, out_ref):
    z = c_ref[...].astype(F32) * u_ref[...].astype(F32)
    row = lax.broadcasted_iota(jnp.int32, z.shape, 0)
    z1 = jnp.where(row >= 1, pltpu.roll(z, 1, 0), 0.0)
    z2 = jnp.where(row >= 2, pltpu.roll(z, 2, 0), 0.0)
    w = w_ref[...]
    y = w[0:1, :] * z2 + w[1:2, :] * z1 + w[2:3, :] * z
    out_ref[...] = (b_ref[...].astype(F32) * y).astype(BF16)


def _conv(tok, conv_w, batch, seq):
    width = conv_w.shape[1]
    blk = lambda c: pl.BlockSpec((seq, width), lambda b: (b, c // width))
    return pl.pallas_call(
        _conv_kernel,
        grid=(batch,),
        in_specs=[blk(COL_CU), blk(COL_CB), blk(COL_CC),
                  pl.BlockSpec(conv_w.shape, lambda b: (0, 0))],
        out_specs=pl.BlockSpec((seq, width), lambda b: (b, 0)),
        out_shape=jax.ShapeDtypeStruct((batch * seq, width), BF16),
        compiler_params=pltpu.CompilerParams(
            dimension_semantics=("parallel",), vmem_limit_bytes=VMEM_LIMIT),
        name="short_conv",
    )(tok, tok, tok, conv_w)


def _mlstm_kernel(q_ref, v_ref, o_ref, k_ref, small_ref, bias_ref, gain_ref, out_ref,
                  c_ref, m_ref, b_ref, u_ref, ut_ref, *, chunk):
    @pl.when(pl.program_id(1) == 0)
    def _():
        c_ref[...] = jnp.zeros_like(c_ref)
        m_ref[...] = jnp.zeros_like(m_ref)
        g = small_ref[...] + bias_ref[...]
        b_all = _cumsum_lanes(_log_sigmoid(g), segment=chunk)
        b_ref[...] = b_all
        u_all = _row_select([g[4 + h:5 + h, :] - b_all[8 + h:9 + h, :] for h in range(MLSTM_HEADS)],
                            8, g.shape[1])
        u_ref[...] = u_all
        ut_ref[...] = jnp.concatenate([u_all, jnp.zeros((120, g.shape[1]), F32)], axis=0).T

    off = pl.multiple_of(pl.program_id(1) * chunk, chunk)
    bcum = b_ref[:, pl.ds(off, chunk)]
    rows = [u_ref[h:h + 1, pl.ds(off, chunk)] for h in range(MLSTM_HEADS)]
    u_cols = ut_ref[pl.ds(off, chunk), :]

    src = lax.broadcasted_iota(jnp.int32, (chunk, chunk), 0)
    tgt = lax.broadcasted_iota(jnp.int32, (chunk, chunk), 1)
    causal = src <= tgt
    ones_rows = (lax.broadcasted_iota(jnp.int32, (HEAD_DIM, chunk), 0) == 0).astype(BF16)

    heads = range(MLSTM_HEADS)
    sls = [slice(h * HEAD_DIM, (h + 1) * HEAD_DIM) for h in heads]
    qs = [(q_ref[sl, :].astype(F32) * QK_SCALE).astype(BF16) for sl in sls]
    ks = [k_ref[:, sl] for sl in sls]
    c_prev = [c_ref[h] for h in heads]
    scores = [_dot(ks[h], qs[h]) for h in heads]
    carried = [_dot(c_prev[h].astype(BF16), qs[h]) for h in heads]

    sw, kw, m_t, inter, decay = [], [], [], [], []
    for h in heads:
        b_row = bcum[8 + h:9 + h, :]
        b_last = b_row[:, chunk - 1:chunk]
        u_col = u_cols[:, h:h + 1]
        m_prev = m_ref[h][0:1, 0:1]
        dmat = jnp.where(causal, b_row + u_col, NEG)
        g_row = b_row + m_prev
        m_h = jnp.maximum(g_row, jnp.max(dmat, axis=0, keepdims=True))
        sw.append((scores[h] * jnp.exp(dmat - m_h)).astype(BF16))
        m_t.append(m_h)
        inter.append(jnp.exp(g_row - m_h))
        m_loc = jnp.max(rows[h] + b_last, axis=1, keepdims=True)
        m_new = jnp.maximum(b_last + m_prev, m_loc)
        decay.append(jnp.exp(b_last + m_prev - m_new))
        w_col = jnp.exp(u_col + b_last - m_new)
        kw.append((ks[h].astype(F32) * w_col).astype(BF16))
        m_ref[h] = jnp.broadcast_to(m_new, m_ref.shape[1:])

    v_aug = [jnp.concatenate([v_ref[sl, :], ones_rows], axis=0) for sl in sls]
    intra = [_dot(v_aug[h], sw[h]) for h in heads]
    update = [_dot(v_aug[h], kw[h]) for h in heads]
    for h in heads:
        c_ref[h] = decay[h] * c_prev[h] + update[h]
        tot = intra[h] + inter[h] * carried[h]
        num = tot[0:HEAD_DIM]
        den = tot[HEAD_DIM:HEAD_DIM + 1]
        ht = num / jnp.maximum(jnp.abs(den), jnp.exp(-m_t[h]))
        hn = _rms_rows(ht, gain_ref[sls[h], :])
        out_ref[sls[h], :] = (_sigmoid(o_ref[sls[h], :].astype(F32)) * hn).astype(BF16)


def _mlstm(feat, tok, small, bias_col, gain_col, batch, seq):
    chunk = min(L_MLSTM, seq)
    nc = seq // chunk
    t = batch * seq
    width = MLSTM_HEADS * HEAD_DIM
    blk = lambda row: pl.BlockSpec((width, chunk), lambda b, c: (row // width, b * nc + c))
    return pl.pallas_call(
        functools.partial(_mlstm_kernel, chunk=chunk),
        grid=(batch, nc),
        in_specs=[blk(ROW_MQ), blk(ROW_MV), blk(ROW_MO),
                  pl.BlockSpec((chunk, width), lambda b, c: (b * nc + c, COL_MK // width)),
                  pl.BlockSpec((N_SMALL, seq), lambda b, c: (0, b)),
                  pl.BlockSpec((N_SMALL, 1), lambda b, c: (0, 0)),
                  pl.BlockSpec((width, 1), lambda b, c: (0, 0))],
        out_specs=pl.BlockSpec((width, chunk), lambda b, c: (0, b * nc + c)),
        out_shape=jax.ShapeDtypeStruct((width, t), BF16),
        scratch_shapes=[pltpu.VMEM((MLSTM_HEADS, 2 * HEAD_DIM, HEAD_DIM), F32),
                        pltpu.VMEM((MLSTM_HEADS, 8, 128), F32),
                        pltpu.VMEM((N_SMALL, seq), F32),
                        pltpu.VMEM((8, seq), F32),
                        pltpu.VMEM((seq, 128), F32)],
        compiler_params=pltpu.CompilerParams(
            dimension_semantics=("parallel", "arbitrary"), vmem_limit_bytes=VMEM_LIMIT),
        name="mlstm",
    )(feat, feat, feat, tok, small, bias_col, gain_col)


def _swa_kernel(q_ref, kc_ref, kp_ref, vc_ref, vp_ref, bias_ref, sink_ref, gq_ref, gk_ref, out_ref, *, nblk):
    w = WINDOW
    key = lax.broadcasted_iota(jnp.int32, (2 * w, SWA_GROUP * w), 0)
    first_block_pad = jnp.logical_and(pl.program_id(1) == 0, key < w)
    kvs = range(SWA_KV_HEADS)
    ksl = [slice(kv * HEAD_DIM, (kv + 1) * HEAD_DIM) for kv in kvs]
    kt = [jnp.concatenate([_rms_rows(kp_ref[s, :].astype(F32), gk_ref[...]),
                           _rms_rows(kc_ref[s, :].astype(F32), gk_ref[...])], axis=1).astype(BF16) for s in ksl]
    vt = [jnp.concatenate([vp_ref[s, :], vc_ref[s, :]], axis=1) for s in ksl]

    bands = [(blk, kv) for blk in range(nblk) for kv in kvs]
    scores = []
    for blk, kv in bands:
        qs = []
        for g in range(SWA_GROUP):
            hs = slice((kv * SWA_GROUP + g) * HEAD_DIM, (kv * SWA_GROUP + g + 1) * HEAD_DIM)
            q = q_ref[hs, blk * w:(blk + 1) * w].astype(F32)
            qs.append((_rms_rows(q, gq_ref[...]) * QK_SCALE).astype(BF16))
        qt = jnp.concatenate(qs, axis=1)
        s = _dot_tn(kt[kv][:, blk * w:(blk + 2) * w], qt) + bias_ref[kv]
        scores.append(jnp.where(first_block_pad, NEG, s) if blk == 0 else s)
    probs, denoms = [], []
    for (blk, kv), s in zip(bands, scores):
        sink = sink_ref[kv]
        m = jnp.maximum(jnp.max(s, axis=0, keepdims=True), sink)
        p = jnp.exp(s - m)
        denoms.append(jnp.sum(p, axis=0, keepdims=True) + jnp.exp(sink - m))
        probs.append(p.astype(BF16))
    outs = [_dot(vt[kv][:, blk * w:(blk + 2) * w], p) for (blk, kv), p in zip(bands, probs)]
    for (blk, kv), o, d in zip(bands, outs, denoms):
        o = o / d
        for g in range(SWA_GROUP):
            hs = slice((kv * SWA_GROUP + g) * HEAD_DIM, (kv * SWA_GROUP + g + 1) * HEAD_DIM)
            out_ref[hs, blk * w:(blk + 1) * w] = o[:, g * w:(g + 1) * w].astype(BF16)


def _swa(feat, bias_t, sink_rows, q_gain_col, k_gain_col, batch, seq):
    w = WINDOW
    nblk = SWA_BLOCKS
    span = nblk * w
    ns = seq // span
    t = batch * seq
    qw = SWA_Q_HEADS * HEAD_DIM
    kw = SWA_KV_HEADS * HEAD_DIM
    cur = lambda row: pl.BlockSpec((kw, span), lambda b, n: (row // kw, b * ns + n))
    prev = lambda row: pl.BlockSpec(
        (kw, w), lambda b, n: (row // kw, (b * ns + n) * nblk - jnp.minimum(n, 1)))
    return pl.pallas_call(
        functools.partial(_swa_kernel, nblk=nblk),
        grid=(batch, ns),
        in_specs=[pl.BlockSpec((qw, span), lambda b, n: (ROW_SQ // qw, b * ns + n)),
                  cur(ROW_SK), prev(ROW_SK), cur(ROW_SV), prev(ROW_SV),
                  pl.BlockSpec(bias_t.shape, lambda b, n: (0, 0, 0)),
                  pl.BlockSpec(sink_rows.shape, lambda b, n: (0, 0, 0)),
                  pl.BlockSpec((HEAD_DIM, 1), lambda b, n: (0, 0)),
                  pl.BlockSpec((HEAD_DIM, 1), lambda b, n: (0, 0))],
        out_specs=pl.BlockSpec((qw, span), lambda b, n: (0, b * ns + n)),
        out_shape=jax.ShapeDtypeStruct((qw, t), BF16),
        compiler_params=pltpu.CompilerParams(
            dimension_semantics=("parallel", "arbitrary"), vmem_limit_bytes=VMEM_LIMIT),
        name="swa_attention",
    )(feat, feat, feat, feat, feat, bias_t, sink_rows, q_gain_col, k_gain_col)


def _merge_kernel(x_ref, yf_ref, yc_ref, ym_ref, ys_ref, g0_ref, g1_ref, g2_ref, g3_ref,
                  wf_ref, wc_ref, wm_ref, ws_ref, wo_ref, out_ref):
    merged = _sigmoid(g0_ref[...].astype(F32)) * _dot_tn(yf_ref[...], wf_ref[...])
    merged += _sigmoid(g1_ref[...].astype(F32)) * _dot(yc_ref[...], wc_ref[...])
    merged += _sigmoid(g2_ref[...].astype(F32)) * _dot_tn(ym_ref[...], wm_ref[...])
    merged += _sigmoid(g3_ref[...].astype(F32)) * _dot_tn(ys_ref[...], ws_ref[...])
    out_ref[...] = x_ref[...] + _dot(merged.astype(BF16), wo_ref[...])


def _merge(x2, y_fox_t, y_conv, y_mlstm_t, y_swa_t, tok, w_fox, w_conv, w_mlstm, w_swa, w_out):
    t = x2.shape[0]
    tm = min(TM_MERGE, t)
    feat_blk = lambda a: pl.BlockSpec((a.shape[0], tm), lambda i: (0, i))
    full = lambda a: pl.BlockSpec(a.shape, lambda i: (0, 0))
    gate = lambda k: pl.BlockSpec((tm, D_MODEL), lambda i: (i, COL_GATES // D_MODEL + k))
    return pl.pallas_call(
        _merge_kernel,
        grid=(t // tm,),
        in_specs=[pl.BlockSpec((tm, D_MODEL), lambda i: (i, 0)),
                  feat_blk(y_fox_t),
                  pl.BlockSpec((tm, y_conv.shape[1]), lambda i: (i, 0)),
                  feat_blk(y_mlstm_t), feat_blk(y_swa_t),
                  gate(0), gate(1), gate(2), gate(3),
                  full(w_fox), full(w_conv), full(w_mlstm), full(w_swa), full(w_out)],
        out_specs=pl.BlockSpec((tm, D_MODEL), lambda i: (i, 0)),
        out_shape=jax.ShapeDtypeStruct((t, D_MODEL), F32),
        compiler_params=pltpu.CompilerParams(
            dimension_semantics=("parallel",), vmem_limit_bytes=VMEM_LIMIT),
        name="gated_merge",
    )(x2, y_fox_t, y_conv, y_mlstm_t, y_swa_t, tok, tok, tok, tok,
      w_fox, w_conv, w_mlstm, w_swa, w_out)


def _ffn_kernel(x_ref, g_ref, wg_ref, wu_ref, wd_ref, out_ref, hn_ref, acc_ref):
    j = pl.program_id(1)

    @pl.when(j == 0)
    def _():
        x = x_ref[...]
        ms = jnp.mean(x * x, axis=-1, keepdims=True)
        hn_ref[...] = (x * lax.rsqrt(ms + EPS) * g_ref[...]).astype(BF16)
        acc_ref[...] = x

    hn = hn_ref[...]
    gate = _dot(hn, wg_ref[...])
    act = (gate * _sigmoid(gate) * _dot(hn, wu_ref[...])).astype(BF16)
    acc_ref[...] += _dot(act, wd_ref[...])

    @pl.when(j == pl.num_programs(1) - 1)
    def _():
        out_ref[...] = acc_ref[...]


def _ffn(x2, gain, w_gate, w_up, w_down):
    t = x2.shape[0]
    tm = min(TM_FFN, t)
    nf = D_FF // TF_FFN
    return pl.pallas_call(
        _ffn_kernel,
        grid=(t // tm, nf),
        in_specs=[pl.BlockSpec((tm, D_MODEL), lambda i, j: (i, 0)),
                  pl.BlockSpec((1, D_MODEL), lambda i, j: (0, 0)),
                  pl.BlockSpec((D_MODEL, TF_FFN), lambda i, j: (0, j)),
                  pl.BlockSpec((D_MODEL, TF_FFN), lambda i, j: (0, j)),
                  pl.BlockSpec((TF_FFN, D_MODEL), lambda i, j: (j, 0))],
        out_specs=pl.BlockSpec((tm, D_MODEL), lambda i, j: (i, 0)),
        out_shape=jax.ShapeDtypeStruct((t, D_MODEL), F32),
        scratch_shapes=[pltpu.VMEM((tm, D_MODEL), BF16), pltpu.VMEM((tm, D_MODEL), F32)],
        compiler_params=pltpu.CompilerParams(
            dimension_semantics=("parallel", "arbitrary"), vmem_limit_bytes=VMEM_LIMIT),
        name="swiglu_ffn",
    )(x2, gain, w_gate, w_up, w_down)


def _split_w_in(w):
    sizes = (256, 256, 256, 4, 256, 256, 256, 256, 256, 256, 4, 4, 256, 512, 128, 128, 4096)
    parts, off = [], 0
    for s in sizes:
        parts.append(w[:, off:off + s])
        off += s
    (fq, fk, fv, ff, cu, cb, cc, mq, mk, mv, mi, mf, mo, sq, sk, sv, gates) = parts
    w_tok = jnp.concatenate([cu, cb, cc, mk, gates], axis=1).astype(BF16)
    pad = jnp.zeros((w.shape[0], N_SMALL - 12), w.dtype)
    w_feat_t = jnp.concatenate([fq, fk, fv, mq, mv, mo, sq, sk, sv, ff, mi, mf, pad], axis=1).T.astype(BF16)
    return w_tok, w_feat_t


def _col(v, n=None):
    v = v.astype(F32).reshape(-1, 1)
    if n is not None and v.shape[0] < n:
        v = jnp.concatenate([v, jnp.zeros((n - v.shape[0], 1), F32)], axis=0)
    return v


def kernel(x, rel_bias, attn_norm, w_in, fox_f_bias, fox_q_gain, fox_k_gain, conv_w, mlstm_i_bias, mlstm_f_bias, mlstm_h_gain, swa_q_gain, swa_k_gain, swa_sinks, w_fox_out, w_conv_out, w_mlstm_out, w_swa_out, w_merge_out, ffn_norm, w_gate, w_up, w_down):
    batch, seq, _ = x.shape
    depth = w_in.shape[0]
    x2 = x.reshape(batch * seq, D_MODEL)
    bias_t = _swa_bias_table(rel_bias)

    for l in range(depth):
        w_tok, w_feat_t = _split_w_in(w_in[l])
        tok, feat, small = _inproj(x2, attn_norm[l].reshape(1, -1), w_tok, w_feat_t)

        y_fox_t = _fox(feat, small, _col(fox_f_bias[l], N_SMALL), _col(fox_q_gain[l]), _col(fox_k_gain[l]),
                       batch, seq)
        y_conv = _conv(tok, conv_w[l], batch, seq)
        gate_bias = jnp.concatenate([jnp.zeros((4,), F32), mlstm_i_bias[l], mlstm_f_bias[l]])
        y_mlstm_t = _mlstm(feat, tok, small, _col(gate_bias, N_SMALL), _col(mlstm_h_gain[l]), batch, seq)
        sink_rows = jnp.broadcast_to(
            swa_sinks[l].astype(F32).reshape(SWA_KV_HEADS, 1, SWA_GROUP, 1),
            (SWA_KV_HEADS, 1, SWA_GROUP, WINDOW)).reshape(SWA_KV_HEADS, 1, SWA_GROUP * WINDOW)
        y_swa_t = _swa(feat, bias_t, sink_rows, _col(swa_q_gain[l]), _col(swa_k_gain[l]), batch, seq)

        x2 = _merge(x2, y_fox_t, y_conv, y_mlstm_t, y_swa_t, tok,
                    w_fox_out[l].astype(BF16), w_conv_out[l].astype(BF16), w_mlstm_out[l].astype(BF16),
                    w_swa_out[l].astype(BF16), w_merge_out[l].astype(BF16))
        x2 = _ffn(x2, ffn_norm[l].reshape(1, -1), w_gate[l].astype(BF16), w_up[l].astype(BF16),
                  w_down[l].astype(BF16))
    return x2.reshape(batch, seq, D_MODEL)
```

```python
import functools

import numpy as np
import jax
import jax.numpy as jnp
from jax import lax
from jax.experimental import pallas as pl
from jax.experimental.pallas import tpu as pltpu

F32 = jnp.float32
BF16 = jnp.bfloat16

D_MODEL = 1024
HEAD_DIM = 64
FOX_HEADS = 4
MLSTM_HEADS = 4
SWA_Q_HEADS = 8
SWA_KV_HEADS = 2
SWA_GROUP = SWA_Q_HEADS // SWA_KV_HEADS
WINDOW = 128
REL_BUCKETS = 32
REL_MAX_DIST = 128
D_FF = 2816
EPS = 1e-6
NEG = -1e30
QK_SCALE = HEAD_DIM ** -0.5
LOG2E = 1.4426950408889634

ROW_FQ, ROW_FK, ROW_FV = 0, 256, 512
ROW_MQ, ROW_MV, ROW_MO = 768, 1024, 1280
ROW_SQ, ROW_SK, ROW_SV = 1536, 2048, 2176
N_FEAT = 2304
N_SMALL = 16
COL_CU, COL_CB, COL_CC, COL_MK, COL_GATES = 0, 256, 512, 768, 1024
N_TOK = 5120

TM_IN = 512
TN_IN = 512
FEAT_CHUNK = 576
TQ_FOX = 256
L_MLSTM = 256
SWA_BLOCKS = 2
TM_MERGE = 512
TM_FFN = 512
SUB_FFN = 256
TF_FFN = 512
VMEM_LIMIT = 56 * 1024 * 1024


def _dot(a, b):
    return jnp.dot(a, b, preferred_element_type=F32)


def _dot_nt(a, b):
    return lax.dot_general(a, b, (((1,), (1,)), ((), ())), preferred_element_type=F32)


def _dot_tn(a, b):
    return lax.dot_general(a, b, (((0,), (0,)), ((), ())), preferred_element_type=F32)


def _sigmoid(x):
    return 0.5 * jnp.tanh(0.5 * x) + 0.5


def _log_sigmoid(x):
    return jnp.minimum(x, 0.0) - jnp.log(1.0 + jnp.exp(-jnp.abs(x)))


def _cumsum_lanes(x, segment=None):
    n = segment or x.shape[-1]
    lane = lax.broadcasted_iota(jnp.int32, x.shape, x.ndim - 1) & (n - 1)
    k = 1
    while k < n:
        x = x + jnp.where(lane >= k, pltpu.roll(x, k, x.ndim - 1), 0.0)
        k *= 2
    return x


def _row_select(rows, n_rows, width):
    rid = lax.broadcasted_iota(jnp.int32, (n_rows, width), 0)
    out = jnp.zeros((n_rows, width), F32)
    for r, v in enumerate(rows):
        out = jnp.where(rid == r, v, out)
    return out


def _split3(c):
    hi = c.astype(BF16).astype(F32)
    r = c - hi
    mid = r.astype(BF16).astype(F32)
    lo = (r - mid).astype(BF16).astype(F32)
    return hi, mid, lo


def _rms_rows(xt, gain_col):
    ms = jnp.mean(xt * xt, axis=0, keepdims=True)
    return xt * lax.rsqrt(ms + EPS) * gain_col


def _bias_kernel(rb_ref, idx_ref, out_ref):
    kv = pl.program_id(0)
    g = pl.program_id(1)
    head = kv * SWA_GROUP + g
    idx = idx_ref[...]
    acc = jnp.full(idx.shape, NEG, F32)
    for b in range(REL_BUCKETS):
        acc = jnp.where(idx == b, rb_ref[b * SWA_Q_HEADS + head], acc)
    out_ref[0] = acc


def _bucket_table():
    j = np.arange(2 * WINDOW)[:, None]
    i = np.arange(WINDOW)[None, :]
    dist = i + WINDOW - j
    n = np.maximum(dist, 0).astype(np.int32)
    max_exact = REL_BUCKETS // 2
    nf = np.maximum(n, 1).astype(np.float32)
    large = max_exact + (np.log(nf / np.float32(max_exact)) / np.float32(np.log(REL_MAX_DIST / max_exact))
                         * np.float32(REL_BUCKETS - max_exact)).astype(np.int32)
    large = np.minimum(large, REL_BUCKETS - 1)
    bucket = np.where(n < max_exact, n, large)
    return np.where((dist >= 0) & (dist < WINDOW), bucket, -1).astype(np.int32)


def _swa_bias_table(rel_bias):
    idx = jnp.asarray(_bucket_table())
    return pl.pallas_call(
        _bias_kernel,
        grid=(SWA_KV_HEADS, SWA_GROUP),
        in_specs=[
            pl.BlockSpec(memory_space=pltpu.SMEM),
            pl.BlockSpec((2 * WINDOW, WINDOW), lambda kv, g: (0, 0)),
        ],
        out_specs=pl.BlockSpec((1, 2 * WINDOW, WINDOW), lambda kv, g: (kv, 0, g)),
        out_shape=jax.ShapeDtypeStruct((SWA_KV_HEADS, 2 * WINDOW, SWA_GROUP * WINDOW), F32),
        name="swa_bias_table",
    )(rel_bias.reshape(-1), idx)


def _inproj_kernel(x_ref, g_ref, wt_ref, wf_ref, tok_ref, feat_ref, small_ref):
    x = x_ref[...]
    ms = jnp.mean(x * x, axis=-1, keepdims=True)
    xn = (x * lax.rsqrt(ms + EPS) * g_ref[...]).astype(BF16)
    for c in range(N_TOK // TN_IN):
        cols = slice(c * TN_IN, (c + 1) * TN_IN)
        tok_ref[:, cols] = _dot(xn, wt_ref[:, cols]).astype(BF16)
    n_chunks = N_FEAT // FEAT_CHUNK
    for c in range(n_chunks):
        lo = c * FEAT_CHUNK
        hi = lo + FEAT_CHUNK + (N_SMALL if c == n_chunks - 1 else 0)
        r = _dot_nt(wf_ref[lo:hi, :], xn)
        feat_ref[lo:lo + FEAT_CHUNK, :] = r[:FEAT_CHUNK].astype(BF16)
        if c == n_chunks - 1:
            small_ref[...] = r[FEAT_CHUNK:]


def _resident(shape, layer=None):
    if layer is None:
        return pl.BlockSpec(shape, lambda *_: (0,) * len(shape), pipeline_mode=pl.Buffered(1))
    return pl.BlockSpec((None,) + tuple(shape), lambda *_: (layer,) + (0,) * len(shape),
                        pipeline_mode=pl.Buffered(1))


def _inproj(x2, gain, w_tok, w_feat_t, layer):
    t = x2.shape[0]
    tm = min(TM_IN, t)
    return pl.pallas_call(
        _inproj_kernel,
        grid=(t // tm,),
        in_specs=[
            pl.BlockSpec((tm, D_MODEL), lambda i: (i, 0)),
            _resident((1, D_MODEL), layer),
            _resident((D_MODEL, N_TOK), layer),
            _resident((N_FEAT + N_SMALL, D_MODEL), layer),
        ],
        out_specs=[
            pl.BlockSpec((tm, N_TOK), lambda i: (i, 0)),
            pl.BlockSpec((N_FEAT, tm), lambda i: (0, i)),
            pl.BlockSpec((N_SMALL, tm), lambda i: (0, i)),
        ],
        out_shape=[
            jax.ShapeDtypeStruct((t, N_TOK), BF16),
            jax.ShapeDtypeStruct((N_FEAT, t), BF16),
            jax.ShapeDtypeStruct((N_SMALL, t), F32),
        ],
        compiler_params=pltpu.CompilerParams(
            dimension_semantics=("parallel",), vmem_limit_bytes=VMEM_LIMIT),
        name="inproj",
    )(x2, gain, w_tok, w_feat_t)


def _fox_kernel(fq_ref, fk_ref, fv_ref, small_ref, fb_ref, gq_ref, gk_ref, out_ref,
                qa_ref, ka_ref, s_ref, p_ref, acc_ref, m_ref, l_ref, a_ref, *, seq, tq):
    i = pl.program_id(1)
    heads = range(FOX_HEADS)

    @pl.when(i == 0)
    def _():
        c_all = _cumsum_lanes(_log_sigmoid(small_ref[...] + fb_ref[...]))
        for h in heads:
            sl = slice(h * HEAD_DIM, (h + 1) * HEAD_DIM)
            c = c_all[h:h + 1, :] * LOG2E
            hi, mid, lo = _split3(c)
            one = jnp.ones_like(c)
            qn = _rms_rows(fq_ref[sl, :].astype(F32), gq_ref[...]) * (QK_SCALE * LOG2E)
            qa_ref[h, 0:HEAD_DIM, :] = qn.astype(BF16)
            qa_ref[h, HEAD_DIM:, :] = _row_select([hi, mid, lo, one, one, one], HEAD_DIM, seq).astype(BF16)
            kn = _rms_rows(fk_ref[sl, :].astype(F32), gk_ref[...])
            k_aug = _row_select([one, one, one, -hi, -mid, -lo], HEAD_DIM, seq)
            ka_ref[h] = jnp.concatenate([kn, k_aug], axis=0).T.astype(BF16)

    q_off = pl.multiple_of(i * tq, tq)
    qa = [qa_ref[h, :, pl.ds(q_off, tq)] for h in heads]

    m_ref[...] = jnp.full(m_ref.shape, NEG, F32)
    l_ref[...] = jnp.zeros_like(l_ref)
    acc_ref[...] = jnp.zeros_like(acc_ref)
    a_ref[...] = jnp.ones_like(a_ref)
    p_ref[...] = jnp.zeros_like(p_ref)

    def qk(j, slot):
        off = pl.multiple_of(j * tq, tq)
        for h in heads:
            s_ref[slot, h] = _dot(ka_ref[h, pl.ds(off, tq), :], qa[h])

    def pv(j):
        off = pl.multiple_of(j * tq, tq)
        for h in heads:
            v = fv_ref[h * HEAD_DIM:(h + 1) * HEAD_DIM, pl.ds(off, tq)]
            acc_ref[h] = a_ref[h] * acc_ref[h] + _dot(v, p_ref[h])

    def softmax(slot, masked):
        for h in heads:
            s = s_ref[slot, h]
            if masked:
                key = lax.broadcasted_iota(jnp.int32, s.shape, 0)
                qry = lax.broadcasted_iota(jnp.int32, s.shape, 1)
                s = jnp.where(key <= qry, s, NEG)
            m = m_ref[h]
            m_new = jnp.maximum(m, jnp.max(s, axis=0, keepdims=True))
            p = jnp.exp2(s - m_new)
            alpha = jnp.exp2(m - m_new)
            l_ref[h] = alpha * l_ref[h] + jnp.sum(p, axis=0, keepdims=True)
            m_ref[h] = m_new
            a_ref[h] = alpha
            p_ref[h] = p.astype(BF16)

    def stage(j, slot):
        qk(j + 1, 1 - slot)
        pv(jnp.maximum(j - 1, 0))
        softmax(slot, False)

    qk(0, 0)

    @pl.loop(0, i // 2)
    def _(k):
        stage(2 * k, 0)
        stage(2 * k + 1, 1)

    odd = lax.rem(i, 2) == 1

    @pl.when(odd)
    def _():
        stage(i - 1, 0)
        pv(i - 1)
        softmax(1, True)

    @pl.when(jnp.logical_not(odd))
    def _():
        pv(jnp.maximum(i - 1, 0))
        softmax(0, True)

    pv(i)
    for h in heads:
        out_ref[h * HEAD_DIM:(h + 1) * HEAD_DIM, :] = (acc_ref[h] / l_ref[h]).astype(BF16)


def _fox(feat, small, f_bias_col, q_gain_col, k_gain_col, batch, seq):
    tq = min(TQ_FOX, seq)
    nq = seq // tq
    t = batch * seq
    width = FOX_HEADS * HEAD_DIM
    blk = lambda row: pl.BlockSpec((width, seq), lambda b, i: (row // width, b))
    col = lambda n: pl.BlockSpec((n, 1), lambda b, i: (0, 0))
    return pl.pallas_call(
        functools.partial(_fox_kernel, seq=seq, tq=tq),
        grid=(batch, nq),
        in_specs=[blk(ROW_FQ), blk(ROW_FK), blk(ROW_FV),
                  pl.BlockSpec((N_SMALL, seq), lambda b, i: (0, b)),
                  col(N_SMALL), col(HEAD_DIM), col(HEAD_DIM)],
        out_specs=pl.BlockSpec((width, tq), lambda b, i: (0, b * nq + i)),
        out_shape=jax.ShapeDtypeStruct((width, t), BF16),
        scratch_shapes=[pltpu.VMEM((FOX_HEADS, 2 * HEAD_DIM, seq), BF16),
                        pltpu.VMEM((FOX_HEADS, seq, 2 * HEAD_DIM), BF16),
                        pltpu.VMEM((2, FOX_HEADS, tq, tq), F32),
                        pltpu.VMEM((FOX_HEADS, tq, tq), BF16),
                        pltpu.VMEM((FOX_HEADS, HEAD_DIM, tq), F32),
                        pltpu.VMEM((FOX_HEADS, 1, tq), F32),
                        pltpu.VMEM((FOX_HEADS, 1, tq), F32),
                        pltpu.VMEM((FOX_HEADS, 1, tq), F32)],
        compiler_params=pltpu.CompilerParams(
            dimension_semantics=("parallel", "arbitrary"), vmem_limit_bytes=VMEM_LIMIT),
        name="fox_attention",
    )(feat, feat, feat, small, f_bias_col, q_gain_col, k_gain_col)


def _conv_kernel(u_ref, b_ref, c_ref, w_ref, out_ref):
    z = c_ref[...].astype(F32) * u_ref[...].astype(F32)
    row = lax.broadcasted_iota(jnp.int32, z.shape, 0)
    z1 = jnp.where(row >= 1, pltpu.roll(z, 1, 0), 0.0)
    z2 = jnp.where(row >= 2, pltpu.roll(z, 2, 0), 0.0)
    w = w_ref[...]
    y = w[0:1, :] * z2 + w[1:2, :] * z1 + w[2:3, :] * z
    out_ref[...] = (b_ref[...].astype(F32) * y).astype(BF16)


def _conv(tok, conv_w, batch, seq):
    width = conv_w.shape[1]
    blk = lambda c: pl.BlockSpec((seq, width), lambda b: (b, c // width))
    return pl.pallas_call(
        _conv_kernel,
        grid=(batch,),
        in_specs=[blk(COL_CU), blk(COL_CB), blk(COL_CC),
                  pl.BlockSpec(conv_w.shape, lambda b: (0, 0))],
        out_specs=pl.BlockSpec((seq, width), lambda b: (b, 0)),
        out_shape=jax.ShapeDtypeStruct((batch * seq, width), BF16),
        compiler_params=pltpu.CompilerParams(
            dimension_semantics=("parallel",), vmem_limit_bytes=VMEM_LIMIT),
        name="short_conv",
    )(tok, tok, tok, conv_w)


def _mlstm_kernel(q_ref, v_ref, o_ref, k_ref, small_ref, bias_ref, gain_ref, out_ref,
                  c_ref, m_ref, b_ref, u_ref, ut_ref, *, chunk):
    @pl.when(pl.program_id(1) == 0)
    def _():
        c_ref[...] = jnp.zeros_like(c_ref)
        m_ref[...] = jnp.zeros_like(m_ref)
        g = small_ref[...] + bias_ref[...]
        b_all = _cumsum_lanes(_log_sigmoid(g), segment=chunk)
        b_ref[...] = b_all
        u_all = _row_select([g[4 + h:5 + h, :] - b_all[8 + h:9 + h, :] for h in range(MLSTM_HEADS)],
                            8, g.shape[1])
        u_ref[...] = u_all
        ut_ref[...] = jnp.concatenate([u_all, jnp.zeros((120, g.shape[1]), F32)], axis=0).T

    off = pl.multiple_of(pl.program_id(1) * chunk, chunk)
    bcum = b_ref[:, pl.ds(off, chunk)]
    rows = [u_ref[h:h + 1, pl.ds(off, chunk)] for h in range(MLSTM_HEADS)]
    u_cols = ut_ref[pl.ds(off, chunk), :]

    src = lax.broadcasted_iota(jnp.int32, (chunk, chunk), 0)
    tgt = lax.broadcasted_iota(jnp.int32, (chunk, chunk), 1)
    causal = src <= tgt
    ones_rows = (lax.broadcasted_iota(jnp.int32, (HEAD_DIM, chunk), 0) == 0).astype(BF16)

    heads = range(MLSTM_HEADS)
    sls = [slice(h * HEAD_DIM, (h + 1) * HEAD_DIM) for h in heads]
    qs = [(q_ref[sl, :].astype(F32) * QK_SCALE).astype(BF16) for sl in sls]
    ks = [k_ref[:, sl] for sl in sls]
    c_prev = [c_ref[h] for h in heads]
    scores = [_dot(ks[h], qs[h]) for h in heads]
    carried = [_dot(c_prev[h].astype(BF16), qs[h]) for h in heads]

    sw, kw, m_t, inter, decay = [], [], [], [], []
    for h in heads:
        b_row = bcum[8 + h:9 + h, :]
        b_last = b_row[:, chunk - 1:chunk]
        u_col = u_cols[:, h:h + 1]
        m_prev = m_ref[h][0:1, 0:1]
        dmat = jnp.where(causal, b_row + u_col, NEG)
        g_row = b_row + m_prev
        m_h = jnp.maximum(g_row, jnp.max(dmat, axis=0, keepdims=True))
        sw.append((scores[h] * jnp.exp(dmat - m_h)).astype(BF16))
        m_t.append(m_h)
        inter.append(jnp.exp(g_row - m_h))
        m_loc = jnp.max(rows[h] + b_last, axis=1, keepdims=True)
        m_new = jnp.maximum(b_last + m_prev, m_loc)
        decay.append(jnp.exp(b_last + m_prev - m_new))
        w_col = jnp.exp(u_col + b_last - m_new)
        kw.append((ks[h].astype(F32) * w_col).astype(BF16))
        m_ref[h] = jnp.broadcast_to(m_new, m_ref.shape[1:])

    v_aug = [jnp.concatenate([v_ref[sl, :], ones_rows], axis=0) for sl in sls]
    intra = [_dot(v_aug[h], sw[h]) for h in heads]
    update = [_dot(v_aug[h], kw[h]) for h in heads]
    for h in heads:
        c_ref[h] = decay[h] * c_prev[h] + update[h]
        tot = intra[h] + inter[h] * carried[h]
        num = tot[0:HEAD_DIM]
        den = tot[HEAD_DIM:HEAD_DIM + 1]
        ht = num / jnp.maximum(jnp.abs(den), jnp.exp(-m_t[h]))
        hn = _rms_rows(ht, gain_ref[sls[h], :])
        out_ref[sls[h], :] = (_sigmoid(o_ref[sls[h], :].astype(F32)) * hn).astype(BF16)


def _mlstm(feat, tok, small, bias_col, gain_col, batch, seq):
    chunk = min(L_MLSTM, seq)
    nc = seq // chunk
    t = batch * seq
    width = MLSTM_HEADS * HEAD_DIM
    blk = lambda row: pl.BlockSpec((width, chunk), lambda b, c: (row // width, b * nc + c))
    return pl.pallas_call(
        functools.partial(_mlstm_kernel, chunk=chunk),
        grid=(batch, nc),
        in_specs=[blk(ROW_MQ), blk(ROW_MV), blk(ROW_MO),
                  pl.BlockSpec((chunk, width), lambda b, c: (b * nc + c, COL_MK // width)),
                  pl.BlockSpec((N_SMALL, seq), lambda b, c: (0, b)),
                  pl.BlockSpec((N_SMALL, 1), lambda b, c: (0, 0)),
                  pl.BlockSpec((width, 1), lambda b, c: (0, 0))],
        out_specs=pl.BlockSpec((width, chunk), lambda b, c: (0, b * nc + c)),
        out_shape=jax.ShapeDtypeStruct((width, t), BF16),
        scratch_shapes=[pltpu.VMEM((MLSTM_HEADS, 2 * HEAD_DIM, HEAD_DIM), F32),
                        pltpu.VMEM((MLSTM_HEADS, 8, 128), F32),
                        pltpu.VMEM((N_SMALL, seq), F32),
                        pltpu.VMEM((8, seq), F32),
                        pltpu.VMEM((seq, 128), F32)],
        compiler_params=pltpu.CompilerParams(
            dimension_semantics=("parallel", "arbitrary"), vmem_limit_bytes=VMEM_LIMIT),
        name="mlstm",
    )(feat, feat, feat, tok, small, bias_col, gain_col)


def _swa_kernel(q_ref, kc_ref, kp_ref, vc_ref, vp_ref, bias_ref, sink_ref, gq_ref, gk_ref, out_ref, *, nblk):
    w = WINDOW
    key = lax.broadcasted_iota(jnp.int32, (2 * w, SWA_GROUP * w), 0)
    first_block_pad = jnp.logical_and(pl.program_id(1) == 0, key < w)
    kvs = range(SWA_KV_HEADS)
    ksl = [slice(kv * HEAD_DIM, (kv + 1) * HEAD_DIM) for kv in kvs]
    kt = [jnp.concatenate([_rms_rows(kp_ref[s, :].astype(F32), gk_ref[...]),
                           _rms_rows(kc_ref[s, :].astype(F32), gk_ref[...])], axis=1).astype(BF16) for s in ksl]
    vt = [jnp.concatenate([vp_ref[s, :], vc_ref[s, :]], axis=1) for s in ksl]

    bands = [(blk, kv) for blk in range(nblk) for kv in kvs]
    scores = []
    for blk, kv in bands:
        qs = []
        for g in range(SWA_GROUP):
            hs = slice((kv * SWA_GROUP + g) * HEAD_DIM, (kv * SWA_GROUP + g + 1) * HEAD_DIM)
            q = q_ref[hs, blk * w:(blk + 1) * w].astype(F32)
            qs.append((_rms_rows(q, gq_ref[...]) * QK_SCALE).astype(BF16))
        qt = jnp.concatenate(qs, axis=1)
        s = _dot_tn(kt[kv][:, blk * w:(blk + 2) * w], qt) + bias_ref[kv]
        scores.append(jnp.where(first_block_pad, NEG, s) if blk == 0 else s)
    probs, denoms = [], []
    for (blk, kv), s in zip(bands, scores):
        sink = sink_ref[kv]
        m = jnp.maximum(jnp.max(s, axis=0, keepdims=True), sink)
        p = jnp.exp(s - m)
        denoms.append(jnp.sum(p, axis=0, keepdims=True) + jnp.exp(sink - m))
        probs.append(p.astype(BF16))
    outs = [_dot(vt[kv][:, blk * w:(blk + 2) * w], p) for (blk, kv), p in zip(bands, probs)]
    for (blk, kv), o, d in zip(bands, outs, denoms):
        o = o / d
        for g in range(SWA_GROUP):
            hs = slice((kv * SWA_GROUP + g) * HEAD_DIM, (kv * SWA_GROUP + g + 1) * HEAD_DIM)
            out_ref[hs, blk * w:(blk + 1) * w] = o[:, g * w:(g + 1) * w].astype(BF16)


def _swa(feat, bias_t, sink_rows, q_gain_col, k_gain_col, batch, seq):
    w = WINDOW
    nblk = SWA_BLOCKS
    span = nblk * w
    ns = seq // span
    t = batch * seq
    qw = SWA_Q_HEADS * HEAD_DIM
    kw = SWA_KV_HEADS * HEAD_DIM
    cur = lambda row: pl.BlockSpec((kw, span), lambda b, n: (row // kw, b * ns + n))
    prev = lambda row: pl.BlockSpec(
        (kw, w), lambda b, n: (row // kw, (b * ns + n) * nblk - jnp.minimum(n, 1)))
    return pl.pallas_call(
        functools.partial(_swa_kernel, nblk=nblk),
        grid=(batch, ns),
        in_specs=[pl.BlockSpec((qw, span), lambda b, n: (ROW_SQ // qw, b * ns + n)),
                  cur(ROW_SK), prev(ROW_SK), cur(ROW_SV), prev(ROW_SV),
                  pl.BlockSpec(bias_t.shape, lambda b, n: (0, 0, 0)),
                  pl.BlockSpec(sink_rows.shape, lambda b, n: (0, 0, 0)),
                  pl.BlockSpec((HEAD_DIM, 1), lambda b, n: (0, 0)),
                  pl.BlockSpec((HEAD_DIM, 1), lambda b, n: (0, 0))],
        out_specs=pl.BlockSpec((qw, span), lambda b, n: (0, b * ns + n)),
        out_shape=jax.ShapeDtypeStruct((qw, t), BF16),
        compiler_params=pltpu.CompilerParams(
            dimension_semantics=("parallel", "arbitrary"), vmem_limit_bytes=VMEM_LIMIT),
        name="swa_attention",
    )(feat, feat, feat, feat, feat, bias_t, sink_rows, q_gain_col, k_gain_col)


def _merge_kernel(x_ref, yf_ref, yc_ref, ym_ref, ys_ref, g0_ref, g1_ref, g2_ref, g3_ref,
                  wf_ref, wc_ref, wm_ref, ws_ref, wo_ref, out_ref):
    def gated(g_ref, y):
        return (1.0 + jnp.tanh(g_ref[...].astype(F32))) * y

    merged = gated(g0_ref, _dot_tn(yf_ref[...], wf_ref[...]))
    merged += gated(g1_ref, _dot(yc_ref[...], wc_ref[...]))
    merged += gated(g2_ref, _dot_tn(ym_ref[...], wm_ref[...]))
    merged += gated(g3_ref, _dot_tn(ys_ref[...], ws_ref[...]))
    out_ref[...] = x_ref[...] + _dot(merged.astype(BF16), wo_ref[...])


def _merge(x2, y_fox_t, y_conv, y_mlstm_t, y_swa_t, tok, w_fox, w_conv, w_mlstm, w_swa, w_out, layer):
    t = x2.shape[0]
    tm = min(TM_MERGE, t)
    feat_blk = lambda a: pl.BlockSpec((a.shape[0], tm), lambda i: (0, i))
    full = lambda a: _resident(a.shape[1:], layer)
    gate = lambda k: pl.BlockSpec((tm, D_MODEL), lambda i: (i, COL_GATES // D_MODEL + k))
    return pl.pallas_call(
        _merge_kernel,
        grid=(t // tm,),
        in_specs=[pl.BlockSpec((tm, D_MODEL), lambda i: (i, 0)),
                  feat_blk(y_fox_t),
                  pl.BlockSpec((tm, y_conv.shape[1]), lambda i: (i, 0)),
                  feat_blk(y_mlstm_t), feat_blk(y_swa_t),
                  gate(0), gate(1), gate(2), gate(3),
                  full(w_fox), full(w_conv), full(w_mlstm), full(w_swa), full(w_out)],
        out_specs=pl.BlockSpec((tm, D_MODEL), lambda i: (i, 0)),
        out_shape=jax.ShapeDtypeStruct((t, D_MODEL), F32),
        compiler_params=pltpu.CompilerParams(
            dimension_semantics=("parallel",), vmem_limit_bytes=VMEM_LIMIT),
        name="gated_merge",
    )(x2, y_fox_t, y_conv, y_mlstm_t, y_swa_t, tok, tok, tok, tok,
      w_fox, w_conv, w_mlstm, w_swa, w_out)


def _ffn_kernel(x_ref, g_ref, wg_ref, wu_ref, wd_ref, out_ref, *, sub):
    bounds = list(range(0, D_FF, TF_FFN)) + [D_FF]
    for r in range(0, x_ref.shape[0], sub):
        x = x_ref[r:r + sub, :]
        ms = jnp.mean(x * x, axis=-1, keepdims=True)
        hn = (x * lax.rsqrt(ms + EPS) * g_ref[...]).astype(BF16)
        acc = x
        for lo, hi in zip(bounds[:-1], bounds[1:]):
            h = _dot(hn, wg_ref[:, lo:hi])
            act = (h * (1.0 + jnp.tanh(h)) * _dot(hn, wu_ref[:, lo:hi])).astype(BF16)
            acc = acc + _dot(act, wd_ref[lo:hi, :])
        out_ref[r:r + sub, :] = acc


def _ffn(x2, gain, w_gate_half, w_up, w_down, layer):
    t = x2.shape[0]
    tm = min(TM_FFN, t)
    return pl.pallas_call(
        functools.partial(_ffn_kernel, sub=min(SUB_FFN, tm)),
        grid=(t // tm,),
        in_specs=[pl.BlockSpec((tm, D_MODEL), lambda i: (i, 0)),
                  _resident((1, D_MODEL), layer),
                  _resident((D_MODEL, D_FF), layer),
                  _resident((D_MODEL, D_FF), layer),
                  _resident((D_FF, D_MODEL), layer)],
        out_specs=pl.BlockSpec((tm, D_MODEL), lambda i: (i, 0)),
        out_shape=jax.ShapeDtypeStruct((t, D_MODEL), F32),
        compiler_params=pltpu.CompilerParams(
            dimension_semantics=("parallel",), vmem_limit_bytes=VMEM_LIMIT),
        name="swiglu_ffn",
    )(x2, gain, w_gate_half, w_up, w_down)


IN_FQKV, IN_FF, IN_CONV, IN_MQ, IN_MK, IN_MV = 0, 768, 772, 1540, 1796, 2052
IN_MI, IN_MF, IN_MO, IN_SQ, IN_SK, IN_SV, IN_GATES, IN_COLS = 2308, 2312, 2316, 2572, 3084, 3212, 3340, 7436
TK_PREP = 256


def _wprep_kernel(w_ref, small_ref, tok_ref, feat_ref):
    tok_ref[:, COL_CU:COL_MK] = w_ref[:, IN_CONV:IN_MQ].astype(BF16)
    tok_ref[:, COL_MK:COL_GATES] = w_ref[:, IN_MK:IN_MV].astype(BF16)
    tok_ref[:, COL_GATES:] = (0.5 * w_ref[:, IN_GATES:IN_COLS]).astype(BF16)
    for row, lo, hi in ((ROW_FQ, IN_FQKV, IN_FF), (ROW_MQ, IN_MQ, IN_MK), (ROW_MV, IN_MV, IN_MI),
                        (ROW_MO, IN_MO, IN_SQ), (ROW_SQ, IN_SQ, IN_GATES)):
        feat_ref[row:row + hi - lo, :] = w_ref[:, lo:hi].T.astype(BF16)
    feat_ref[N_FEAT:, :] = small_ref[...].astype(BF16)


def _prep_w_in(w_in):
    depth, d, cols = w_in.shape
    small_t = jnp.concatenate(
        [w_in[:, :, IN_FF:IN_CONV], w_in[:, :, IN_MI:IN_MO], jnp.zeros((depth, d, N_SMALL - 12), w_in.dtype)],
        axis=2).transpose(0, 2, 1)
    return pl.pallas_call(
        _wprep_kernel,
        grid=(depth, d // TK_PREP),
        in_specs=[pl.BlockSpec((None, TK_PREP, cols), lambda l, j: (l, j, 0)),
                  pl.BlockSpec((None, N_SMALL, TK_PREP), lambda l, j: (l, 0, j))],
        out_specs=[pl.BlockSpec((None, TK_PREP, N_TOK), lambda l, j: (l, j, 0)),
                   pl.BlockSpec((None, N_FEAT + N_SMALL, TK_PREP), lambda l, j: (l, 0, j))],
        out_shape=[jax.ShapeDtypeStruct((depth, d, N_TOK), BF16),
                   jax.ShapeDtypeStruct((depth, N_FEAT + N_SMALL, d), BF16)],
        compiler_params=pltpu.CompilerParams(
            dimension_semantics=("parallel", "parallel"), vmem_limit_bytes=VMEM_LIMIT),
        name="w_in_relayout",
    )(w_in, small_t)


def _col(v, n=None):
    v = v.astype(F32).reshape(-1, 1)
    if n is not None and v.shape[0] < n:
        v = jnp.concatenate([v, jnp.zeros((n - v.shape[0], 1), F32)], axis=0)
    return v


def kernel(x, rel_bias, attn_norm, w_in, fox_f_bias, fox_q_gain, fox_k_gain, conv_w, mlstm_i_bias, mlstm_f_bias, mlstm_h_gain, swa_q_gain, swa_k_gain, swa_sinks, w_fox_out, w_conv_out, w_mlstm_out, w_swa_out, w_merge_out, ffn_norm, w_gate, w_up, w_down):
    batch, seq, _ = x.shape
    depth = w_in.shape[0]
    x2 = x.reshape(batch * seq, D_MODEL)
    bias_t = _swa_bias_table(rel_bias)

    w_tok, w_feat_t = _prep_w_in(w_in)
    attn_gain = attn_norm.reshape(depth, 1, D_MODEL)
    ffn_gain = ffn_norm.reshape(depth, 1, D_MODEL)
    w_fox_b, w_conv_b, w_mlstm_b, w_swa_b = (w.astype(BF16) for w in (w_fox_out, w_conv_out, w_mlstm_out, w_swa_out))
    w_merge_half = (0.5 * w_merge_out).astype(BF16)
    w_gate_half = (0.5 * w_gate).astype(BF16)
    w_up_b = w_up.astype(BF16)
    w_down_b = w_down.astype(BF16)

    for l in range(depth):
        tok, feat, small = _inproj(x2, attn_gain, w_tok, w_feat_t, l)

        y_fox_t = _fox(feat, small, _col(fox_f_bias[l], N_SMALL), _col(fox_q_gain[l]), _col(fox_k_gain[l]),
                       batch, seq)
        y_conv = _conv(tok, conv_w[l], batch, seq)
        gate_bias = jnp.concatenate([jnp.zeros((4,), F32), mlstm_i_bias[l], mlstm_f_bias[l]])
        y_mlstm_t = _mlstm(feat, tok, small, _col(gate_bias, N_SMALL), _col(mlstm_h_gain[l]), batch, seq)
        sink_rows = jnp.broadcast_to(
            swa_sinks[l].astype(F32).reshape(SWA_KV_HEADS, 1, SWA_GROUP, 1),
            (SWA_KV_HEADS, 1, SWA_GROUP, WINDOW)).reshape(SWA_KV_HEADS, 1, SWA_GROUP * WINDOW)
        y_swa_t = _swa(feat, bias_t, sink_rows, _col(swa_q_gain[l]), _col(swa_k_gain[l]), batch, seq)

        x2 = _merge(x2, y_fox_t, y_conv, y_mlstm_t, y_swa_t, tok,
                    w_fox_b, w_conv_b, w_mlstm_b, w_swa_b, w_merge_half, l)
        x2 = _ffn(x2, ffn_gain, w_gate_half, w_up_b, w_down_b, l)
    return x2.reshape(batch, seq, D_MODEL)
```

```python
import functools

import numpy as np
import jax
import jax.numpy as jnp
from jax import lax
from jax.experimental import pallas as pl
from jax.experimental.pallas import tpu as pltpu

F32 = jnp.float32
BF16 = jnp.bfloat16

D_MODEL = 1024
HEAD_DIM = 64
FOX_HEADS = 4
MLSTM_HEADS = 4
SWA_Q_HEADS = 8
SWA_KV_HEADS = 2
SWA_GROUP = SWA_Q_HEADS // SWA_KV_HEADS
WINDOW = 128
REL_BUCKETS = 32
REL_MAX_DIST = 128
D_FF = 2816
EPS = 1e-6
NEG = -1e30
QK_SCALE = HEAD_DIM ** -0.5
LOG2E = 1.4426950408889634

ROW_FQ, ROW_FK, ROW_FV = 0, 256, 512
ROW_MQ, ROW_MV, ROW_MO = 768, 1024, 1280
ROW_SQ, ROW_SK, ROW_SV = 1536, 2048, 2176
N_FEAT = 2304
N_SMALL = 16
COL_CU, COL_CB, COL_CC, COL_MK, COL_GATES = 0, 256, 512, 768, 1024
N_TOK = 5120

TM_IN = 512
TN_IN = 512
FEAT_CHUNK = 576
TQ_FOX = 256
L_MLSTM = 256
SWA_BLOCKS = 4
TM_MERGE = 512
TM_FFN = 512
SUB_FFN = 256
TF_FFN = 512
VMEM_LIMIT = 56 * 1024 * 1024


def _dot(a, b):
    return jnp.dot(a, b, preferred_element_type=F32)


def _dot_nt(a, b):
    return lax.dot_general(a, b, (((1,), (1,)), ((), ())), preferred_element_type=F32)


def _dot_tn(a, b):
    return lax.dot_general(a, b, (((0,), (0,)), ((), ())), preferred_element_type=F32)


def _sigmoid(x):
    return 0.5 * jnp.tanh(0.5 * x) + 0.5


def _log_sigmoid(x):
    return jnp.minimum(x, 0.0) - jnp.log(1.0 + jnp.exp(-jnp.abs(x)))


def _cumsum_lanes(x, segment=None):
    n = segment or x.shape[-1]
    lane = lax.broadcasted_iota(jnp.int32, x.shape, x.ndim - 1) & (n - 1)
    k = 1
    while k < n:
        x = x + jnp.where(lane >= k, pltpu.roll(x, k, x.ndim - 1), 0.0)
        k *= 2
    return x


def _row_select(rows, n_rows, width):
    rid = lax.broadcasted_iota(jnp.int32, (n_rows, width), 0)
    out = jnp.zeros((n_rows, width), F32)
    for r, v in enumerate(rows):
        out = jnp.where(rid == r, v, out)
    return out


def _split3(c):
    hi = c.astype(BF16).astype(F32)
    r = c - hi
    mid = r.astype(BF16).astype(F32)
    lo = (r - mid).astype(BF16).astype(F32)
    return hi, mid, lo


def _rms_rows(xt, gain_col):
    ms = jnp.mean(xt * xt, axis=0, keepdims=True)
    return xt * lax.rsqrt(ms + EPS) * gain_col


def _bias_kernel(rb_ref, idx_ref, out_ref):
    kv = pl.program_id(0)
    g = pl.program_id(1)
    head = kv * SWA_GROUP + g
    idx = idx_ref[...]
    acc = jnp.full(idx.shape, NEG, F32)
    for b in range(REL_BUCKETS):
        acc = jnp.where(idx == b, rb_ref[b * SWA_Q_HEADS + head], acc)
    out_ref[0] = acc


def _bucket_table():
    j = np.arange(2 * WINDOW)[:, None]
    i = np.arange(WINDOW)[None, :]
    dist = i + WINDOW - j
    n = np.maximum(dist, 0).astype(np.int32)
    max_exact = REL_BUCKETS // 2
    nf = np.maximum(n, 1).astype(np.float32)
    large = max_exact + (np.log(nf / np.float32(max_exact)) / np.float32(np.log(REL_MAX_DIST / max_exact))
                         * np.float32(REL_BUCKETS - max_exact)).astype(np.int32)
    large = np.minimum(large, REL_BUCKETS - 1)
    bucket = np.where(n < max_exact, n, large)
    return np.where((dist >= 0) & (dist < WINDOW), bucket, -1).astype(np.int32)


def _swa_bias_table(rel_bias):
    idx = jnp.asarray(_bucket_table())
    return pl.pallas_call(
        _bias_kernel,
        grid=(SWA_KV_HEADS, SWA_GROUP),
        in_specs=[
            pl.BlockSpec(memory_space=pltpu.SMEM),
            pl.BlockSpec((2 * WINDOW, WINDOW), lambda kv, g: (0, 0)),
        ],
        out_specs=pl.BlockSpec((1, 2 * WINDOW, WINDOW), lambda kv, g: (kv, 0, g)),
        out_shape=jax.ShapeDtypeStruct((SWA_KV_HEADS, 2 * WINDOW, SWA_GROUP * WINDOW), F32),
        name="swa_bias_table",
    )(rel_bias.reshape(-1), idx)


def _inproj_kernel(x_ref, g_ref, wt_ref, wf_ref, tok_ref, feat_ref, small_ref):
    x = x_ref[...]
    ms = jnp.mean(x * x, axis=-1, keepdims=True)
    xn = (x * lax.rsqrt(ms + EPS) * g_ref[...]).astype(BF16)
    for c in range(N_TOK // TN_IN):
        cols = slice(c * TN_IN, (c + 1) * TN_IN)
        tok_ref[:, cols] = _dot(xn, wt_ref[:, cols]).astype(BF16)
    n_chunks = N_FEAT // FEAT_CHUNK
    for c in range(n_chunks):
        lo = c * FEAT_CHUNK
        hi = lo + FEAT_CHUNK + (N_SMALL if c == n_chunks - 1 else 0)
        r = _dot_nt(wf_ref[lo:hi, :], xn)
        feat_ref[lo:lo + FEAT_CHUNK, :] = r[:FEAT_CHUNK].astype(BF16)
        if c == n_chunks - 1:
            small_ref[...] = r[FEAT_CHUNK:]


def _resident(shape, layer=None):
    if layer is None:
        return pl.BlockSpec(shape, lambda *_: (0,) * len(shape), pipeline_mode=pl.Buffered(1))
    return pl.BlockSpec((None,) + tuple(shape), lambda *_: (layer,) + (0,) * len(shape),
                        pipeline_mode=pl.Buffered(1))


def _inproj(x2, gain, w_tok, w_feat_t, layer):
    t = x2.shape[0]
    tm = min(TM_IN, t)
    return pl.pallas_call(
        _inproj_kernel,
        grid=(t // tm,),
        in_specs=[
            pl.BlockSpec((tm, D_MODEL), lambda i: (i, 0)),
            _resident((1, D_MODEL), layer),
            _resident((D_MODEL, N_TOK), layer),
            _resident((N_FEAT + N_SMALL, D_MODEL), layer),
        ],
        out_specs=[
            pl.BlockSpec((tm, N_TOK), lambda i: (i, 0)),
            pl.BlockSpec((N_FEAT, tm), lambda i: (0, i)),
            pl.BlockSpec((N_SMALL, tm), lambda i: (0, i)),
        ],
        out_shape=[
            jax.ShapeDtypeStruct((t, N_TOK), BF16),
            jax.ShapeDtypeStruct((N_FEAT, t), BF16),
            jax.ShapeDtypeStruct((N_SMALL, t), F32),
        ],
        compiler_params=pltpu.CompilerParams(
            dimension_semantics=("parallel",), vmem_limit_bytes=VMEM_LIMIT),
        name="inproj",
    )(x2, gain, w_tok, w_feat_t)


def _fox_kernel(fq_ref, fk_ref, fv_ref, small_ref, fb_ref, gq_ref, gk_ref, out_ref,
                qa_ref, ka_ref, s_ref, p_ref, acc_ref, m_ref, l_ref, a_ref, *, seq, tq):
    i = pl.program_id(1)
    heads = range(FOX_HEADS)

    @pl.when(i == 0)
    def _():
        c_all = _cumsum_lanes(_log_sigmoid(small_ref[...] + fb_ref[...]))
        for h in heads:
            sl = slice(h * HEAD_DIM, (h + 1) * HEAD_DIM)
            c = c_all[h:h + 1, :] * LOG2E
            hi, mid, lo = _split3(c)
            one = jnp.ones_like(c)
            qn = _rms_rows(fq_ref[sl, :].astype(F32), gq_ref[...]) * (QK_SCALE * LOG2E)
            qa_ref[h, 0:HEAD_DIM, :] = qn.astype(BF16)
            qa_ref[h, HEAD_DIM:, :] = _row_select([hi, mid, lo, one, one, one], HEAD_DIM, seq).astype(BF16)
            kn = _rms_rows(fk_ref[sl, :].astype(F32), gk_ref[...])
            k_aug = _row_select([one, one, one, -hi, -mid, -lo], HEAD_DIM, seq)
            ka_ref[h] = jnp.concatenate([kn, k_aug], axis=0).T.astype(BF16)

    q_off = pl.multiple_of(i * tq, tq)
    qa = [qa_ref[h, :, pl.ds(q_off, tq)] for h in heads]

    m_ref[...] = jnp.full(m_ref.shape, NEG, F32)
    l_ref[...] = jnp.zeros_like(l_ref)
    acc_ref[...] = jnp.zeros_like(acc_ref)
    a_ref[...] = jnp.ones_like(a_ref)
    p_ref[...] = jnp.zeros_like(p_ref)

    def qk(j, slot):
        off = pl.multiple_of(j * tq, tq)
        for h in heads:
            s_ref[slot, h] = _dot(ka_ref[h, pl.ds(off, tq), :], qa[h])

    def pv(j):
        off = pl.multiple_of(j * tq, tq)
        for h in heads:
            v = fv_ref[h * HEAD_DIM:(h + 1) * HEAD_DIM, pl.ds(off, tq)]
            acc_ref[h] = a_ref[h] * acc_ref[h] + _dot(v, p_ref[h])

    def softmax(slot, masked):
        for h in heads:
            s = s_ref[slot, h]
            if masked:
                key = lax.broadcasted_iota(jnp.int32, s.shape, 0)
                qry = lax.broadcasted_iota(jnp.int32, s.shape, 1)
                s = jnp.where(key <= qry, s, NEG)
            m = m_ref[h]
            m_new = jnp.maximum(m, jnp.max(s, axis=0, keepdims=True))
            p = jnp.exp2(s - m_new)
            alpha = jnp.exp2(m - m_new)
            l_ref[h] = alpha * l_ref[h] + jnp.sum(p, axis=0, keepdims=True)
            m_ref[h] = m_new
            a_ref[h] = alpha
            p_ref[h] = p.astype(BF16)

    def stage(j, slot):
        qk(j + 1, 1 - slot)
        pv(jnp.maximum(j - 1, 0))
        softmax(slot, False)

    qk(0, 0)

    @pl.loop(0, i // 2)
    def _(k):
        stage(2 * k, 0)
        stage(2 * k + 1, 1)

    odd = lax.rem(i, 2) == 1

    @pl.when(odd)
    def _():
        stage(i - 1, 0)
        pv(i - 1)
        softmax(1, True)

    @pl.when(jnp.logical_not(odd))
    def _():
        pv(jnp.maximum(i - 1, 0))
        softmax(0, True)

    pv(i)
    for h in heads:
        out_ref[h * HEAD_DIM:(h + 1) * HEAD_DIM, :] = (acc_ref[h] / l_ref[h]).astype(BF16)


def _fox(feat, small, f_bias_col, q_gain_col, k_gain_col, batch, seq):
    tq = min(TQ_FOX, seq)
    nq = seq // tq
    t = batch * seq
    width = FOX_HEADS * HEAD_DIM
    blk = lambda row: pl.BlockSpec((width, seq), lambda b, i: (row // width, b))
    col = lambda n: pl.BlockSpec((n, 1), lambda b, i: (0, 0))
    return pl.pallas_call(
        functools.partial(_fox_kernel, seq=seq, tq=tq),
        grid=(batch, nq),
        in_specs=[blk(ROW_FQ), blk(ROW_FK), blk(ROW_FV),
                  pl.BlockSpec((N_SMALL, seq), lambda b, i: (0, b)),
                  col(N_SMALL), col(HEAD_DIM), col(HEAD_DIM)],
        out_specs=pl.BlockSpec((width, tq), lambda b, i: (0, b * nq + i)),
        out_shape=jax.ShapeDtypeStruct((width, t), BF16),
        scratch_shapes=[pltpu.VMEM((FOX_HEADS, 2 * HEAD_DIM, seq), BF16),
                        pltpu.VMEM((FOX_HEADS, seq, 2 * HEAD_DIM), BF16),
                        pltpu.VMEM((2, FOX_HEADS, tq, tq), F32),
                        pltpu.VMEM((FOX_HEADS, tq, tq), BF16),
                        pltpu.VMEM((FOX_HEADS, HEAD_DIM, tq), F32),
                        pltpu.VMEM((FOX_HEADS, 1, tq), F32),
                        pltpu.VMEM((FOX_HEADS, 1, tq), F32),
                        pltpu.VMEM((FOX_HEADS, 1, tq), F32)],
        compiler_params=pltpu.CompilerParams(
            dimension_semantics=("parallel", "arbitrary"), vmem_limit_bytes=VMEM_LIMIT),
        name="fox_attention",
    )(feat, feat, feat, small, f_bias_col, q_gain_col, k_gain_col)


def _conv_kernel(u_ref, b_ref, c_ref, w_ref, out_ref):
    z = c_ref[...].astype(F32) * u_ref[...].astype(F32)
    row = lax.broadcasted_iota(jnp.int32, z.shape, 0)
    z1 = jnp.where(row >= 1, pltpu.roll(z, 1, 0), 0.0)
    z2 = jnp.where(row >= 2, pltpu.roll(z, 2, 0), 0.0)
    w = w_ref[...]
    y = w[0:1, :] * z2 + w[1:2, :] * z1 + w[2:3, :] * z
    out_ref[...] = (b_ref[...].astype(F32) * y).astype(BF16)


def _conv(tok, conv_w, batch, seq):
    width = conv_w.shape[1]
    blk = lambda c: pl.BlockSpec((seq, width), lambda b: (b, c // width))
    return pl.pallas_call(
        _conv_kernel,
        grid=(batch,),
        in_specs=[blk(COL_CU), blk(COL_CB), blk(COL_CC),
                  pl.BlockSpec(conv_w.shape, lambda b: (0, 0))],
        out_specs=pl.BlockSpec((seq, width), lambda b: (b, 0)),
        out_shape=jax.ShapeDtypeStruct((batch * seq, width), BF16),
        compiler_params=pltpu.CompilerParams(
            dimension_semantics=("parallel",), vmem_limit_bytes=VMEM_LIMIT),
        name="short_conv",
    )(tok, tok, tok, conv_w)


def _mlstm_kernel(q_ref, v_ref, o_ref, k_ref, small_ref, bias_ref, gain_ref, out_ref,
                  c_ref, m_ref, b_ref, u_ref, ut_ref, *, chunk):
    @pl.when(pl.program_id(1) == 0)
    def _():
        c_ref[...] = jnp.zeros_like(c_ref)
        m_ref[...] = jnp.zeros_like(m_ref)
        g = small_ref[...] + bias_ref[...]
        b_all = _cumsum_lanes(_log_sigmoid(g), segment=chunk)
        b_ref[...] = b_all
        u_all = _row_select([g[4 + h:5 + h, :] - b_all[8 + h:9 + h, :] for h in range(MLSTM_HEADS)],
                            8, g.shape[1])
        u_ref[...] = u_all
        ut_ref[...] = jnp.concatenate([u_all, jnp.zeros((120, g.shape[1]), F32)], axis=0).T

    off = pl.multiple_of(pl.program_id(1) * chunk, chunk)
    bcum = b_ref[:, pl.ds(off, chunk)]
    rows = [u_ref[h:h + 1, pl.ds(off, chunk)] for h in range(MLSTM_HEADS)]
    u_cols = ut_ref[pl.ds(off, chunk), :]

    src = lax.broadcasted_iota(jnp.int32, (chunk, chunk), 0)
    tgt = lax.broadcasted_iota(jnp.int32, (chunk, chunk), 1)
    causal = src <= tgt
    ones_rows = (lax.broadcasted_iota(jnp.int32, (HEAD_DIM, chunk), 0) == 0).astype(BF16)

    heads = range(MLSTM_HEADS)
    sls = [slice(h * HEAD_DIM, (h + 1) * HEAD_DIM) for h in heads]
    qs = [(q_ref[sl, :].astype(F32) * QK_SCALE).astype(BF16) for sl in sls]
    ks = [k_ref[:, sl] for sl in sls]
    c_prev = [c_ref[h] for h in heads]
    scores = [_dot(ks[h], qs[h]) for h in heads]
    carried = [_dot(c_prev[h].astype(BF16), qs[h]) for h in heads]

    sw, kw, m_t, inter, decay = [], [], [], [], []
    for h in heads:
        b_row = bcum[8 + h:9 + h, :]
        b_last = b_row[:, chunk - 1:chunk]
        u_col = u_cols[:, h:h + 1]
        m_prev = m_ref[h][0:1, 0:1]
        dmat = jnp.where(causal, b_row + u_col, NEG)
        g_row = b_row + m_prev
        m_h = jnp.maximum(g_row, jnp.max(dmat, axis=0, keepdims=True))
        sw.append((scores[h] * jnp.exp(dmat - m_h)).astype(BF16))
        m_t.append(m_h)
        inter.append(jnp.exp(g_row - m_h))
        m_loc = jnp.max(rows[h] + b_last, axis=1, keepdims=True)
        m_new = jnp.maximum(b_last + m_prev, m_loc)
        decay.append(jnp.exp(b_last + m_prev - m_new))
        w_col = jnp.exp(u_col + b_last - m_new)
        kw.append((ks[h].astype(F32) * w_col).astype(BF16))
        m_ref[h] = jnp.broadcast_to(m_new, m_ref.shape[1:])

    v_aug = [jnp.concatenate([v_ref[sl, :], ones_rows], axis=0) for sl in sls]
    intra = [_dot(v_aug[h], sw[h]) for h in heads]
    update = [_dot(v_aug[h], kw[h]) for h in heads]
    for h in heads:
        c_ref[h] = decay[h] * c_prev[h] + update[h]
        tot = intra[h] + inter[h] * carried[h]
        num = tot[0:HEAD_DIM]
        den = tot[HEAD_DIM:HEAD_DIM + 1]
        ht = num / jnp.maximum(jnp.abs(den), jnp.exp(-m_t[h]))
        hn = _rms_rows(ht, gain_ref[sls[h], :])
        out_ref[sls[h], :] = (_sigmoid(o_ref[sls[h], :].astype(F32)) * hn).astype(BF16)


def _mlstm(feat, tok, small, bias_col, gain_col, batch, seq):
    chunk = min(L_MLSTM, seq)
    nc = seq // chunk
    t = batch * seq
    width = MLSTM_HEADS * HEAD_DIM
    blk = lambda row: pl.BlockSpec((width, chunk), lambda b, c: (row // width, b * nc + c))
    return pl.pallas_call(
        functools.partial(_mlstm_kernel, chunk=chunk),
        grid=(batch, nc),
        in_specs=[blk(ROW_MQ), blk(ROW_MV), blk(ROW_MO),
                  pl.BlockSpec((chunk, width), lambda b, c: (b * nc + c, COL_MK // width)),
                  pl.BlockSpec((N_SMALL, seq), lambda b, c: (0, b)),
                  pl.BlockSpec((N_SMALL, 1), lambda b, c: (0, 0)),
                  pl.BlockSpec((width, 1), lambda b, c: (0, 0))],
        out_specs=pl.BlockSpec((width, chunk), lambda b, c: (0, b * nc + c)),
        out_shape=jax.ShapeDtypeStruct((width, t), BF16),
        scratch_shapes=[pltpu.VMEM((MLSTM_HEADS, 2 * HEAD_DIM, HEAD_DIM), F32),
                        pltpu.VMEM((MLSTM_HEADS, 8, 128), F32),
                        pltpu.VMEM((N_SMALL, seq), F32),
                        pltpu.VMEM((8, seq), F32),
                        pltpu.VMEM((seq, 128), F32)],
        compiler_params=pltpu.CompilerParams(
            dimension_semantics=("parallel", "arbitrary"), vmem_limit_bytes=VMEM_LIMIT),
        name="mlstm",
    )(feat, feat, feat, tok, small, bias_col, gain_col)


def _swa_kernel(q_ref, kc_ref, kp_ref, vc_ref, vp_ref, bias_ref, sink_ref, gq_ref, gk_ref, out_ref, *, nblk):
    w = WINDOW
    key = lax.broadcasted_iota(jnp.int32, (2 * w, SWA_GROUP * w), 0)
    first_block_pad = jnp.logical_and(pl.program_id(1) == 0, key < w)
    kvs = range(SWA_KV_HEADS)
    ksl = [slice(kv * HEAD_DIM, (kv + 1) * HEAD_DIM) for kv in kvs]
    kt = [jnp.concatenate([_rms_rows(kp_ref[s, :].astype(F32), gk_ref[...]),
                           _rms_rows(kc_ref[s, :].astype(F32), gk_ref[...])], axis=1).astype(BF16) for s in ksl]
    vt = [jnp.concatenate([vp_ref[s, :], vc_ref[s, :]], axis=1) for s in ksl]

    bands = [(blk, kv) for blk in range(nblk) for kv in kvs]
    scores = []
    for blk, kv in bands:
        qs = []
        for g in range(SWA_GROUP):
            hs = slice((kv * SWA_GROUP + g) * HEAD_DIM, (kv * SWA_GROUP + g + 1) * HEAD_DIM)
            q = q_ref[hs, blk * w:(blk + 1) * w].astype(F32)
            qs.append((_rms_rows(q, gq_ref[...]) * QK_SCALE).astype(BF16))
        qt = jnp.concatenate(qs, axis=1)
        s = _dot_tn(kt[kv][:, blk * w:(blk + 2) * w], qt) + bias_ref[kv]
        scores.append(jnp.where(first_block_pad, NEG, s) if blk == 0 else s)
    probs, denoms = [], []
    for (blk, kv), s in zip(bands, scores):
        sink = sink_ref[kv]
        m = jnp.maximum(jnp.max(s, axis=0, keepdims=True), sink)
        p = jnp.exp(s - m)
        denoms.append(jnp.sum(p, axis=0, keepdims=True) + jnp.exp(sink - m))
        probs.append(p.astype(BF16))
    outs = [_dot(vt[kv][:, blk * w:(blk + 2) * w], p) for (blk, kv), p in zip(bands, probs)]
    for (blk, kv), o, d in zip(bands, outs, denoms):
        o = o / d
        for g in range(SWA_GROUP):
            hs = slice((kv * SWA_GROUP + g) * HEAD_DIM, (kv * SWA_GROUP + g + 1) * HEAD_DIM)
            out_ref[hs, blk * w:(blk + 1) * w] = o[:, g * w:(g + 1) * w].astype(BF16)


def _swa(feat, bias_t, sink_rows, q_gain_col, k_gain_col, batch, seq):
    w = WINDOW
    nblk = SWA_BLOCKS
    span = nblk * w
    ns = seq // span
    t = batch * seq
    qw = SWA_Q_HEADS * HEAD_DIM
    kw = SWA_KV_HEADS * HEAD_DIM
    cur = lambda row: pl.BlockSpec((kw, span), lambda b, n: (row // kw, b * ns + n))
    prev = lambda row: pl.BlockSpec(
        (kw, w), lambda b, n: (row // kw, (b * ns + n) * nblk - jnp.minimum(n, 1)))
    return pl.pallas_call(
        functools.partial(_swa_kernel, nblk=nblk),
        grid=(batch, ns),
        in_specs=[pl.BlockSpec((qw, span), lambda b, n: (ROW_SQ // qw, b * ns + n)),
                  cur(ROW_SK), prev(ROW_SK), cur(ROW_SV), prev(ROW_SV),
                  pl.BlockSpec(bias_t.shape, lambda b, n: (0, 0, 0)),
                  pl.BlockSpec(sink_rows.shape, lambda b, n: (0, 0, 0)),
                  pl.BlockSpec((HEAD_DIM, 1), lambda b, n: (0, 0)),
                  pl.BlockSpec((HEAD_DIM, 1), lambda b, n: (0, 0))],
        out_specs=pl.BlockSpec((qw, span), lambda b, n: (0, b * ns + n)),
        out_shape=jax.ShapeDtypeStruct((qw, t), BF16),
        compiler_params=pltpu.CompilerParams(
            dimension_semantics=("parallel", "arbitrary"), vmem_limit_bytes=VMEM_LIMIT),
        name="swa_attention",
    )(feat, feat, feat, feat, feat, bias_t, sink_rows, q_gain_col, k_gain_col)


def _merge_ffn_kernel(x_ref, yf_ref, yc_ref, ym_ref, ys_ref, g0_ref, g1_ref, g2_ref, g3_ref,
                      wf_ref, wc_ref, wm_ref, ws_ref, wo_ref, gain_ref, wg_ref, wu_ref, wd_ref,
                      out_ref, *, sub):
    bounds = list(range(0, D_FF, TF_FFN)) + [D_FF]
    for r in range(0, x_ref.shape[0], sub):
        rows = slice(r, r + sub)

        def gated(g_ref, y):
            return (1.0 + jnp.tanh(g_ref[rows, :].astype(F32))) * y

        merged = gated(g0_ref, _dot_tn(yf_ref[:, rows], wf_ref[...]))
        merged += gated(g1_ref, _dot(yc_ref[rows, :], wc_ref[...]))
        merged += gated(g2_ref, _dot_tn(ym_ref[:, rows], wm_ref[...]))
        merged += gated(g3_ref, _dot_tn(ys_ref[:, rows], ws_ref[...]))
        x = x_ref[rows, :] + _dot(merged.astype(BF16), wo_ref[...])

        ms = jnp.mean(x * x, axis=-1, keepdims=True)
        hn = (x * lax.rsqrt(ms + EPS) * gain_ref[...]).astype(BF16)
        acc = x
        for lo, hi in zip(bounds[:-1], bounds[1:]):
            h = _dot(hn, wg_ref[:, lo:hi])
            act = (h * (1.0 + jnp.tanh(h)) * _dot(hn, wu_ref[:, lo:hi])).astype(BF16)
            acc = acc + _dot(act, wd_ref[lo:hi, :])
        out_ref[rows, :] = acc


def _merge_ffn(x2, y_fox_t, y_conv, y_mlstm_t, y_swa_t, tok, w_fox, w_conv, w_mlstm, w_swa, w_out,
               gain, w_gate_half, w_up, w_down, layer):
    t = x2.shape[0]
    tm = min(TM_MERGE, t)
    feat_blk = lambda a: pl.BlockSpec((a.shape[0], tm), lambda i: (0, i))
    full = lambda a: _resident(a.shape[1:], layer)
    gate = lambda k: pl.BlockSpec((tm, D_MODEL), lambda i: (i, COL_GATES // D_MODEL + k))
    return pl.pallas_call(
        functools.partial(_merge_ffn_kernel, sub=min(SUB_FFN, tm)),
        grid=(t // tm,),
        in_specs=[pl.BlockSpec((tm, D_MODEL), lambda i: (i, 0)),
                  feat_blk(y_fox_t),
                  pl.BlockSpec((tm, y_conv.shape[1]), lambda i: (i, 0)),
                  feat_blk(y_mlstm_t), feat_blk(y_swa_t),
                  gate(0), gate(1), gate(2), gate(3),
                  full(w_fox), full(w_conv), full(w_mlstm), full(w_swa), full(w_out),
                  full(gain), full(w_gate_half), full(w_up), full(w_down)],
        out_specs=pl.BlockSpec((tm, D_MODEL), lambda i: (i, 0)),
        out_shape=jax.ShapeDtypeStruct((t, D_MODEL), F32),
        compiler_params=pltpu.CompilerParams(
            dimension_semantics=("parallel",), vmem_limit_bytes=VMEM_LIMIT),
        name="merge_ffn",
    )(x2, y_fox_t, y_conv, y_mlstm_t, y_swa_t, tok, tok, tok, tok,
      w_fox, w_conv, w_mlstm, w_swa, w_out, gain, w_gate_half, w_up, w_down)


IN_FQKV, IN_FF, IN_CONV, IN_MQ, IN_MK, IN_MV = 0, 768, 772, 1540, 1796, 2052
IN_MI, IN_MF, IN_MO, IN_SQ, IN_SK, IN_SV, IN_GATES, IN_COLS = 2308, 2312, 2316, 2572, 3084, 3212, 3340, 7436
TK_PREP = 256


def _wprep_kernel(w_ref, tok_ref, feat_ref):
    tok_ref[:, COL_CU:COL_MK] = w_ref[:, IN_CONV:IN_MQ].astype(BF16)
    tok_ref[:, COL_MK:COL_GATES] = w_ref[:, IN_MK:IN_MV].astype(BF16)
    tok_ref[:, COL_GATES:] = (0.5 * w_ref[:, IN_GATES:IN_COLS]).astype(BF16)
    for row, lo, hi in ((ROW_FQ, IN_FQKV, IN_FF), (ROW_MQ, IN_MQ, IN_MK), (ROW_MV, IN_MV, IN_MI),
                        (ROW_MO, IN_MO, IN_SQ), (ROW_SQ, IN_SQ, IN_GATES)):
        feat_ref[row:row + hi - lo, :] = w_ref[:, lo:hi].T.astype(BF16)
    fox_f = w_ref[:, IN_FF:IN_FF + 128].T[0:N_SMALL]
    mlstm_if = w_ref[:, IN_MI - 4:IN_MI + 124].T[0:N_SMALL]
    row = lax.broadcasted_iota(jnp.int32, fox_f.shape, 0)
    small = jnp.where(row < 4, fox_f, jnp.where(row < 12, mlstm_if, 0.0))
    feat_ref[N_FEAT:, :] = small.astype(BF16)


def _prep_w_in(w_in):
    depth, d, cols = w_in.shape
    return pl.pallas_call(
        _wprep_kernel,
        grid=(depth, d // TK_PREP),
        in_specs=[pl.BlockSpec((None, TK_PREP, cols), lambda l, j: (l, j, 0))],
        out_specs=[pl.BlockSpec((None, TK_PREP, N_TOK), lambda l, j: (l, j, 0)),
                   pl.BlockSpec((None, N_FEAT + N_SMALL, TK_PREP), lambda l, j: (l, 0, j))],
        out_shape=[jax.ShapeDtypeStruct((depth, d, N_TOK), BF16),
                   jax.ShapeDtypeStruct((depth, N_FEAT + N_SMALL, d), BF16)],
        compiler_params=pltpu.CompilerParams(
            dimension_semantics=("parallel", "parallel"), vmem_limit_bytes=VMEM_LIMIT),
        name="w_in_relayout",
    )(w_in)


def _col(v, n=None):
    v = v.astype(F32).reshape(-1, 1)
    if n is not None and v.shape[0] < n:
        v = jnp.concatenate([v, jnp.zeros((n - v.shape[0], 1), F32)], axis=0)
    return v


def kernel(x, rel_bias, attn_norm, w_in, fox_f_bias, fox_q_gain, fox_k_gain, conv_w, mlstm_i_bias, mlstm_f_bias, mlstm_h_gain, swa_q_gain, swa_k_gain, swa_sinks, w_fox_out, w_conv_out, w_mlstm_out, w_swa_out, w_merge_out, ffn_norm, w_gate, w_up, w_down):
    batch, seq, _ = x.shape
    depth = w_in.shape[0]
    x2 = x.reshape(batch * seq, D_MODEL)
    bias_t = _swa_bias_table(rel_bias)

    w_tok, w_feat_t = _prep_w_in(w_in)
    attn_gain = attn_norm.reshape(depth, 1, D_MODEL)
    ffn_gain = ffn_norm.reshape(depth, 1, D_MODEL)
    w_fox_b, w_conv_b, w_mlstm_b, w_swa_b = (w.astype(BF16) for w in (w_fox_out, w_conv_out, w_mlstm_out, w_swa_out))
    w_merge_half = (0.5 * w_merge_out).astype(BF16)
    w_gate_half = (0.5 * w_gate).astype(BF16)
    w_up_b = w_up.astype(BF16)
    w_down_b = w_down.astype(BF16)

    for l in range(depth):
        tok, feat, small = _inproj(x2, attn_gain, w_tok, w_feat_t, l)

        y_fox_t = _fox(feat, small, _col(fox_f_bias[l], N_SMALL), _col(fox_q_gain[l]), _col(fox_k_gain[l]),
                       batch, seq)
        y_conv = _conv(tok, conv_w[l], batch, seq)
        gate_bias = jnp.concatenate([jnp.zeros((4,), F32), mlstm_i_bias[l], mlstm_f_bias[l]])
        y_mlstm_t = _mlstm(feat, tok, small, _col(gate_bias, N_SMALL), _col(mlstm_h_gain[l]), batch, seq)
        sink_rows = jnp.broadcast_to(
            swa_sinks[l].astype(F32).reshape(SWA_KV_HEADS, 1, SWA_GROUP, 1),
            (SWA_KV_HEADS, 1, SWA_GROUP, WINDOW)).reshape(SWA_KV_HEADS, 1, SWA_GROUP * WINDOW)
        y_swa_t = _swa(feat, bias_t, sink_rows, _col(swa_q_gain[l]), _col(swa_k_gain[l]), batch, seq)

        x2 = _merge_ffn(x2, y_fox_t, y_conv, y_mlstm_t, y_swa_t, tok,
                        w_fox_b, w_conv_b, w_mlstm_b, w_swa_b, w_merge_half,
                        ffn_gain, w_gate_half, w_up_b, w_down_b, l)
    return x2.reshape(batch, seq, D_MODEL)
```

```python
import functools

import numpy as np
import jax
import jax.numpy as jnp
from jax import lax
from jax.experimental import pallas as pl
from jax.experimental.pallas import tpu as pltpu

F32 = jnp.float32
BF16 = jnp.bfloat16

D_MODEL = 1024
HEAD_DIM = 64
FOX_HEADS = 4
MLSTM_HEADS = 4
SWA_Q_HEADS = 8
SWA_KV_HEADS = 2
SWA_GROUP = SWA_Q_HEADS // SWA_KV_HEADS
WINDOW = 128
REL_BUCKETS = 32
REL_MAX_DIST = 128
D_FF = 2816
EPS = 1e-6
NEG = -1e30
QK_SCALE = HEAD_DIM ** -0.5
LOG2E = 1.4426950408889634

ROW_FQ, ROW_FK, ROW_FV = 0, 256, 512
ROW_MQ, ROW_MV, ROW_MO = 768, 1024, 1280
ROW_SQ, ROW_SK, ROW_SV = 1536, 2048, 2176
N_FEAT = 2304
N_SMALL = 16
COL_CU, COL_CB, COL_CC, COL_MK, COL_GATES = 0, 256, 512, 768, 1024
N_TOK = 5120

TM_IN = 512
TN_IN = 512
FEAT_CHUNK = 576
TQ_FOX = 256
TK_FOX = 256
L_MLSTM = 256
SWA_BLOCKS = 4
TM_MERGE = 512
TM_FFN = 512
SUB_FFN = 512
TF_FFN = 512
VMEM_LIMIT = 56 * 1024 * 1024


def _dot(a, b):
    return jnp.dot(a, b, preferred_element_type=F32)


def _dot_nt(a, b):
    return lax.dot_general(a, b, (((1,), (1,)), ((), ())), preferred_element_type=F32)


def _dot_tn(a, b):
    return lax.dot_general(a, b, (((0,), (0,)), ((), ())), preferred_element_type=F32)


def _sigmoid(x):
    return 0.5 * jnp.tanh(0.5 * x) + 0.5


def _log_sigmoid(x):
    return jnp.minimum(x, 0.0) - jnp.log(1.0 + jnp.exp(-jnp.abs(x)))


def _cumsum_lanes(x, segment=None):
    n = segment or x.shape[-1]
    lane = lax.broadcasted_iota(jnp.int32, x.shape, x.ndim - 1) & (n - 1)
    k = 1
    while k < n:
        x = x + jnp.where(lane >= k, pltpu.roll(x, k, x.ndim - 1), 0.0)
        k *= 2
    return x


def _row_select(rows, n_rows, width):
    rid = lax.broadcasted_iota(jnp.int32, (n_rows, width), 0)
    out = jnp.zeros((n_rows, width), F32)
    for r, v in enumerate(rows):
        out = jnp.where(rid == r, v, out)
    return out


def _split3(c):
    hi = c.astype(BF16).astype(F32)
    r = c - hi
    mid = r.astype(BF16).astype(F32)
    lo = (r - mid).astype(BF16).astype(F32)
    return hi, mid, lo


def _rms_rows(xt, gain_col):
    ms = jnp.mean(xt * xt, axis=0, keepdims=True)
    return xt * lax.rsqrt(ms + EPS) * gain_col


def _bias_kernel(rb_ref, idx_ref, out_ref):
    kv = pl.program_id(0)
    g = pl.program_id(1)
    head = kv * SWA_GROUP + g
    idx = idx_ref[...]
    acc = jnp.full(idx.shape, NEG, F32)
    for b in range(REL_BUCKETS):
        acc = jnp.where(idx == b, rb_ref[b * SWA_Q_HEADS + head], acc)
    out_ref[0] = acc


def _bucket_table():
    j = np.arange(2 * WINDOW)[:, None]
    i = np.arange(WINDOW)[None, :]
    dist = i + WINDOW - j
    n = np.maximum(dist, 0).astype(np.int32)
    max_exact = REL_BUCKETS // 2
    nf = np.maximum(n, 1).astype(np.float32)
    large = max_exact + (np.log(nf / np.float32(max_exact)) / np.float32(np.log(REL_MAX_DIST / max_exact))
                         * np.float32(REL_BUCKETS - max_exact)).astype(np.int32)
    large = np.minimum(large, REL_BUCKETS - 1)
    bucket = np.where(n < max_exact, n, large)
    return np.where((dist >= 0) & (dist < WINDOW), bucket, -1).astype(np.int32)


def _swa_bias_table(rel_bias):
    idx = jnp.asarray(_bucket_table())
    return pl.pallas_call(
        _bias_kernel,
        grid=(SWA_KV_HEADS, SWA_GROUP),
        in_specs=[
            pl.BlockSpec(memory_space=pltpu.SMEM),
            pl.BlockSpec((2 * WINDOW, WINDOW), lambda kv, g: (0, 0)),
        ],
        out_specs=pl.BlockSpec((1, 2 * WINDOW, WINDOW), lambda kv, g: (kv, 0, g)),
        out_shape=jax.ShapeDtypeStruct((SWA_KV_HEADS, 2 * WINDOW, SWA_GROUP * WINDOW), F32),
        name="swa_bias_table",
    )(rel_bias.reshape(-1), idx)


def _inproj_kernel(x_ref, g_ref, wt_ref, wf_ref, tok_ref, gates_ref, feat_ref, small_ref):
    x = x_ref[...]
    ms = jnp.mean(x * x, axis=-1, keepdims=True)
    xn = (x * lax.rsqrt(ms + EPS) * g_ref[...]).astype(BF16)
    for c in range(N_TOK // TN_IN):
        lo = c * TN_IN
        r = _dot(xn, wt_ref[:, lo:lo + TN_IN]).astype(BF16)
        if lo < COL_GATES:
            tok_ref[:, lo:lo + TN_IN] = r
        else:
            gates_ref[:, lo - COL_GATES:lo - COL_GATES + TN_IN] = r
    n_chunks = N_FEAT // FEAT_CHUNK
    for c in range(n_chunks):
        lo = c * FEAT_CHUNK
        hi = lo + FEAT_CHUNK + (N_SMALL if c == n_chunks - 1 else 0)
        r = _dot_nt(wf_ref[lo:hi, :], xn)
        feat_ref[lo:lo + FEAT_CHUNK, :] = r[:FEAT_CHUNK].astype(BF16)
        if c == n_chunks - 1:
            small_ref[...] = r[FEAT_CHUNK:]


def _resident(shape, layer=None):
    if layer is None:
        return pl.BlockSpec(shape, lambda *_: (0,) * len(shape), pipeline_mode=pl.Buffered(1))
    return pl.BlockSpec((None,) + tuple(shape), lambda *_: (layer,) + (0,) * len(shape),
                        pipeline_mode=pl.Buffered(1))


def _inproj(x2, gain, w_tok, w_feat_t, layer):
    t = x2.shape[0]
    tm = min(TM_IN, t)
    return pl.pallas_call(
        _inproj_kernel,
        grid=(t // tm,),
        in_specs=[
            pl.BlockSpec((tm, D_MODEL), lambda i: (i, 0)),
            _resident((1, D_MODEL), layer),
            _resident((D_MODEL, N_TOK), layer),
            _resident((N_FEAT + N_SMALL, D_MODEL), layer),
        ],
        out_specs=[
            pl.BlockSpec((tm, COL_GATES), lambda i: (i, 0)),
            pl.BlockSpec((tm, N_TOK - COL_GATES), lambda i: (i, 0)),
            pl.BlockSpec((N_FEAT, tm), lambda i: (0, i)),
            pl.BlockSpec((N_SMALL, tm), lambda i: (0, i)),
        ],
        out_shape=[
            jax.ShapeDtypeStruct((t, COL_GATES), BF16),
            jax.ShapeDtypeStruct((t, N_TOK - COL_GATES), BF16),
            jax.ShapeDtypeStruct((N_FEAT, t), BF16),
            jax.ShapeDtypeStruct((N_SMALL, t), F32),
        ],
        compiler_params=pltpu.CompilerParams(
            dimension_semantics=("parallel",), vmem_limit_bytes=VMEM_LIMIT),
        name="inproj",
    )(x2, gain, w_tok, w_feat_t)


def _fox_kernel(fq_ref, fk_ref, fv_ref, small_ref, fb_ref, gq_ref, gk_ref, out_ref,
                qa_ref, ka_ref, s_ref, p_ref, acc_ref, m_ref, l_ref, a_ref, *, seq, tq, tk):
    i = pl.program_id(1)
    heads = range(FOX_HEADS)
    ratio = tq // tk

    @pl.when(i == 0)
    def _():
        c_all = _cumsum_lanes(_log_sigmoid(small_ref[...] + fb_ref[...]))
        for h in heads:
            sl = slice(h * HEAD_DIM, (h + 1) * HEAD_DIM)
            c = c_all[h:h + 1, :] * LOG2E
            hi, mid, lo = _split3(c)
            one = jnp.ones_like(c)
            qn = _rms_rows(fq_ref[sl, :].astype(F32), gq_ref[...]) * (QK_SCALE * LOG2E)
            qa_ref[h, 0:HEAD_DIM, :] = qn.astype(BF16)
            qa_ref[h, HEAD_DIM:, :] = _row_select([hi, mid, lo, one, one, one], HEAD_DIM, seq).astype(BF16)
            kn = _rms_rows(fk_ref[sl, :].astype(F32), gk_ref[...])
            k_aug = _row_select([one, one, one, -hi, -mid, -lo], HEAD_DIM, seq)
            ka_ref[h] = jnp.concatenate([kn, k_aug], axis=0).T.astype(BF16)

    q_off = pl.multiple_of(i * tq, tq)
    qa = [qa_ref[h, :, pl.ds(q_off, tq)] for h in heads]

    m_ref[...] = jnp.full(m_ref.shape, NEG, F32)
    l_ref[...] = jnp.zeros_like(l_ref)
    acc_ref[...] = jnp.zeros_like(acc_ref)
    a_ref[...] = jnp.ones_like(a_ref)
    p_ref[...] = jnp.zeros_like(p_ref)

    def qk(j, slot):
        off = pl.multiple_of(j * tk, tk)
        for h in heads:
            s_ref[slot, h] = _dot(ka_ref[h, pl.ds(off, tk), :], qa[h])

    def pv(j):
        off = pl.multiple_of(j * tk, tk)
        for h in heads:
            v = fv_ref[h * HEAD_DIM:(h + 1) * HEAD_DIM, pl.ds(off, tk)]
            acc_ref[h] = a_ref[h] * acc_ref[h] + _dot(v, p_ref[h])

    def softmax(slot, diag=None):
        for h in heads:
            s = s_ref[slot, h]
            if diag is not None:
                ahead = lax.broadcasted_iota(jnp.int32, s.shape, 1) - lax.broadcasted_iota(jnp.int32, s.shape, 0)
                s = jnp.where(ahead >= diag * tk, s, NEG)
            m = m_ref[h]
            m_new = jnp.maximum(m, jnp.max(s, axis=0, keepdims=True))
            p = jnp.exp2(s - m_new)
            alpha = jnp.exp2(m - m_new)
            l_ref[h] = alpha * l_ref[h] + jnp.sum(p, axis=0, keepdims=True)
            m_ref[h] = m_new
            a_ref[h] = alpha
            p_ref[h] = p.astype(BF16)

    def stage(j, slot, diag=None):
        qk(j + 1, 1 - slot)
        pv(jnp.maximum(j - 1, 0))
        softmax(slot, diag)

    qk(0, 0)
    first_diag = ratio * i

    @pl.loop(0, first_diag // 2)
    def _(k):
        stage(2 * k, 0)
        stage(2 * k + 1, 1)

    def tail(parity):
        if parity == 1:
            stage(first_diag - 1, 0)
        for r in range(ratio - 1):
            stage(first_diag + r, (parity + r) % 2, diag=r)
        last = first_diag + ratio - 1
        pv(jnp.maximum(last - 1, 0))
        softmax((parity + ratio - 1) % 2, diag=ratio - 1)
        pv(last)

    if ratio % 2 == 0:
        tail(0)
    else:
        odd = lax.rem(first_diag, 2) == 1
        pl.when(odd)(lambda: tail(1))
        pl.when(jnp.logical_not(odd))(lambda: tail(0))

    for h in heads:
        out_ref[h * HEAD_DIM:(h + 1) * HEAD_DIM, :] = (acc_ref[h] / l_ref[h]).astype(BF16)


def _fox(feat, small, f_bias_col, q_gain_col, k_gain_col, batch, seq):
    tq = min(TQ_FOX, seq)
    tk = min(TK_FOX, tq)
    nq = seq // tq
    t = batch * seq
    width = FOX_HEADS * HEAD_DIM
    blk = lambda row: pl.BlockSpec((width, seq), lambda b, i: (row // width, b))
    col = lambda n: pl.BlockSpec((n, 1), lambda b, i: (0, 0))
    return pl.pallas_call(
        functools.partial(_fox_kernel, seq=seq, tq=tq, tk=tk),
        grid=(batch, nq),
        in_specs=[blk(ROW_FQ), blk(ROW_FK), blk(ROW_FV),
                  pl.BlockSpec((N_SMALL, seq), lambda b, i: (0, b)),
                  col(N_SMALL), col(HEAD_DIM), col(HEAD_DIM)],
        out_specs=pl.BlockSpec((width, tq), lambda b, i: (0, b * nq + i)),
        out_shape=jax.ShapeDtypeStruct((width, t), BF16),
        scratch_shapes=[pltpu.VMEM((FOX_HEADS, 2 * HEAD_DIM, seq), BF16),
                        pltpu.VMEM((FOX_HEADS, seq, 2 * HEAD_DIM), BF16),
                        pltpu.VMEM((2, FOX_HEADS, tk, tq), F32),
                        pltpu.VMEM((FOX_HEADS, tk, tq), BF16),
                        pltpu.VMEM((FOX_HEADS, HEAD_DIM, tq), F32),
                        pltpu.VMEM((FOX_HEADS, 1, tq), F32),
                        pltpu.VMEM((FOX_HEADS, 1, tq), F32),
                        pltpu.VMEM((FOX_HEADS, 1, tq), F32)],
        compiler_params=pltpu.CompilerParams(
            dimension_semantics=("parallel", "arbitrary"), vmem_limit_bytes=VMEM_LIMIT),
        name="fox_attention",
    )(feat, feat, feat, small, f_bias_col, q_gain_col, k_gain_col)


def _conv_kernel(u_ref, b_ref, c_ref, w_ref, out_ref):
    z = c_ref[...].astype(F32) * u_ref[...].astype(F32)
    row = lax.broadcasted_iota(jnp.int32, z.shape, 0)
    z1 = jnp.where(row >= 1, pltpu.roll(z, 1, 0), 0.0)
    z2 = jnp.where(row >= 2, pltpu.roll(z, 2, 0), 0.0)
    w = w_ref[...]
    y = w[0:1, :] * z2 + w[1:2, :] * z1 + w[2:3, :] * z
    out_ref[...] = (b_ref[...].astype(F32) * y).astype(BF16)


def _conv(tok, conv_w, batch, seq):
    width = conv_w.shape[1]
    blk = lambda c: pl.BlockSpec((seq, width), lambda b: (b, c // width))
    return pl.pallas_call(
        _conv_kernel,
        grid=(batch,),
        in_specs=[blk(COL_CU), blk(COL_CB), blk(COL_CC),
                  pl.BlockSpec(conv_w.shape, lambda b: (0, 0))],
        out_specs=pl.BlockSpec((seq, width), lambda b: (b, 0)),
        out_shape=jax.ShapeDtypeStruct((batch * seq, width), BF16),
        compiler_params=pltpu.CompilerParams(
            dimension_semantics=("parallel",), vmem_limit_bytes=VMEM_LIMIT),
        name="short_conv",
    )(tok, tok, tok, conv_w)


def _mlstm_kernel(q_ref, v_ref, o_ref, k_ref, small_ref, bias_ref, gain_ref, out_ref,
                  c_ref, m_ref, b_ref, u_ref, ut_ref, *, chunk):
    @pl.when(pl.program_id(1) == 0)
    def _():
        c_ref[...] = jnp.zeros_like(c_ref)
        m_ref[...] = jnp.zeros_like(m_ref)
        g = small_ref[...] + bias_ref[...]
        b_all = _cumsum_lanes(_log_sigmoid(g), segment=chunk)
        b_ref[...] = b_all
        u_all = _row_select([g[4 + h:5 + h, :] - b_all[8 + h:9 + h, :] for h in range(MLSTM_HEADS)],
                            8, g.shape[1])
        u_ref[...] = u_all
        ut_ref[...] = jnp.concatenate([u_all, jnp.zeros((120, g.shape[1]), F32)], axis=0).T

    off = pl.multiple_of(pl.program_id(1) * chunk, chunk)
    bcum = b_ref[:, pl.ds(off, chunk)]
    rows = [u_ref[h:h + 1, pl.ds(off, chunk)] for h in range(MLSTM_HEADS)]
    u_cols = ut_ref[pl.ds(off, chunk), :]

    src = lax.broadcasted_iota(jnp.int32, (chunk, chunk), 0)
    tgt = lax.broadcasted_iota(jnp.int32, (chunk, chunk), 1)
    causal = src <= tgt
    ones_rows = (lax.broadcasted_iota(jnp.int32, (HEAD_DIM, chunk), 0) == 0).astype(BF16)

    heads = range(MLSTM_HEADS)
    sls = [slice(h * HEAD_DIM, (h + 1) * HEAD_DIM) for h in heads]
    qs = [(q_ref[sl, :].astype(F32) * QK_SCALE).astype(BF16) for sl in sls]
    ks = [k_ref[:, sl] for sl in sls]
    c_prev = [c_ref[h] for h in heads]
    scores = [_dot(ks[h], qs[h]) for h in heads]
    carried = [_dot(c_prev[h].astype(BF16), qs[h]) for h in heads]

    sw, kw, m_t, inter, decay = [], [], [], [], []
    for h in heads:
        b_row = bcum[8 + h:9 + h, :]
        b_last = b_row[:, chunk - 1:chunk]
        u_col = u_cols[:, h:h + 1]
        m_prev = m_ref[h][0:1, 0:1]
        dmat = jnp.where(causal, b_row + u_col, NEG)
        g_row = b_row + m_prev
        m_h = jnp.maximum(g_row, jnp.max(dmat, axis=0, keepdims=True))
        sw.append((scores[h] * jnp.exp(dmat - m_h)).astype(BF16))
        m_t.append(m_h)
        inter.append(jnp.exp(g_row - m_h))
        m_loc = jnp.max(rows[h] + b_last, axis=1, keepdims=True)
        m_new = jnp.maximum(b_last + m_prev, m_loc)
        decay.append(jnp.exp(b_last + m_prev - m_new))
        w_col = jnp.exp(u_col + b_last - m_new)
        kw.append((ks[h].astype(F32) * w_col).astype(BF16))
        m_ref[h] = jnp.broadcast_to(m_new, m_ref.shape[1:])

    v_aug = [jnp.concatenate([v_ref[sl, :], ones_rows], axis=0) for sl in sls]
    intra = [_dot(v_aug[h], sw[h]) for h in heads]
    update = [_dot(v_aug[h], kw[h]) for h in heads]
    for h in heads:
        c_ref[h] = decay[h] * c_prev[h] + update[h]
        tot = intra[h] + inter[h] * carried[h]
        num = tot[0:HEAD_DIM]
        den = tot[HEAD_DIM:HEAD_DIM + 1]
        ht = num / jnp.maximum(jnp.abs(den), jnp.exp(-m_t[h]))
        hn = _rms_rows(ht, gain_ref[sls[h], :])
        out_ref[sls[h], :] = (_sigmoid(o_ref[sls[h], :].astype(F32)) * hn).astype(BF16)


def _mlstm(feat, tok, small, bias_col, gain_col, batch, seq):
    chunk = min(L_MLSTM, seq)
    nc = seq // chunk
    t = batch * seq
    width = MLSTM_HEADS * HEAD_DIM
    blk = lambda row: pl.BlockSpec((width, chunk), lambda b, c: (row // width, b * nc + c))
    return pl.pallas_call(
        functools.partial(_mlstm_kernel, chunk=chunk),
        grid=(batch, nc),
        in_specs=[blk(ROW_MQ), blk(ROW_MV), blk(ROW_MO),
                  pl.BlockSpec((chunk, width), lambda b, c: (b * nc + c, COL_MK // width)),
                  pl.BlockSpec((N_SMALL, seq), lambda b, c: (0, b)),
                  pl.BlockSpec((N_SMALL, 1), lambda b, c: (0, 0)),
                  pl.BlockSpec((width, 1), lambda b, c: (0, 0))],
        out_specs=pl.BlockSpec((width, chunk), lambda b, c: (0, b * nc + c)),
        out_shape=jax.ShapeDtypeStruct((width, t), BF16),
        scratch_shapes=[pltpu.VMEM((MLSTM_HEADS, 2 * HEAD_DIM, HEAD_DIM), F32),
                        pltpu.VMEM((MLSTM_HEADS, 8, 128), F32),
                        pltpu.VMEM((N_SMALL, seq), F32),
                        pltpu.VMEM((8, seq), F32),
                        pltpu.VMEM((seq, 128), F32)],
        compiler_params=pltpu.CompilerParams(
            dimension_semantics=("parallel", "arbitrary"), vmem_limit_bytes=VMEM_LIMIT),
        name="mlstm",
    )(feat, feat, feat, tok, small, bias_col, gain_col)


def _swa_kernel(q_ref, kc_ref, kp_ref, vc_ref, vp_ref, bias_ref, sink_ref, gq_ref, gk_ref, out_ref, *, nblk):
    w = WINDOW
    key = lax.broadcasted_iota(jnp.int32, (2 * w, SWA_GROUP * w), 0)
    first_block_pad = jnp.logical_and(pl.program_id(1) == 0, key < w)
    kvs = range(SWA_KV_HEADS)
    ksl = [slice(kv * HEAD_DIM, (kv + 1) * HEAD_DIM) for kv in kvs]
    kt = [jnp.concatenate([_rms_rows(kp_ref[s, :].astype(F32), gk_ref[...]),
                           _rms_rows(kc_ref[s, :].astype(F32), gk_ref[...])], axis=1).astype(BF16) for s in ksl]
    vt = [jnp.concatenate([vp_ref[s, :], vc_ref[s, :]], axis=1) for s in ksl]

    bands = [(blk, kv) for blk in range(nblk) for kv in kvs]
    scores = []
    for blk, kv in bands:
        qs = []
        for g in range(SWA_GROUP):
            hs = slice((kv * SWA_GROUP + g) * HEAD_DIM, (kv * SWA_GROUP + g + 1) * HEAD_DIM)
            q = q_ref[hs, blk * w:(blk + 1) * w].astype(F32)
            qs.append((_rms_rows(q, gq_ref[...]) * QK_SCALE).astype(BF16))
        qt = jnp.concatenate(qs, axis=1)
        s = _dot_tn(kt[kv][:, blk * w:(blk + 2) * w], qt) + bias_ref[kv]
        scores.append(jnp.where(first_block_pad, NEG, s) if blk == 0 else s)
    probs, denoms = [], []
    for (blk, kv), s in zip(bands, scores):
        sink = sink_ref[kv]
        m = jnp.maximum(jnp.max(s, axis=0, keepdims=True), sink)
        p = jnp.exp(s - m)
        denoms.append(jnp.sum(p, axis=0, keepdims=True) + jnp.exp(sink - m))
        probs.append(p.astype(BF16))
    outs = [_dot(vt[kv][:, blk * w:(blk + 2) * w], p) for (blk, kv), p in zip(bands, probs)]
    for (blk, kv), o, d in zip(bands, outs, denoms):
        o = o / d
        for g in range(SWA_GROUP):
            hs = slice((kv * SWA_GROUP + g) * HEAD_DIM, (kv * SWA_GROUP + g + 1) * HEAD_DIM)
            out_ref[hs, blk * w:(blk + 1) * w] = o[:, g * w:(g + 1) * w].astype(BF16)


def _swa(feat, bias_t, sink_rows, q_gain_col, k_gain_col, batch, seq):
    w = WINDOW
    nblk = SWA_BLOCKS
    span = nblk * w
    ns = seq // span
    t = batch * seq
    qw = SWA_Q_HEADS * HEAD_DIM
    kw = SWA_KV_HEADS * HEAD_DIM
    cur = lambda row: pl.BlockSpec((kw, span), lambda b, n: (row // kw, b * ns + n))
    prev = lambda row: pl.BlockSpec(
        (kw, w), lambda b, n: (row // kw, (b * ns + n) * nblk - jnp.minimum(n, 1)))
    return pl.pallas_call(
        functools.partial(_swa_kernel, nblk=nblk),
        grid=(batch, ns),
        in_specs=[pl.BlockSpec((qw, span), lambda b, n: (ROW_SQ // qw, b * ns + n)),
                  cur(ROW_SK), prev(ROW_SK), cur(ROW_SV), prev(ROW_SV),
                  pl.BlockSpec(bias_t.shape, lambda b, n: (0, 0, 0)),
                  pl.BlockSpec(sink_rows.shape, lambda b, n: (0, 0, 0)),
                  pl.BlockSpec((HEAD_DIM, 1), lambda b, n: (0, 0)),
                  pl.BlockSpec((HEAD_DIM, 1), lambda b, n: (0, 0))],
        out_specs=pl.BlockSpec((qw, span), lambda b, n: (0, b * ns + n)),
        out_shape=jax.ShapeDtypeStruct((qw, t), BF16),
        compiler_params=pltpu.CompilerParams(
            dimension_semantics=("parallel", "arbitrary"), vmem_limit_bytes=VMEM_LIMIT),
        name="swa_attention",
    )(feat, feat, feat, feat, feat, bias_t, sink_rows, q_gain_col, k_gain_col)


def _merge_ffn_kernel(x_ref, yf_ref, yc_ref, ym_ref, ys_ref, g_ref,
                      wf_ref, wc_ref, wm_ref, ws_ref, wo_ref, gain_ref, wg_ref, wu_ref, wd_ref,
                      out_ref, *, sub):
    bounds = list(range(0, D_FF, TF_FFN)) + [D_FF]
    for r in range(0, x_ref.shape[0], sub):
        rows = slice(r, r + sub)

        def gated(branch, y):
            g = g_ref[rows, branch * D_MODEL:(branch + 1) * D_MODEL]
            return (1.0 + jnp.tanh(g.astype(F32))) * y

        merged = gated(0, _dot_tn(yf_ref[:, rows], wf_ref[...]))
        merged += gated(1, _dot(yc_ref[rows, :], wc_ref[...]))
        merged += gated(2, _dot_tn(ym_ref[:, rows], wm_ref[...]))
        merged += gated(3, _dot_tn(ys_ref[:, rows], ws_ref[...]))
        x = x_ref[rows, :] + _dot(merged.astype(BF16), wo_ref[...])

        ms = jnp.mean(x * x, axis=-1, keepdims=True)
        hn = (x * lax.rsqrt(ms + EPS) * gain_ref[...]).astype(BF16)
        acc = x
        for lo, hi in zip(bounds[:-1], bounds[1:]):
            h = _dot(hn, wg_ref[:, lo:hi])
            act = (h * (1.0 + jnp.tanh(h)) * _dot(hn, wu_ref[:, lo:hi])).astype(BF16)
            acc = acc + _dot(act, wd_ref[lo:hi, :])
        out_ref[rows, :] = acc


def _merge_ffn(x2, y_fox_t, y_conv, y_mlstm_t, y_swa_t, gates, w_fox, w_conv, w_mlstm, w_swa, w_out,
               gain, w_gate_half, w_up, w_down, layer):
    t = x2.shape[0]
    tm = min(TM_MERGE, t)
    feat_blk = lambda a: pl.BlockSpec((a.shape[0], tm), lambda i: (0, i))
    full = lambda a: _resident(a.shape[1:], layer)
    return pl.pallas_call(
        functools.partial(_merge_ffn_kernel, sub=min(SUB_FFN, tm)),
        grid=(t // tm,),
        in_specs=[pl.BlockSpec((tm, D_MODEL), lambda i: (i, 0)),
                  feat_blk(y_fox_t),
                  pl.BlockSpec((tm, y_conv.shape[1]), lambda i: (i, 0)),
                  feat_blk(y_mlstm_t), feat_blk(y_swa_t),
                  pl.BlockSpec((tm, gates.shape[1]), lambda i: (i, 0)),
                  full(w_fox), full(w_conv), full(w_mlstm), full(w_swa), full(w_out),
                  full(gain), full(w_gate_half), full(w_up), full(w_down)],
        out_specs=pl.BlockSpec((tm, D_MODEL), lambda i: (i, 0)),
        out_shape=jax.ShapeDtypeStruct((t, D_MODEL), F32),
        compiler_params=pltpu.CompilerParams(
            dimension_semantics=("parallel",), vmem_limit_bytes=VMEM_LIMIT),
        name="merge_ffn",
    )(x2, y_fox_t, y_conv, y_mlstm_t, y_swa_t, gates,
      w_fox, w_conv, w_mlstm, w_swa, w_out, gain, w_gate_half, w_up, w_down)


IN_FQKV, IN_FF, IN_CONV, IN_MQ, IN_MK, IN_MV = 0, 768, 772, 1540, 1796, 2052
IN_MI, IN_MF, IN_MO, IN_SQ, IN_SK, IN_SV, IN_GATES, IN_COLS = 2308, 2312, 2316, 2572, 3084, 3212, 3340, 7436
TK_PREP = 256


def _wprep_kernel(w_ref, tok_ref, feat_ref):
    tok_ref[:, COL_CU:COL_MK] = w_ref[:, IN_CONV:IN_MQ].astype(BF16)
    tok_ref[:, COL_MK:COL_GATES] = w_ref[:, IN_MK:IN_MV].astype(BF16)
    tok_ref[:, COL_GATES:] = (0.5 * w_ref[:, IN_GATES:IN_COLS]).astype(BF16)
    for row, lo, hi in ((ROW_FQ, IN_FQKV, IN_FF), (ROW_MQ, IN_MQ, IN_MK), (ROW_MV, IN_MV, IN_MI),
                        (ROW_MO, IN_MO, IN_SQ), (ROW_SQ, IN_SQ, IN_GATES)):
        feat_ref[row:row + hi - lo, :] = w_ref[:, lo:hi].T.astype(BF16)
    fox_f = w_ref[:, IN_FF:IN_FF + 128].T[0:N_SMALL]
    mlstm_if = w_ref[:, IN_MI - 4:IN_MI + 124].T[0:N_SMALL]
    row = lax.broadcasted_iota(jnp.int32, fox_f.shape, 0)
    small = jnp.where(row < 4, fox_f, jnp.where(row < 12, mlstm_if, 0.0))
    feat_ref[N_FEAT:, :] = small.astype(BF16)


def _prep_w_in(w_in):
    depth, d, cols = w_in.shape
    return pl.pallas_call(
        _wprep_kernel,
        grid=(depth, d // TK_PREP),
        in_specs=[pl.BlockSpec((None, TK_PREP, cols), lambda l, j: (l, j, 0))],
        out_specs=[pl.BlockSpec((None, TK_PREP, N_TOK), lambda l, j: (l, j, 0)),
                   pl.BlockSpec((None, N_FEAT + N_SMALL, TK_PREP), lambda l, j: (l, 0, j))],
        out_shape=[jax.ShapeDtypeStruct((depth, d, N_TOK), BF16),
                   jax.ShapeDtypeStruct((depth, N_FEAT + N_SMALL, d), BF16)],
        compiler_params=pltpu.CompilerParams(
            dimension_semantics=("parallel", "parallel"), vmem_limit_bytes=VMEM_LIMIT),
        name="w_in_relayout",
    )(w_in)


def _col(v, n=None):
    v = v.astype(F32).reshape(-1, 1)
    if n is not None and v.shape[0] < n:
        v = jnp.concatenate([v, jnp.zeros((n - v.shape[0], 1), F32)], axis=0)
    return v


def kernel(x, rel_bias, attn_norm, w_in, fox_f_bias, fox_q_gain, fox_k_gain, conv_w, mlstm_i_bias, mlstm_f_bias, mlstm_h_gain, swa_q_gain, swa_k_gain, swa_sinks, w_fox_out, w_conv_out, w_mlstm_out, w_swa_out, w_merge_out, ffn_norm, w_gate, w_up, w_down):
    batch, seq, _ = x.shape
    depth = w_in.shape[0]
    x2 = x.reshape(batch * seq, D_MODEL)
    bias_t = _swa_bias_table(rel_bias)

    w_tok, w_feat_t = _prep_w_in(w_in)
    attn_gain = attn_norm.reshape(depth, 1, D_MODEL)
    ffn_gain = ffn_norm.reshape(depth, 1, D_MODEL)
    w_fox_b, w_conv_b, w_mlstm_b, w_swa_b = (w.astype(BF16) for w in (w_fox_out, w_conv_out, w_mlstm_out, w_swa_out))
    w_merge_half = (0.5 * w_merge_out).astype(BF16)
    w_gate_half = (0.5 * w_gate).astype(BF16)
    w_up_b = w_up.astype(BF16)
    w_down_b = w_down.astype(BF16)

    for l in range(depth):
        tok, gates, feat, small = _inproj(x2, attn_gain, w_tok, w_feat_t, l)

        y_fox_t = _fox(feat, small, _col(fox_f_bias[l], N_SMALL), _col(fox_q_gain[l]), _col(fox_k_gain[l]),
                       batch, seq)
        y_conv = _conv(tok, conv_w[l], batch, seq)
        gate_bias = jnp.concatenate([jnp.zeros((4,), F32), mlstm_i_bias[l], mlstm_f_bias[l]])
        y_mlstm_t = _mlstm(feat, tok, small, _col(gate_bias, N_SMALL), _col(mlstm_h_gain[l]), batch, seq)
        sink_rows = jnp.broadcast_to(
            swa_sinks[l].astype(F32).reshape(SWA_KV_HEADS, 1, SWA_GROUP, 1),
            (SWA_KV_HEADS, 1, SWA_GROUP, WINDOW)).reshape(SWA_KV_HEADS, 1, SWA_GROUP * WINDOW)
        y_swa_t = _swa(feat, bias_t, sink_rows, _col(swa_q_gain[l]), _col(swa_k_gain[l]), batch, seq)

        x2 = _merge_ffn(x2, y_fox_t, y_conv, y_mlstm_t, y_swa_t, gates,
                        w_fox_b, w_conv_b, w_mlstm_b, w_swa_b, w_merge_half,
                        ffn_gain, w_gate_half, w_up_b, w_down_b, l)
    return x2.reshape(batch, seq, D_MODEL)
```

```python
import functools

import numpy as np
import jax
import jax.numpy as jnp
from jax import lax
from jax.experimental import pallas as pl
from jax.experimental.pallas import tpu as pltpu

F32 = jnp.float32
BF16 = jnp.bfloat16

D_MODEL = 1024
HEAD_DIM = 64
FOX_HEADS = 4
MLSTM_HEADS = 4
SWA_Q_HEADS = 8
SWA_KV_HEADS = 2
SWA_GROUP = SWA_Q_HEADS // SWA_KV_HEADS
WINDOW = 128
REL_BUCKETS = 32
REL_MAX_DIST = 128
D_FF = 2816
EPS = 1e-6
NEG = -1e30
QK_SCALE = HEAD_DIM ** -0.5
LOG2E = 1.4426950408889634

ROW_FQ, ROW_FK, ROW_FV = 0, 256, 512
ROW_MQ, ROW_MK, ROW_MV = 768, 1024, 1280
ROW_SQ, ROW_MO, ROW_SK, ROW_SV = 1536, 2048, 2304, 2432
N_FEAT = 2560
N_SMALL = 16
COL_CU, COL_CB, COL_CC, COL_GATES = 0, 256, 512, 768
N_TOK = 4864

TM_IN = 512
TN_IN = 512
FEAT_CHUNK = 640
TQ_FOX = 256
TK_FOX = 256
L_MLSTM = 256
SWA_BLOCKS = 4
TM_MERGE = 512
TM_FFN = 512
SUB_FFN = 512
TF_FFN = 512
VMEM_LIMIT = 56 * 1024 * 1024


def _dot(a, b):
    return jnp.dot(a, b, preferred_element_type=F32)


def _dot_nt(a, b):
    return lax.dot_general(a, b, (((1,), (1,)), ((), ())), preferred_element_type=F32)


def _dot_tn(a, b):
    return lax.dot_general(a, b, (((0,), (0,)), ((), ())), preferred_element_type=F32)


def _sigmoid(x):
    return 0.5 * jnp.tanh(0.5 * x) + 0.5


def _log_sigmoid(x):
    return jnp.minimum(x, 0.0) - jnp.log(1.0 + jnp.exp(-jnp.abs(x)))


def _cumsum_lanes(x, segment=None):
    n = segment or x.shape[-1]
    lane = lax.broadcasted_iota(jnp.int32, x.shape, x.ndim - 1) & (n - 1)
    k = 1
    while k < n:
        x = x + jnp.where(lane >= k, pltpu.roll(x, k, x.ndim - 1), 0.0)
        k *= 2
    return x


def _row_select(rows, n_rows, width):
    rid = lax.broadcasted_iota(jnp.int32, (n_rows, width), 0)
    out = jnp.zeros((n_rows, width), F32)
    for r, v in enumerate(rows):
        out = jnp.where(rid == r, v, out)
    return out


def _split3(c):
    hi = c.astype(BF16).astype(F32)
    r = c - hi
    mid = r.astype(BF16).astype(F32)
    lo = (r - mid).astype(BF16).astype(F32)
    return hi, mid, lo


def _rms_rows(xt, gain_col):
    ms = jnp.mean(xt * xt, axis=0, keepdims=True)
    return xt * lax.rsqrt(ms + EPS) * gain_col


def _bias_kernel(rb_ref, idx_ref, out_ref):
    kv = pl.program_id(0)
    g = pl.program_id(1)
    head = kv * SWA_GROUP + g
    idx = idx_ref[...]
    acc = jnp.full(idx.shape, NEG, F32)
    for b in range(REL_BUCKETS):
        acc = jnp.where(idx == b, rb_ref[b * SWA_Q_HEADS + head], acc)
    out_ref[0] = acc


def _bucket_table():
    j = np.arange(2 * WINDOW)[:, None]
    i = np.arange(WINDOW)[None, :]
    dist = i + WINDOW - j
    n = np.maximum(dist, 0).astype(np.int32)
    max_exact = REL_BUCKETS // 2
    nf = np.maximum(n, 1).astype(np.float32)
    large = max_exact + (np.log(nf / np.float32(max_exact)) / np.float32(np.log(REL_MAX_DIST / max_exact))
                         * np.float32(REL_BUCKETS - max_exact)).astype(np.int32)
    large = np.minimum(large, REL_BUCKETS - 1)
    bucket = np.where(n < max_exact, n, large)
    return np.where((dist >= 0) & (dist < WINDOW), bucket, -1).astype(np.int32)


def _swa_bias_table(rel_bias):
    idx = jnp.asarray(_bucket_table())
    return pl.pallas_call(
        _bias_kernel,
        grid=(SWA_KV_HEADS, SWA_GROUP),
        in_specs=[
            pl.BlockSpec(memory_space=pltpu.SMEM),
            pl.BlockSpec((2 * WINDOW, WINDOW), lambda kv, g: (0, 0)),
        ],
        out_specs=pl.BlockSpec((1, 2 * WINDOW, WINDOW), lambda kv, g: (kv, 0, g)),
        out_shape=jax.ShapeDtypeStruct((SWA_KV_HEADS, 2 * WINDOW, SWA_GROUP * WINDOW), F32),
        name="swa_bias_table",
    )(rel_bias.reshape(-1), idx)


def _inproj_kernel(x_ref, g_ref, wt_ref, wf_ref, tok_ref, gates_ref, feat_ref, small_ref):
    x = x_ref[...]
    ms = jnp.mean(x * x, axis=-1, keepdims=True)
    xn = (x * lax.rsqrt(ms + EPS) * g_ref[...]).astype(BF16)
    for out_ref, base, width in ((tok_ref, COL_CU, COL_GATES), (gates_ref, COL_GATES, N_TOK - COL_GATES)):
        for lo in range(0, width, TN_IN):
            hi = min(lo + TN_IN, width)
            out_ref[:, lo:hi] = _dot(xn, wt_ref[:, base + lo:base + hi]).astype(BF16)
    n_chunks = N_FEAT // FEAT_CHUNK
    for c in range(n_chunks):
        lo = c * FEAT_CHUNK
        hi = lo + FEAT_CHUNK + (N_SMALL if c == n_chunks - 1 else 0)
        r = _dot_nt(wf_ref[lo:hi, :], xn)
        feat_ref[lo:lo + FEAT_CHUNK, :] = r[:FEAT_CHUNK].astype(BF16)
        if c == n_chunks - 1:
            small_ref[...] = r[FEAT_CHUNK:]


def _resident(shape, layer=None):
    if layer is None:
        return pl.BlockSpec(shape, lambda *_: (0,) * len(shape), pipeline_mode=pl.Buffered(1))
    return pl.BlockSpec((None,) + tuple(shape), lambda *_: (layer,) + (0,) * len(shape),
                        pipeline_mode=pl.Buffered(1))


def _inproj(x2, gain, w_tok, w_feat_t, layer):
    t = x2.shape[0]
    tm = min(TM_IN, t)
    return pl.pallas_call(
        _inproj_kernel,
        grid=(t // tm,),
        in_specs=[
            pl.BlockSpec((tm, D_MODEL), lambda i: (i, 0)),
            _resident((1, D_MODEL), layer),
            _resident((D_MODEL, N_TOK), layer),
            _resident((N_FEAT + N_SMALL, D_MODEL), layer),
        ],
        out_specs=[
            pl.BlockSpec((tm, COL_GATES), lambda i: (i, 0)),
            pl.BlockSpec((tm, N_TOK - COL_GATES), lambda i: (i, 0)),
            pl.BlockSpec((N_FEAT, tm), lambda i: (0, i)),
            pl.BlockSpec((N_SMALL, tm), lambda i: (0, i)),
        ],
        out_shape=[
            jax.ShapeDtypeStruct((t, COL_GATES), BF16),
            jax.ShapeDtypeStruct((t, N_TOK - COL_GATES), BF16),
            jax.ShapeDtypeStruct((N_FEAT, t), BF16),
            jax.ShapeDtypeStruct((N_SMALL, t), F32),
        ],
        compiler_params=pltpu.CompilerParams(
            dimension_semantics=("parallel",), vmem_limit_bytes=VMEM_LIMIT),
        name="inproj",
    )(x2, gain, w_tok, w_feat_t)


def _fox_kernel(fq_ref, fk_ref, fv_ref, small_ref, fb_ref, gq_ref, gk_ref, out_ref,
                qa_ref, ka_ref, s_ref, p_ref, acc_ref, m_ref, l_ref, a_ref, *, seq, tq, tk):
    i = pl.program_id(1)
    heads = range(FOX_HEADS)
    ratio = tq // tk

    @pl.when(i == 0)
    def _():
        c_all = _cumsum_lanes(_log_sigmoid(small_ref[...] + fb_ref[...]))
        for h in heads:
            sl = slice(h * HEAD_DIM, (h + 1) * HEAD_DIM)
            c = c_all[h:h + 1, :] * LOG2E
            hi, mid, lo = _split3(c)
            one = jnp.ones_like(c)
            qn = _rms_rows(fq_ref[sl, :].astype(F32), gq_ref[...]) * (QK_SCALE * LOG2E)
            qa_ref[h, 0:HEAD_DIM, :] = qn.astype(BF16)
            qa_ref[h, HEAD_DIM:, :] = _row_select([hi, mid, lo, one, one, one], HEAD_DIM, seq).astype(BF16)
            kn = _rms_rows(fk_ref[sl, :].astype(F32), gk_ref[...])
            k_aug = _row_select([one, one, one, -hi, -mid, -lo], HEAD_DIM, seq)
            ka_ref[h] = jnp.concatenate([kn, k_aug], axis=0).T.astype(BF16)

    q_off = pl.multiple_of(i * tq, tq)
    qa = [qa_ref[h, :, pl.ds(q_off, tq)] for h in heads]

    m_ref[...] = jnp.full(m_ref.shape, NEG, F32)
    l_ref[...] = jnp.zeros_like(l_ref)
    acc_ref[...] = jnp.zeros_like(acc_ref)
    a_ref[...] = jnp.ones_like(a_ref)
    p_ref[...] = jnp.zeros_like(p_ref)

    def qk(j, slot):
        off = pl.multiple_of(j * tk, tk)
        for h in heads:
            s_ref[slot, h] = _dot(ka_ref[h, pl.ds(off, tk), :], qa[h])

    def pv(j):
        off = pl.multiple_of(j * tk, tk)
        for h in heads:
            v = fv_ref[h * HEAD_DIM:(h + 1) * HEAD_DIM, pl.ds(off, tk)]
            acc_ref[h] = a_ref[h] * acc_ref[h] + _dot(v, p_ref[h])

    def softmax(slot, diag=None):
        for h in heads:
            s = s_ref[slot, h]
            if diag is not None:
                ahead = lax.broadcasted_iota(jnp.int32, s.shape, 1) - lax.broadcasted_iota(jnp.int32, s.shape, 0)
                s = jnp.where(ahead >= diag * tk, s, NEG)
            m = m_ref[h]
            m_new = jnp.maximum(m, jnp.max(s, axis=0, keepdims=True))
            p = jnp.exp2(s - m_new)
            alpha = jnp.exp2(m - m_new)
            l_ref[h] = alpha * l_ref[h] + jnp.sum(p, axis=0, keepdims=True)
            m_ref[h] = m_new
            a_ref[h] = alpha
            p_ref[h] = p.astype(BF16)

    def stage(j, slot, diag=None):
        qk(j + 1, 1 - slot)
        pv(jnp.maximum(j - 1, 0))
        softmax(slot, diag)

    qk(0, 0)
    first_diag = ratio * i

    @pl.loop(0, first_diag // 2)
    def _(k):
        stage(2 * k, 0)
        stage(2 * k + 1, 1)

    def tail(parity):
        if parity == 1:
            stage(first_diag - 1, 0)
        for r in range(ratio - 1):
            stage(first_diag + r, (parity + r) % 2, diag=r)
        last = first_diag + ratio - 1
        pv(jnp.maximum(last - 1, 0))
        softmax((parity + ratio - 1) % 2, diag=ratio - 1)
        pv(last)

    if ratio % 2 == 0:
        tail(0)
    else:
        odd = lax.rem(first_diag, 2) == 1
        pl.when(odd)(lambda: tail(1))
        pl.when(jnp.logical_not(odd))(lambda: tail(0))

    for h in heads:
        out_ref[h * HEAD_DIM:(h + 1) * HEAD_DIM, :] = (acc_ref[h] / l_ref[h]).astype(BF16)


def _fox(feat, small, f_bias_col, q_gain_col, k_gain_col, batch, seq):
    tq = min(TQ_FOX, seq)
    tk = min(TK_FOX, tq)
    nq = seq // tq
    t = batch * seq
    width = FOX_HEADS * HEAD_DIM
    blk = lambda row: pl.BlockSpec((width, seq), lambda b, i: (row // width, b))
    col = lambda n: pl.BlockSpec((n, 1), lambda b, i: (0, 0))
    return pl.pallas_call(
        functools.partial(_fox_kernel, seq=seq, tq=tq, tk=tk),
        grid=(batch, nq),
        in_specs=[blk(ROW_FQ), blk(ROW_FK), blk(ROW_FV),
                  pl.BlockSpec((N_SMALL, seq), lambda b, i: (0, b)),
                  col(N_SMALL), col(HEAD_DIM), col(HEAD_DIM)],
        out_specs=pl.BlockSpec((width, tq), lambda b, i: (0, b * nq + i)),
        out_shape=jax.ShapeDtypeStruct((width, t), BF16),
        scratch_shapes=[pltpu.VMEM((FOX_HEADS, 2 * HEAD_DIM, seq), BF16),
                        pltpu.VMEM((FOX_HEADS, seq, 2 * HEAD_DIM), BF16),
                        pltpu.VMEM((2, FOX_HEADS, tk, tq), F32),
                        pltpu.VMEM((FOX_HEADS, tk, tq), BF16),
                        pltpu.VMEM((FOX_HEADS, HEAD_DIM, tq), F32),
                        pltpu.VMEM((FOX_HEADS, 1, tq), F32),
                        pltpu.VMEM((FOX_HEADS, 1, tq), F32),
                        pltpu.VMEM((FOX_HEADS, 1, tq), F32)],
        compiler_params=pltpu.CompilerParams(
            dimension_semantics=("parallel", "arbitrary"), vmem_limit_bytes=VMEM_LIMIT),
        name="fox_attention",
    )(feat, feat, feat, small, f_bias_col, q_gain_col, k_gain_col)


def _short_conv_tile(ct_ref, w_ref, carry_ref, sequence_start):
    width = w_ref.shape[1]
    u = ct_ref[:, COL_CU:COL_CU + width].astype(F32)
    b_gate = ct_ref[:, COL_CB:COL_CB + width].astype(F32)
    c_gate = ct_ref[:, COL_CC:COL_CC + width].astype(F32)
    z = c_gate * u
    prev = jnp.where(sequence_start, 0.0, carry_ref[...])
    carry_ref[...] = z[z.shape[0] - 8:, :]
    last1, last2 = prev[7:8, :], prev[6:7, :]
    row = lax.broadcasted_iota(jnp.int32, z.shape, 0)
    z1 = jnp.where(row >= 1, pltpu.roll(z, 1, 0), last1)
    z2 = jnp.where(row >= 2, pltpu.roll(z, 2, 0), jnp.where(row == 1, last1, last2))
    w = w_ref[...]
    return (b_gate * (w[0:1, :] * z2 + w[1:2, :] * z1 + w[2:3, :] * z)).astype(BF16)


def _mlstm_kernel(q_ref, k_ref, v_ref, o_ref, small_ref, bias_ref, gain_ref, out_ref,
                  c_ref, m_ref, b_ref, u_ref, ut_ref, *, chunk):
    @pl.when(pl.program_id(1) == 0)
    def _():
        c_ref[...] = jnp.zeros_like(c_ref)
        m_ref[...] = jnp.zeros_like(m_ref)
        g = small_ref[...] + bias_ref[...]
        b_all = _cumsum_lanes(_log_sigmoid(g), segment=chunk)
        b_ref[...] = b_all
        u_all = _row_select([g[4 + h:5 + h, :] - b_all[8 + h:9 + h, :] for h in range(MLSTM_HEADS)],
                            8, g.shape[1])
        u_ref[...] = u_all
        ut_ref[...] = jnp.concatenate([u_all, jnp.zeros((120, g.shape[1]), F32)], axis=0).T

    off = pl.multiple_of(pl.program_id(1) * chunk, chunk)
    cols = pl.ds(off, chunk)
    bcum = b_ref[:, cols]
    rows = [u_ref[h:h + 1, cols] for h in range(MLSTM_HEADS)]
    u_cols = ut_ref[cols, :]

    src = lax.broadcasted_iota(jnp.int32, (chunk, chunk), 0)
    tgt = lax.broadcasted_iota(jnp.int32, (chunk, chunk), 1)
    causal = src <= tgt
    ones_rows = (lax.broadcasted_iota(jnp.int32, (HEAD_DIM, chunk), 0) == 0).astype(BF16)

    heads = range(MLSTM_HEADS)
    sls = [slice(h * HEAD_DIM, (h + 1) * HEAD_DIM) for h in heads]
    qs = [(q_ref[sl, cols].astype(F32) * QK_SCALE).astype(BF16) for sl in sls]
    ks = [k_ref[sl, cols] for sl in sls]
    c_prev = [c_ref[h] for h in heads]
    scores = [_dot_tn(ks[h], qs[h]) for h in heads]
    carried = [_dot(c_prev[h].astype(BF16), qs[h]) for h in heads]

    sw, kw, m_t, inter, decay = [], [], [], [], []
    for h in heads:
        b_row = bcum[8 + h:9 + h, :]
        b_last = b_row[:, chunk - 1:chunk]
        u_col = u_cols[:, h:h + 1]
        m_prev = m_ref[h][0:1, 0:1]
        dmat = jnp.where(causal, b_row + u_col, NEG)
        g_row = b_row + m_prev
        m_h = jnp.maximum(g_row, jnp.max(dmat, axis=0, keepdims=True))
        sw.append((scores[h] * jnp.exp(dmat - m_h)).astype(BF16))
        m_t.append(m_h)
        inter.append(jnp.exp(g_row - m_h))
        m_loc = jnp.max(rows[h] + b_last, axis=1, keepdims=True)
        m_new = jnp.maximum(b_last + m_prev, m_loc)
        decay.append(jnp.exp(b_last + m_prev - m_new))
        w_row = jnp.exp(rows[h] + b_last - m_new)
        kw.append((ks[h].astype(F32) * w_row).astype(BF16))
        m_ref[h] = jnp.broadcast_to(m_new, m_ref.shape[1:])

    v_aug = [jnp.concatenate([v_ref[sl, cols], ones_rows], axis=0) for sl in sls]
    intra = [_dot(v_aug[h], sw[h]) for h in heads]
    update = [_dot_nt(v_aug[h], kw[h]) for h in heads]
    for h in heads:
        c_ref[h] = decay[h] * c_prev[h] + update[h]
        tot = intra[h] + inter[h] * carried[h]
        num = tot[0:HEAD_DIM]
        den = tot[HEAD_DIM:HEAD_DIM + 1]
        ht = num / jnp.maximum(jnp.abs(den), jnp.exp(-m_t[h]))
        hn = _rms_rows(ht, gain_ref[sls[h], :])
        out_ref[sls[h], cols] = (_sigmoid(o_ref[sls[h], cols].astype(F32)) * hn).astype(BF16)


def _mlstm(feat, small, bias_col, gain_col, batch, seq):
    chunk = min(L_MLSTM, seq)
    nc = seq // chunk
    t = batch * seq
    width = MLSTM_HEADS * HEAD_DIM
    blk = lambda row: pl.BlockSpec((width, seq), lambda b, c: (row // width, b))
    return pl.pallas_call(
        functools.partial(_mlstm_kernel, chunk=chunk),
        grid=(batch, nc),
        in_specs=[blk(ROW_MQ), blk(ROW_MK), blk(ROW_MV), blk(ROW_MO),
                  pl.BlockSpec((N_SMALL, seq), lambda b, c: (0, b)),
                  pl.BlockSpec((N_SMALL, 1), lambda b, c: (0, 0)),
                  pl.BlockSpec((width, 1), lambda b, c: (0, 0))],
        out_specs=pl.BlockSpec((width, seq), lambda b, c: (0, b)),
        out_shape=jax.ShapeDtypeStruct((width, t), BF16),
        scratch_shapes=[pltpu.VMEM((MLSTM_HEADS, 2 * HEAD_DIM, HEAD_DIM), F32),
                        pltpu.VMEM((MLSTM_HEADS, 8, 128), F32),
                        pltpu.VMEM((N_SMALL, seq), F32),
                        pltpu.VMEM((8, seq), F32),
                        pltpu.VMEM((seq, 128), F32)],
        compiler_params=pltpu.CompilerParams(
            dimension_semantics=("parallel", "arbitrary"), vmem_limit_bytes=VMEM_LIMIT),
        name="mlstm",
    )(feat, feat, feat, feat, small, bias_col, gain_col)


def _swa_kernel(q_ref, kc_ref, kp_ref, vc_ref, vp_ref, bias_ref, sink_ref, gq_ref, gk_ref, out_ref, *, nblk):
    w = WINDOW
    key = lax.broadcasted_iota(jnp.int32, (2 * w, SWA_GROUP * w), 0)
    first_block_pad = jnp.logical_and(pl.program_id(1) == 0, key < w)
    kvs = range(SWA_KV_HEADS)
    ksl = [slice(kv * HEAD_DIM, (kv + 1) * HEAD_DIM) for kv in kvs]
    kt = [jnp.concatenate([_rms_rows(kp_ref[s, :].astype(F32), gk_ref[...]),
                           _rms_rows(kc_ref[s, :].astype(F32), gk_ref[...])], axis=1).astype(BF16) for s in ksl]
    vt = [jnp.concatenate([vp_ref[s, :], vc_ref[s, :]], axis=1) for s in ksl]

    bands = [(blk, kv) for blk in range(nblk) for kv in kvs]
    scores = []
    for blk, kv in bands:
        qs = []
        for g in range(SWA_GROUP):
            hs = slice((kv * SWA_GROUP + g) * HEAD_DIM, (kv * SWA_GROUP + g + 1) * HEAD_DIM)
            q = q_ref[hs, blk * w:(blk + 1) * w].astype(F32)
            qs.append((_rms_rows(q, gq_ref[...]) * QK_SCALE).astype(BF16))
        qt = jnp.concatenate(qs, axis=1)
        s = _dot_tn(kt[kv][:, blk * w:(blk + 2) * w], qt) + bias_ref[kv]
        scores.append(jnp.where(first_block_pad, NEG, s) if blk == 0 else s)
    probs, denoms = [], []
    for (blk, kv), s in zip(bands, scores):
        sink = sink_ref[kv]
        m = jnp.maximum(jnp.max(s, axis=0, keepdims=True), sink)
        p = jnp.exp(s - m)
        denoms.append(jnp.sum(p, axis=0, keepdims=True) + jnp.exp(sink - m))
        probs.append(p.astype(BF16))
    outs = [_dot(vt[kv][:, blk * w:(blk + 2) * w], p) for (blk, kv), p in zip(bands, probs)]
    for (blk, kv), o, d in zip(bands, outs, denoms):
        o = o / d
        for g in range(SWA_GROUP):
            hs = slice((kv * SWA_GROUP + g) * HEAD_DIM, (kv * SWA_GROUP + g + 1) * HEAD_DIM)
            out_ref[hs, blk * w:(blk + 1) * w] = o[:, g * w:(g + 1) * w].astype(BF16)


def _swa(feat, bias_t, sink_rows, q_gain_col, k_gain_col, batch, seq):
    w = WINDOW
    nblk = SWA_BLOCKS
    span = nblk * w
    ns = seq // span
    t = batch * seq
    qw = SWA_Q_HEADS * HEAD_DIM
    kw = SWA_KV_HEADS * HEAD_DIM
    cur = lambda row: pl.BlockSpec((kw, span), lambda b, n: (row // kw, b * ns + n))
    prev = lambda row: pl.BlockSpec(
        (kw, w), lambda b, n: (row // kw, (b * ns + n) * nblk - jnp.minimum(n, 1)))
    return pl.pallas_call(
        functools.partial(_swa_kernel, nblk=nblk),
        grid=(batch, ns),
        in_specs=[pl.BlockSpec((qw, span), lambda b, n: (ROW_SQ // qw, b * ns + n)),
                  cur(ROW_SK), prev(ROW_SK), cur(ROW_SV), prev(ROW_SV),
                  pl.BlockSpec(bias_t.shape, lambda b, n: (0, 0, 0)),
                  pl.BlockSpec(sink_rows.shape, lambda b, n: (0, 0, 0)),
                  pl.BlockSpec((HEAD_DIM, 1), lambda b, n: (0, 0)),
                  pl.BlockSpec((HEAD_DIM, 1), lambda b, n: (0, 0))],
        out_specs=pl.BlockSpec((qw, span), lambda b, n: (0, b * ns + n)),
        out_shape=jax.ShapeDtypeStruct((qw, t), BF16),
        compiler_params=pltpu.CompilerParams(
            dimension_semantics=("parallel", "arbitrary"), vmem_limit_bytes=VMEM_LIMIT),
        name="swa_attention",
    )(feat, feat, feat, feat, feat, bias_t, sink_rows, q_gain_col, k_gain_col)


def _merge_ffn_kernel(x_ref, yf_ref, ct_ref, ym_ref, ys_ref, g_ref, cw_ref,
                      wf_ref, wc_ref, wm_ref, ws_ref, wo_ref, gain_ref, wg_ref, wu_ref, wd_ref,
                      out_ref, carry_ref, *, sub, tiles_per_seq):
    y_conv = _short_conv_tile(ct_ref, cw_ref, carry_ref, lax.rem(pl.program_id(0), tiles_per_seq) == 0)
    bounds = list(range(0, D_FF, TF_FFN)) + [D_FF]
    for r in range(0, x_ref.shape[0], sub):
        rows = slice(r, r + sub)

        def gated(branch, y):
            g = g_ref[rows, branch * D_MODEL:(branch + 1) * D_MODEL]
            return (1.0 + jnp.tanh(g.astype(F32))) * y

        merged = gated(0, _dot_tn(yf_ref[:, rows], wf_ref[...]))
        merged += gated(1, _dot(y_conv[rows, :], wc_ref[...]))
        merged += gated(2, _dot_tn(ym_ref[:, rows], wm_ref[...]))
        merged += gated(3, _dot_tn(ys_ref[:, rows], ws_ref[...]))
        x = x_ref[rows, :] + _dot(merged.astype(BF16), wo_ref[...])

        ms = jnp.mean(x * x, axis=-1, keepdims=True)
        hn = (x * lax.rsqrt(ms + EPS) * gain_ref[...]).astype(BF16)
        acc = x
        for lo, hi in zip(bounds[:-1], bounds[1:]):
            h = _dot(hn, wg_ref[:, lo:hi])
            act = (h * (1.0 + jnp.tanh(h)) * _dot(hn, wu_ref[:, lo:hi])).astype(BF16)
            acc = acc + _dot(act, wd_ref[lo:hi, :])
        out_ref[rows, :] = acc


def _merge_ffn(x2, y_fox_t, conv_tok, y_mlstm_t, y_swa_t, gates, conv_w, w_fox, w_conv, w_mlstm, w_swa,
               w_out, gain, w_gate_half, w_up, w_down, layer, seq):
    t = x2.shape[0]
    tm = min(TM_MERGE, t, seq)
    feat_blk = lambda a: pl.BlockSpec((a.shape[0], tm), lambda i: (0, i))
    full = lambda a: _resident(a.shape[1:], layer)
    return pl.pallas_call(
        functools.partial(_merge_ffn_kernel, sub=min(SUB_FFN, tm), tiles_per_seq=seq // tm),
        grid=(t // tm,),
        in_specs=[pl.BlockSpec((tm, D_MODEL), lambda i: (i, 0)),
                  feat_blk(y_fox_t),
                  pl.BlockSpec((tm, conv_tok.shape[1]), lambda i: (i, 0)),
                  feat_blk(y_mlstm_t), feat_blk(y_swa_t),
                  pl.BlockSpec((tm, gates.shape[1]), lambda i: (i, 0)),
                  full(conv_w),
                  full(w_fox), full(w_conv), full(w_mlstm), full(w_swa), full(w_out),
                  full(gain), full(w_gate_half), full(w_up), full(w_down)],
        out_specs=pl.BlockSpec((tm, D_MODEL), lambda i: (i, 0)),
        out_shape=jax.ShapeDtypeStruct((t, D_MODEL), F32),
        scratch_shapes=[pltpu.VMEM((8, conv_w.shape[2]), F32)],
        compiler_params=pltpu.CompilerParams(
            dimension_semantics=("arbitrary",), vmem_limit_bytes=VMEM_LIMIT),
        name="merge_ffn",
    )(x2, y_fox_t, conv_tok, y_mlstm_t, y_swa_t, gates, conv_w,
      w_fox, w_conv, w_mlstm, w_swa, w_out, gain, w_gate_half, w_up, w_down)


IN_FQKV, IN_FF, IN_CONV, IN_MQ, IN_MK, IN_MV = 0, 768, 772, 1540, 1796, 2052
IN_MI, IN_MF, IN_MO, IN_SQ, IN_SK, IN_SV, IN_GATES, IN_COLS = 2308, 2312, 2316, 2572, 3084, 3212, 3340, 7436
TK_PREP = 256


def _wprep_kernel(w_ref, tok_ref, feat_ref):
    tok_ref[:, COL_CU:COL_GATES] = w_ref[:, IN_CONV:IN_MQ].astype(BF16)
    tok_ref[:, COL_GATES:] = (0.5 * w_ref[:, IN_GATES:IN_COLS]).astype(BF16)
    for row, lo, hi in ((ROW_FQ, IN_FQKV, IN_FF), (ROW_MQ, IN_MQ, IN_MI), (ROW_SQ, IN_SQ, IN_SK),
                        (ROW_MO, IN_MO, IN_SQ), (ROW_SK, IN_SK, IN_GATES)):
        feat_ref[row:row + hi - lo, :] = w_ref[:, lo:hi].T.astype(BF16)
    fox_f = w_ref[:, IN_FF:IN_FF + 128].T[0:N_SMALL]
    mlstm_if = w_ref[:, IN_MI - 4:IN_MI + 124].T[0:N_SMALL]
    row = lax.broadcasted_iota(jnp.int32, fox_f.shape, 0)
    small = jnp.where(row < 4, fox_f, jnp.where(row < 12, mlstm_if, 0.0))
    feat_ref[N_FEAT:, :] = small.astype(BF16)


def _prep_w_in(w_in):
    depth, d, cols = w_in.shape
    return pl.pallas_call(
        _wprep_kernel,
        grid=(depth, d // TK_PREP),
        in_specs=[pl.BlockSpec((None, TK_PREP, cols), lambda l, j: (l, j, 0))],
        out_specs=[pl.BlockSpec((None, TK_PREP, N_TOK), lambda l, j: (l, j, 0)),
                   pl.BlockSpec((None, N_FEAT + N_SMALL, TK_PREP), lambda l, j: (l, 0, j))],
        out_shape=[jax.ShapeDtypeStruct((depth, d, N_TOK), BF16),
                   jax.ShapeDtypeStruct((depth, N_FEAT + N_SMALL, d), BF16)],
        compiler_params=pltpu.CompilerParams(
            dimension_semantics=("parallel", "parallel"), vmem_limit_bytes=VMEM_LIMIT),
        name="w_in_relayout",
    )(w_in)


def _col(v, n=None):
    v = v.astype(F32).reshape(-1, 1)
    if n is not None and v.shape[0] < n:
        v = jnp.concatenate([v, jnp.zeros((n - v.shape[0], 1), F32)], axis=0)
    return v


def kernel(x, rel_bias, attn_norm, w_in, fox_f_bias, fox_q_gain, fox_k_gain, conv_w, mlstm_i_bias, mlstm_f_bias, mlstm_h_gain, swa_q_gain, swa_k_gain, swa_sinks, w_fox_out, w_conv_out, w_mlstm_out, w_swa_out, w_merge_out, ffn_norm, w_gate, w_up, w_down):
    batch, seq, _ = x.shape
    depth = w_in.shape[0]
    x2 = x.reshape(batch * seq, D_MODEL)
    bias_t = _swa_bias_table(rel_bias)

    w_tok, w_feat_t = _prep_w_in(w_in)
    attn_gain = attn_norm.reshape(depth, 1, D_MODEL)
    ffn_gain = ffn_norm.reshape(depth, 1, D_MODEL)
    w_fox_b, w_conv_b, w_mlstm_b, w_swa_b = (w.astype(BF16) for w in (w_fox_out, w_conv_out, w_mlstm_out, w_swa_out))
    w_merge_half = (0.5 * w_merge_out).astype(BF16)
    w_gate_half = (0.5 * w_gate).astype(BF16)
    w_up_b = w_up.astype(BF16)
    w_down_b = w_down.astype(BF16)

    for l in range(depth):
        tok, gates, feat, small = _inproj(x2, attn_gain, w_tok, w_feat_t, l)

        y_fox_t = _fox(feat, small, _col(fox_f_bias[l], N_SMALL), _col(fox_q_gain[l]), _col(fox_k_gain[l]),
                       batch, seq)
        gate_bias = jnp.concatenate([jnp.zeros((4,), F32), mlstm_i_bias[l], mlstm_f_bias[l]])
        y_mlstm_t = _mlstm(feat, small, _col(gate_bias, N_SMALL), _col(mlstm_h_gain[l]), batch, seq)
        sink_rows = jnp.broadcast_to(
            swa_sinks[l].astype(F32).reshape(SWA_KV_HEADS, 1, SWA_GROUP, 1),
            (SWA_KV_HEADS, 1, SWA_GROUP, WINDOW)).reshape(SWA_KV_HEADS, 1, SWA_GROUP * WINDOW)
        y_swa_t = _swa(feat, bias_t, sink_rows, _col(swa_q_gain[l]), _col(swa_k_gain[l]), batch, seq)

        x2 = _merge_ffn(x2, y_fox_t, tok, y_mlstm_t, y_swa_t, gates, conv_w,
                        w_fox_b, w_conv_b, w_mlstm_b, w_swa_b, w_merge_half,
                        ffn_gain, w_gate_half, w_up_b, w_down_b, l, seq)
    return x2.reshape(batch, seq, D_MODEL)
```

```python
import functools

import numpy as np
import jax
import jax.numpy as jnp
from jax import lax
from jax.experimental import pallas as pl
from jax.experimental.pallas import tpu as pltpu

F32 = jnp.float32
BF16 = jnp.bfloat16

D_MODEL = 1024
HEAD_DIM = 64
FOX_HEADS = 4
MLSTM_HEADS = 4
SWA_Q_HEADS = 8
SWA_KV_HEADS = 2
SWA_GROUP = SWA_Q_HEADS // SWA_KV_HEADS
WINDOW = 128
REL_BUCKETS = 32
REL_MAX_DIST = 128
D_FF = 2816
EPS = 1e-6
NEG = -1e30
QK_SCALE = HEAD_DIM ** -0.5
LOG2E = 1.4426950408889634

ROW_FQ, ROW_FK, ROW_FV = 0, 256, 512
ROW_MQ, ROW_MK, ROW_MV = 768, 1024, 1280
ROW_SQ, ROW_MO, ROW_SK, ROW_SV = 1536, 2048, 2304, 2432
N_FEAT = 2560
N_SMALL = 16
COL_CU, COL_CB, COL_CC, COL_GATES = 0, 256, 512, 768
N_TOK = 4864

TM_IN = 512
TN_IN = 512
FEAT_CHUNK = 640
TQ_FOX = 256
TK_FOX = 256
L_MLSTM = 256
SWA_BLOCKS = 4
TM_MERGE = 512
TM_FFN = 512
SUB_FFN = 512
TF_FFN = 512
VMEM_LIMIT = 56 * 1024 * 1024


def _dot(a, b):
    return jnp.dot(a, b, preferred_element_type=F32)


def _dot_nt(a, b):
    return lax.dot_general(a, b, (((1,), (1,)), ((), ())), preferred_element_type=F32)


def _dot_tn(a, b):
    return lax.dot_general(a, b, (((0,), (0,)), ((), ())), preferred_element_type=F32)


def _sigmoid(x):
    return 0.5 * jnp.tanh(0.5 * x) + 0.5


def _log_sigmoid(x):
    return jnp.minimum(x, 0.0) - jnp.log(1.0 + jnp.exp(-jnp.abs(x)))


def _cumsum_lanes(x, segment=None):
    n = segment or x.shape[-1]
    lane = lax.broadcasted_iota(jnp.int32, x.shape, x.ndim - 1) & (n - 1)
    k = 1
    while k < n:
        x = x + jnp.where(lane >= k, pltpu.roll(x, k, x.ndim - 1), 0.0)
        k *= 2
    return x


def _row_select(rows, n_rows, width):
    rid = lax.broadcasted_iota(jnp.int32, (n_rows, width), 0)
    out = jnp.zeros((n_rows, width), F32)
    for r, v in enumerate(rows):
        out = jnp.where(rid == r, v, out)
    return out


def _split3(c):
    hi = c.astype(BF16).astype(F32)
    r = c - hi
    mid = r.astype(BF16).astype(F32)
    lo = (r - mid).astype(BF16).astype(F32)
    return hi, mid, lo


def _rms_rows(xt, gain_col):
    ms = jnp.mean(xt * xt, axis=0, keepdims=True)
    return xt * lax.rsqrt(ms + EPS) * gain_col


def _bias_kernel(rb_ref, idx_ref, out_ref):
    kv = pl.program_id(0)
    g = pl.program_id(1)
    head = kv * SWA_GROUP + g
    idx = idx_ref[...]
    acc = jnp.full(idx.shape, NEG, F32)
    for b in range(REL_BUCKETS):
        acc = jnp.where(idx == b, rb_ref[b * SWA_Q_HEADS + head], acc)
    out_ref[0] = acc


def _bucket_table():
    j = np.arange(2 * WINDOW)[:, None]
    i = np.arange(WINDOW)[None, :]
    dist = i + WINDOW - j
    n = np.maximum(dist, 0).astype(np.int32)
    max_exact = REL_BUCKETS // 2
    nf = np.maximum(n, 1).astype(np.float32)
    large = max_exact + (np.log(nf / np.float32(max_exact)) / np.float32(np.log(REL_MAX_DIST / max_exact))
                         * np.float32(REL_BUCKETS - max_exact)).astype(np.int32)
    large = np.minimum(large, REL_BUCKETS - 1)
    bucket = np.where(n < max_exact, n, large)
    return np.where((dist >= 0) & (dist < WINDOW), bucket, -1).astype(np.int32)


def _swa_bias_table(rel_bias):
    idx = jnp.asarray(_bucket_table())
    return pl.pallas_call(
        _bias_kernel,
        grid=(SWA_KV_HEADS, SWA_GROUP),
        in_specs=[
            pl.BlockSpec(memory_space=pltpu.SMEM),
            pl.BlockSpec((2 * WINDOW, WINDOW), lambda kv, g: (0, 0)),
        ],
        out_specs=pl.BlockSpec((1, 2 * WINDOW, WINDOW), lambda kv, g: (kv, 0, g)),
        out_shape=jax.ShapeDtypeStruct((SWA_KV_HEADS, 2 * WINDOW, SWA_GROUP * WINDOW), F32),
        name="swa_bias_table",
    )(rel_bias.reshape(-1), idx)


def _inproj_kernel(x_ref, g_ref, wt_ref, wf_ref, tok_ref, gates_ref, feat_ref, small_ref):
    x = x_ref[...]
    ms = jnp.mean(x * x, axis=-1, keepdims=True)
    xn = (x * lax.rsqrt(ms + EPS) * g_ref[...]).astype(BF16)
    for out_ref, base, width in ((tok_ref, COL_CU, COL_GATES), (gates_ref, COL_GATES, N_TOK - COL_GATES)):
        for lo in range(0, width, TN_IN):
            hi = min(lo + TN_IN, width)
            out_ref[:, lo:hi] = _dot(xn, wt_ref[:, base + lo:base + hi]).astype(BF16)
    n_chunks = N_FEAT // FEAT_CHUNK
    for c in range(n_chunks):
        lo = c * FEAT_CHUNK
        hi = lo + FEAT_CHUNK + (N_SMALL if c == n_chunks - 1 else 0)
        r = _dot_nt(wf_ref[lo:hi, :], xn)
        feat_ref[lo:lo + FEAT_CHUNK, :] = r[:FEAT_CHUNK].astype(BF16)
        if c == n_chunks - 1:
            small_ref[...] = r[FEAT_CHUNK:]


def _resident(shape, layer=None):
    if layer is None:
        return pl.BlockSpec(shape, lambda *_: (0,) * len(shape), pipeline_mode=pl.Buffered(1))
    return pl.BlockSpec((None,) + tuple(shape), lambda *_: (layer,) + (0,) * len(shape),
                        pipeline_mode=pl.Buffered(1))


def _inproj(x2, gain, w_tok, w_feat_t, layer):
    t = x2.shape[0]
    tm = min(TM_IN, t)
    return pl.pallas_call(
        _inproj_kernel,
        grid=(t // tm,),
        in_specs=[
            pl.BlockSpec((tm, D_MODEL), lambda i: (i, 0)),
            _resident((1, D_MODEL), layer),
            _resident((D_MODEL, N_TOK), layer),
            _resident((N_FEAT + N_SMALL, D_MODEL), layer),
        ],
        out_specs=[
            pl.BlockSpec((tm, COL_GATES), lambda i: (i, 0)),
            pl.BlockSpec((tm, N_TOK - COL_GATES), lambda i: (i, 0)),
            pl.BlockSpec((N_FEAT, tm), lambda i: (0, i)),
            pl.BlockSpec((N_SMALL, tm), lambda i: (0, i)),
        ],
        out_shape=[
            jax.ShapeDtypeStruct((t, COL_GATES), BF16),
            jax.ShapeDtypeStruct((t, N_TOK - COL_GATES), BF16),
            jax.ShapeDtypeStruct((N_FEAT, t), BF16),
            jax.ShapeDtypeStruct((N_SMALL, t), F32),
        ],
        compiler_params=pltpu.CompilerParams(
            dimension_semantics=("parallel",), vmem_limit_bytes=VMEM_LIMIT),
        name="inproj",
    )(x2, gain, w_tok, w_feat_t)


def _fox_kernel(fq_ref, fk_ref, fv_ref, small_ref, fb_ref, gq_ref, gk_ref, out_ref,
                qa_ref, ka_ref, s_ref, p_ref, acc_ref, m_ref, l_ref, a_ref, *, seq, tq, tk):
    c_all = _cumsum_lanes(_log_sigmoid(small_ref[...] + fb_ref[...]))
    for h in range(FOX_HEADS):
        sl = slice(h * HEAD_DIM, (h + 1) * HEAD_DIM)
        c = c_all[h:h + 1, :] * LOG2E
        hi, mid, lo = _split3(c)
        one = jnp.ones_like(c)
        qn = _rms_rows(fq_ref[sl, :].astype(F32), gq_ref[...]) * (QK_SCALE * LOG2E)
        qa_ref[h, 0:HEAD_DIM, :] = qn.astype(BF16)
        qa_ref[h, HEAD_DIM:, :] = _row_select([hi, mid, lo, one, one, one], HEAD_DIM, seq).astype(BF16)
        kn = _rms_rows(fk_ref[sl, :].astype(F32), gk_ref[...])
        k_aug = _row_select([one, one, one, -hi, -mid, -lo], HEAD_DIM, seq)
        ka_ref[h] = jnp.concatenate([kn, k_aug], axis=0).T.astype(BF16)
    refs = (fv_ref, out_ref, qa_ref, ka_ref, s_ref, p_ref, acc_ref, m_ref, l_ref, a_ref)
    pl.loop(0, seq // tq)(lambda i: _fox_tile(i, *refs, tq=tq, tk=tk))


def _fox_tile(i, fv_ref, out_ref, qa_ref, ka_ref, s_ref, p_ref, acc_ref, m_ref, l_ref, a_ref, *, tq, tk):
    heads = range(FOX_HEADS)
    ratio = tq // tk
    q_off = pl.multiple_of(i * tq, tq)
    qa = [qa_ref[h, :, pl.ds(q_off, tq)] for h in heads]

    m_ref[...] = jnp.full(m_ref.shape, NEG, F32)
    l_ref[...] = jnp.zeros_like(l_ref)
    acc_ref[...] = jnp.zeros_like(acc_ref)
    a_ref[...] = jnp.ones_like(a_ref)
    p_ref[...] = jnp.zeros_like(p_ref)

    def qk(j, slot):
        off = pl.multiple_of(j * tk, tk)
        for h in heads:
            s_ref[slot, h] = _dot(ka_ref[h, pl.ds(off, tk), :], qa[h])

    def pv(j):
        off = pl.multiple_of(j * tk, tk)
        for h in heads:
            v = fv_ref[h * HEAD_DIM:(h + 1) * HEAD_DIM, pl.ds(off, tk)]
            acc_ref[h] = a_ref[h] * acc_ref[h] + _dot(v, p_ref[h])

    def softmax(slot, diag=None):
        for h in heads:
            s = s_ref[slot, h]
            if diag is not None:
                ahead = lax.broadcasted_iota(jnp.int32, s.shape, 1) - lax.broadcasted_iota(jnp.int32, s.shape, 0)
                s = jnp.where(ahead >= diag * tk, s, NEG)
            m = m_ref[h]
            m_new = jnp.maximum(m, jnp.max(s, axis=0, keepdims=True))
            p = jnp.exp2(s - m_new)
            alpha = jnp.exp2(m - m_new)
            l_ref[h] = alpha * l_ref[h] + jnp.sum(p, axis=0, keepdims=True)
            m_ref[h] = m_new
            a_ref[h] = alpha
            p_ref[h] = p.astype(BF16)

    def stage(j, slot, diag=None):
        qk(j + 1, 1 - slot)
        pv(jnp.maximum(j - 1, 0))
        softmax(slot, diag)

    qk(0, 0)
    first_diag = ratio * i

    @pl.loop(0, first_diag // 2)
    def _(k):
        stage(2 * k, 0)
        stage(2 * k + 1, 1)

    def tail(parity):
        if parity == 1:
            stage(first_diag - 1, 0)
        for r in range(ratio - 1):
            stage(first_diag + r, (parity + r) % 2, diag=r)
        last = first_diag + ratio - 1
        pv(jnp.maximum(last - 1, 0))
        softmax((parity + ratio - 1) % 2, diag=ratio - 1)
        pv(last)

    if ratio % 2 == 0:
        tail(0)
    else:
        odd = lax.rem(first_diag, 2) == 1
        pl.when(odd)(lambda: tail(1))
        pl.when(jnp.logical_not(odd))(lambda: tail(0))

    for h in heads:
        out_ref[h * HEAD_DIM:(h + 1) * HEAD_DIM, pl.ds(q_off, tq)] = (acc_ref[h] / l_ref[h]).astype(BF16)


def _fox(feat, small, f_bias_col, q_gain_col, k_gain_col, batch, seq):
    tq = min(TQ_FOX, seq)
    tk = min(TK_FOX, tq)
    t = batch * seq
    width = FOX_HEADS * HEAD_DIM
    blk = lambda row: pl.BlockSpec((width, seq), lambda b: (row // width, b))
    col = lambda n: pl.BlockSpec((n, 1), lambda b: (0, 0))
    return pl.pallas_call(
        functools.partial(_fox_kernel, seq=seq, tq=tq, tk=tk),
        grid=(batch,),
        in_specs=[blk(ROW_FQ), blk(ROW_FK), blk(ROW_FV),
                  pl.BlockSpec((N_SMALL, seq), lambda b: (0, b)),
                  col(N_SMALL), col(HEAD_DIM), col(HEAD_DIM)],
        out_specs=pl.BlockSpec((width, seq), lambda b: (0, b)),
        out_shape=jax.ShapeDtypeStruct((width, t), BF16),
        scratch_shapes=[pltpu.VMEM((FOX_HEADS, 2 * HEAD_DIM, seq), BF16),
                        pltpu.VMEM((FOX_HEADS, seq, 2 * HEAD_DIM), BF16),
                        pltpu.VMEM((2, FOX_HEADS, tk, tq), F32),
                        pltpu.VMEM((FOX_HEADS, tk, tq), BF16),
                        pltpu.VMEM((FOX_HEADS, HEAD_DIM, tq), F32),
                        pltpu.VMEM((FOX_HEADS, 1, tq), F32),
                        pltpu.VMEM((FOX_HEADS, 1, tq), F32),
                        pltpu.VMEM((FOX_HEADS, 1, tq), F32)],
        compiler_params=pltpu.CompilerParams(
            dimension_semantics=("parallel",), vmem_limit_bytes=VMEM_LIMIT),
        name="fox_attention",
    )(feat, feat, feat, small, f_bias_col, q_gain_col, k_gain_col)


def _short_conv_tile(ct_ref, w_ref, carry_ref, sequence_start):
    width = w_ref.shape[1]
    u = ct_ref[:, COL_CU:COL_CU + width].astype(F32)
    b_gate = ct_ref[:, COL_CB:COL_CB + width].astype(F32)
    c_gate = ct_ref[:, COL_CC:COL_CC + width].astype(F32)
    z = c_gate * u
    prev = jnp.where(sequence_start, 0.0, carry_ref[...])
    carry_ref[...] = z[z.shape[0] - 8:, :]
    last1, last2 = prev[7:8, :], prev[6:7, :]
    row = lax.broadcasted_iota(jnp.int32, z.shape, 0)
    z1 = jnp.where(row >= 1, pltpu.roll(z, 1, 0), last1)
    z2 = jnp.where(row >= 2, pltpu.roll(z, 2, 0), jnp.where(row == 1, last1, last2))
    w = w_ref[...]
    return (b_gate * (w[0:1, :] * z2 + w[1:2, :] * z1 + w[2:3, :] * z)).astype(BF16)


def _mlstm_kernel(q_ref, k_ref, v_ref, o_ref, small_ref, bias_ref, gain_ref, out_ref,
                  c_ref, m_ref, b_ref, u_ref, ut_ref, *, chunk):
    c_ref[...] = jnp.zeros_like(c_ref)
    m_ref[...] = jnp.zeros_like(m_ref)
    g = small_ref[...] + bias_ref[...]
    b_all = _cumsum_lanes(_log_sigmoid(g), segment=chunk)
    b_ref[...] = b_all
    u_all = _row_select([g[4 + h:5 + h, :] - b_all[8 + h:9 + h, :] for h in range(MLSTM_HEADS)],
                        8, g.shape[1])
    u_ref[...] = u_all
    ut_ref[...] = jnp.concatenate([u_all, jnp.zeros((120, g.shape[1]), F32)], axis=0).T
    refs = (q_ref, k_ref, v_ref, o_ref, gain_ref, out_ref, c_ref, m_ref, b_ref, u_ref, ut_ref)
    pl.loop(0, g.shape[1] // chunk)(lambda c: _mlstm_chunk(c, *refs, chunk=chunk))


def _mlstm_chunk(c, q_ref, k_ref, v_ref, o_ref, gain_ref, out_ref, c_ref, m_ref, b_ref, u_ref, ut_ref, *, chunk):
    off = pl.multiple_of(c * chunk, chunk)
    cols = pl.ds(off, chunk)
    bcum = b_ref[:, cols]
    rows = [u_ref[h:h + 1, cols] for h in range(MLSTM_HEADS)]
    u_cols = ut_ref[cols, :]

    src = lax.broadcasted_iota(jnp.int32, (chunk, chunk), 0)
    tgt = lax.broadcasted_iota(jnp.int32, (chunk, chunk), 1)
    causal = src <= tgt
    ones_rows = (lax.broadcasted_iota(jnp.int32, (HEAD_DIM, chunk), 0) == 0).astype(BF16)

    heads = range(MLSTM_HEADS)
    sls = [slice(h * HEAD_DIM, (h + 1) * HEAD_DIM) for h in heads]
    qs = [(q_ref[sl, cols].astype(F32) * QK_SCALE).astype(BF16) for sl in sls]
    ks = [k_ref[sl, cols] for sl in sls]
    c_prev = [c_ref[h] for h in heads]
    scores = [_dot_tn(ks[h], qs[h]) for h in heads]
    carried = [_dot(c_prev[h].astype(BF16), qs[h]) for h in heads]

    sw, kw, m_t, inter, decay = [], [], [], [], []
    for h in heads:
        b_row = bcum[8 + h:9 + h, :]
        b_last = b_row[:, chunk - 1:chunk]
        u_col = u_cols[:, h:h + 1]
        m_prev = m_ref[h][0:1, 0:1]
        dmat = jnp.where(causal, b_row + u_col, NEG)
        g_row = b_row + m_prev
        m_h = jnp.maximum(g_row, jnp.max(dmat, axis=0, keepdims=True))
        sw.append((scores[h] * jnp.exp(dmat - m_h)).astype(BF16))
        m_t.append(m_h)
        inter.append(jnp.exp(g_row - m_h))
        m_loc = jnp.max(rows[h] + b_last, axis=1, keepdims=True)
        m_new = jnp.maximum(b_last + m_prev, m_loc)
        decay.append(jnp.exp(b_last + m_prev - m_new))
        w_row = jnp.exp(rows[h] + b_last - m_new)
        kw.append((ks[h].astype(F32) * w_row).astype(BF16))
        m_ref[h] = jnp.broadcast_to(m_new, m_ref.shape[1:])

    v_aug = [jnp.concatenate([v_ref[sl, cols], ones_rows], axis=0) for sl in sls]
    intra = [_dot(v_aug[h], sw[h]) for h in heads]
    update = [_dot_nt(v_aug[h], kw[h]) for h in heads]
    for h in heads:
        c_ref[h] = decay[h] * c_prev[h] + update[h]
        tot = intra[h] + inter[h] * carried[h]
        num = tot[0:HEAD_DIM]
        den = tot[HEAD_DIM:HEAD_DIM + 1]
        ht = num / jnp.maximum(jnp.abs(den), jnp.exp(-m_t[h]))
        hn = _rms_rows(ht, gain_ref[sls[h], :])
        out_ref[sls[h], cols] = (_sigmoid(o_ref[sls[h], cols].astype(F32)) * hn).astype(BF16)


def _mlstm(feat, small, bias_col, gain_col, batch, seq):
    chunk = min(L_MLSTM, seq)
    t = batch * seq
    width = MLSTM_HEADS * HEAD_DIM
    blk = lambda row: pl.BlockSpec((width, seq), lambda b: (row // width, b))
    return pl.pallas_call(
        functools.partial(_mlstm_kernel, chunk=chunk),
        grid=(batch,),
        in_specs=[blk(ROW_MQ), blk(ROW_MK), blk(ROW_MV), blk(ROW_MO),
                  pl.BlockSpec((N_SMALL, seq), lambda b: (0, b)),
                  pl.BlockSpec((N_SMALL, 1), lambda b: (0, 0)),
                  pl.BlockSpec((width, 1), lambda b: (0, 0))],
        out_specs=pl.BlockSpec((width, seq), lambda b: (0, b)),
        out_shape=jax.ShapeDtypeStruct((width, t), BF16),
        scratch_shapes=[pltpu.VMEM((MLSTM_HEADS, 2 * HEAD_DIM, HEAD_DIM), F32),
                        pltpu.VMEM((MLSTM_HEADS, 8, 128), F32),
                        pltpu.VMEM((N_SMALL, seq), F32),
                        pltpu.VMEM((8, seq), F32),
                        pltpu.VMEM((seq, 128), F32)],
        compiler_params=pltpu.CompilerParams(
            dimension_semantics=("parallel",), vmem_limit_bytes=VMEM_LIMIT),
        name="mlstm",
    )(feat, feat, feat, feat, small, bias_col, gain_col)


def _swa_kernel(q_ref, k_ref, v_ref, bias_ref, sink_ref, gq_ref, gk_ref, out_ref, *, nblk):
    refs = (q_ref, k_ref, v_ref, bias_ref, sink_ref, gq_ref, gk_ref, out_ref)
    pl.loop(0, q_ref.shape[1] // (nblk * WINDOW))(lambda n: _swa_span(n, *refs, nblk=nblk))


def _swa_span(n, q_ref, k_ref, v_ref, bias_ref, sink_ref, gq_ref, gk_ref, out_ref, *, nblk):
    w = WINDOW
    off = pl.multiple_of(n * (nblk * w), nblk * w)
    cur = pl.ds(off, nblk * w)
    prev = pl.ds(pl.multiple_of(jnp.maximum(off - w, 0), w), w)
    key = lax.broadcasted_iota(jnp.int32, (2 * w, SWA_GROUP * w), 0)
    first_block_pad = jnp.logical_and(n == 0, key < w)
    kvs = range(SWA_KV_HEADS)
    ksl = [slice(kv * HEAD_DIM, (kv + 1) * HEAD_DIM) for kv in kvs]
    kt = [jnp.concatenate([_rms_rows(k_ref[s, prev].astype(F32), gk_ref[...]),
                           _rms_rows(k_ref[s, cur].astype(F32), gk_ref[...])], axis=1).astype(BF16) for s in ksl]
    vt = [jnp.concatenate([v_ref[s, prev], v_ref[s, cur]], axis=1) for s in ksl]

    bands = [(blk, kv) for blk in range(nblk) for kv in kvs]
    scores = []
    for blk, kv in bands:
        qs = []
        for g in range(SWA_GROUP):
            hs = slice((kv * SWA_GROUP + g) * HEAD_DIM, (kv * SWA_GROUP + g + 1) * HEAD_DIM)
            q = q_ref[hs, pl.ds(off + blk * w, w)].astype(F32)
            qs.append((_rms_rows(q, gq_ref[...]) * QK_SCALE).astype(BF16))
        qt = jnp.concatenate(qs, axis=1)
        s = _dot_tn(kt[kv][:, blk * w:(blk + 2) * w], qt) + bias_ref[kv]
        scores.append(jnp.where(first_block_pad, NEG, s) if blk == 0 else s)
    probs, denoms = [], []
    for (blk, kv), s in zip(bands, scores):
        sink = sink_ref[kv]
        m = jnp.maximum(jnp.max(s, axis=0, keepdims=True), sink)
        p = jnp.exp(s - m)
        denoms.append(jnp.sum(p, axis=0, keepdims=True) + jnp.exp(sink - m))
        probs.append(p.astype(BF16))
    outs = [_dot(vt[kv][:, blk * w:(blk + 2) * w], p) for (blk, kv), p in zip(bands, probs)]
    for (blk, kv), o, d in zip(bands, outs, denoms):
        o = o / d
        for g in range(SWA_GROUP):
            hs = slice((kv * SWA_GROUP + g) * HEAD_DIM, (kv * SWA_GROUP + g + 1) * HEAD_DIM)
            out_ref[hs, pl.ds(off + blk * w, w)] = o[:, g * w:(g + 1) * w].astype(BF16)


def _swa(feat, bias_t, sink_rows, q_gain_col, k_gain_col, batch, seq):
    nblk = min(SWA_BLOCKS, seq // WINDOW)
    t = batch * seq
    qw = SWA_Q_HEADS * HEAD_DIM
    kw = SWA_KV_HEADS * HEAD_DIM
    return pl.pallas_call(
        functools.partial(_swa_kernel, nblk=nblk),
        grid=(batch,),
        in_specs=[pl.BlockSpec((qw, seq), lambda b: (ROW_SQ // qw, b)),
                  pl.BlockSpec((kw, seq), lambda b: (ROW_SK // kw, b)),
                  pl.BlockSpec((kw, seq), lambda b: (ROW_SV // kw, b)),
                  pl.BlockSpec(bias_t.shape, lambda b: (0, 0, 0)),
                  pl.BlockSpec(sink_rows.shape, lambda b: (0, 0, 0)),
                  pl.BlockSpec((HEAD_DIM, 1), lambda b: (0, 0)),
                  pl.BlockSpec((HEAD_DIM, 1), lambda b: (0, 0))],
        out_specs=pl.BlockSpec((qw, seq), lambda b: (0, b)),
        out_shape=jax.ShapeDtypeStruct((qw, t), BF16),
        compiler_params=pltpu.CompilerParams(
            dimension_semantics=("parallel",), vmem_limit_bytes=VMEM_LIMIT),
        name="swa_attention",
    )(feat, feat, feat, bias_t, sink_rows, q_gain_col, k_gain_col)


def _merge_ffn_kernel(x_ref, yf_ref, ct_ref, ym_ref, ys_ref, g_ref, cw_ref,
                      wf_ref, wc_ref, wm_ref, ws_ref, wo_ref, gain_ref, wg_ref, wu_ref, wd_ref,
                      out_ref, carry_ref, *, sub, tiles_per_seq):
    y_conv = _short_conv_tile(ct_ref, cw_ref, carry_ref, lax.rem(pl.program_id(0), tiles_per_seq) == 0)
    bounds = list(range(0, D_FF, TF_FFN)) + [D_FF]
    for r in range(0, x_ref.shape[0], sub):
        rows = slice(r, r + sub)

        def gated(branch, y):
            g = g_ref[rows, branch * D_MODEL:(branch + 1) * D_MODEL]
            return (1.0 + jnp.tanh(g.astype(F32))) * y

        merged = gated(0, _dot_tn(yf_ref[:, rows], wf_ref[...]))
        merged += gated(1, _dot(y_conv[rows, :], wc_ref[...]))
        merged += gated(2, _dot_tn(ym_ref[:, rows], wm_ref[...]))
        merged += gated(3, _dot_tn(ys_ref[:, rows], ws_ref[...]))
        x = x_ref[rows, :] + _dot(merged.astype(BF16), wo_ref[...])

        ms = jnp.mean(x * x, axis=-1, keepdims=True)
        hn = (x * lax.rsqrt(ms + EPS) * gain_ref[...]).astype(BF16)
        acc = x
        for lo, hi in zip(bounds[:-1], bounds[1:]):
            h = _dot(hn, wg_ref[:, lo:hi])
            act = (h * (1.0 + jnp.tanh(h)) * _dot(hn, wu_ref[:, lo:hi])).astype(BF16)
            acc = acc + _dot(act, wd_ref[lo:hi, :])
        out_ref[rows, :] = acc


def _merge_ffn(x2, y_fox_t, conv_tok, y_mlstm_t, y_swa_t, gates, conv_w, w_fox, w_conv, w_mlstm, w_swa,
               w_out, gain, w_gate_half, w_up, w_down, layer, seq):
    t = x2.shape[0]
    tm = min(TM_MERGE, t, seq)
    feat_blk = lambda a: pl.BlockSpec((a.shape[0], tm), lambda i: (0, i))
    full = lambda a: _resident(a.shape[1:], layer)
    return pl.pallas_call(
        functools.partial(_merge_ffn_kernel, sub=min(SUB_FFN, tm), tiles_per_seq=seq // tm),
        grid=(t // tm,),
        in_specs=[pl.BlockSpec((tm, D_MODEL), lambda i: (i, 0)),
                  feat_blk(y_fox_t),
                  pl.BlockSpec((tm, conv_tok.shape[1]), lambda i: (i, 0)),
                  feat_blk(y_mlstm_t), feat_blk(y_swa_t),
                  pl.BlockSpec((tm, gates.shape[1]), lambda i: (i, 0)),
                  full(conv_w),
                  full(w_fox), full(w_conv), full(w_mlstm), full(w_swa), full(w_out),
                  full(gain), full(w_gate_half), full(w_up), full(w_down)],
        out_specs=pl.BlockSpec((tm, D_MODEL), lambda i: (i, 0)),
        out_shape=jax.ShapeDtypeStruct((t, D_MODEL), F32),
        scratch_shapes=[pltpu.VMEM((8, conv_w.shape[2]), F32)],
        compiler_params=pltpu.CompilerParams(
            dimension_semantics=("arbitrary",), vmem_limit_bytes=VMEM_LIMIT),
        name="merge_ffn",
    )(x2, y_fox_t, conv_tok, y_mlstm_t, y_swa_t, gates, conv_w,
      w_fox, w_conv, w_mlstm, w_swa, w_out, gain, w_gate_half, w_up, w_down)


IN_FQKV, IN_FF, IN_CONV, IN_MQ, IN_MK, IN_MV = 0, 768, 772, 1540, 1796, 2052
IN_MI, IN_MF, IN_MO, IN_SQ, IN_SK, IN_SV, IN_GATES, IN_COLS = 2308, 2312, 2316, 2572, 3084, 3212, 3340, 7436
TK_PREP = 256


def _wprep_kernel(w_ref, tok_ref, feat_ref):
    tok_ref[:, COL_CU:COL_GATES] = w_ref[:, IN_CONV:IN_MQ].astype(BF16)
    tok_ref[:, COL_GATES:] = (0.5 * w_ref[:, IN_GATES:IN_COLS]).astype(BF16)
    for row, lo, hi in ((ROW_FQ, IN_FQKV, IN_FF), (ROW_MQ, IN_MQ, IN_MI), (ROW_SQ, IN_SQ, IN_SK),
                        (ROW_MO, IN_MO, IN_SQ), (ROW_SK, IN_SK, IN_GATES)):
        feat_ref[row:row + hi - lo, :] = w_ref[:, lo:hi].T.astype(BF16)
    fox_f = w_ref[:, IN_FF:IN_FF + 128].T[0:N_SMALL]
    mlstm_if = w_ref[:, IN_MI - 4:IN_MI + 124].T[0:N_SMALL]
    row = lax.broadcasted_iota(jnp.int32, fox_f.shape, 0)
    small = jnp.where(row < 4, fox_f, jnp.where(row < 12, mlstm_if, 0.0))
    feat_ref[N_FEAT:, :] = small.astype(BF16)


def _prep_w_in(w_in):
    depth, d, cols = w_in.shape
    return pl.pallas_call(
        _wprep_kernel,
        grid=(depth, d // TK_PREP),
        in_specs=[pl.BlockSpec((None, TK_PREP, cols), lambda l, j: (l, j, 0))],
        out_specs=[pl.BlockSpec((None, TK_PREP, N_TOK), lambda l, j: (l, j, 0)),
                   pl.BlockSpec((None, N_FEAT + N_SMALL, TK_PREP), lambda l, j: (l, 0, j))],
        out_shape=[jax.ShapeDtypeStruct((depth, d, N_TOK), BF16),
                   jax.ShapeDtypeStruct((depth, N_FEAT + N_SMALL, d), BF16)],
        compiler_params=pltpu.CompilerParams(
            dimension_semantics=("parallel", "parallel"), vmem_limit_bytes=VMEM_LIMIT),
        name="w_in_relayout",
    )(w_in)


def _col(v, n=None):
    v = v.astype(F32).reshape(-1, 1)
    if n is not None and v.shape[0] < n:
        v = jnp.concatenate([v, jnp.zeros((n - v.shape[0], 1), F32)], axis=0)
    return v


def kernel(x, rel_bias, attn_norm, w_in, fox_f_bias, fox_q_gain, fox_k_gain, conv_w, mlstm_i_bias, mlstm_f_bias, mlstm_h_gain, swa_q_gain, swa_k_gain, swa_sinks, w_fox_out, w_conv_out, w_mlstm_out, w_swa_out, w_merge_out, ffn_norm, w_gate, w_up, w_down):
    batch, seq, _ = x.shape
    depth = w_in.shape[0]
    x2 = x.reshape(batch * seq, D_MODEL)
    bias_t = _swa_bias_table(rel_bias)

    w_tok, w_feat_t = _prep_w_in(w_in)
    attn_gain = attn_norm.reshape(depth, 1, D_MODEL)
    ffn_gain = ffn_norm.reshape(depth, 1, D_MODEL)
    w_fox_b, w_conv_b, w_mlstm_b, w_swa_b = (w.astype(BF16) for w in (w_fox_out, w_conv_out, w_mlstm_out, w_swa_out))
    w_merge_half = (0.5 * w_merge_out).astype(BF16)
    w_gate_half = (0.5 * w_gate).astype(BF16)
    w_up_b = w_up.astype(BF16)
    w_down_b = w_down.astype(BF16)

    for l in range(depth):
        tok, gates, feat, small = _inproj(x2, attn_gain, w_tok, w_feat_t, l)

        y_fox_t = _fox(feat, small, _col(fox_f_bias[l], N_SMALL), _col(fox_q_gain[l]), _col(fox_k_gain[l]),
                       batch, seq)
        gate_bias = jnp.concatenate([jnp.zeros((4,), F32), mlstm_i_bias[l], mlstm_f_bias[l]])
        y_mlstm_t = _mlstm(feat, small, _col(gate_bias, N_SMALL), _col(mlstm_h_gain[l]), batch, seq)
        sink_rows = jnp.broadcast_to(
            swa_sinks[l].astype(F32).reshape(SWA_KV_HEADS, 1, SWA_GROUP, 1),
            (SWA_KV_HEADS, 1, SWA_GROUP, WINDOW)).reshape(SWA_KV_HEADS, 1, SWA_GROUP * WINDOW)
        y_swa_t = _swa(feat, bias_t, sink_rows, _col(swa_q_gain[l]), _col(swa_k_gain[l]), batch, seq)

        x2 = _merge_ffn(x2, y_fox_t, tok, y_mlstm_t, y_swa_t, gates, conv_w,
                        w_fox_b, w_conv_b, w_mlstm_b, w_swa_b, w_merge_half,
                        ffn_gain, w_gate_half, w_up_b, w_down_b, l, seq)
    return x2.reshape(batch, seq, D_MODEL)
```

```python
import functools

import numpy as np
import jax
import jax.numpy as jnp
from jax import lax
from jax.experimental import pallas as pl
from jax.experimental.pallas import tpu as pltpu

F32 = jnp.float32
BF16 = jnp.bfloat16

D_MODEL = 1024
HEAD_DIM = 64
FOX_HEADS = 4
MLSTM_HEADS = 4
SWA_Q_HEADS = 8
SWA_KV_HEADS = 2
SWA_GROUP = SWA_Q_HEADS // SWA_KV_HEADS
WINDOW = 128
REL_BUCKETS = 32
REL_MAX_DIST = 128
D_FF = 2816
EPS = 1e-6
NEG = -1e30
QK_SCALE = HEAD_DIM ** -0.5
LOG2E = 1.4426950408889634

ROW_FQ, ROW_FK, ROW_FV = 0, 256, 512
ROW_MQ, ROW_MK, ROW_MV = 768, 1024, 1280
ROW_SQ, ROW_MO, ROW_SK, ROW_SV = 1536, 2048, 2304, 2432
N_FEAT = 2560
N_SMALL = 16
COL_CU, COL_CB, COL_CC, COL_GATES = 0, 256, 512, 768
N_TOK = 4864

TM_IN = 512
TN_IN = 512
FEAT_CHUNK = 640
TQ_FOX = 256
TK_FOX = 256
L_MLSTM = 256
SWA_BLOCKS = 4
TM_MERGE = 512
TF_FFN = 512
VMEM_LIMIT = 56 * 1024 * 1024


def _dot(a, b):
    return jnp.dot(a, b, preferred_element_type=F32)


def _dot_nt(a, b):
    return lax.dot_general(a, b, (((1,), (1,)), ((), ())), preferred_element_type=F32)


def _dot_tn(a, b):
    return lax.dot_general(a, b, (((0,), (0,)), ((), ())), preferred_element_type=F32)


def _sigmoid(x):
    return 0.5 * jnp.tanh(0.5 * x) + 0.5


def _log_sigmoid(x):
    return jnp.minimum(x, 0.0) - jnp.log(1.0 + jnp.exp(-jnp.abs(x)))


def _cumsum_lanes(x, segment=None):
    n = segment or x.shape[-1]
    lane = lax.broadcasted_iota(jnp.int32, x.shape, x.ndim - 1) & (n - 1)
    k = 1
    while k < n:
        x = x + jnp.where(lane >= k, pltpu.roll(x, k, x.ndim - 1), 0.0)
        k *= 2
    return x


def _row_select(rows, n_rows, width):
    rid = lax.broadcasted_iota(jnp.int32, (n_rows, width), 0)
    out = jnp.zeros((n_rows, width), F32)
    for r, v in enumerate(rows):
        out = jnp.where(rid == r, v, out)
    return out


def _split3(c):
    hi = c.astype(BF16).astype(F32)
    r = c - hi
    mid = r.astype(BF16).astype(F32)
    lo = (r - mid).astype(BF16).astype(F32)
    return hi, mid, lo


def _rms_rows(xt, gain_col):
    ms = jnp.mean(xt * xt, axis=0, keepdims=True)
    return xt * lax.rsqrt(ms + EPS) * gain_col


def _bias_kernel(rb_ref, idx_ref, out_ref):
    kv = pl.program_id(0)
    g = pl.program_id(1)
    head = kv * SWA_GROUP + g
    idx = idx_ref[...]
    acc = jnp.full(idx.shape, NEG, F32)
    for b in range(REL_BUCKETS):
        acc = jnp.where(idx == b, rb_ref[b * SWA_Q_HEADS + head] * LOG2E, acc)
    out_ref[0] = acc


def _bucket_table():
    j = np.arange(2 * WINDOW)[:, None]
    i = np.arange(WINDOW)[None, :]
    dist = i + WINDOW - j
    n = np.maximum(dist, 0).astype(np.int32)
    max_exact = REL_BUCKETS // 2
    nf = np.maximum(n, 1).astype(np.float32)
    large = max_exact + (np.log(nf / np.float32(max_exact)) / np.float32(np.log(REL_MAX_DIST / max_exact))
                         * np.float32(REL_BUCKETS - max_exact)).astype(np.int32)
    large = np.minimum(large, REL_BUCKETS - 1)
    bucket = np.where(n < max_exact, n, large)
    return np.where((dist >= 0) & (dist < WINDOW), bucket, -1).astype(np.int32)


def _swa_bias_table(rel_bias):
    idx = jnp.asarray(_bucket_table())
    return pl.pallas_call(
        _bias_kernel,
        grid=(SWA_KV_HEADS, SWA_GROUP),
        in_specs=[
            pl.BlockSpec(memory_space=pltpu.SMEM),
            pl.BlockSpec((2 * WINDOW, WINDOW), lambda kv, g: (0, 0)),
        ],
        out_specs=pl.BlockSpec((1, 2 * WINDOW, WINDOW), lambda kv, g: (kv, 0, g)),
        out_shape=jax.ShapeDtypeStruct((SWA_KV_HEADS, 2 * WINDOW, SWA_GROUP * WINDOW), F32),
        name="swa_bias_table",
    )(rel_bias.reshape(-1), idx)


def _inproj_kernel(x_ref, g_ref, wt_ref, wf_ref, tok_ref, gates_ref, feat_ref, small_ref):
    x = x_ref[...]
    ms = jnp.mean(x * x, axis=-1, keepdims=True)
    xn = (x * lax.rsqrt(ms + EPS) * g_ref[...]).astype(BF16)
    for out_ref, base, width in ((tok_ref, COL_CU, COL_GATES), (gates_ref, COL_GATES, N_TOK - COL_GATES)):
        for lo in range(0, width, TN_IN):
            hi = min(lo + TN_IN, width)
            out_ref[:, lo:hi] = _dot(xn, wt_ref[:, base + lo:base + hi]).astype(BF16)
    n_chunks = N_FEAT // FEAT_CHUNK
    for c in range(n_chunks):
        lo = c * FEAT_CHUNK
        hi = lo + FEAT_CHUNK + (N_SMALL if c == n_chunks - 1 else 0)
        r = _dot_nt(wf_ref[lo:hi, :], xn)
        feat_ref[lo:lo + FEAT_CHUNK, :] = r[:FEAT_CHUNK].astype(BF16)
        if c == n_chunks - 1:
            small_ref[...] = r[FEAT_CHUNK:]


def _resident(shape, layer=None):
    if layer is None:
        return pl.BlockSpec(shape, lambda *_: (0,) * len(shape), pipeline_mode=pl.Buffered(1))
    return pl.BlockSpec((None,) + tuple(shape), lambda *_: (layer,) + (0,) * len(shape),
                        pipeline_mode=pl.Buffered(1))


def _inproj(x2, gain, w_tok, w_feat_t, layer):
    t = x2.shape[0]
    tm = min(TM_IN, t)
    return pl.pallas_call(
        _inproj_kernel,
        grid=(t // tm,),
        in_specs=[
            pl.BlockSpec((tm, D_MODEL), lambda i: (i, 0)),
            _resident((1, D_MODEL), layer),
            _resident((D_MODEL, N_TOK), layer),
            _resident((N_FEAT + N_SMALL, D_MODEL), layer),
        ],
        out_specs=[
            pl.BlockSpec((tm, COL_GATES), lambda i: (i, 0)),
            pl.BlockSpec((tm, N_TOK - COL_GATES), lambda i: (i, 0)),
            pl.BlockSpec((N_FEAT, tm), lambda i: (0, i)),
            pl.BlockSpec((N_SMALL, tm), lambda i: (0, i)),
        ],
        out_shape=[
            jax.ShapeDtypeStruct((t, COL_GATES), BF16),
            jax.ShapeDtypeStruct((t, N_TOK - COL_GATES), BF16),
            jax.ShapeDtypeStruct((N_FEAT, t), BF16),
            jax.ShapeDtypeStruct((N_SMALL, t), F32),
        ],
        compiler_params=pltpu.CompilerParams(
            dimension_semantics=("parallel",), vmem_limit_bytes=VMEM_LIMIT),
        name="inproj",
    )(x2, gain, w_tok, w_feat_t)


def _fox_kernel(fq_ref, fk_ref, fv_ref, small_ref, fb_ref, gq_ref, gk_ref, out_ref,
                qa_ref, ka_ref, s_ref, p_ref, acc_ref, m_ref, l_ref, a_ref, *, seq, tq, tk):
    c_all = _cumsum_lanes(_log_sigmoid(small_ref[...] + fb_ref[...]))
    for h in range(FOX_HEADS):
        sl = slice(h * HEAD_DIM, (h + 1) * HEAD_DIM)
        c = c_all[h:h + 1, :] * LOG2E
        hi, mid, lo = _split3(c)
        one = jnp.ones_like(c)
        qn = _rms_rows(fq_ref[sl, :].astype(F32), gq_ref[...]) * (QK_SCALE * LOG2E)
        qa_ref[h, 0:HEAD_DIM, :] = qn.astype(BF16)
        qa_ref[h, HEAD_DIM:, :] = _row_select([hi, mid, lo, one, one, one], HEAD_DIM, seq).astype(BF16)
        kn = _rms_rows(fk_ref[sl, :].astype(F32), gk_ref[...])
        k_aug = _row_select([one, one, one, -hi, -mid, -lo], HEAD_DIM, seq)
        ka_ref[h] = jnp.concatenate([kn, k_aug], axis=0).T.astype(BF16)
    refs = (fv_ref, out_ref, qa_ref, ka_ref, s_ref, p_ref, acc_ref, m_ref, l_ref, a_ref)
    pl.loop(0, seq // tq)(lambda i: _fox_tile(i, *refs, tq=tq, tk=tk))


def _fox_tile(i, fv_ref, out_ref, qa_ref, ka_ref, s_ref, p_ref, acc_ref, m_ref, l_ref, a_ref, *, tq, tk):
    heads = range(FOX_HEADS)
    q_off = pl.multiple_of(i * tq, tq)
    qa = [qa_ref[h, :, pl.ds(q_off, tq)] for h in heads]

    m_ref[...] = jnp.full(m_ref.shape, NEG, F32)
    l_ref[...] = jnp.zeros_like(l_ref)
    acc_ref[...] = jnp.zeros_like(acc_ref)
    a_ref[...] = jnp.ones_like(a_ref)
    p_ref[...] = jnp.zeros_like(p_ref)

    def qk(j, slot):
        off = pl.multiple_of(j * tk, tk)
        for h in heads:
            s_ref[slot, h] = _dot(ka_ref[h, pl.ds(off, tk), :], qa[h])

    def pv(j):
        off = pl.multiple_of(j * tk, tk)
        for h in heads:
            v = fv_ref[h * HEAD_DIM:(h + 1) * HEAD_DIM, pl.ds(off, tk)]
            acc_ref[h] = a_ref[h] * acc_ref[h] + _dot(v, p_ref[h])

    def softmax(slot, diagonal=False):
        for h in heads:
            s = s_ref[slot, h]
            if diagonal:
                key = lax.broadcasted_iota(jnp.int32, s.shape, 0)
                qry = lax.broadcasted_iota(jnp.int32, s.shape, 1)
                s = jnp.where(key <= qry, s, NEG)
            m = m_ref[h]
            m_new = jnp.maximum(m, jnp.max(s, axis=0, keepdims=True))
            p = jnp.exp2(s - m_new)
            alpha = jnp.exp2(m - m_new)
            l_ref[h] = alpha * l_ref[h] + jnp.sum(p, axis=0, keepdims=True)
            m_ref[h] = m_new
            a_ref[h] = alpha
            p_ref[h] = p.astype(BF16)

    def stage(j, slot):
        qk(j + 1, 1 - slot)
        pv(jnp.maximum(j - 1, 0))
        softmax(slot)

    qk(0, 0)

    @pl.loop(0, i // 2)
    def _(k):
        stage(2 * k, 0)
        stage(2 * k + 1, 1)

    def tail(parity):
        if parity == 1:
            stage(i - 1, 0)
        pv(jnp.maximum(i - 1, 0))
        softmax(parity, diagonal=True)
        pv(i)

    odd = lax.rem(i, 2) == 1
    pl.when(odd)(lambda: tail(1))
    pl.when(jnp.logical_not(odd))(lambda: tail(0))

    for h in heads:
        out_ref[h * HEAD_DIM:(h + 1) * HEAD_DIM, pl.ds(q_off, tq)] = (acc_ref[h] / l_ref[h]).astype(BF16)


def _fox(feat, small, f_bias_col, q_gain_col, k_gain_col, batch, seq):
    tq = min(TQ_FOX, seq)
    tk = min(TK_FOX, tq)
    assert tq == tk, "the masked block of a query tile is exactly one key block"
    t = batch * seq
    width = FOX_HEADS * HEAD_DIM
    blk = lambda row: pl.BlockSpec((width, seq), lambda b: (row // width, b))
    col = lambda n: pl.BlockSpec((n, 1), lambda b: (0, 0))
    return pl.pallas_call(
        functools.partial(_fox_kernel, seq=seq, tq=tq, tk=tk),
        grid=(batch,),
        in_specs=[blk(ROW_FQ), blk(ROW_FK), blk(ROW_FV),
                  pl.BlockSpec((N_SMALL, seq), lambda b: (0, b)),
                  col(N_SMALL), col(HEAD_DIM), col(HEAD_DIM)],
        out_specs=pl.BlockSpec((width, seq), lambda b: (0, b)),
        out_shape=jax.ShapeDtypeStruct((width, t), BF16),
        scratch_shapes=[pltpu.VMEM((FOX_HEADS, 2 * HEAD_DIM, seq), BF16),
                        pltpu.VMEM((FOX_HEADS, seq, 2 * HEAD_DIM), BF16),
                        pltpu.VMEM((2, FOX_HEADS, tk, tq), F32),
                        pltpu.VMEM((FOX_HEADS, tk, tq), BF16),
                        pltpu.VMEM((FOX_HEADS, HEAD_DIM, tq), F32),
                        pltpu.VMEM((FOX_HEADS, 1, tq), F32),
                        pltpu.VMEM((FOX_HEADS, 1, tq), F32),
                        pltpu.VMEM((FOX_HEADS, 1, tq), F32)],
        compiler_params=pltpu.CompilerParams(
            dimension_semantics=("parallel",), vmem_limit_bytes=VMEM_LIMIT),
        name="fox_attention",
    )(feat, feat, feat, small, f_bias_col, q_gain_col, k_gain_col)


def _short_conv_tile(ct_ref, w_ref, carry_ref, sequence_start):
    width = w_ref.shape[1]
    u = ct_ref[:, COL_CU:COL_CU + width].astype(F32)
    b_gate = ct_ref[:, COL_CB:COL_CB + width].astype(F32)
    c_gate = ct_ref[:, COL_CC:COL_CC + width].astype(F32)
    z = c_gate * u
    prev = jnp.where(sequence_start, 0.0, carry_ref[...])
    carry_ref[...] = z[z.shape[0] - 8:, :]
    last1, last2 = prev[7:8, :], prev[6:7, :]
    row = lax.broadcasted_iota(jnp.int32, z.shape, 0)
    z1 = jnp.where(row >= 1, pltpu.roll(z, 1, 0), last1)
    z2 = jnp.where(row >= 2, pltpu.roll(z, 2, 0), jnp.where(row == 1, last1, last2))
    w = w_ref[...]
    return (b_gate * (w[0:1, :] * z2 + w[1:2, :] * z1 + w[2:3, :] * z)).astype(BF16)


def _mlstm_kernel(q_ref, k_ref, v_ref, o_ref, small_ref, bias_ref, gain_ref, out_ref,
                  c_ref, m_ref, b_ref, u_ref, ut_ref, *, chunk):
    c_ref[...] = jnp.zeros_like(c_ref)
    m_ref[...] = jnp.zeros_like(m_ref)
    g = small_ref[...] + bias_ref[...]
    b_all = _cumsum_lanes(_log_sigmoid(g), segment=chunk)
    b_ref[...] = b_all
    u_all = _row_select([g[4 + h:5 + h, :] - b_all[8 + h:9 + h, :] for h in range(MLSTM_HEADS)],
                        8, g.shape[1])
    u_ref[...] = u_all
    ut_ref[...] = jnp.concatenate([u_all, jnp.zeros((120, g.shape[1]), F32)], axis=0).T
    refs = (q_ref, k_ref, v_ref, o_ref, gain_ref, out_ref, c_ref, m_ref, b_ref, u_ref, ut_ref)
    pl.loop(0, g.shape[1] // chunk)(lambda c: _mlstm_chunk(c, *refs, chunk=chunk))


def _mlstm_chunk(c, q_ref, k_ref, v_ref, o_ref, gain_ref, out_ref, c_ref, m_ref, b_ref, u_ref, ut_ref, *, chunk):
    off = pl.multiple_of(c * chunk, chunk)
    cols = pl.ds(off, chunk)
    bcum = b_ref[:, cols]
    rows = [u_ref[h:h + 1, cols] for h in range(MLSTM_HEADS)]
    u_cols = ut_ref[cols, :]

    src = lax.broadcasted_iota(jnp.int32, (chunk, chunk), 0)
    tgt = lax.broadcasted_iota(jnp.int32, (chunk, chunk), 1)
    causal = src <= tgt
    ones_rows = (lax.broadcasted_iota(jnp.int32, (HEAD_DIM, chunk), 0) == 0).astype(BF16)

    heads = range(MLSTM_HEADS)
    sls = [slice(h * HEAD_DIM, (h + 1) * HEAD_DIM) for h in heads]
    qs = [(q_ref[sl, cols].astype(F32) * QK_SCALE).astype(BF16) for sl in sls]
    ks = [k_ref[sl, cols] for sl in sls]
    c_prev = [c_ref[h] for h in heads]
    scores = [_dot_tn(ks[h], qs[h]) for h in heads]
    carried = [_dot(c_prev[h].astype(BF16), qs[h]) for h in heads]

    sw, kw, m_t, inter, decay = [], [], [], [], []
    for h in heads:
        b_row = bcum[8 + h:9 + h, :]
        b_last = b_row[:, chunk - 1:chunk]
        u_col = u_cols[:, h:h + 1]
        m_prev = m_ref[h][0:1, 0:1]
        dmat = jnp.where(causal, b_row + u_col, NEG)
        g_row = b_row + m_prev
        m_h = jnp.maximum(g_row, jnp.max(dmat, axis=0, keepdims=True))
        sw.append((scores[h] * jnp.exp(dmat - m_h)).astype(BF16))
        m_t.append(m_h)
        inter.append(jnp.exp(g_row - m_h))
        m_loc = jnp.max(rows[h] + b_last, axis=1, keepdims=True)
        m_new = jnp.maximum(b_last + m_prev, m_loc)
        decay.append(jnp.exp(b_last + m_prev - m_new))
        w_row = jnp.exp(rows[h] + b_last - m_new)
        kw.append((ks[h].astype(F32) * w_row).astype(BF16))
        m_ref[h] = jnp.broadcast_to(m_new, m_ref.shape[1:])

    v_aug = [jnp.concatenate([v_ref[sl, cols], ones_rows], axis=0) for sl in sls]
    intra = [_dot(v_aug[h], sw[h]) for h in heads]
    update = [_dot_nt(v_aug[h], kw[h]) for h in heads]
    for h in heads:
        c_ref[h] = decay[h] * c_prev[h] + update[h]
        tot = intra[h] + inter[h] * carried[h]
        num = tot[0:HEAD_DIM]
        den = tot[HEAD_DIM:HEAD_DIM + 1]
        ht = num / jnp.maximum(jnp.abs(den), jnp.exp(-m_t[h]))
        hn = _rms_rows(ht, gain_ref[sls[h], :])
        out_ref[sls[h], cols] = (_sigmoid(o_ref[sls[h], cols].astype(F32)) * hn).astype(BF16)


def _mlstm(feat, small, bias_col, gain_col, batch, seq):
    chunk = min(L_MLSTM, seq)
    t = batch * seq
    width = MLSTM_HEADS * HEAD_DIM
    blk = lambda row: pl.BlockSpec((width, seq), lambda b: (row // width, b))
    return pl.pallas_call(
        functools.partial(_mlstm_kernel, chunk=chunk),
        grid=(batch,),
        in_specs=[blk(ROW_MQ), blk(ROW_MK), blk(ROW_MV), blk(ROW_MO),
                  pl.BlockSpec((N_SMALL, seq), lambda b: (0, b)),
                  pl.BlockSpec((N_SMALL, 1), lambda b: (0, 0)),
                  pl.BlockSpec((width, 1), lambda b: (0, 0))],
        out_specs=pl.BlockSpec((width, seq), lambda b: (0, b)),
        out_shape=jax.ShapeDtypeStruct((width, t), BF16),
        scratch_shapes=[pltpu.VMEM((MLSTM_HEADS, 2 * HEAD_DIM, HEAD_DIM), F32),
                        pltpu.VMEM((MLSTM_HEADS, 8, 128), F32),
                        pltpu.VMEM((N_SMALL, seq), F32),
                        pltpu.VMEM((8, seq), F32),
                        pltpu.VMEM((seq, 128), F32)],
        compiler_params=pltpu.CompilerParams(
            dimension_semantics=("parallel",), vmem_limit_bytes=VMEM_LIMIT),
        name="mlstm",
    )(feat, feat, feat, feat, small, bias_col, gain_col)


def _swa_kernel(q_ref, k_ref, v_ref, bias_ref, sink_ref, gq_ref, gk_ref, out_ref, *, nblk):
    refs = (q_ref, k_ref, v_ref, bias_ref, sink_ref, gq_ref, gk_ref, out_ref)
    pl.loop(0, q_ref.shape[1] // (nblk * WINDOW))(lambda n: _swa_span(n, *refs, nblk=nblk))


def _swa_span(n, q_ref, k_ref, v_ref, bias_ref, sink_ref, gq_ref, gk_ref, out_ref, *, nblk):
    w = WINDOW
    off = pl.multiple_of(n * (nblk * w), nblk * w)
    cur = pl.ds(off, nblk * w)
    prev = pl.ds(pl.multiple_of(jnp.maximum(off - w, 0), w), w)
    key = lax.broadcasted_iota(jnp.int32, (2 * w, SWA_GROUP * w), 0)
    first_block_pad = jnp.logical_and(n == 0, key < w)
    kvs = range(SWA_KV_HEADS)
    ksl = [slice(kv * HEAD_DIM, (kv + 1) * HEAD_DIM) for kv in kvs]
    kt = [jnp.concatenate([_rms_rows(k_ref[s, prev].astype(F32), gk_ref[...]),
                           _rms_rows(k_ref[s, cur].astype(F32), gk_ref[...])], axis=1).astype(BF16) for s in ksl]
    vt = [jnp.concatenate([v_ref[s, prev], v_ref[s, cur]], axis=1) for s in ksl]

    bands = [(blk, kv) for blk in range(nblk) for kv in kvs]
    scores = []
    for blk, kv in bands:
        qs = []
        for g in range(SWA_GROUP):
            hs = slice((kv * SWA_GROUP + g) * HEAD_DIM, (kv * SWA_GROUP + g + 1) * HEAD_DIM)
            q = q_ref[hs, pl.ds(off + blk * w, w)].astype(F32)
            qs.append((_rms_rows(q, gq_ref[...]) * (QK_SCALE * LOG2E)).astype(BF16))
        qt = jnp.concatenate(qs, axis=1)
        s = _dot_tn(kt[kv][:, blk * w:(blk + 2) * w], qt) + bias_ref[kv]
        scores.append(jnp.where(first_block_pad, NEG, s) if blk == 0 else s)
    probs, denoms = [], []
    for (blk, kv), s in zip(bands, scores):
        sink = sink_ref[kv]
        m = jnp.maximum(jnp.max(s, axis=0, keepdims=True), sink)
        p = jnp.exp2(s - m)
        denoms.append(jnp.sum(p, axis=0, keepdims=True) + jnp.exp2(sink - m))
        probs.append(p.astype(BF16))
    outs = [_dot(vt[kv][:, blk * w:(blk + 2) * w], p) for (blk, kv), p in zip(bands, probs)]
    for (blk, kv), o, d in zip(bands, outs, denoms):
        o = o / d
        for g in range(SWA_GROUP):
            hs = slice((kv * SWA_GROUP + g) * HEAD_DIM, (kv * SWA_GROUP + g + 1) * HEAD_DIM)
            out_ref[hs, pl.ds(off + blk * w, w)] = o[:, g * w:(g + 1) * w].astype(BF16)


def _swa(feat, bias_t, sink_rows, q_gain_col, k_gain_col, batch, seq):
    nblk = min(SWA_BLOCKS, seq // WINDOW)
    t = batch * seq
    qw = SWA_Q_HEADS * HEAD_DIM
    kw = SWA_KV_HEADS * HEAD_DIM
    return pl.pallas_call(
        functools.partial(_swa_kernel, nblk=nblk),
        grid=(batch,),
        in_specs=[pl.BlockSpec((qw, seq), lambda b: (ROW_SQ // qw, b)),
                  pl.BlockSpec((kw, seq), lambda b: (ROW_SK // kw, b)),
                  pl.BlockSpec((kw, seq), lambda b: (ROW_SV // kw, b)),
                  pl.BlockSpec(bias_t.shape, lambda b: (0, 0, 0)),
                  pl.BlockSpec(sink_rows.shape, lambda b: (0, 0, 0)),
                  pl.BlockSpec((HEAD_DIM, 1), lambda b: (0, 0)),
                  pl.BlockSpec((HEAD_DIM, 1), lambda b: (0, 0))],
        out_specs=pl.BlockSpec((qw, seq), lambda b: (0, b)),
        out_shape=jax.ShapeDtypeStruct((qw, t), BF16),
        compiler_params=pltpu.CompilerParams(
            dimension_semantics=("parallel",), vmem_limit_bytes=VMEM_LIMIT),
        name="swa_attention",
    )(feat, feat, feat, bias_t, sink_rows, q_gain_col, k_gain_col)


def _merge_ffn_kernel(x_ref, yf_ref, ct_ref, ym_ref, ys_ref, g_ref, cw_ref,
                      wf_ref, wc_ref, wm_ref, ws_ref, wo_ref, gain_ref, wg_ref, wu_ref, wd_ref,
                      out_ref, carry_ref, *, tiles_per_seq):
    def gated(branch, y):
        g = g_ref[:, branch * D_MODEL:(branch + 1) * D_MODEL]
        return (1.0 + jnp.tanh(g.astype(F32))) * y

    y_conv = _short_conv_tile(ct_ref, cw_ref, carry_ref, lax.rem(pl.program_id(0), tiles_per_seq) == 0)
    merged = gated(0, _dot_tn(yf_ref[...], wf_ref[...]))
    merged += gated(1, _dot(y_conv, wc_ref[...]))
    merged += gated(2, _dot_tn(ym_ref[...], wm_ref[...]))
    merged += gated(3, _dot_tn(ys_ref[...], ws_ref[...]))
    x = x_ref[...] + _dot(merged.astype(BF16), wo_ref[...])

    ms = jnp.mean(x * x, axis=-1, keepdims=True)
    hn = (x * lax.rsqrt(ms + EPS) * gain_ref[...]).astype(BF16)
    acc = x
    bounds = list(range(0, D_FF, TF_FFN)) + [D_FF]
    for lo, hi in zip(bounds[:-1], bounds[1:]):
        h = _dot(hn, wg_ref[:, lo:hi])
        act = (h * (1.0 + jnp.tanh(h)) * _dot(hn, wu_ref[:, lo:hi])).astype(BF16)
        acc = acc + _dot(act, wd_ref[lo:hi, :])
    out_ref[...] = acc


def _merge_ffn(x2, y_fox_t, conv_tok, y_mlstm_t, y_swa_t, gates, conv_w, w_fox, w_conv, w_mlstm, w_swa,
               w_out, gain, w_gate_half, w_up, w_down, layer, seq):
    t = x2.shape[0]
    tm = min(TM_MERGE, t, seq)
    row_blk = lambda width: pl.BlockSpec((tm, width), lambda i: (i, 0))
    feat_blk = lambda a: pl.BlockSpec((a.shape[0], tm), lambda i: (0, i))
    full = lambda a: _resident(a.shape[1:], layer)
    return pl.pallas_call(
        functools.partial(_merge_ffn_kernel, tiles_per_seq=seq // tm),
        grid=(t // tm,),
        in_specs=[row_blk(D_MODEL),
                  feat_blk(y_fox_t),
                  row_blk(conv_tok.shape[1]),
                  feat_blk(y_mlstm_t), feat_blk(y_swa_t),
                  row_blk(gates.shape[1]),
                  full(conv_w),
                  full(w_fox), full(w_conv), full(w_mlstm), full(w_swa), full(w_out),
                  full(gain), full(w_gate_half), full(w_up), full(w_down)],
        out_specs=row_blk(D_MODEL),
        out_shape=jax.ShapeDtypeStruct((t, D_MODEL), F32),
        scratch_shapes=[pltpu.VMEM((8, conv_w.shape[2]), F32)],
        compiler_params=pltpu.CompilerParams(
            dimension_semantics=("arbitrary",), vmem_limit_bytes=VMEM_LIMIT),
        name="merge_ffn",
    )(x2, y_fox_t, conv_tok, y_mlstm_t, y_swa_t, gates, conv_w,
      w_fox, w_conv, w_mlstm, w_swa, w_out, gain, w_gate_half, w_up, w_down)


IN_FQKV, IN_FF, IN_CONV, IN_MQ, IN_MK, IN_MV = 0, 768, 772, 1540, 1796, 2052
IN_MI, IN_MF, IN_MO, IN_SQ, IN_SK, IN_SV, IN_GATES, IN_COLS = 2308, 2312, 2316, 2572, 3084, 3212, 3340, 7436
TK_PREP = 256


def _wprep_kernel(wt_ref, tok_ref, feat_ref):
    tok_ref[:, COL_CU:COL_GATES] = wt_ref[IN_CONV:IN_MQ, :].T.astype(BF16)
    for lo in range(0, N_TOK - COL_GATES, TN_IN):
        gates = wt_ref[IN_GATES + lo:IN_GATES + lo + TN_IN, :]
        tok_ref[:, COL_GATES + lo:COL_GATES + lo + TN_IN] = (0.5 * gates).T.astype(BF16)
    for row, lo, hi in ((ROW_FQ, IN_FQKV, IN_FF), (ROW_MQ, IN_MQ, IN_MI), (ROW_SQ, IN_SQ, IN_SK),
                        (ROW_MO, IN_MO, IN_SQ), (ROW_SK, IN_SK, IN_GATES)):
        feat_ref[row:row + hi - lo, :] = wt_ref[lo:hi, :].astype(BF16)
    fox_f = wt_ref[IN_FF:IN_FF + N_SMALL, :]
    mlstm_if = wt_ref[IN_MI - 4:IN_MI - 4 + N_SMALL, :]
    row = lax.broadcasted_iota(jnp.int32, fox_f.shape, 0)
    small = jnp.where(row < 4, fox_f, jnp.where(row < 12, mlstm_if, 0.0))
    feat_ref[N_FEAT:, :] = small.astype(BF16)


def _prep_w_in(w_in):
    depth, d, cols = w_in.shape
    return pl.pallas_call(
        _wprep_kernel,
        grid=(depth, d // TK_PREP),
        in_specs=[pl.BlockSpec((None, cols, TK_PREP), lambda l, j: (l, 0, j))],
        out_specs=[pl.BlockSpec((None, TK_PREP, N_TOK), lambda l, j: (l, j, 0)),
                   pl.BlockSpec((None, N_FEAT + N_SMALL, TK_PREP), lambda l, j: (l, 0, j))],
        out_shape=[jax.ShapeDtypeStruct((depth, d, N_TOK), BF16),
                   jax.ShapeDtypeStruct((depth, N_FEAT + N_SMALL, d), BF16)],
        compiler_params=pltpu.CompilerParams(
            dimension_semantics=("parallel", "parallel"), vmem_limit_bytes=VMEM_LIMIT),
        name="w_in_relayout",
    )(jnp.transpose(w_in, (0, 2, 1)))


def _col(v, n=None):
    v = v.astype(F32).reshape(-1, 1)
    if n is not None and v.shape[0] < n:
        v = jnp.concatenate([v, jnp.zeros((n - v.shape[0], 1), F32)], axis=0)
    return v


def kernel(x, rel_bias, attn_norm, w_in, fox_f_bias, fox_q_gain, fox_k_gain, conv_w, mlstm_i_bias, mlstm_f_bias, mlstm_h_gain, swa_q_gain, swa_k_gain, swa_sinks, w_fox_out, w_conv_out, w_mlstm_out, w_swa_out, w_merge_out, ffn_norm, w_gate, w_up, w_down):
    batch, seq, _ = x.shape
    depth = w_in.shape[0]
    x2 = x.reshape(batch * seq, D_MODEL)
    bias_t = _swa_bias_table(rel_bias)

    w_tok, w_feat_t = _prep_w_in(w_in)
    attn_gain = attn_norm.reshape(depth, 1, D_MODEL)
    ffn_gain = ffn_norm.reshape(depth, 1, D_MODEL)
    w_fox_b, w_conv_b, w_mlstm_b, w_swa_b = (w.astype(BF16) for w in (w_fox_out, w_conv_out, w_mlstm_out, w_swa_out))
    w_merge_half = (0.5 * w_merge_out).astype(BF16)
    w_gate_half = (0.5 * w_gate).astype(BF16)
    w_up_b = w_up.astype(BF16)
    w_down_b = w_down.astype(BF16)

    for l in range(depth):
        tok, gates, feat, small = _inproj(x2, attn_gain, w_tok, w_feat_t, l)

        y_fox_t = _fox(feat, small, _col(fox_f_bias[l], N_SMALL), _col(fox_q_gain[l]), _col(fox_k_gain[l]),
                       batch, seq)
        gate_bias = jnp.concatenate([jnp.zeros((4,), F32), mlstm_i_bias[l], mlstm_f_bias[l]])
        y_mlstm_t = _mlstm(feat, small, _col(gate_bias, N_SMALL), _col(mlstm_h_gain[l]), batch, seq)
        sink_rows = jnp.broadcast_to(
            (swa_sinks[l].astype(F32) * LOG2E).reshape(SWA_KV_HEADS, 1, SWA_GROUP, 1),
            (SWA_KV_HEADS, 1, SWA_GROUP, WINDOW)).reshape(SWA_KV_HEADS, 1, SWA_GROUP * WINDOW)
        y_swa_t = _swa(feat, bias_t, sink_rows, _col(swa_q_gain[l]), _col(swa_k_gain[l]), batch, seq)

        x2 = _merge_ffn(x2, y_fox_t, tok, y_mlstm_t, y_swa_t, gates, conv_w,
                        w_fox_b, w_conv_b, w_mlstm_b, w_swa_b, w_merge_half,
                        ffn_gain, w_gate_half, w_up_b, w_down_b, l, seq)
    return x2.reshape(batch, seq, D_MODEL)
```

```python
import functools

import numpy as np
import jax
import jax.numpy as jnp
from jax import lax
from jax.experimental import pallas as pl
from jax.experimental.pallas import tpu as pltpu

F32 = jnp.float32
BF16 = jnp.bfloat16

D_MODEL = 1024
HEAD_DIM = 64
FOX_HEADS = 4
MLSTM_HEADS = 4
SWA_Q_HEADS = 8
SWA_KV_HEADS = 2
SWA_GROUP = SWA_Q_HEADS // SWA_KV_HEADS
WINDOW = 128
REL_BUCKETS = 32
REL_MAX_DIST = 128
D_FF = 2816
EPS = 1e-6
NEG = -1e30
QK_SCALE = HEAD_DIM ** -0.5
LOG2E = 1.4426950408889634

ROW_FQ, ROW_FK, ROW_FV = 0, 256, 512
ROW_MQ, ROW_MK, ROW_MV = 768, 1024, 1280
ROW_SQ, ROW_MO, ROW_SK, ROW_SV = 1536, 2048, 2304, 2432
N_FEAT = 2560
N_SMALL = 16
COL_CU, COL_CB, COL_CC, COL_GATES = 0, 256, 512, 768
N_TOK = 4864

TM_IN = 512
TN_IN = 512
FEAT_CHUNK = 640
FOX_GROUP = 2
TQ_FOX = 256
TK_FOX = 256
L_MLSTM = 256
MLSTM_GROUP = 2
SWA_BLOCKS = 4
TM_MERGE = 512
TF_FFN = 512
VMEM_LIMIT = 56 * 1024 * 1024


def _dot(a, b):
    return jnp.dot(a, b, preferred_element_type=F32)


def _dot_nt(a, b):
    return lax.dot_general(a, b, (((1,), (1,)), ((), ())), preferred_element_type=F32)


def _dot_tn(a, b):
    return lax.dot_general(a, b, (((0,), (0,)), ((), ())), preferred_element_type=F32)


def _sigmoid(x):
    return 0.5 * jnp.tanh(0.5 * x) + 0.5


def _log_sigmoid(x):
    return jnp.minimum(x, 0.0) - jnp.log(1.0 + jnp.exp(-jnp.abs(x)))


def _cumsum_lanes(x, segment=None):
    n = segment or x.shape[-1]
    lane = lax.broadcasted_iota(jnp.int32, x.shape, x.ndim - 1) & (n - 1)
    k = 1
    while k < n:
        x = x + jnp.where(lane >= k, pltpu.roll(x, k, x.ndim - 1), 0.0)
        k *= 2
    return x


def _row_select(rows, n_rows, width):
    rid = lax.broadcasted_iota(jnp.int32, (n_rows, width), 0)
    out = jnp.zeros((n_rows, width), F32)
    for r, v in enumerate(rows):
        out = jnp.where(rid == r, v, out)
    return out


def _split3(c):
    hi = c.astype(BF16).astype(F32)
    r = c - hi
    mid = r.astype(BF16).astype(F32)
    lo = (r - mid).astype(BF16).astype(F32)
    return hi, mid, lo


def _rms_rows(xt, gain_col):
    ms = jnp.mean(xt * xt, axis=0, keepdims=True)
    return xt * lax.rsqrt(ms + EPS) * gain_col


def _bias_kernel(rb_ref, idx_ref, out_ref):
    kv = pl.program_id(0)
    g = pl.program_id(1)
    head = kv * SWA_GROUP + g
    idx = idx_ref[...]
    acc = jnp.full(idx.shape, NEG, F32)
    for b in range(REL_BUCKETS):
        acc = jnp.where(idx == b, rb_ref[b * SWA_Q_HEADS + head] * LOG2E, acc)
    out_ref[0] = acc


def _bucket_table():
    j = np.arange(2 * WINDOW)[:, None]
    i = np.arange(WINDOW)[None, :]
    dist = i + WINDOW - j
    n = np.maximum(dist, 0).astype(np.int32)
    max_exact = REL_BUCKETS // 2
    nf = np.maximum(n, 1).astype(np.float32)
    large = max_exact + (np.log(nf / np.float32(max_exact)) / np.float32(np.log(REL_MAX_DIST / max_exact))
                         * np.float32(REL_BUCKETS - max_exact)).astype(np.int32)
    large = np.minimum(large, REL_BUCKETS - 1)
    bucket = np.where(n < max_exact, n, large)
    return np.where((dist >= 0) & (dist < WINDOW), bucket, -1).astype(np.int32)


def _swa_bias_table(rel_bias):
    idx = jnp.asarray(_bucket_table())
    return pl.pallas_call(
        _bias_kernel,
        grid=(SWA_KV_HEADS, SWA_GROUP),
        in_specs=[
            pl.BlockSpec(memory_space=pltpu.SMEM),
            pl.BlockSpec((2 * WINDOW, WINDOW), lambda kv, g: (0, 0)),
        ],
        out_specs=pl.BlockSpec((1, 2 * WINDOW, WINDOW), lambda kv, g: (kv, 0, g)),
        out_shape=jax.ShapeDtypeStruct((SWA_KV_HEADS, 2 * WINDOW, SWA_GROUP * WINDOW), F32),
        name="swa_bias_table",
    )(rel_bias.reshape(-1), idx)


def _inproj_kernel(x_ref, g_ref, wt_ref, wf_ref, tok_ref, gates_ref, feat_ref, small_ref):
    x = x_ref[...]
    ms = jnp.mean(x * x, axis=-1, keepdims=True)
    xn = (x * lax.rsqrt(ms + EPS) * g_ref[...]).astype(BF16)
    for out_ref, base, width in ((tok_ref, COL_CU, COL_GATES), (gates_ref, COL_GATES, N_TOK - COL_GATES)):
        for lo in range(0, width, TN_IN):
            hi = min(lo + TN_IN, width)
            out_ref[:, lo:hi] = _dot(xn, wt_ref[:, base + lo:base + hi]).astype(BF16)
    n_chunks = N_FEAT // FEAT_CHUNK
    for c in range(n_chunks):
        lo = c * FEAT_CHUNK
        hi = lo + FEAT_CHUNK + (N_SMALL if c == n_chunks - 1 else 0)
        r = _dot_nt(wf_ref[lo:hi, :], xn)
        feat_ref[lo:lo + FEAT_CHUNK, :] = r[:FEAT_CHUNK].astype(BF16)
        if c == n_chunks - 1:
            small_ref[...] = r[FEAT_CHUNK:]


def _resident(shape, layer=None):
    if layer is None:
        return pl.BlockSpec(shape, lambda *_: (0,) * len(shape), pipeline_mode=pl.Buffered(1))
    return pl.BlockSpec((None,) + tuple(shape), lambda *_: (layer,) + (0,) * len(shape),
                        pipeline_mode=pl.Buffered(1))


def _inproj(x2, gain, w_tok, w_feat_t, layer):
    t = x2.shape[0]
    tm = min(TM_IN, t)
    return pl.pallas_call(
        _inproj_kernel,
        grid=(t // tm,),
        in_specs=[
            pl.BlockSpec((tm, D_MODEL), lambda i: (i, 0)),
            _resident((1, D_MODEL), layer),
            _resident((D_MODEL, N_TOK), layer),
            _resident((N_FEAT + N_SMALL, D_MODEL), layer),
        ],
        out_specs=[
            pl.BlockSpec((tm, COL_GATES), lambda i: (i, 0)),
            pl.BlockSpec((tm, N_TOK - COL_GATES), lambda i: (i, 0)),
            pl.BlockSpec((N_FEAT, tm), lambda i: (0, i)),
            pl.BlockSpec((N_SMALL, tm), lambda i: (0, i)),
        ],
        out_shape=[
            jax.ShapeDtypeStruct((t, COL_GATES), BF16),
            jax.ShapeDtypeStruct((t, N_TOK - COL_GATES), BF16),
            jax.ShapeDtypeStruct((N_FEAT, t), BF16),
            jax.ShapeDtypeStruct((N_SMALL, t), F32),
        ],
        compiler_params=pltpu.CompilerParams(
            dimension_semantics=("parallel",), vmem_limit_bytes=VMEM_LIMIT),
        name="inproj",
    )(x2, gain, w_tok, w_feat_t)


def _fox_kernel(fq_ref, fk_ref, fv_ref, small_ref, fb_ref, gq_ref, gk_ref, out_ref,
                qa_ref, ka_ref, s_ref, p_ref, acc_ref, m_ref, l_ref, a_ref, *, seq, tq, tk):
    n_units = qa_ref.shape[0]
    c_all = _cumsum_lanes(_log_sigmoid(small_ref[...] + fb_ref[...]), segment=seq)
    for u in range(n_units):
        e, h = divmod(u, FOX_HEADS)
        sl = slice(h * HEAD_DIM, (h + 1) * HEAD_DIM)
        cols = slice(e * seq, (e + 1) * seq)
        c = c_all[h:h + 1, cols] * LOG2E
        hi, mid, lo = _split3(c)
        one = jnp.ones_like(c)
        qn = _rms_rows(fq_ref[sl, cols].astype(F32), gq_ref[...]) * (QK_SCALE * LOG2E)
        qa_ref[u, 0:HEAD_DIM, :] = qn.astype(BF16)
        qa_ref[u, HEAD_DIM:, :] = _row_select([hi, mid, lo, one, one, one], HEAD_DIM, seq).astype(BF16)
        kn = _rms_rows(fk_ref[sl, cols].astype(F32), gk_ref[...])
        k_aug = _row_select([one, one, one, -hi, -mid, -lo], HEAD_DIM, seq)
        ka_ref[u] = jnp.concatenate([kn, k_aug], axis=0).T.astype(BF16)
    refs = (fv_ref, out_ref, qa_ref, ka_ref, s_ref, p_ref, acc_ref, m_ref, l_ref, a_ref)
    pl.loop(0, seq // tq)(lambda i: _fox_tile(i, *refs, seq=seq, tq=tq, tk=tk))


def _fox_tile(i, fv_ref, out_ref, qa_ref, ka_ref, s_ref, p_ref, acc_ref, m_ref, l_ref, a_ref, *, seq, tq, tk):
    heads = range(qa_ref.shape[0])
    rows = lambda u: slice((u % FOX_HEADS) * HEAD_DIM, (u % FOX_HEADS + 1) * HEAD_DIM)
    base = lambda u: (u // FOX_HEADS) * seq
    q_off = pl.multiple_of(i * tq, tq)
    qa = [qa_ref[h, :, pl.ds(q_off, tq)] for h in heads]

    m_ref[...] = jnp.full(m_ref.shape, NEG, F32)
    l_ref[...] = jnp.zeros_like(l_ref)
    acc_ref[...] = jnp.zeros_like(acc_ref)
    a_ref[...] = jnp.ones_like(a_ref)
    p_ref[...] = jnp.zeros_like(p_ref)

    def qk(j, slot):
        off = pl.multiple_of(j * tk, tk)
        for h in heads:
            s_ref[slot, h] = _dot(ka_ref[h, pl.ds(off, tk), :], qa[h])

    def pv(j):
        off = pl.multiple_of(j * tk, tk)
        for h in heads:
            v = fv_ref[rows(h), pl.ds(pl.multiple_of(base(h) + off, tk), tk)]
            acc_ref[h] = a_ref[h] * acc_ref[h] + _dot(v, p_ref[h])

    def softmax(slot, diagonal=False):
        for h in heads:
            s = s_ref[slot, h]
            if diagonal:
                key = lax.broadcasted_iota(jnp.int32, s.shape, 0)
                qry = lax.broadcasted_iota(jnp.int32, s.shape, 1)
                s = jnp.where(key <= qry, s, NEG)
            m = m_ref[h]
            m_new = jnp.maximum(m, jnp.max(s, axis=0, keepdims=True))
            p = jnp.exp2(s - m_new)
            alpha = jnp.exp2(m - m_new)
            l_ref[h] = alpha * l_ref[h] + jnp.sum(p, axis=0, keepdims=True)
            m_ref[h] = m_new
            a_ref[h] = alpha
            p_ref[h] = p.astype(BF16)

    def stage(j, slot):
        qk(j + 1, 1 - slot)
        pv(jnp.maximum(j - 1, 0))
        softmax(slot)

    qk(0, 0)

    @pl.loop(0, i // 2)
    def _(k):
        stage(2 * k, 0)
        stage(2 * k + 1, 1)

    def tail(parity):
        if parity == 1:
            stage(i - 1, 0)
        pv(jnp.maximum(i - 1, 0))
        softmax(parity, diagonal=True)
        pv(i)

    odd = lax.rem(i, 2) == 1
    pl.when(odd)(lambda: tail(1))
    pl.when(jnp.logical_not(odd))(lambda: tail(0))

    for h in heads:
        out_ref[rows(h), pl.ds(pl.multiple_of(base(h) + q_off, tq), tq)] = (acc_ref[h] / l_ref[h]).astype(BF16)


def _fox(feat, small, f_bias_col, q_gain_col, k_gain_col, batch, seq):
    tq = min(TQ_FOX, seq)
    tk = min(TK_FOX, tq)
    assert tq == tk, "the masked block of a query tile is exactly one key block"
    group = FOX_GROUP if batch % FOX_GROUP == 0 else 1
    span = group * seq
    units = group * FOX_HEADS
    t = batch * seq
    width = FOX_HEADS * HEAD_DIM
    blk = lambda row: pl.BlockSpec((width, span), lambda b: (row // width, b))
    col = lambda n: pl.BlockSpec((n, 1), lambda b: (0, 0))
    return pl.pallas_call(
        functools.partial(_fox_kernel, seq=seq, tq=tq, tk=tk),
        grid=(batch // group,),
        in_specs=[blk(ROW_FQ), blk(ROW_FK), blk(ROW_FV),
                  pl.BlockSpec((N_SMALL, span), lambda b: (0, b)),
                  col(N_SMALL), col(HEAD_DIM), col(HEAD_DIM)],
        out_specs=pl.BlockSpec((width, span), lambda b: (0, b)),
        out_shape=jax.ShapeDtypeStruct((width, t), BF16),
        scratch_shapes=[pltpu.VMEM((units, 2 * HEAD_DIM, seq), BF16),
                        pltpu.VMEM((units, seq, 2 * HEAD_DIM), BF16),
                        pltpu.VMEM((2, units, tk, tq), F32),
                        pltpu.VMEM((units, tk, tq), BF16),
                        pltpu.VMEM((units, HEAD_DIM, tq), F32),
                        pltpu.VMEM((units, 1, tq), F32),
                        pltpu.VMEM((units, 1, tq), F32),
                        pltpu.VMEM((units, 1, tq), F32)],
        compiler_params=pltpu.CompilerParams(
            dimension_semantics=("parallel",), vmem_limit_bytes=VMEM_LIMIT),
        name="fox_attention",
    )(feat, feat, feat, small, f_bias_col, q_gain_col, k_gain_col)


def _short_conv_tile(ct_ref, w_ref, carry_ref, sequence_start):
    width = w_ref.shape[1]
    u = ct_ref[:, COL_CU:COL_CU + width].astype(F32)
    b_gate = ct_ref[:, COL_CB:COL_CB + width].astype(F32)
    c_gate = ct_ref[:, COL_CC:COL_CC + width].astype(F32)
    z = c_gate * u
    prev = jnp.where(sequence_start, 0.0, carry_ref[...])
    carry_ref[...] = z[z.shape[0] - 8:, :]
    last1, last2 = prev[7:8, :], prev[6:7, :]
    row = lax.broadcasted_iota(jnp.int32, z.shape, 0)
    z1 = jnp.where(row >= 1, pltpu.roll(z, 1, 0), last1)
    z2 = jnp.where(row >= 2, pltpu.roll(z, 2, 0), jnp.where(row == 1, last1, last2))
    w = w_ref[...]
    return (b_gate * (w[0:1, :] * z2 + w[1:2, :] * z1 + w[2:3, :] * z)).astype(BF16)


def _mlstm_kernel(q_ref, k_ref, v_ref, o_ref, small_ref, bias_ref, gain_ref, out_ref,
                  c_ref, m_ref, b_ref, u_ref, ut_ref, *, seq, chunk):
    c_ref[...] = jnp.zeros_like(c_ref)
    m_ref[...] = jnp.zeros_like(m_ref)
    g = small_ref[...] + bias_ref[...]
    b_all = _cumsum_lanes(_log_sigmoid(g), segment=chunk)
    b_ref[...] = b_all
    u_all = _row_select([g[4 + h:5 + h, :] - b_all[8 + h:9 + h, :] for h in range(MLSTM_HEADS)],
                        8, g.shape[1])
    u_ref[...] = u_all
    ut_ref[...] = jnp.concatenate([u_all, jnp.zeros((120, g.shape[1]), F32)], axis=0).T
    refs = (q_ref, k_ref, v_ref, o_ref, gain_ref, out_ref, c_ref, m_ref, b_ref, u_ref, ut_ref)
    pl.loop(0, seq // chunk)(lambda c: _mlstm_chunk(c, *refs, seq=seq, chunk=chunk))


def _mlstm_chunk(c, q_ref, k_ref, v_ref, o_ref, gain_ref, out_ref, c_ref, m_ref, b_ref, u_ref, ut_ref, *,
                 seq, chunk):
    units = range(c_ref.shape[0])
    head = lambda u: u % MLSTM_HEADS
    sl = lambda u: slice(head(u) * HEAD_DIM, (head(u) + 1) * HEAD_DIM)
    cols = lambda u: pl.ds(pl.multiple_of((u // MLSTM_HEADS) * seq + c * chunk, chunk), chunk)

    src = lax.broadcasted_iota(jnp.int32, (chunk, chunk), 0)
    tgt = lax.broadcasted_iota(jnp.int32, (chunk, chunk), 1)
    causal = src <= tgt
    ones_rows = (lax.broadcasted_iota(jnp.int32, (HEAD_DIM, chunk), 0) == 0).astype(BF16)

    qs = [(q_ref[sl(u), cols(u)].astype(F32) * QK_SCALE).astype(BF16) for u in units]
    ks = [k_ref[sl(u), cols(u)] for u in units]
    c_prev = [c_ref[u] for u in units]
    scores = [_dot_tn(ks[u], qs[u]) for u in units]
    carried = [_dot(c_prev[u].astype(BF16), qs[u]) for u in units]

    sw, kw, m_t, inter, decay = [], [], [], [], []
    for u in units:
        h = head(u)
        b_row = b_ref[8 + h:9 + h, cols(u)]
        b_last = b_row[:, chunk - 1:chunk]
        u_row = u_ref[h:h + 1, cols(u)]
        u_col = ut_ref[cols(u), h:h + 1]
        m_prev = m_ref[u][0:1, 0:1]
        dmat = jnp.where(causal, b_row + u_col, NEG)
        g_row = b_row + m_prev
        m_u = jnp.maximum(g_row, jnp.max(dmat, axis=0, keepdims=True))
        sw.append((scores[u] * jnp.exp(dmat - m_u)).astype(BF16))
        m_t.append(m_u)
        inter.append(jnp.exp(g_row - m_u))
        m_loc = jnp.max(u_row + b_last, axis=1, keepdims=True)
        m_new = jnp.maximum(b_last + m_prev, m_loc)
        decay.append(jnp.exp(b_last + m_prev - m_new))
        w_row = jnp.exp(u_row + b_last - m_new)
        kw.append((ks[u].astype(F32) * w_row).astype(BF16))
        m_ref[u] = jnp.broadcast_to(m_new, m_ref.shape[1:])

    v_aug = [jnp.concatenate([v_ref[sl(u), cols(u)], ones_rows], axis=0) for u in units]
    intra = [_dot(v_aug[u], sw[u]) for u in units]
    update = [_dot_nt(v_aug[u], kw[u]) for u in units]
    for u in units:
        c_ref[u] = decay[u] * c_prev[u] + update[u]
        tot = intra[u] + inter[u] * carried[u]
        num = tot[0:HEAD_DIM]
        den = tot[HEAD_DIM:HEAD_DIM + 1]
        ht = num / jnp.maximum(jnp.abs(den), jnp.exp(-m_t[u]))
        hn = _rms_rows(ht, gain_ref[sl(u), :])
        out_ref[sl(u), cols(u)] = (_sigmoid(o_ref[sl(u), cols(u)].astype(F32)) * hn).astype(BF16)


def _mlstm(feat, small, bias_col, gain_col, batch, seq):
    chunk = min(L_MLSTM, seq)
    group = MLSTM_GROUP if batch % MLSTM_GROUP == 0 else 1
    span = group * seq
    units = group * MLSTM_HEADS
    t = batch * seq
    width = MLSTM_HEADS * HEAD_DIM
    blk = lambda row: pl.BlockSpec((width, span), lambda b: (row // width, b))
    return pl.pallas_call(
        functools.partial(_mlstm_kernel, seq=seq, chunk=chunk),
        grid=(batch // group,),
        in_specs=[blk(ROW_MQ), blk(ROW_MK), blk(ROW_MV), blk(ROW_MO),
                  pl.BlockSpec((N_SMALL, span), lambda b: (0, b)),
                  pl.BlockSpec((N_SMALL, 1), lambda b: (0, 0)),
                  pl.BlockSpec((width, 1), lambda b: (0, 0))],
        out_specs=pl.BlockSpec((width, span), lambda b: (0, b)),
        out_shape=jax.ShapeDtypeStruct((width, t), BF16),
        scratch_shapes=[pltpu.VMEM((units, 2 * HEAD_DIM, HEAD_DIM), F32),
                        pltpu.VMEM((units, 8, 128), F32),
                        pltpu.VMEM((N_SMALL, span), F32),
                        pltpu.VMEM((8, span), F32),
                        pltpu.VMEM((span, 128), F32)],
        compiler_params=pltpu.CompilerParams(
            dimension_semantics=("parallel",), vmem_limit_bytes=VMEM_LIMIT),
        name="mlstm",
    )(feat, feat, feat, feat, small, bias_col, gain_col)


def _swa_kernel(q_ref, k_ref, v_ref, bias_ref, sink_ref, gq_ref, gk_ref, out_ref, *, nblk):
    refs = (q_ref, k_ref, v_ref, bias_ref, sink_ref, gq_ref, gk_ref, out_ref)
    pl.loop(0, q_ref.shape[1] // (nblk * WINDOW))(lambda n: _swa_span(n, *refs, nblk=nblk))


def _swa_span(n, q_ref, k_ref, v_ref, bias_ref, sink_ref, gq_ref, gk_ref, out_ref, *, nblk):
    w = WINDOW
    off = pl.multiple_of(n * (nblk * w), nblk * w)
    cur = pl.ds(off, nblk * w)
    prev = pl.ds(pl.multiple_of(jnp.maximum(off - w, 0), w), w)
    key = lax.broadcasted_iota(jnp.int32, (2 * w, SWA_GROUP * w), 0)
    first_block_pad = jnp.logical_and(n == 0, key < w)
    kvs = range(SWA_KV_HEADS)
    ksl = [slice(kv * HEAD_DIM, (kv + 1) * HEAD_DIM) for kv in kvs]
    kt = [jnp.concatenate([_rms_rows(k_ref[s, prev].astype(F32), gk_ref[...]),
                           _rms_rows(k_ref[s, cur].astype(F32), gk_ref[...])], axis=1).astype(BF16) for s in ksl]
    vt = [jnp.concatenate([v_ref[s, prev], v_ref[s, cur]], axis=1) for s in ksl]

    bands = [(blk, kv) for blk in range(nblk) for kv in kvs]
    scores = []
    for blk, kv in bands:
        qs = []
        for g in range(SWA_GROUP):
            hs = slice((kv * SWA_GROUP + g) * HEAD_DIM, (kv * SWA_GROUP + g + 1) * HEAD_DIM)
            q = q_ref[hs, pl.ds(off + blk * w, w)].astype(F32)
            qs.append((_rms_rows(q, gq_ref[...]) * (QK_SCALE * LOG2E)).astype(BF16))
        qt = jnp.concatenate(qs, axis=1)
        s = _dot_tn(kt[kv][:, blk * w:(blk + 2) * w], qt) + bias_ref[kv]
        scores.append(jnp.where(first_block_pad, NEG, s) if blk == 0 else s)
    probs, denoms = [], []
    for (blk, kv), s in zip(bands, scores):
        sink = sink_ref[kv]
        m = jnp.maximum(jnp.max(s, axis=0, keepdims=True), sink)
        p = jnp.exp2(s - m)
        denoms.append(jnp.sum(p, axis=0, keepdims=True) + jnp.exp2(sink - m))
        probs.append(p.astype(BF16))
    outs = [_dot(vt[kv][:, blk * w:(blk + 2) * w], p) for (blk, kv), p in zip(bands, probs)]
    for (blk, kv), o, d in zip(bands, outs, denoms):
        o = o / d
        for g in range(SWA_GROUP):
            hs = slice((kv * SWA_GROUP + g) * HEAD_DIM, (kv * SWA_GROUP + g + 1) * HEAD_DIM)
            out_ref[hs, pl.ds(off + blk * w, w)] = o[:, g * w:(g + 1) * w].astype(BF16)


def _swa(feat, bias_t, sink_rows, q_gain_col, k_gain_col, batch, seq):
    nblk = min(SWA_BLOCKS, seq // WINDOW)
    t = batch * seq
    qw = SWA_Q_HEADS * HEAD_DIM
    kw = SWA_KV_HEADS * HEAD_DIM
    return pl.pallas_call(
        functools.partial(_swa_kernel, nblk=nblk),
        grid=(batch,),
        in_specs=[pl.BlockSpec((qw, seq), lambda b: (ROW_SQ // qw, b)),
                  pl.BlockSpec((kw, seq), lambda b: (ROW_SK // kw, b)),
                  pl.BlockSpec((kw, seq), lambda b: (ROW_SV // kw, b)),
                  pl.BlockSpec(bias_t.shape, lambda b: (0, 0, 0)),
                  pl.BlockSpec(sink_rows.shape, lambda b: (0, 0, 0)),
                  pl.BlockSpec((HEAD_DIM, 1), lambda b: (0, 0)),
                  pl.BlockSpec((HEAD_DIM, 1), lambda b: (0, 0))],
        out_specs=pl.BlockSpec((qw, seq), lambda b: (0, b)),
        out_shape=jax.ShapeDtypeStruct((qw, t), BF16),
        compiler_params=pltpu.CompilerParams(
            dimension_semantics=("parallel",), vmem_limit_bytes=VMEM_LIMIT),
        name="swa_attention",
    )(feat, feat, feat, bias_t, sink_rows, q_gain_col, k_gain_col)


def _merge_ffn_kernel(x_ref, yf_ref, ct_ref, ym_ref, ys_ref, g_ref, cw_ref,
                      wf_ref, wc_ref, wm_ref, ws_ref, wo_ref, gain_ref, wg_ref, wu_ref, wd_ref,
                      out_ref, carry_ref, *, tiles_per_seq):
    def gated(branch, y):
        g = g_ref[:, branch * D_MODEL:(branch + 1) * D_MODEL]
        return (1.0 + jnp.tanh(g.astype(F32))) * y

    y_conv = _short_conv_tile(ct_ref, cw_ref, carry_ref, lax.rem(pl.program_id(0), tiles_per_seq) == 0)
    merged = gated(0, _dot_tn(yf_ref[...], wf_ref[...]))
    merged += gated(1, _dot(y_conv, wc_ref[...]))
    merged += gated(2, _dot_tn(ym_ref[...], wm_ref[...]))
    merged += gated(3, _dot_tn(ys_ref[...], ws_ref[...]))
    x = x_ref[...] + _dot(merged.astype(BF16), wo_ref[...])

    ms = jnp.mean(x * x, axis=-1, keepdims=True)
    hn = (x * lax.rsqrt(ms + EPS) * gain_ref[...]).astype(BF16)
    acc = x
    bounds = list(range(0, D_FF, TF_FFN)) + [D_FF]
    for lo, hi in zip(bounds[:-1], bounds[1:]):
        h = _dot(hn, wg_ref[:, lo:hi])
        act = (h * (1.0 + jnp.tanh(h)) * _dot(hn, wu_ref[:, lo:hi])).astype(BF16)
        acc = acc + _dot(act, wd_ref[lo:hi, :])
    out_ref[...] = acc


def _merge_ffn(x2, y_fox_t, conv_tok, y_mlstm_t, y_swa_t, gates, conv_w, w_fox, w_conv, w_mlstm, w_swa,
               w_out, gain, w_gate_half, w_up, w_down, layer, seq):
    t = x2.shape[0]
    tm = min(TM_MERGE, t, seq)
    row_blk = lambda width: pl.BlockSpec((tm, width), lambda i: (i, 0))
    feat_blk = lambda a: pl.BlockSpec((a.shape[0], tm), lambda i: (0, i))
    full = lambda a: _resident(a.shape[1:], layer)
    return pl.pallas_call(
        functools.partial(_merge_ffn_kernel, tiles_per_seq=seq // tm),
        grid=(t // tm,),
        in_specs=[row_blk(D_MODEL),
                  feat_blk(y_fox_t),
                  row_blk(conv_tok.shape[1]),
                  feat_blk(y_mlstm_t), feat_blk(y_swa_t),
                  row_blk(gates.shape[1]),
                  full(conv_w),
                  full(w_fox), full(w_conv), full(w_mlstm), full(w_swa), full(w_out),
                  full(gain), full(w_gate_half), full(w_up), full(w_down)],
        out_specs=row_blk(D_MODEL),
        out_shape=jax.ShapeDtypeStruct((t, D_MODEL), F32),
        scratch_shapes=[pltpu.VMEM((8, conv_w.shape[2]), F32)],
        compiler_params=pltpu.CompilerParams(
            dimension_semantics=("arbitrary",), vmem_limit_bytes=VMEM_LIMIT),
        name="merge_ffn",
    )(x2, y_fox_t, conv_tok, y_mlstm_t, y_swa_t, gates, conv_w,
      w_fox, w_conv, w_mlstm, w_swa, w_out, gain, w_gate_half, w_up, w_down)


IN_FQKV, IN_FF, IN_CONV, IN_MQ, IN_MK, IN_MV = 0, 768, 772, 1540, 1796, 2052
IN_MI, IN_MF, IN_MO, IN_SQ, IN_SK, IN_SV, IN_GATES, IN_COLS = 2308, 2312, 2316, 2572, 3084, 3212, 3340, 7436
TK_PREP = 256


def _wprep_kernel(wt_ref, tok_ref, feat_ref):
    tok_ref[:, COL_CU:COL_GATES] = wt_ref[IN_CONV:IN_MQ, :].T.astype(BF16)
    for lo in range(0, N_TOK - COL_GATES, TN_IN):
        gates = wt_ref[IN_GATES + lo:IN_GATES + lo + TN_IN, :]
        tok_ref[:, COL_GATES + lo:COL_GATES + lo + TN_IN] = (0.5 * gates).T.astype(BF16)
    for row, lo, hi in ((ROW_FQ, IN_FQKV, IN_FF), (ROW_MQ, IN_MQ, IN_MI), (ROW_SQ, IN_SQ, IN_SK),
                        (ROW_MO, IN_MO, IN_SQ), (ROW_SK, IN_SK, IN_GATES)):
        feat_ref[row:row + hi - lo, :] = wt_ref[lo:hi, :].astype(BF16)
    fox_f = wt_ref[IN_FF:IN_FF + N_SMALL, :]
    mlstm_if = wt_ref[IN_MI - 4:IN_MI - 4 + N_SMALL, :]
    row = lax.broadcasted_iota(jnp.int32, fox_f.shape, 0)
    small = jnp.where(row < 4, fox_f, jnp.where(row < 12, mlstm_if, 0.0))
    feat_ref[N_FEAT:, :] = small.astype(BF16)


def _prep_w_in(w_in):
    depth, d, cols = w_in.shape
    return pl.pallas_call(
        _wprep_kernel,
        grid=(depth, d // TK_PREP),
        in_specs=[pl.BlockSpec((None, cols, TK_PREP), lambda l, j: (l, 0, j))],
        out_specs=[pl.BlockSpec((None, TK_PREP, N_TOK), lambda l, j: (l, j, 0)),
                   pl.BlockSpec((None, N_FEAT + N_SMALL, TK_PREP), lambda l, j: (l, 0, j))],
        out_shape=[jax.ShapeDtypeStruct((depth, d, N_TOK), BF16),
                   jax.ShapeDtypeStruct((depth, N_FEAT + N_SMALL, d), BF16)],
        compiler_params=pltpu.CompilerParams(
            dimension_semantics=("parallel", "parallel"), vmem_limit_bytes=VMEM_LIMIT),
        name="w_in_relayout",
    )(jnp.transpose(w_in, (0, 2, 1)))


def _col(v, n=None):
    v = v.astype(F32).reshape(-1, 1)
    if n is not None and v.shape[0] < n:
        v = jnp.concatenate([v, jnp.zeros((n - v.shape[0], 1), F32)], axis=0)
    return v


def kernel(x, rel_bias, attn_norm, w_in, fox_f_bias, fox_q_gain, fox_k_gain, conv_w, mlstm_i_bias, mlstm_f_bias, mlstm_h_gain, swa_q_gain, swa_k_gain, swa_sinks, w_fox_out, w_conv_out, w_mlstm_out, w_swa_out, w_merge_out, ffn_norm, w_gate, w_up, w_down):
    batch, seq, _ = x.shape
    depth = w_in.shape[0]
    x2 = x.reshape(batch * seq, D_MODEL)
    bias_t = _swa_bias_table(rel_bias)

    w_tok, w_feat_t = _prep_w_in(w_in)
    attn_gain = attn_norm.reshape(depth, 1, D_MODEL)
    ffn_gain = ffn_norm.reshape(depth, 1, D_MODEL)
    w_fox_b, w_conv_b, w_mlstm_b, w_swa_b = (w.astype(BF16) for w in (w_fox_out, w_conv_out, w_mlstm_out, w_swa_out))
    w_merge_half = (0.5 * w_merge_out).astype(BF16)
    w_gate_half = (0.5 * w_gate).astype(BF16)
    w_up_b = w_up.astype(BF16)
    w_down_b = w_down.astype(BF16)

    for l in range(depth):
        tok, gates, feat, small = _inproj(x2, attn_gain, w_tok, w_feat_t, l)

        y_fox_t = _fox(feat, small, _col(fox_f_bias[l], N_SMALL), _col(fox_q_gain[l]), _col(fox_k_gain[l]),
                       batch, seq)
        gate_bias = jnp.concatenate([jnp.zeros((4,), F32), mlstm_i_bias[l], mlstm_f_bias[l]])
        y_mlstm_t = _mlstm(feat, small, _col(gate_bias, N_SMALL), _col(mlstm_h_gain[l]), batch, seq)
        sink_rows = jnp.broadcast_to(
            (swa_sinks[l].astype(F32) * LOG2E).reshape(SWA_KV_HEADS, 1, SWA_GROUP, 1),
            (SWA_KV_HEADS, 1, SWA_GROUP, WINDOW)).reshape(SWA_KV_HEADS, 1, SWA_GROUP * WINDOW)
        y_swa_t = _swa(feat, bias_t, sink_rows, _col(swa_q_gain[l]), _col(swa_k_gain[l]), batch, seq)

        x2 = _merge_ffn(x2, y_fox_t, tok, y_mlstm_t, y_swa_t, gates, conv_w,
                        w_fox_b, w_conv_b, w_mlstm_b, w_swa_b, w_merge_half,
                        ffn_gain, w_gate_half, w_up_b, w_down_b, l, seq)
    return x2.reshape(batch, seq, D_MODEL)
```

```python
import functools

import numpy as np
import jax
import jax.numpy as jnp
from jax import lax
from jax.experimental import pallas as pl
from jax.experimental.pallas import tpu as pltpu

F32 = jnp.float32
BF16 = jnp.bfloat16

D_MODEL = 1024
HEAD_DIM = 64
FOX_HEADS = 4
MLSTM_HEADS = 4
SWA_Q_HEADS = 8
SWA_KV_HEADS = 2
SWA_GROUP = SWA_Q_HEADS // SWA_KV_HEADS
WINDOW = 128
REL_BUCKETS = 32
REL_MAX_DIST = 128
D_FF = 2816
EPS = 1e-6
NEG = -1e30
QK_SCALE = HEAD_DIM ** -0.5
LOG2E = 1.4426950408889634

ROW_FQ, ROW_FK, ROW_FV = 0, 256, 512
ROW_MQ, ROW_MK, ROW_MV = 768, 1024, 1280
ROW_SQ, ROW_MO, ROW_SK, ROW_SV = 1536, 2048, 2304, 2432
N_FEAT = 2560
N_SMALL = 16
COL_CU, COL_CB, COL_CC, COL_GATES = 0, 256, 512, 768
N_TOK = 4864

TM_IN = 512
TN_IN = 512
FEAT_CHUNK = 640
FOX_GROUP = 2
TQ_FOX = 256
TK_FOX = 256
L_MLSTM = 256
MLSTM_GROUP = 2
SWA_BLOCKS = 4
TM_MERGE = 512
TF_FFN = 512
VMEM_LIMIT = 56 * 1024 * 1024


def _dot(a, b):
    return jnp.dot(a, b, preferred_element_type=F32)


def _dot_nt(a, b):
    return lax.dot_general(a, b, (((1,), (1,)), ((), ())), preferred_element_type=F32)


def _dot_tn(a, b):
    return lax.dot_general(a, b, (((0,), (0,)), ((), ())), preferred_element_type=F32)


def _sigmoid(x):
    return 0.5 * jnp.tanh(0.5 * x) + 0.5


def _log_sigmoid(x):
    return jnp.minimum(x, 0.0) - jnp.log(1.0 + jnp.exp(-jnp.abs(x)))


def _cumsum_lanes(x, segment=None):
    n = segment or x.shape[-1]
    lane = lax.broadcasted_iota(jnp.int32, x.shape, x.ndim - 1) & (n - 1)
    k = 1
    while k < n:
        x = x + jnp.where(lane >= k, pltpu.roll(x, k, x.ndim - 1), 0.0)
        k *= 2
    return x


def _row_select(rows, n_rows, width):
    rid = lax.broadcasted_iota(jnp.int32, (n_rows, width), 0)
    out = jnp.zeros((n_rows, width), F32)
    for r, v in enumerate(rows):
        out = jnp.where(rid == r, v, out)
    return out


def _split3(c):
    hi = c.astype(BF16).astype(F32)
    r = c - hi
    mid = r.astype(BF16).astype(F32)
    lo = (r - mid).astype(BF16).astype(F32)
    return hi, mid, lo


def _rms_rows(xt, gain_col):
    ms = jnp.mean(xt * xt, axis=0, keepdims=True)
    return xt * lax.rsqrt(ms + EPS) * gain_col


def _bias_kernel(rb_ref, idx_ref, out_ref):
    kv = pl.program_id(0)
    g = pl.program_id(1)
    head = kv * SWA_GROUP + g
    idx = idx_ref[...]
    acc = jnp.full(idx.shape, NEG, F32)
    for b in range(REL_BUCKETS):
        acc = jnp.where(idx == b, rb_ref[b * SWA_Q_HEADS + head] * LOG2E, acc)
    out_ref[0] = acc


def _bucket_table():
    j = np.arange(2 * WINDOW)[:, None]
    i = np.arange(WINDOW)[None, :]
    dist = i + WINDOW - j
    n = np.maximum(dist, 0).astype(np.int32)
    max_exact = REL_BUCKETS // 2
    nf = np.maximum(n, 1).astype(np.float32)
    large = max_exact + (np.log(nf / np.float32(max_exact)) / np.float32(np.log(REL_MAX_DIST / max_exact))
                         * np.float32(REL_BUCKETS - max_exact)).astype(np.int32)
    large = np.minimum(large, REL_BUCKETS - 1)
    bucket = np.where(n < max_exact, n, large)
    return np.where((dist >= 0) & (dist < WINDOW), bucket, -1).astype(np.int32)


def _swa_bias_table(rel_bias):
    idx = jnp.asarray(_bucket_table())
    return pl.pallas_call(
        _bias_kernel,
        grid=(SWA_KV_HEADS, SWA_GROUP),
        in_specs=[
            pl.BlockSpec(memory_space=pltpu.SMEM),
            pl.BlockSpec((2 * WINDOW, WINDOW), lambda kv, g: (0, 0)),
        ],
        out_specs=pl.BlockSpec((1, 2 * WINDOW, WINDOW), lambda kv, g: (kv, 0, g)),
        out_shape=jax.ShapeDtypeStruct((SWA_KV_HEADS, 2 * WINDOW, SWA_GROUP * WINDOW), F32),
        name="swa_bias_table",
    )(rel_bias.reshape(-1), idx)


QK_NORM_ROWS = ((ROW_FQ, ROW_FV), (ROW_SQ, ROW_MO), (ROW_SK, ROW_SV))


def _inproj_kernel(x_ref, g_ref, wt_ref, wf_ref, hs_ref, tok_ref, gates_ref, feat_ref, small_ref):
    x = x_ref[...]
    ms = jnp.mean(x * x, axis=-1, keepdims=True)
    xn = (x * lax.rsqrt(ms + EPS) * g_ref[...]).astype(BF16)
    for out_ref, base, width in ((tok_ref, COL_CU, COL_GATES), (gates_ref, COL_GATES, N_TOK - COL_GATES)):
        for lo in range(0, width, TN_IN):
            hi = min(lo + TN_IN, width)
            out_ref[:, lo:hi] = _dot(xn, wt_ref[:, base + lo:base + hi]).astype(BF16)
    n_chunks = N_FEAT // FEAT_CHUNK
    for c in range(n_chunks):
        lo = c * FEAT_CHUNK
        hi = lo + FEAT_CHUNK + (N_SMALL if c == n_chunks - 1 else 0)
        r = _dot_nt(wf_ref[lo:hi, :], xn)
        for row in range(lo, lo + FEAT_CHUNK, HEAD_DIM):
            head = r[row - lo:row - lo + HEAD_DIM]
            scale = hs_ref[row:row + HEAD_DIM, :]
            if any(a <= row < b for a, b in QK_NORM_ROWS):
                head = _rms_rows(head, scale)
            elif ROW_MQ <= row < ROW_MK:
                head = head * scale
            feat_ref[row:row + HEAD_DIM, :] = head.astype(BF16)
        if c == n_chunks - 1:
            small_ref[...] = r[FEAT_CHUNK:]


def _resident(shape, layer=None):
    if layer is None:
        return pl.BlockSpec(shape, lambda *_: (0,) * len(shape), pipeline_mode=pl.Buffered(1))
    return pl.BlockSpec((None,) + tuple(shape), lambda *_: (layer,) + (0,) * len(shape),
                        pipeline_mode=pl.Buffered(1))


def _inproj(x2, gain, w_tok, w_feat_t, head_scale, layer):
    t = x2.shape[0]
    tm = min(TM_IN, t)
    return pl.pallas_call(
        _inproj_kernel,
        grid=(t // tm,),
        in_specs=[
            pl.BlockSpec((tm, D_MODEL), lambda i: (i, 0)),
            _resident((1, D_MODEL), layer),
            _resident((D_MODEL, N_TOK), layer),
            _resident((N_FEAT + N_SMALL, D_MODEL), layer),
            _resident((N_FEAT, 1), layer),
        ],
        out_specs=[
            pl.BlockSpec((tm, COL_GATES), lambda i: (i, 0)),
            pl.BlockSpec((tm, N_TOK - COL_GATES), lambda i: (i, 0)),
            pl.BlockSpec((N_FEAT, tm), lambda i: (0, i)),
            pl.BlockSpec((N_SMALL, tm), lambda i: (0, i)),
        ],
        out_shape=[
            jax.ShapeDtypeStruct((t, COL_GATES), BF16),
            jax.ShapeDtypeStruct((t, N_TOK - COL_GATES), BF16),
            jax.ShapeDtypeStruct((N_FEAT, t), BF16),
            jax.ShapeDtypeStruct((N_SMALL, t), F32),
        ],
        compiler_params=pltpu.CompilerParams(
            dimension_semantics=("parallel",), vmem_limit_bytes=VMEM_LIMIT),
        name="inproj",
    )(x2, gain, w_tok, w_feat_t, head_scale)


def _fox_kernel(fq_ref, fk_ref, fv_ref, small_ref, fb_ref, out_ref,
                qa_ref, ka_ref, s_ref, p_ref, acc_ref, m_ref, l_ref, a_ref, *, seq, tq, tk):
    n_units = qa_ref.shape[0]
    c_all = _cumsum_lanes(_log_sigmoid(small_ref[...] + fb_ref[...]), segment=seq)
    for u in range(n_units):
        e, h = divmod(u, FOX_HEADS)
        sl = slice(h * HEAD_DIM, (h + 1) * HEAD_DIM)
        cols = slice(e * seq, (e + 1) * seq)
        c = c_all[h:h + 1, cols] * LOG2E
        hi, mid, lo = _split3(c)
        one = jnp.ones_like(c)
        qa_ref[u, 0:HEAD_DIM, :] = fq_ref[sl, cols]
        qa_ref[u, HEAD_DIM:, :] = _row_select([hi, mid, lo, one, one, one], HEAD_DIM, seq).astype(BF16)
        k_aug = _row_select([one, one, one, -hi, -mid, -lo], HEAD_DIM, seq)
        ka_ref[u] = jnp.concatenate([fk_ref[sl, cols].astype(F32), k_aug], axis=0).T.astype(BF16)
    refs = (fv_ref, out_ref, qa_ref, ka_ref, s_ref, p_ref, acc_ref, m_ref, l_ref, a_ref)
    pl.loop(0, seq // tq)(lambda i: _fox_tile(i, *refs, seq=seq, tq=tq, tk=tk))


def _fox_tile(i, fv_ref, out_ref, qa_ref, ka_ref, s_ref, p_ref, acc_ref, m_ref, l_ref, a_ref, *, seq, tq, tk):
    heads = range(qa_ref.shape[0])
    rows = lambda u: slice((u % FOX_HEADS) * HEAD_DIM, (u % FOX_HEADS + 1) * HEAD_DIM)
    base = lambda u: (u // FOX_HEADS) * seq
    q_off = pl.multiple_of(i * tq, tq)
    qa = [qa_ref[h, :, pl.ds(q_off, tq)] for h in heads]

    m_ref[...] = jnp.full(m_ref.shape, NEG, F32)
    l_ref[...] = jnp.zeros_like(l_ref)
    acc_ref[...] = jnp.zeros_like(acc_ref)
    a_ref[...] = jnp.ones_like(a_ref)
    p_ref[...] = jnp.zeros_like(p_ref)

    def qk(j, slot):
        off = pl.multiple_of(j * tk, tk)
        for h in heads:
            s_ref[slot, h] = _dot(ka_ref[h, pl.ds(off, tk), :], qa[h])

    def pv(j):
        off = pl.multiple_of(j * tk, tk)
        for h in heads:
            v = fv_ref[rows(h), pl.ds(pl.multiple_of(base(h) + off, tk), tk)]
            acc_ref[h] = a_ref[h] * acc_ref[h] + _dot(v, p_ref[h])

    def softmax(slot, diagonal=False):
        for h in heads:
            s = s_ref[slot, h]
            if diagonal:
                key = lax.broadcasted_iota(jnp.int32, s.shape, 0)
                qry = lax.broadcasted_iota(jnp.int32, s.shape, 1)
                s = jnp.where(key <= qry, s, NEG)
            m = m_ref[h]
            m_new = jnp.maximum(m, jnp.max(s, axis=0, keepdims=True))
            p = jnp.exp2(s - m_new)
            alpha = jnp.exp2(m - m_new)
            l_ref[h] = alpha * l_ref[h] + jnp.sum(p, axis=0, keepdims=True)
            m_ref[h] = m_new
            a_ref[h] = alpha
            p_ref[h] = p.astype(BF16)

    def stage(j, slot):
        qk(j + 1, 1 - slot)
        pv(jnp.maximum(j - 1, 0))
        softmax(slot)

    qk(0, 0)

    @pl.loop(0, i // 2)
    def _(k):
        stage(2 * k, 0)
        stage(2 * k + 1, 1)

    def tail(parity):
        if parity == 1:
            stage(i - 1, 0)
        pv(jnp.maximum(i - 1, 0))
        softmax(parity, diagonal=True)
        pv(i)

    odd = lax.rem(i, 2) == 1
    pl.when(odd)(lambda: tail(1))
    pl.when(jnp.logical_not(odd))(lambda: tail(0))

    for h in heads:
        out_ref[rows(h), pl.ds(pl.multiple_of(base(h) + q_off, tq), tq)] = (acc_ref[h] / l_ref[h]).astype(BF16)


def _fox(feat, small, f_bias_col, batch, seq):
    tq = min(TQ_FOX, seq)
    tk = min(TK_FOX, tq)
    assert tq == tk, "the masked block of a query tile is exactly one key block"
    group = FOX_GROUP if batch % FOX_GROUP == 0 else 1
    span = group * seq
    units = group * FOX_HEADS
    t = batch * seq
    width = FOX_HEADS * HEAD_DIM
    blk = lambda row: pl.BlockSpec((width, span), lambda b: (row // width, b))
    col = lambda n: pl.BlockSpec((n, 1), lambda b: (0, 0))
    return pl.pallas_call(
        functools.partial(_fox_kernel, seq=seq, tq=tq, tk=tk),
        grid=(batch // group,),
        in_specs=[blk(ROW_FQ), blk(ROW_FK), blk(ROW_FV),
                  pl.BlockSpec((N_SMALL, span), lambda b: (0, b)),
                  col(N_SMALL)],
        out_specs=pl.BlockSpec((width, span), lambda b: (0, b)),
        out_shape=jax.ShapeDtypeStruct((width, t), BF16),
        scratch_shapes=[pltpu.VMEM((units, 2 * HEAD_DIM, seq), BF16),
                        pltpu.VMEM((units, seq, 2 * HEAD_DIM), BF16),
                        pltpu.VMEM((2, units, tk, tq), F32),
                        pltpu.VMEM((units, tk, tq), BF16),
                        pltpu.VMEM((units, HEAD_DIM, tq), F32),
                        pltpu.VMEM((units, 1, tq), F32),
                        pltpu.VMEM((units, 1, tq), F32),
                        pltpu.VMEM((units, 1, tq), F32)],
        compiler_params=pltpu.CompilerParams(
            dimension_semantics=("parallel",), vmem_limit_bytes=VMEM_LIMIT),
        name="fox_attention",
    )(feat, feat, feat, small, f_bias_col)


def _short_conv_tile(ct_ref, w_ref, carry_ref, sequence_start):
    width = w_ref.shape[1]
    u = ct_ref[:, COL_CU:COL_CU + width].astype(F32)
    b_gate = ct_ref[:, COL_CB:COL_CB + width].astype(F32)
    c_gate = ct_ref[:, COL_CC:COL_CC + width].astype(F32)
    z = c_gate * u
    prev = jnp.where(sequence_start, 0.0, carry_ref[...])
    carry_ref[...] = z[z.shape[0] - 8:, :]
    last1, last2 = prev[7:8, :], prev[6:7, :]
    row = lax.broadcasted_iota(jnp.int32, z.shape, 0)
    z1 = jnp.where(row >= 1, pltpu.roll(z, 1, 0), last1)
    z2 = jnp.where(row >= 2, pltpu.roll(z, 2, 0), jnp.where(row == 1, last1, last2))
    w = w_ref[...]
    return (b_gate * (w[0:1, :] * z2 + w[1:2, :] * z1 + w[2:3, :] * z)).astype(BF16)


def _mlstm_kernel(q_ref, k_ref, v_ref, o_ref, small_ref, bias_ref, gain_ref, out_ref,
                  c_ref, m_ref, b_ref, u_ref, ut_ref, *, seq, chunk):
    c_ref[...] = jnp.zeros_like(c_ref)
    m_ref[...] = jnp.zeros_like(m_ref)
    g = small_ref[...] + bias_ref[...]
    b_all = _cumsum_lanes(_log_sigmoid(g), segment=chunk)
    b_ref[...] = b_all
    u_all = _row_select([g[4 + h:5 + h, :] - b_all[8 + h:9 + h, :] for h in range(MLSTM_HEADS)],
                        8, g.shape[1])
    u_ref[...] = u_all
    ut_ref[...] = jnp.concatenate([u_all, jnp.zeros((120, g.shape[1]), F32)], axis=0).T
    refs = (q_ref, k_ref, v_ref, o_ref, gain_ref, out_ref, c_ref, m_ref, b_ref, u_ref, ut_ref)
    pl.loop(0, seq // chunk)(lambda c: _mlstm_chunk(c, *refs, seq=seq, chunk=chunk))


def _mlstm_chunk(c, q_ref, k_ref, v_ref, o_ref, gain_ref, out_ref, c_ref, m_ref, b_ref, u_ref, ut_ref, *,
                 seq, chunk):
    units = range(c_ref.shape[0])
    head = lambda u: u % MLSTM_HEADS
    sl = lambda u: slice(head(u) * HEAD_DIM, (head(u) + 1) * HEAD_DIM)
    cols = lambda u: pl.ds(pl.multiple_of((u // MLSTM_HEADS) * seq + c * chunk, chunk), chunk)

    src = lax.broadcasted_iota(jnp.int32, (chunk, chunk), 0)
    tgt = lax.broadcasted_iota(jnp.int32, (chunk, chunk), 1)
    causal = src <= tgt
    ones_rows = (lax.broadcasted_iota(jnp.int32, (HEAD_DIM, chunk), 0) == 0).astype(BF16)

    qs = [q_ref[sl(u), cols(u)] for u in units]
    ks = [k_ref[sl(u), cols(u)] for u in units]
    c_prev = [c_ref[u] for u in units]
    scores = [_dot_tn(ks[u], qs[u]) for u in units]
    carried = [_dot(c_prev[u].astype(BF16), qs[u]) for u in units]

    sw, kw, m_t, inter, decay = [], [], [], [], []
    for u in units:
        h = head(u)
        b_row = b_ref[8 + h:9 + h, cols(u)]
        b_last = b_row[:, chunk - 1:chunk]
        u_row = u_ref[h:h + 1, cols(u)]
        u_col = ut_ref[cols(u), h:h + 1]
        m_prev = m_ref[u][0:1, 0:1]
        dmat = jnp.where(causal, b_row + u_col, NEG)
        g_row = b_row + m_prev
        m_u = jnp.maximum(g_row, jnp.max(dmat, axis=0, keepdims=True))
        sw.append((scores[u] * jnp.exp(dmat - m_u)).astype(BF16))
        m_t.append(m_u)
        inter.append(jnp.exp(g_row - m_u))
        m_loc = jnp.max(u_row + b_last, axis=1, keepdims=True)
        m_new = jnp.maximum(b_last + m_prev, m_loc)
        decay.append(jnp.exp(b_last + m_prev - m_new))
        w_row = jnp.exp(u_row + b_last - m_new)
        kw.append((ks[u].astype(F32) * w_row).astype(BF16))
        m_ref[u] = jnp.broadcast_to(m_new, m_ref.shape[1:])

    v_aug = [jnp.concatenate([v_ref[sl(u), cols(u)], ones_rows], axis=0) for u in units]
    intra = [_dot(v_aug[u], sw[u]) for u in units]
    update = [_dot_nt(v_aug[u], kw[u]) for u in units]
    for u in units:
        c_ref[u] = decay[u] * c_prev[u] + update[u]
        tot = intra[u] + inter[u] * carried[u]
        num = tot[0:HEAD_DIM]
        den = tot[HEAD_DIM:HEAD_DIM + 1]
        ht = num / jnp.maximum(jnp.abs(den), jnp.exp(-m_t[u]))
        hn = _rms_rows(ht, gain_ref[sl(u), :])
        out_ref[sl(u), cols(u)] = (_sigmoid(o_ref[sl(u), cols(u)].astype(F32)) * hn).astype(BF16)


def _mlstm(feat, small, bias_col, gain_col, batch, seq):
    chunk = min(L_MLSTM, seq)
    group = MLSTM_GROUP if batch % MLSTM_GROUP == 0 else 1
    span = group * seq
    units = group * MLSTM_HEADS
    t = batch * seq
    width = MLSTM_HEADS * HEAD_DIM
    blk = lambda row: pl.BlockSpec((width, span), lambda b: (row // width, b))
    return pl.pallas_call(
        functools.partial(_mlstm_kernel, seq=seq, chunk=chunk),
        grid=(batch // group,),
        in_specs=[blk(ROW_MQ), blk(ROW_MK), blk(ROW_MV), blk(ROW_MO),
                  pl.BlockSpec((N_SMALL, span), lambda b: (0, b)),
                  pl.BlockSpec((N_SMALL, 1), lambda b: (0, 0)),
                  pl.BlockSpec((width, 1), lambda b: (0, 0))],
        out_specs=pl.BlockSpec((width, span), lambda b: (0, b)),
        out_shape=jax.ShapeDtypeStruct((width, t), BF16),
        scratch_shapes=[pltpu.VMEM((units, 2 * HEAD_DIM, HEAD_DIM), F32),
                        pltpu.VMEM((units, 8, 128), F32),
                        pltpu.VMEM((N_SMALL, span), F32),
                        pltpu.VMEM((8, span), F32),
                        pltpu.VMEM((span, 128), F32)],
        compiler_params=pltpu.CompilerParams(
            dimension_semantics=("parallel",), vmem_limit_bytes=VMEM_LIMIT),
        name="mlstm",
    )(feat, feat, feat, feat, small, bias_col, gain_col)


def _swa_kernel(q_ref, k_ref, v_ref, bias_ref, sink_ref, out_ref, *, nblk):
    refs = (q_ref, k_ref, v_ref, bias_ref, sink_ref, out_ref)
    pl.loop(0, q_ref.shape[1] // (nblk * WINDOW))(lambda n: _swa_span(n, *refs, nblk=nblk))


def _swa_span(n, q_ref, k_ref, v_ref, bias_ref, sink_ref, out_ref, *, nblk):
    w = WINDOW
    off = pl.multiple_of(n * (nblk * w), nblk * w)
    cur = pl.ds(off, nblk * w)
    prev = pl.ds(pl.multiple_of(jnp.maximum(off - w, 0), w), w)
    key = lax.broadcasted_iota(jnp.int32, (2 * w, SWA_GROUP * w), 0)
    first_block_pad = jnp.logical_and(n == 0, key < w)
    kvs = range(SWA_KV_HEADS)
    ksl = [slice(kv * HEAD_DIM, (kv + 1) * HEAD_DIM) for kv in kvs]
    kt = [jnp.concatenate([k_ref[s, prev], k_ref[s, cur]], axis=1) for s in ksl]
    vt = [jnp.concatenate([v_ref[s, prev], v_ref[s, cur]], axis=1) for s in ksl]

    bands = [(blk, kv) for blk in range(nblk) for kv in kvs]
    scores = []
    for blk, kv in bands:
        qs = []
        for g in range(SWA_GROUP):
            hs = slice((kv * SWA_GROUP + g) * HEAD_DIM, (kv * SWA_GROUP + g + 1) * HEAD_DIM)
            qs.append(q_ref[hs, pl.ds(off + blk * w, w)])
        qt = jnp.concatenate(qs, axis=1)
        s = _dot_tn(kt[kv][:, blk * w:(blk + 2) * w], qt) + bias_ref[kv]
        scores.append(jnp.where(first_block_pad, NEG, s) if blk == 0 else s)
    probs, denoms = [], []
    for (blk, kv), s in zip(bands, scores):
        sink = sink_ref[kv]
        m = jnp.maximum(jnp.max(s, axis=0, keepdims=True), sink)
        p = jnp.exp2(s - m)
        denoms.append(jnp.sum(p, axis=0, keepdims=True) + jnp.exp2(sink - m))
        probs.append(p.astype(BF16))
    outs = [_dot(vt[kv][:, blk * w:(blk + 2) * w], p) for (blk, kv), p in zip(bands, probs)]
    for (blk, kv), o, d in zip(bands, outs, denoms):
        o = o / d
        for g in range(SWA_GROUP):
            hs = slice((kv * SWA_GROUP + g) * HEAD_DIM, (kv * SWA_GROUP + g + 1) * HEAD_DIM)
            out_ref[hs, pl.ds(off + blk * w, w)] = o[:, g * w:(g + 1) * w].astype(BF16)


def _swa(feat, bias_t, sink_rows, batch, seq):
    nblk = min(SWA_BLOCKS, seq // WINDOW)
    t = batch * seq
    qw = SWA_Q_HEADS * HEAD_DIM
    kw = SWA_KV_HEADS * HEAD_DIM
    return pl.pallas_call(
        functools.partial(_swa_kernel, nblk=nblk),
        grid=(batch,),
        in_specs=[pl.BlockSpec((qw, seq), lambda b: (ROW_SQ // qw, b)),
                  pl.BlockSpec((kw, seq), lambda b: (ROW_SK // kw, b)),
                  pl.BlockSpec((kw, seq), lambda b: (ROW_SV // kw, b)),
                  pl.BlockSpec(bias_t.shape, lambda b: (0, 0, 0)),
                  pl.BlockSpec(sink_rows.shape, lambda b: (0, 0, 0))],
        out_specs=pl.BlockSpec((qw, seq), lambda b: (0, b)),
        out_shape=jax.ShapeDtypeStruct((qw, t), BF16),
        compiler_params=pltpu.CompilerParams(
            dimension_semantics=("parallel",), vmem_limit_bytes=VMEM_LIMIT),
        name="swa_attention",
    )(feat, feat, feat, bias_t, sink_rows)


def _merge_ffn_kernel(x_ref, yf_ref, ct_ref, ym_ref, ys_ref, g_ref, cw_ref,
                      wf_ref, wc_ref, wm_ref, ws_ref, wo_ref, gain_ref, wg_ref, wu_ref, wd_ref,
                      out_ref, carry_ref, *, tiles_per_seq):
    def gated(branch, y):
        g = g_ref[:, branch * D_MODEL:(branch + 1) * D_MODEL]
        return (1.0 + jnp.tanh(g.astype(F32))) * y

    y_conv = _short_conv_tile(ct_ref, cw_ref, carry_ref, lax.rem(pl.program_id(0), tiles_per_seq) == 0)
    merged = gated(0, _dot_tn(yf_ref[...], wf_ref[...]))
    merged += gated(1, _dot(y_conv, wc_ref[...]))
    merged += gated(2, _dot_tn(ym_ref[...], wm_ref[...]))
    merged += gated(3, _dot_tn(ys_ref[...], ws_ref[...]))
    x = x_ref[...] + _dot(merged.astype(BF16), wo_ref[...])

    ms = jnp.mean(x * x, axis=-1, keepdims=True)
    hn = (x * lax.rsqrt(ms + EPS) * gain_ref[...]).astype(BF16)
    acc = x
    bounds = list(range(0, D_FF, TF_FFN)) + [D_FF]
    for lo, hi in zip(bounds[:-1], bounds[1:]):
        h = _dot(hn, wg_ref[:, lo:hi])
        act = (h * (1.0 + jnp.tanh(h)) * _dot(hn, wu_ref[:, lo:hi])).astype(BF16)
        acc = acc + _dot(act, wd_ref[lo:hi, :])
    out_ref[...] = acc


def _merge_ffn(x2, y_fox_t, conv_tok, y_mlstm_t, y_swa_t, gates, conv_w, w_fox, w_conv, w_mlstm, w_swa,
               w_out, gain, w_gate_half, w_up, w_down, layer, seq):
    t = x2.shape[0]
    tm = min(TM_MERGE, t, seq)
    row_blk = lambda width: pl.BlockSpec((tm, width), lambda i: (i, 0))
    feat_blk = lambda a: pl.BlockSpec((a.shape[0], tm), lambda i: (0, i))
    full = lambda a: _resident(a.shape[1:], layer)
    return pl.pallas_call(
        functools.partial(_merge_ffn_kernel, tiles_per_seq=seq // tm),
        grid=(t // tm,),
        in_specs=[row_blk(D_MODEL),
                  feat_blk(y_fox_t),
                  row_blk(conv_tok.shape[1]),
                  feat_blk(y_mlstm_t), feat_blk(y_swa_t),
                  row_blk(gates.shape[1]),
                  full(conv_w),
                  full(w_fox), full(w_conv), full(w_mlstm), full(w_swa), full(w_out),
                  full(gain), full(w_gate_half), full(w_up), full(w_down)],
        out_specs=row_blk(D_MODEL),
        out_shape=jax.ShapeDtypeStruct((t, D_MODEL), F32),
        scratch_shapes=[pltpu.VMEM((8, conv_w.shape[2]), F32)],
        compiler_params=pltpu.CompilerParams(
            dimension_semantics=("arbitrary",), vmem_limit_bytes=VMEM_LIMIT),
        name="merge_ffn",
    )(x2, y_fox_t, conv_tok, y_mlstm_t, y_swa_t, gates, conv_w,
      w_fox, w_conv, w_mlstm, w_swa, w_out, gain, w_gate_half, w_up, w_down)


IN_FQKV, IN_FF, IN_CONV, IN_MQ, IN_MK, IN_MV = 0, 768, 772, 1540, 1796, 2052
IN_MI, IN_MF, IN_MO, IN_SQ, IN_SK, IN_SV, IN_GATES, IN_COLS = 2308, 2312, 2316, 2572, 3084, 3212, 3340, 7436
TK_PREP = 256


def _wprep_kernel(wt_ref, tok_ref, feat_ref):
    tok_ref[:, COL_CU:COL_GATES] = wt_ref[IN_CONV:IN_MQ, :].T.astype(BF16)
    for lo in range(0, N_TOK - COL_GATES, TN_IN):
        gates = wt_ref[IN_GATES + lo:IN_GATES + lo + TN_IN, :]
        tok_ref[:, COL_GATES + lo:COL_GATES + lo + TN_IN] = (0.5 * gates).T.astype(BF16)
    for row, lo, hi in ((ROW_FQ, IN_FQKV, IN_FF), (ROW_MQ, IN_MQ, IN_MI), (ROW_SQ, IN_SQ, IN_SK),
                        (ROW_MO, IN_MO, IN_SQ), (ROW_SK, IN_SK, IN_GATES)):
        feat_ref[row:row + hi - lo, :] = wt_ref[lo:hi, :].astype(BF16)
    fox_f = wt_ref[IN_FF:IN_FF + N_SMALL, :]
    mlstm_if = wt_ref[IN_MI - 4:IN_MI - 4 + N_SMALL, :]
    row = lax.broadcasted_iota(jnp.int32, fox_f.shape, 0)
    small = jnp.where(row < 4, fox_f, jnp.where(row < 12, mlstm_if, 0.0))
    feat_ref[N_FEAT:, :] = small.astype(BF16)


def _prep_w_in(w_in):
    depth, d, cols = w_in.shape
    return pl.pallas_call(
        _wprep_kernel,
        grid=(depth, d // TK_PREP),
        in_specs=[pl.BlockSpec((None, cols, TK_PREP), lambda l, j: (l, 0, j))],
        out_specs=[pl.BlockSpec((None, TK_PREP, N_TOK), lambda l, j: (l, j, 0)),
                   pl.BlockSpec((None, N_FEAT + N_SMALL, TK_PREP), lambda l, j: (l, 0, j))],
        out_shape=[jax.ShapeDtypeStruct((depth, d, N_TOK), BF16),
                   jax.ShapeDtypeStruct((depth, N_FEAT + N_SMALL, d), BF16)],
        compiler_params=pltpu.CompilerParams(
            dimension_semantics=("parallel", "parallel"), vmem_limit_bytes=VMEM_LIMIT),
        name="w_in_relayout",
    )(jnp.transpose(w_in, (0, 2, 1)))


def _head_scale(depth, fox_q_gain, fox_k_gain, swa_q_gain, swa_k_gain):
    def tiled(gain, heads, factor=1.0):
        return jnp.tile(gain.astype(F32) * factor, (1, heads))
    ones = lambda n: jnp.ones((depth, n), F32)
    cols = [(ROW_FQ, tiled(fox_q_gain, FOX_HEADS, QK_SCALE * LOG2E)), (ROW_FK, tiled(fox_k_gain, FOX_HEADS)),
            (ROW_FV, ones(ROW_MQ - ROW_FV)), (ROW_MQ, QK_SCALE * ones(ROW_MK - ROW_MQ)),
            (ROW_MK, ones(ROW_SQ - ROW_MK)), (ROW_SQ, tiled(swa_q_gain, SWA_Q_HEADS, QK_SCALE * LOG2E)),
            (ROW_MO, ones(ROW_SK - ROW_MO)), (ROW_SK, tiled(swa_k_gain, SWA_KV_HEADS)),
            (ROW_SV, ones(N_FEAT - ROW_SV))]
    assert [r for r, _ in cols] == sorted(r for r, _ in cols)
    return jnp.concatenate([c for _, c in cols], axis=1)[:, :, None]


def _col(v, n=None):
    v = v.astype(F32).reshape(-1, 1)
    if n is not None and v.shape[0] < n:
        v = jnp.concatenate([v, jnp.zeros((n - v.shape[0], 1), F32)], axis=0)
    return v


def kernel(x, rel_bias, attn_norm, w_in, fox_f_bias, fox_q_gain, fox_k_gain, conv_w, mlstm_i_bias, mlstm_f_bias, mlstm_h_gain, swa_q_gain, swa_k_gain, swa_sinks, w_fox_out, w_conv_out, w_mlstm_out, w_swa_out, w_merge_out, ffn_norm, w_gate, w_up, w_down):
    batch, seq, _ = x.shape
    depth = w_in.shape[0]
    x2 = x.reshape(batch * seq, D_MODEL)
    bias_t = _swa_bias_table(rel_bias)

    w_tok, w_feat_t = _prep_w_in(w_in)
    attn_gain = attn_norm.reshape(depth, 1, D_MODEL)
    ffn_gain = ffn_norm.reshape(depth, 1, D_MODEL)
    w_fox_b, w_conv_b, w_mlstm_b, w_swa_b = (w.astype(BF16) for w in (w_fox_out, w_conv_out, w_mlstm_out, w_swa_out))
    w_merge_half = (0.5 * w_merge_out).astype(BF16)
    w_gate_half = (0.5 * w_gate).astype(BF16)
    w_up_b = w_up.astype(BF16)
    w_down_b = w_down.astype(BF16)

    head_scale = _head_scale(depth, fox_q_gain, fox_k_gain, swa_q_gain, swa_k_gain)

    for l in range(depth):
        tok, gates, feat, small = _inproj(x2, attn_gain, w_tok, w_feat_t, head_scale, l)

        y_fox_t = _fox(feat, small, _col(fox_f_bias[l], N_SMALL), batch, seq)
        gate_bias = jnp.concatenate([jnp.zeros((4,), F32), mlstm_i_bias[l], mlstm_f_bias[l]])
        y_mlstm_t = _mlstm(feat, small, _col(gate_bias, N_SMALL), _col(mlstm_h_gain[l]), batch, seq)
        sink_rows = jnp.broadcast_to(
            (swa_sinks[l].astype(F32) * LOG2E).reshape(SWA_KV_HEADS, 1, SWA_GROUP, 1),
            (SWA_KV_HEADS, 1, SWA_GROUP, WINDOW)).reshape(SWA_KV_HEADS, 1, SWA_GROUP * WINDOW)
        y_swa_t = _swa(feat, bias_t, sink_rows, batch, seq)

        x2 = _merge_ffn(x2, y_fox_t, tok, y_mlstm_t, y_swa_t, gates, conv_w,
                        w_fox_b, w_conv_b, w_mlstm_b, w_swa_b, w_merge_half,
                        ffn_gain, w_gate_half, w_up_b, w_down_b, l, seq)
    return x2.reshape(batch, seq, D_MODEL)
```

```python
import functools

import numpy as np
import jax
import jax.numpy as jnp
from jax import lax
from jax.experimental import pallas as pl
from jax.experimental.pallas import tpu as pltpu

F32 = jnp.float32
BF16 = jnp.bfloat16

D_MODEL = 1024
HEAD_DIM = 64
FOX_HEADS = 4
MLSTM_HEADS = 4
SWA_Q_HEADS = 8
SWA_KV_HEADS = 2
SWA_GROUP = SWA_Q_HEADS // SWA_KV_HEADS
WINDOW = 128
REL_BUCKETS = 32
REL_MAX_DIST = 128
D_FF = 2816
EPS = 1e-6
NEG = -1e30
QK_SCALE = HEAD_DIM ** -0.5
LOG2E = 1.4426950408889634

ROW_FQ, ROW_FK, ROW_FV = 0, 256, 512
ROW_MQ, ROW_MK, ROW_MV = 768, 1024, 1280
ROW_SQ, ROW_MO, ROW_SK, ROW_SV = 1536, 2048, 2304, 2432
N_FEAT = 2560
N_SMALL = 16
COL_CU, COL_CB, COL_CC, COL_GATES = 0, 256, 512, 768
N_TOK = 4864

TM_IN = 512
TN_IN = 512
FEAT_CHUNK = 640
FOX_GROUP = 2
TQ_FOX = 256
TK_FOX = 256
L_MLSTM = 256
MLSTM_GROUP = 2
SWA_BLOCKS = 16
TM_MERGE = 512
TF_FFN = 512
VMEM_LIMIT = 56 * 1024 * 1024


def _dot(a, b):
    return jnp.dot(a, b, preferred_element_type=F32)


def _dot_nt(a, b):
    return lax.dot_general(a, b, (((1,), (1,)), ((), ())), preferred_element_type=F32)


def _dot_tn(a, b):
    return lax.dot_general(a, b, (((0,), (0,)), ((), ())), preferred_element_type=F32)


def _sigmoid(x):
    return 0.5 * jnp.tanh(0.5 * x) + 0.5


def _log_sigmoid(x):
    return jnp.minimum(x, 0.0) - jnp.log(1.0 + jnp.exp(-jnp.abs(x)))


def _cumsum_lanes(x, segment=None):
    n = segment or x.shape[-1]
    lane = lax.broadcasted_iota(jnp.int32, x.shape, x.ndim - 1) & (n - 1)
    k = 1
    while k < n:
        x = x + jnp.where(lane >= k, pltpu.roll(x, k, x.ndim - 1), 0.0)
        k *= 2
    return x


def _row_select(rows, n_rows, width):
    rid = lax.broadcasted_iota(jnp.int32, (n_rows, width), 0)
    out = jnp.zeros((n_rows, width), F32)
    for r, v in enumerate(rows):
        out = jnp.where(rid == r, v, out)
    return out


def _split3(c):
    hi = c.astype(BF16).astype(F32)
    r = c - hi
    mid = r.astype(BF16).astype(F32)
    lo = (r - mid).astype(BF16).astype(F32)
    return hi, mid, lo


def _rms_rows(xt, gain_col):
    ms = jnp.mean(xt * xt, axis=0, keepdims=True)
    return xt * lax.rsqrt(ms + EPS) * gain_col


def _bias_kernel(rb_ref, idx_ref, out_ref):
    kv = pl.program_id(0)
    g = pl.program_id(1)
    head = kv * SWA_GROUP + g
    idx = idx_ref[...]
    acc = jnp.full(idx.shape, NEG, F32)
    for b in range(REL_BUCKETS):
        acc = jnp.where(idx == b, rb_ref[b * SWA_Q_HEADS + head] * LOG2E, acc)
    out_ref[0] = acc


def _bucket_table():
    j = np.arange(2 * WINDOW)[:, None]
    i = np.arange(WINDOW)[None, :]
    dist = i + WINDOW - j
    n = np.maximum(dist, 0).astype(np.int32)
    max_exact = REL_BUCKETS // 2
    nf = np.maximum(n, 1).astype(np.float32)
    large = max_exact + (np.log(nf / np.float32(max_exact)) / np.float32(np.log(REL_MAX_DIST / max_exact))
                         * np.float32(REL_BUCKETS - max_exact)).astype(np.int32)
    large = np.minimum(large, REL_BUCKETS - 1)
    bucket = np.where(n < max_exact, n, large)
    return np.where((dist >= 0) & (dist < WINDOW), bucket, -1).astype(np.int32)


def _swa_bias_table(rel_bias):
    idx = jnp.asarray(_bucket_table())
    return pl.pallas_call(
        _bias_kernel,
        grid=(SWA_KV_HEADS, SWA_GROUP),
        in_specs=[
            pl.BlockSpec(memory_space=pltpu.SMEM),
            pl.BlockSpec((2 * WINDOW, WINDOW), lambda kv, g: (0, 0)),
        ],
        out_specs=pl.BlockSpec((1, 2 * WINDOW, WINDOW), lambda kv, g: (kv, 0, g)),
        out_shape=jax.ShapeDtypeStruct((SWA_KV_HEADS, 2 * WINDOW, SWA_GROUP * WINDOW), F32),
        name="swa_bias_table",
    )(rel_bias.reshape(-1), idx)


QK_NORM_ROWS = ((ROW_FQ, ROW_FV), (ROW_SQ, ROW_MO), (ROW_SK, ROW_SV))


def _inproj_kernel(x_ref, g_ref, wt_ref, wf_ref, hs_ref, tok_ref, gates_ref, feat_ref, small_ref):
    x = x_ref[...]
    ms = jnp.mean(x * x, axis=-1, keepdims=True)
    xn = (x * lax.rsqrt(ms + EPS) * g_ref[...]).astype(BF16)
    for out_ref, base, width in ((tok_ref, COL_CU, COL_GATES), (gates_ref, COL_GATES, N_TOK - COL_GATES)):
        for lo in range(0, width, TN_IN):
            hi = min(lo + TN_IN, width)
            out_ref[:, lo:hi] = _dot(xn, wt_ref[:, base + lo:base + hi]).astype(BF16)
    n_chunks = N_FEAT // FEAT_CHUNK
    for c in range(n_chunks):
        lo = c * FEAT_CHUNK
        hi = lo + FEAT_CHUNK + (N_SMALL if c == n_chunks - 1 else 0)
        r = _dot_nt(wf_ref[lo:hi, :], xn)
        for row in range(lo, lo + FEAT_CHUNK, HEAD_DIM):
            head = r[row - lo:row - lo + HEAD_DIM]
            scale = hs_ref[row:row + HEAD_DIM, :]
            if any(a <= row < b for a, b in QK_NORM_ROWS):
                head = _rms_rows(head, scale)
            elif ROW_MQ <= row < ROW_MK:
                head = head * scale
            feat_ref[row:row + HEAD_DIM, :] = head.astype(BF16)
        if c == n_chunks - 1:
            small_ref[...] = r[FEAT_CHUNK:]


def _resident(shape, layer=None):
    if layer is None:
        return pl.BlockSpec(shape, lambda *_: (0,) * len(shape), pipeline_mode=pl.Buffered(1))
    return pl.BlockSpec((None,) + tuple(shape), lambda *_: (layer,) + (0,) * len(shape),
                        pipeline_mode=pl.Buffered(1))


def _inproj(x2, gain, w_tok, w_feat_t, head_scale, layer):
    t = x2.shape[0]
    tm = min(TM_IN, t)
    return pl.pallas_call(
        _inproj_kernel,
        grid=(t // tm,),
        in_specs=[
            pl.BlockSpec((tm, D_MODEL), lambda i: (i, 0)),
            _resident((1, D_MODEL), layer),
            _resident((D_MODEL, N_TOK), layer),
            _resident((N_FEAT + N_SMALL, D_MODEL), layer),
            _resident((N_FEAT, 1), layer),
        ],
        out_specs=[
            pl.BlockSpec((tm, COL_GATES), lambda i: (i, 0)),
            pl.BlockSpec((tm, N_TOK - COL_GATES), lambda i: (i, 0)),
            pl.BlockSpec((N_FEAT, tm), lambda i: (0, i)),
            pl.BlockSpec((N_SMALL, tm), lambda i: (0, i)),
        ],
        out_shape=[
            jax.ShapeDtypeStruct((t, COL_GATES), BF16),
            jax.ShapeDtypeStruct((t, N_TOK - COL_GATES), BF16),
            jax.ShapeDtypeStruct((N_FEAT, t), BF16),
            jax.ShapeDtypeStruct((N_SMALL, t), F32),
        ],
        compiler_params=pltpu.CompilerParams(
            dimension_semantics=("parallel",), vmem_limit_bytes=VMEM_LIMIT),
        name="inproj",
    )(x2, gain, w_tok, w_feat_t, head_scale)


def _fox_kernel(fq_ref, fk_ref, fv_ref, small_ref, fb_ref, out_ref,
                qa_ref, ka_ref, s_ref, p_ref, acc_ref, m_ref, l_ref, a_ref, *, seq, tq, tk):
    n_units = qa_ref.shape[0]
    c_all = _cumsum_lanes(_log_sigmoid(small_ref[...] + fb_ref[...]), segment=seq)
    for u in range(n_units):
        e, h = divmod(u, FOX_HEADS)
        sl = slice(h * HEAD_DIM, (h + 1) * HEAD_DIM)
        cols = slice(e * seq, (e + 1) * seq)
        c = c_all[h:h + 1, cols] * LOG2E
        hi, mid, lo = _split3(c)
        one = jnp.ones_like(c)
        qa_ref[u, 0:HEAD_DIM, :] = fq_ref[sl, cols]
        qa_ref[u, HEAD_DIM:, :] = _row_select([hi, mid, lo, one, one, one], HEAD_DIM, seq).astype(BF16)
        k_aug = _row_select([one, one, one, -hi, -mid, -lo], HEAD_DIM, seq)
        ka_ref[u] = jnp.concatenate([fk_ref[sl, cols].astype(F32), k_aug], axis=0).T.astype(BF16)
    refs = (fv_ref, out_ref, qa_ref, ka_ref, s_ref, p_ref, acc_ref, m_ref, l_ref, a_ref)
    pl.loop(0, seq // tq)(lambda i: _fox_tile(i, *refs, seq=seq, tq=tq, tk=tk))


def _fox_tile(i, fv_ref, out_ref, qa_ref, ka_ref, s_ref, p_ref, acc_ref, m_ref, l_ref, a_ref, *, seq, tq, tk):
    heads = range(qa_ref.shape[0])
    rows = lambda u: slice((u % FOX_HEADS) * HEAD_DIM, (u % FOX_HEADS + 1) * HEAD_DIM)
    base = lambda u: (u // FOX_HEADS) * seq
    q_off = pl.multiple_of(i * tq, tq)
    qa = [qa_ref[h, :, pl.ds(q_off, tq)] for h in heads]

    m_ref[...] = jnp.full(m_ref.shape, NEG, F32)
    l_ref[...] = jnp.zeros_like(l_ref)
    acc_ref[...] = jnp.zeros_like(acc_ref)
    a_ref[...] = jnp.ones_like(a_ref)
    p_ref[...] = jnp.zeros_like(p_ref)

    def qk(j, slot):
        off = pl.multiple_of(j * tk, tk)
        for h in heads:
            s_ref[slot, h] = _dot(ka_ref[h, pl.ds(off, tk), :], qa[h])

    def pv(j):
        off = pl.multiple_of(j * tk, tk)
        for h in heads:
            v = fv_ref[rows(h), pl.ds(pl.multiple_of(base(h) + off, tk), tk)]
            acc_ref[h] = a_ref[h] * acc_ref[h] + _dot(v, p_ref[h])

    def softmax(slot, diagonal=False):
        for h in heads:
            s = s_ref[slot, h]
            if diagonal:
                key = lax.broadcasted_iota(jnp.int32, s.shape, 0)
                qry = lax.broadcasted_iota(jnp.int32, s.shape, 1)
                s = jnp.where(key <= qry, s, NEG)
            m = m_ref[h]
            m_new = jnp.maximum(m, jnp.max(s, axis=0, keepdims=True))
            p = jnp.exp2(s - m_new)
            alpha = jnp.exp2(m - m_new)
            l_ref[h] = alpha * l_ref[h] + jnp.sum(p, axis=0, keepdims=True)
            m_ref[h] = m_new
            a_ref[h] = alpha
            p_ref[h] = p.astype(BF16)

    def stage(j, slot):
        qk(j + 1, 1 - slot)
        pv(jnp.maximum(j - 1, 0))
        softmax(slot)

    qk(0, 0)

    @pl.loop(0, i // 2)
    def _(k):
        stage(2 * k, 0)
        stage(2 * k + 1, 1)

    def tail(parity):
        if parity == 1:
            stage(i - 1, 0)
        pv(jnp.maximum(i - 1, 0))
        softmax(parity, diagonal=True)
        pv(i)

    odd = lax.rem(i, 2) == 1
    pl.when(odd)(lambda: tail(1))
    pl.when(jnp.logical_not(odd))(lambda: tail(0))

    for h in heads:
        out_ref[rows(h), pl.ds(pl.multiple_of(base(h) + q_off, tq), tq)] = (acc_ref[h] / l_ref[h]).astype(BF16)


def _fox(feat, small, f_bias_col, batch, seq):
    tq = min(TQ_FOX, seq)
    tk = min(TK_FOX, tq)
    assert tq == tk, "the masked block of a query tile is exactly one key block"
    group = FOX_GROUP if batch % FOX_GROUP == 0 else 1
    span = group * seq
    units = group * FOX_HEADS
    t = batch * seq
    width = FOX_HEADS * HEAD_DIM
    blk = lambda row: pl.BlockSpec((width, span), lambda b: (row // width, b))
    col = lambda n: pl.BlockSpec((n, 1), lambda b: (0, 0))
    return pl.pallas_call(
        functools.partial(_fox_kernel, seq=seq, tq=tq, tk=tk),
        grid=(batch // group,),
        in_specs=[blk(ROW_FQ), blk(ROW_FK), blk(ROW_FV),
                  pl.BlockSpec((N_SMALL, span), lambda b: (0, b)),
                  col(N_SMALL)],
        out_specs=pl.BlockSpec((width, span), lambda b: (0, b)),
        out_shape=jax.ShapeDtypeStruct((width, t), BF16),
        scratch_shapes=[pltpu.VMEM((units, 2 * HEAD_DIM, seq), BF16),
                        pltpu.VMEM((units, seq, 2 * HEAD_DIM), BF16),
                        pltpu.VMEM((2, units, tk, tq), F32),
                        pltpu.VMEM((units, tk, tq), BF16),
                        pltpu.VMEM((units, HEAD_DIM, tq), F32),
                        pltpu.VMEM((units, 1, tq), F32),
                        pltpu.VMEM((units, 1, tq), F32),
                        pltpu.VMEM((units, 1, tq), F32)],
        compiler_params=pltpu.CompilerParams(
            dimension_semantics=("parallel",), vmem_limit_bytes=VMEM_LIMIT),
        name="fox_attention",
    )(feat, feat, feat, small, f_bias_col)


def _short_conv_tile(ct_ref, w_ref, carry_ref, sequence_start):
    width = w_ref.shape[1]
    u = ct_ref[:, COL_CU:COL_CU + width].astype(F32)
    b_gate = ct_ref[:, COL_CB:COL_CB + width].astype(F32)
    c_gate = ct_ref[:, COL_CC:COL_CC + width].astype(F32)
    z = c_gate * u
    prev = jnp.where(sequence_start, 0.0, carry_ref[...])
    carry_ref[...] = z[z.shape[0] - 8:, :]
    last1, last2 = prev[7:8, :], prev[6:7, :]
    row = lax.broadcasted_iota(jnp.int32, z.shape, 0)
    z1 = jnp.where(row >= 1, pltpu.roll(z, 1, 0), last1)
    z2 = jnp.where(row >= 2, pltpu.roll(z, 2, 0), jnp.where(row == 1, last1, last2))
    w = w_ref[...]
    return (b_gate * (w[0:1, :] * z2 + w[1:2, :] * z1 + w[2:3, :] * z)).astype(BF16)


def _mlstm_kernel(q_ref, k_ref, v_ref, o_ref, small_ref, bias_ref, gain_ref, out_ref,
                  c_ref, m_ref, b_ref, u_ref, ut_ref, *, seq, chunk):
    c_ref[...] = jnp.zeros_like(c_ref)
    m_ref[...] = jnp.zeros_like(m_ref)
    g = small_ref[...] + bias_ref[...]
    b_all = _cumsum_lanes(_log_sigmoid(g), segment=chunk)
    b_ref[...] = b_all
    u_all = _row_select([g[4 + h:5 + h, :] - b_all[8 + h:9 + h, :] for h in range(MLSTM_HEADS)],
                        8, g.shape[1])
    u_ref[...] = u_all
    ut_ref[...] = jnp.concatenate([u_all, jnp.zeros((120, g.shape[1]), F32)], axis=0).T
    refs = (q_ref, k_ref, v_ref, o_ref, gain_ref, out_ref, c_ref, m_ref, b_ref, u_ref, ut_ref)
    pl.loop(0, seq // chunk)(lambda c: _mlstm_chunk(c, *refs, seq=seq, chunk=chunk))


def _mlstm_chunk(c, q_ref, k_ref, v_ref, o_ref, gain_ref, out_ref, c_ref, m_ref, b_ref, u_ref, ut_ref, *,
                 seq, chunk):
    units = range(c_ref.shape[0])
    head = lambda u: u % MLSTM_HEADS
    sl = lambda u: slice(head(u) * HEAD_DIM, (head(u) + 1) * HEAD_DIM)
    cols = lambda u: pl.ds(pl.multiple_of((u // MLSTM_HEADS) * seq + c * chunk, chunk), chunk)

    src = lax.broadcasted_iota(jnp.int32, (chunk, chunk), 0)
    tgt = lax.broadcasted_iota(jnp.int32, (chunk, chunk), 1)
    causal = src <= tgt
    ones_rows = (lax.broadcasted_iota(jnp.int32, (HEAD_DIM, chunk), 0) == 0).astype(BF16)

    def first_matmuls(u):
        qs = q_ref[sl(u), cols(u)]
        ks = k_ref[sl(u), cols(u)]
        c_prev = c_ref[u]
        return dict(ks=ks, c_prev=c_prev, scores=_dot_tn(ks, qs),
                    carried=_dot(c_prev.astype(BF16), qs))

    def gate_arithmetic(u, st):
        h = head(u)
        b_row = b_ref[8 + h:9 + h, cols(u)]
        b_last = b_row[:, chunk - 1:chunk]
        u_row = u_ref[h:h + 1, cols(u)]
        u_col = ut_ref[cols(u), h:h + 1]
        m_prev = m_ref[u][0:1, 0:1]
        dmat = jnp.where(causal, b_row + u_col, NEG)
        g_row = b_row + m_prev
        m_u = jnp.maximum(g_row, jnp.max(dmat, axis=0, keepdims=True))
        st.update(sw=(st["scores"] * jnp.exp(dmat - m_u)).astype(BF16), m_t=m_u, inter=jnp.exp(g_row - m_u))
        m_loc = jnp.max(u_row + b_last, axis=1, keepdims=True)
        m_new = jnp.maximum(b_last + m_prev, m_loc)
        w_row = jnp.exp(u_row + b_last - m_new)
        st.update(decay=jnp.exp(b_last + m_prev - m_new),
                  kw=(st["ks"].astype(F32) * w_row).astype(BF16))
        m_ref[u] = jnp.broadcast_to(m_new, m_ref.shape[1:])

    def second_matmuls(u, st):
        v_aug = jnp.concatenate([v_ref[sl(u), cols(u)], ones_rows], axis=0)
        st.update(intra=_dot(v_aug, st["sw"]), update=_dot_nt(v_aug, st["kw"]))

    def epilogue(u, st):
        c_ref[u] = st["decay"] * st["c_prev"] + st["update"]
        tot = st["intra"] + st["inter"] * st["carried"]
        num = tot[0:HEAD_DIM]
        den = tot[HEAD_DIM:HEAD_DIM + 1]
        ht = num / jnp.maximum(jnp.abs(den), jnp.exp(-st["m_t"]))
        hn = _rms_rows(ht, gain_ref[sl(u), :])
        out_ref[sl(u), cols(u)] = (_sigmoid(o_ref[sl(u), cols(u)].astype(F32)) * hn).astype(BF16)

    n = len(units)
    ahead = 2
    state = {u: first_matmuls(u) for u in range(min(ahead, n))}
    for u in units:
        gate_arithmetic(u, state[u])
        if u + ahead < n:
            state[u + ahead] = first_matmuls(u + ahead)
        second_matmuls(u, state[u])
        if u >= 1:
            epilogue(u - 1, state.pop(u - 1))
    epilogue(n - 1, state.pop(n - 1))


def _mlstm(feat, small, bias_col, gain_col, batch, seq):
    chunk = min(L_MLSTM, seq)
    group = MLSTM_GROUP if batch % MLSTM_GROUP == 0 else 1
    span = group * seq
    units = group * MLSTM_HEADS
    t = batch * seq
    width = MLSTM_HEADS * HEAD_DIM
    blk = lambda row: pl.BlockSpec((width, span), lambda b: (row // width, b))
    return pl.pallas_call(
        functools.partial(_mlstm_kernel, seq=seq, chunk=chunk),
        grid=(batch // group,),
        in_specs=[blk(ROW_MQ), blk(ROW_MK), blk(ROW_MV), blk(ROW_MO),
                  pl.BlockSpec((N_SMALL, span), lambda b: (0, b)),
                  pl.BlockSpec((N_SMALL, 1), lambda b: (0, 0)),
                  pl.BlockSpec((width, 1), lambda b: (0, 0))],
        out_specs=pl.BlockSpec((width, span), lambda b: (0, b)),
        out_shape=jax.ShapeDtypeStruct((width, t), BF16),
        scratch_shapes=[pltpu.VMEM((units, 2 * HEAD_DIM, HEAD_DIM), F32),
                        pltpu.VMEM((units, 8, 128), F32),
                        pltpu.VMEM((N_SMALL, span), F32),
                        pltpu.VMEM((8, span), F32),
                        pltpu.VMEM((span, 128), F32)],
        compiler_params=pltpu.CompilerParams(
            dimension_semantics=("parallel",), vmem_limit_bytes=VMEM_LIMIT),
        name="mlstm",
    )(feat, feat, feat, feat, small, bias_col, gain_col)


def _swa_kernel(q_ref, k_ref, v_ref, bias_ref, sink_ref, out_ref, *, nblk):
    refs = (q_ref, k_ref, v_ref, bias_ref, sink_ref, out_ref)
    pl.loop(0, q_ref.shape[1] // (nblk * WINDOW))(lambda n: _swa_span(n, *refs, nblk=nblk))


def _swa_span(n, q_ref, k_ref, v_ref, bias_ref, sink_ref, out_ref, *, nblk):
    w = WINDOW
    off = pl.multiple_of(n * (nblk * w), nblk * w)
    cur = pl.ds(off, nblk * w)
    prev = pl.ds(pl.multiple_of(jnp.maximum(off - w, 0), w), w)
    key = lax.broadcasted_iota(jnp.int32, (2 * w, SWA_GROUP * w), 0)
    first_block_pad = jnp.logical_and(n == 0, key < w)
    kvs = range(SWA_KV_HEADS)
    ksl = [slice(kv * HEAD_DIM, (kv + 1) * HEAD_DIM) for kv in kvs]
    kt = [jnp.concatenate([k_ref[s, prev], k_ref[s, cur]], axis=1) for s in ksl]
    vt = [jnp.concatenate([v_ref[s, prev], v_ref[s, cur]], axis=1) for s in ksl]

    bands = [(blk, kv) for blk in range(nblk) for kv in kvs]

    def head_rows(kv, g):
        return slice((kv * SWA_GROUP + g) * HEAD_DIM, (kv * SWA_GROUP + g + 1) * HEAD_DIM)

    def score(blk, kv):
        qt = jnp.concatenate([q_ref[head_rows(kv, g), pl.ds(off + blk * w, w)] for g in range(SWA_GROUP)], axis=1)
        s = _dot_tn(kt[kv][:, blk * w:(blk + 2) * w], qt) + bias_ref[kv]
        return jnp.where(first_block_pad, NEG, s) if blk == 0 else s

    def softmax(s, kv):
        sink = sink_ref[kv]
        m = jnp.maximum(jnp.max(s, axis=0, keepdims=True), sink)
        p = jnp.exp2(s - m)
        return p.astype(BF16), jnp.sum(p, axis=0, keepdims=True) + jnp.exp2(sink - m)

    def finish(blk, kv, o, d):
        o = o / d
        for g in range(SWA_GROUP):
            out_ref[head_rows(kv, g), pl.ds(off + blk * w, w)] = o[:, g * w:(g + 1) * w].astype(BF16)

    ahead = 2
    scores = [score(*band) for band in bands[:ahead]]
    pending = None
    for k, (blk, kv) in enumerate(bands):
        p, d = softmax(scores[k], kv)
        if k + ahead < len(bands):
            scores.append(score(*bands[k + ahead]))
        o = _dot(vt[kv][:, blk * w:(blk + 2) * w], p)
        if pending is not None:
            finish(*pending)
        pending = (blk, kv, o, d)
    finish(*pending)


def _swa(feat, bias_t, sink_rows, batch, seq):
    nblk = min(SWA_BLOCKS, seq // WINDOW)
    t = batch * seq
    qw = SWA_Q_HEADS * HEAD_DIM
    kw = SWA_KV_HEADS * HEAD_DIM
    return pl.pallas_call(
        functools.partial(_swa_kernel, nblk=nblk),
        grid=(batch,),
        in_specs=[pl.BlockSpec((qw, seq), lambda b: (ROW_SQ // qw, b)),
                  pl.BlockSpec((kw, seq), lambda b: (ROW_SK // kw, b)),
                  pl.BlockSpec((kw, seq), lambda b: (ROW_SV // kw, b)),
                  pl.BlockSpec(bias_t.shape, lambda b: (0, 0, 0)),
                  pl.BlockSpec(sink_rows.shape, lambda b: (0, 0, 0))],
        out_specs=pl.BlockSpec((qw, seq), lambda b: (0, b)),
        out_shape=jax.ShapeDtypeStruct((qw, t), BF16),
        compiler_params=pltpu.CompilerParams(
            dimension_semantics=("parallel",), vmem_limit_bytes=VMEM_LIMIT),
        name="swa_attention",
    )(feat, feat, feat, bias_t, sink_rows)


def _merge_ffn_kernel(x_ref, yf_ref, ct_ref, ym_ref, ys_ref, g_ref, cw_ref,
                      wf_ref, wc_ref, wm_ref, ws_ref, wo_ref, gain_ref, wg_ref, wu_ref, wd_ref,
                      out_ref, carry_ref, *, tiles_per_seq):
    def gated(branch, y):
        g = g_ref[:, branch * D_MODEL:(branch + 1) * D_MODEL]
        return (1.0 + jnp.tanh(g.astype(F32))) * y

    y_conv = _short_conv_tile(ct_ref, cw_ref, carry_ref, lax.rem(pl.program_id(0), tiles_per_seq) == 0)
    merged = gated(0, _dot_tn(yf_ref[...], wf_ref[...]))
    merged += gated(1, _dot(y_conv, wc_ref[...]))
    merged += gated(2, _dot_tn(ym_ref[...], wm_ref[...]))
    merged += gated(3, _dot_tn(ys_ref[...], ws_ref[...]))
    x = x_ref[...] + _dot(merged.astype(BF16), wo_ref[...])

    ms = jnp.mean(x * x, axis=-1, keepdims=True)
    hn = (x * lax.rsqrt(ms + EPS) * gain_ref[...]).astype(BF16)
    acc = x
    bounds = list(range(0, D_FF, TF_FFN)) + [D_FF]
    for lo, hi in zip(bounds[:-1], bounds[1:]):
        h = _dot(hn, wg_ref[:, lo:hi])
        act = (h * (1.0 + jnp.tanh(h)) * _dot(hn, wu_ref[:, lo:hi])).astype(BF16)
        acc = acc + _dot(act, wd_ref[lo:hi, :])
    out_ref[...] = acc


def _merge_ffn(x2, y_fox_t, conv_tok, y_mlstm_t, y_swa_t, gates, conv_w, w_fox, w_conv, w_mlstm, w_swa,
               w_out, gain, w_gate_half, w_up, w_down, layer, seq):
    t = x2.shape[0]
    tm = min(TM_MERGE, t, seq)
    row_blk = lambda width: pl.BlockSpec((tm, width), lambda i: (i, 0))
    feat_blk = lambda a: pl.BlockSpec((a.shape[0], tm), lambda i: (0, i))
    full = lambda a: _resident(a.shape[1:], layer)
    return pl.pallas_call(
        functools.partial(_merge_ffn_kernel, tiles_per_seq=seq // tm),
        grid=(t // tm,),
        in_specs=[row_blk(D_MODEL),
                  feat_blk(y_fox_t),
                  row_blk(conv_tok.shape[1]),
                  feat_blk(y_mlstm_t), feat_blk(y_swa_t),
                  row_blk(gates.shape[1]),
                  full(conv_w),
                  full(w_fox), full(w_conv), full(w_mlstm), full(w_swa), full(w_out),
                  full(gain), full(w_gate_half), full(w_up), full(w_down)],
        out_specs=row_blk(D_MODEL),
        out_shape=jax.ShapeDtypeStruct((t, D_MODEL), F32),
        scratch_shapes=[pltpu.VMEM((8, conv_w.shape[2]), F32)],
        compiler_params=pltpu.CompilerParams(
            dimension_semantics=("arbitrary",), vmem_limit_bytes=VMEM_LIMIT),
        name="merge_ffn",
    )(x2, y_fox_t, conv_tok, y_mlstm_t, y_swa_t, gates, conv_w,
      w_fox, w_conv, w_mlstm, w_swa, w_out, gain, w_gate_half, w_up, w_down)


IN_FQKV, IN_FF, IN_CONV, IN_MQ, IN_MK, IN_MV = 0, 768, 772, 1540, 1796, 2052
IN_MI, IN_MF, IN_MO, IN_SQ, IN_SK, IN_SV, IN_GATES, IN_COLS = 2308, 2312, 2316, 2572, 3084, 3212, 3340, 7436
TK_PREP = 256


def _wprep_kernel(wt_ref, tok_ref, feat_ref):
    tok_ref[:, COL_CU:COL_GATES] = wt_ref[IN_CONV:IN_MQ, :].T.astype(BF16)
    for lo in range(0, N_TOK - COL_GATES, TN_IN):
        gates = wt_ref[IN_GATES + lo:IN_GATES + lo + TN_IN, :]
        tok_ref[:, COL_GATES + lo:COL_GATES + lo + TN_IN] = (0.5 * gates).T.astype(BF16)
    for row, lo, hi in ((ROW_FQ, IN_FQKV, IN_FF), (ROW_MQ, IN_MQ, IN_MI), (ROW_SQ, IN_SQ, IN_SK),
                        (ROW_MO, IN_MO, IN_SQ), (ROW_SK, IN_SK, IN_GATES)):
        feat_ref[row:row + hi - lo, :] = wt_ref[lo:hi, :].astype(BF16)
    fox_f = wt_ref[IN_FF:IN_FF + N_SMALL, :]
    mlstm_if = wt_ref[IN_MI - 4:IN_MI - 4 + N_SMALL, :]
    row = lax.broadcasted_iota(jnp.int32, fox_f.shape, 0)
    small = jnp.where(row < 4, fox_f, jnp.where(row < 12, mlstm_if, 0.0))
    feat_ref[N_FEAT:, :] = small.astype(BF16)


def _prep_w_in(w_in):
    depth, d, cols = w_in.shape
    return pl.pallas_call(
        _wprep_kernel,
        grid=(depth, d // TK_PREP),
        in_specs=[pl.BlockSpec((None, cols, TK_PREP), lambda l, j: (l, 0, j))],
        out_specs=[pl.BlockSpec((None, TK_PREP, N_TOK), lambda l, j: (l, j, 0)),
                   pl.BlockSpec((None, N_FEAT + N_SMALL, TK_PREP), lambda l, j: (l, 0, j))],
        out_shape=[jax.ShapeDtypeStruct((depth, d, N_TOK), BF16),
                   jax.ShapeDtypeStruct((depth, N_FEAT + N_SMALL, d), BF16)],
        compiler_params=pltpu.CompilerParams(
            dimension_semantics=("parallel", "parallel"), vmem_limit_bytes=VMEM_LIMIT),
        name="w_in_relayout",
    )(jnp.transpose(w_in, (0, 2, 1)))


def _head_scale(depth, fox_q_gain, fox_k_gain, swa_q_gain, swa_k_gain):
    def tiled(gain, heads, factor=1.0):
        return jnp.tile(gain.astype(F32) * factor, (1, heads))
    ones = lambda n: jnp.ones((depth, n), F32)
    cols = [(ROW_FQ, tiled(fox_q_gain, FOX_HEADS, QK_SCALE * LOG2E)), (ROW_FK, tiled(fox_k_gain, FOX_HEADS)),
            (ROW_FV, ones(ROW_MQ - ROW_FV)), (ROW_MQ, QK_SCALE * ones(ROW_MK - ROW_MQ)),
            (ROW_MK, ones(ROW_SQ - ROW_MK)), (ROW_SQ, tiled(swa_q_gain, SWA_Q_HEADS, QK_SCALE * LOG2E)),
            (ROW_MO, ones(ROW_SK - ROW_MO)), (ROW_SK, tiled(swa_k_gain, SWA_KV_HEADS)),
            (ROW_SV, ones(N_FEAT - ROW_SV))]
    assert [r for r, _ in cols] == sorted(r for r, _ in cols)
    return jnp.concatenate([c for _, c in cols], axis=1)[:, :, None]


def _col(v, n=None):
    v = v.astype(F32).reshape(-1, 1)
    if n is not None and v.shape[0] < n:
        v = jnp.concatenate([v, jnp.zeros((n - v.shape[0], 1), F32)], axis=0)
    return v


def kernel(x, rel_bias, attn_norm, w_in, fox_f_bias, fox_q_gain, fox_k_gain, conv_w, mlstm_i_bias, mlstm_f_bias, mlstm_h_gain, swa_q_gain, swa_k_gain, swa_sinks, w_fox_out, w_conv_out, w_mlstm_out, w_swa_out, w_merge_out, ffn_norm, w_gate, w_up, w_down):
    batch, seq, _ = x.shape
    depth = w_in.shape[0]
    x2 = x.reshape(batch * seq, D_MODEL)
    bias_t = _swa_bias_table(rel_bias)

    w_tok, w_feat_t = _prep_w_in(w_in)
    attn_gain = attn_norm.reshape(depth, 1, D_MODEL)
    ffn_gain = ffn_norm.reshape(depth, 1, D_MODEL)
    w_fox_b, w_conv_b, w_mlstm_b, w_swa_b = (w.astype(BF16) for w in (w_fox_out, w_conv_out, w_mlstm_out, w_swa_out))
    w_merge_half = (0.5 * w_merge_out).astype(BF16)
    w_gate_half = (0.5 * w_gate).astype(BF16)
    w_up_b = w_up.astype(BF16)
    w_down_b = w_down.astype(BF16)

    head_scale = _head_scale(depth, fox_q_gain, fox_k_gain, swa_q_gain, swa_k_gain)

    for l in range(depth):
        tok, gates, feat, small = _inproj(x2, attn_gain, w_tok, w_feat_t, head_scale, l)

        y_fox_t = _fox(feat, small, _col(fox_f_bias[l], N_SMALL), batch, seq)
        gate_bias = jnp.concatenate([jnp.zeros((4,), F32), mlstm_i_bias[l], mlstm_f_bias[l]])
        y_mlstm_t = _mlstm(feat, small, _col(gate_bias, N_SMALL), _col(mlstm_h_gain[l]), batch, seq)
        sink_rows = jnp.broadcast_to(
            (swa_sinks[l].astype(F32) * LOG2E).reshape(SWA_KV_HEADS, 1, SWA_GROUP, 1),
            (SWA_KV_HEADS, 1, SWA_GROUP, WINDOW)).reshape(SWA_KV_HEADS, 1, SWA_GROUP * WINDOW)
        y_swa_t = _swa(feat, bias_t, sink_rows, batch, seq)

        x2 = _merge_ffn(x2, y_fox_t, tok, y_mlstm_t, y_swa_t, gates, conv_w,
                        w_fox_b, w_conv_b, w_mlstm_b, w_swa_b, w_merge_half,
                        ffn_gain, w_gate_half, w_up_b, w_down_b, l, seq)
    return x2.reshape(batch, seq, D_MODEL)
```

```python
import functools

import numpy as np
import jax
import jax.numpy as jnp
from jax import lax
from jax.experimental import pallas as pl
from jax.experimental.pallas import tpu as pltpu

F32 = jnp.float32
BF16 = jnp.bfloat16

D_MODEL = 1024
HEAD_DIM = 64
FOX_HEADS = 4
MLSTM_HEADS = 4
SWA_Q_HEADS = 8
SWA_KV_HEADS = 2
SWA_GROUP = SWA_Q_HEADS // SWA_KV_HEADS
WINDOW = 128
REL_BUCKETS = 32
REL_MAX_DIST = 128
D_FF = 2816
EPS = 1e-6
NEG = -1e30
QK_SCALE = HEAD_DIM ** -0.5
LOG2E = 1.4426950408889634

ROW_FQ, ROW_FK, ROW_FV = 0, 256, 512
ROW_MQ, ROW_MK, ROW_MV = 768, 1024, 1280
ROW_SQ, ROW_MO, ROW_SK, ROW_SV = 1536, 2048, 2304, 2432
N_FEAT = 2560
N_SMALL = 16
COL_CU, COL_CB, COL_CC, COL_GATES = 0, 256, 512, 768
N_TOK = 4864

TM_IN = 512
TN_IN = 512
FEAT_CHUNK = 640
FOX_GROUP = 2
TQ_FOX = 256
TK_FOX = 256
L_MLSTM = 256
MLSTM_GROUP = 2
SWA_BLOCKS = 16
TM_MERGE = 512
TF_FFN = 512
VMEM_LIMIT = 56 * 1024 * 1024


def _dot(a, b):
    return jnp.dot(a, b, preferred_element_type=F32)


def _dot_nt(a, b):
    return lax.dot_general(a, b, (((1,), (1,)), ((), ())), preferred_element_type=F32)


def _dot_tn(a, b):
    return lax.dot_general(a, b, (((0,), (0,)), ((), ())), preferred_element_type=F32)


def _sigmoid(x):
    return 0.5 * jnp.tanh(0.5 * x) + 0.5


def _log_sigmoid(x):
    return jnp.minimum(x, 0.0) - jnp.log(1.0 + jnp.exp(-jnp.abs(x)))


def _cumsum_lanes(x, segment=None):
    n = segment or x.shape[-1]
    lane = lax.broadcasted_iota(jnp.int32, x.shape, x.ndim - 1) & (n - 1)
    k = 1
    while k < n:
        x = x + jnp.where(lane >= k, pltpu.roll(x, k, x.ndim - 1), 0.0)
        k *= 2
    return x


def _row_select(rows, n_rows, width):
    rid = lax.broadcasted_iota(jnp.int32, (n_rows, width), 0)
    out = jnp.zeros((n_rows, width), F32)
    for r, v in enumerate(rows):
        out = jnp.where(rid == r, v, out)
    return out


def _split3(c):
    hi = c.astype(BF16).astype(F32)
    r = c - hi
    mid = r.astype(BF16).astype(F32)
    lo = (r - mid).astype(BF16).astype(F32)
    return hi, mid, lo


def _rms_rows(xt, gain_col):
    ms = jnp.mean(xt * xt, axis=0, keepdims=True)
    return xt * lax.rsqrt(ms + EPS) * gain_col


def _bias_kernel(rb_ref, idx_ref, out_ref):
    kv = pl.program_id(0)
    g = pl.program_id(1)
    head = kv * SWA_GROUP + g
    idx = idx_ref[...]
    acc = jnp.full(idx.shape, NEG, F32)
    for b in range(REL_BUCKETS):
        acc = jnp.where(idx == b, rb_ref[b * SWA_Q_HEADS + head] * LOG2E, acc)
    out_ref[0] = acc


def _bucket_table():
    j = np.arange(2 * WINDOW)[:, None]
    i = np.arange(WINDOW)[None, :]
    dist = i + WINDOW - j
    n = np.maximum(dist, 0).astype(np.int32)
    max_exact = REL_BUCKETS // 2
    nf = np.maximum(n, 1).astype(np.float32)
    large = max_exact + (np.log(nf / np.float32(max_exact)) / np.float32(np.log(REL_MAX_DIST / max_exact))
                         * np.float32(REL_BUCKETS - max_exact)).astype(np.int32)
    large = np.minimum(large, REL_BUCKETS - 1)
    bucket = np.where(n < max_exact, n, large)
    return np.where((dist >= 0) & (dist < WINDOW), bucket, -1).astype(np.int32)


def _swa_bias_table(rel_bias):
    idx = jnp.asarray(_bucket_table())
    return pl.pallas_call(
        _bias_kernel,
        grid=(SWA_KV_HEADS, SWA_GROUP),
        in_specs=[
            pl.BlockSpec(memory_space=pltpu.SMEM),
            pl.BlockSpec((2 * WINDOW, WINDOW), lambda kv, g: (0, 0)),
        ],
        out_specs=pl.BlockSpec((1, 2 * WINDOW, WINDOW), lambda kv, g: (kv, 0, g)),
        out_shape=jax.ShapeDtypeStruct((SWA_KV_HEADS, 2 * WINDOW, SWA_GROUP * WINDOW), F32),
        name="swa_bias_table",
    )(rel_bias.reshape(-1), idx)


QK_NORM_ROWS = ((ROW_FQ, ROW_FV), (ROW_SQ, ROW_MO), (ROW_SK, ROW_SV))


def _inproj_kernel(x_ref, g_ref, wt_ref, wf_ref, hs_ref, tok_ref, gates_ref, feat_ref, small_ref):
    x = x_ref[...]
    ms = jnp.mean(x * x, axis=-1, keepdims=True)
    xn = (x * lax.rsqrt(ms + EPS) * g_ref[...]).astype(BF16)
    for out_ref, base, width in ((tok_ref, COL_CU, COL_GATES), (gates_ref, COL_GATES, N_TOK - COL_GATES)):
        for lo in range(0, width, TN_IN):
            hi = min(lo + TN_IN, width)
            out_ref[:, lo:hi] = _dot(xn, wt_ref[:, base + lo:base + hi]).astype(BF16)
    n_chunks = N_FEAT // FEAT_CHUNK
    for c in range(n_chunks):
        lo = c * FEAT_CHUNK
        hi = lo + FEAT_CHUNK + (N_SMALL if c == n_chunks - 1 else 0)
        r = _dot_nt(wf_ref[lo:hi, :], xn)
        for row in range(lo, lo + FEAT_CHUNK, HEAD_DIM):
            head = r[row - lo:row - lo + HEAD_DIM]
            scale = hs_ref[row:row + HEAD_DIM, :]
            if any(a <= row < b for a, b in QK_NORM_ROWS):
                head = _rms_rows(head, scale)
            elif ROW_MQ <= row < ROW_MK:
                head = head * scale
            feat_ref[row:row + HEAD_DIM, :] = head.astype(BF16)
        if c == n_chunks - 1:
            small_ref[...] = r[FEAT_CHUNK:]


def _resident(shape, layer=None):
    if layer is None:
        return pl.BlockSpec(shape, lambda *_: (0,) * len(shape), pipeline_mode=pl.Buffered(1))
    return pl.BlockSpec((None,) + tuple(shape), lambda *_: (layer,) + (0,) * len(shape),
                        pipeline_mode=pl.Buffered(1))


def _inproj(x2, gain, w_tok, w_feat_t, head_scale, layer):
    t = x2.shape[0]
    tm = min(TM_IN, t)
    return pl.pallas_call(
        _inproj_kernel,
        grid=(t // tm,),
        in_specs=[
            pl.BlockSpec((tm, D_MODEL), lambda i: (i, 0)),
            _resident((1, D_MODEL), layer),
            _resident((D_MODEL, N_TOK), layer),
            _resident((N_FEAT + N_SMALL, D_MODEL), layer),
            _resident((N_FEAT, 1), layer),
        ],
        out_specs=[
            pl.BlockSpec((tm, COL_GATES), lambda i: (i, 0)),
            pl.BlockSpec((tm, N_TOK - COL_GATES), lambda i: (i, 0)),
            pl.BlockSpec((N_FEAT, tm), lambda i: (0, i)),
            pl.BlockSpec((N_SMALL, tm), lambda i: (0, i)),
        ],
        out_shape=[
            jax.ShapeDtypeStruct((t, COL_GATES), BF16),
            jax.ShapeDtypeStruct((t, N_TOK - COL_GATES), BF16),
            jax.ShapeDtypeStruct((N_FEAT, t), BF16),
            jax.ShapeDtypeStruct((N_SMALL, t), F32),
        ],
        compiler_params=pltpu.CompilerParams(
            dimension_semantics=("parallel",), vmem_limit_bytes=VMEM_LIMIT),
        name="inproj",
    )(x2, gain, w_tok, w_feat_t, head_scale)


def _fox_kernel(fq_ref, fk_ref, fv_ref, small_ref, fb_ref, out_ref,
                qa_ref, ka_ref, s_ref, p_ref, acc_ref, m_ref, l_ref, a_ref, *, seq, tq, tk):
    n_units = qa_ref.shape[0]
    c_all = _cumsum_lanes(_log_sigmoid(small_ref[...] + fb_ref[...]), segment=seq)
    for u in range(n_units):
        e, h = divmod(u, FOX_HEADS)
        sl = slice(h * HEAD_DIM, (h + 1) * HEAD_DIM)
        cols = slice(e * seq, (e + 1) * seq)
        c = c_all[h:h + 1, cols] * LOG2E
        hi, mid, lo = _split3(c)
        one = jnp.ones_like(c)
        qa_ref[u, 0:HEAD_DIM, :] = fq_ref[sl, cols]
        qa_ref[u, HEAD_DIM:, :] = _row_select([hi, mid, lo, one, one, one], HEAD_DIM, seq).astype(BF16)
        k_aug = _row_select([one, one, one, -hi, -mid, -lo], HEAD_DIM, seq)
        ka_ref[u] = jnp.concatenate([fk_ref[sl, cols].astype(F32), k_aug], axis=0).T.astype(BF16)
    refs = (fv_ref, out_ref, qa_ref, ka_ref, s_ref, p_ref, acc_ref, m_ref, l_ref, a_ref)
    pl.loop(0, seq // (2 * tq))(lambda i: _fox_tile_pair(i, *refs, seq=seq, tq=tq, tk=tk))


def _fox_tile_pair(i, fv_ref, out_ref, qa_ref, ka_ref, s_ref, p_ref, acc_ref, m_ref, l_ref, a_ref, *, seq, tq, tk):
    n_units = qa_ref.shape[0]
    chains = range(2 * n_units)
    tile_a = [c for c in chains if c % 2 == 0]
    tile_b = [c for c in chains if c % 2 == 1]
    unit = lambda c: c // 2
    rows = lambda c: slice((unit(c) % FOX_HEADS) * HEAD_DIM, (unit(c) % FOX_HEADS + 1) * HEAD_DIM)
    base = lambda c: (unit(c) // FOX_HEADS) * seq
    q_off = lambda c: pl.multiple_of((2 * i + c % 2) * tq, tq)
    qa = [qa_ref[unit(c), :, pl.ds(q_off(c), tq)] for c in chains]

    m_ref[...] = jnp.full(m_ref.shape, NEG, F32)
    l_ref[...] = jnp.zeros_like(l_ref)
    acc_ref[...] = jnp.zeros_like(acc_ref)
    a_ref[...] = jnp.ones_like(a_ref)
    p_ref[...] = jnp.zeros_like(p_ref)

    def qk(j, slot, which):
        off = pl.multiple_of(j * tk, tk)
        for c in which:
            s_ref[slot, c] = _dot(ka_ref[unit(c), pl.ds(off, tk), :], qa[c])

    def pv(j, which):
        off = pl.multiple_of(j * tk, tk)
        for c in which:
            v = fv_ref[rows(c), pl.ds(pl.multiple_of(base(c) + off, tk), tk)]
            acc_ref[c] = a_ref[c] * acc_ref[c] + _dot(v, p_ref[c])

    def softmax(slot, which, own_block=()):
        for c in which:
            s = s_ref[slot, c]
            if c in own_block:
                key = lax.broadcasted_iota(jnp.int32, s.shape, 0)
                qry = lax.broadcasted_iota(jnp.int32, s.shape, 1)
                s = jnp.where(key <= qry, s, NEG)
            m = m_ref[c]
            m_new = jnp.maximum(m, jnp.max(s, axis=0, keepdims=True))
            p = jnp.exp2(s - m_new)
            alpha = jnp.exp2(m - m_new)
            l_ref[c] = alpha * l_ref[c] + jnp.sum(p, axis=0, keepdims=True)
            m_ref[c] = m_new
            a_ref[c] = alpha
            p_ref[c] = p.astype(BF16)

    def stage(j, slot):
        qk(j + 1, 1 - slot, chains)
        pv(jnp.maximum(j - 1, 0), chains)
        softmax(slot, chains)

    qk(0, 0, chains)

    @pl.loop(0, i)
    def _(k):
        stage(2 * k, 0)
        stage(2 * k + 1, 1)

    first_own = 2 * i
    qk(first_own + 1, 1, tile_b)
    pv(jnp.maximum(first_own - 1, 0), chains)
    softmax(0, chains, own_block=tile_a)
    pv(first_own, chains)
    softmax(1, tile_b, own_block=tile_b)
    pv(first_own + 1, tile_b)

    for c in chains:
        out_ref[rows(c), pl.ds(pl.multiple_of(base(c) + q_off(c), tq), tq)] = (acc_ref[c] / l_ref[c]).astype(BF16)


def _fox(feat, small, f_bias_col, batch, seq):
    tq = min(TQ_FOX, seq)
    tk = min(TK_FOX, tq)
    assert tq == tk and seq % (2 * tq) == 0, "query tiles run in pairs; a tile's own block is one key block"
    group = FOX_GROUP if batch % FOX_GROUP == 0 else 1
    span = group * seq
    units = group * FOX_HEADS
    chains = 2 * units
    t = batch * seq
    width = FOX_HEADS * HEAD_DIM
    blk = lambda row: pl.BlockSpec((width, span), lambda b: (row // width, b))
    col = lambda n: pl.BlockSpec((n, 1), lambda b: (0, 0))
    return pl.pallas_call(
        functools.partial(_fox_kernel, seq=seq, tq=tq, tk=tk),
        grid=(batch // group,),
        in_specs=[blk(ROW_FQ), blk(ROW_FK), blk(ROW_FV),
                  pl.BlockSpec((N_SMALL, span), lambda b: (0, b)),
                  col(N_SMALL)],
        out_specs=pl.BlockSpec((width, span), lambda b: (0, b)),
        out_shape=jax.ShapeDtypeStruct((width, t), BF16),
        scratch_shapes=[pltpu.VMEM((units, 2 * HEAD_DIM, seq), BF16),
                        pltpu.VMEM((units, seq, 2 * HEAD_DIM), BF16),
                        pltpu.VMEM((2, chains, tk, tq), F32),
                        pltpu.VMEM((chains, tk, tq), BF16),
                        pltpu.VMEM((chains, HEAD_DIM, tq), F32),
                        pltpu.VMEM((chains, 1, tq), F32),
                        pltpu.VMEM((chains, 1, tq), F32),
                        pltpu.VMEM((chains, 1, tq), F32)],
        compiler_params=pltpu.CompilerParams(
            dimension_semantics=("parallel",), vmem_limit_bytes=VMEM_LIMIT),
        name="fox_attention",
    )(feat, feat, feat, small, f_bias_col)


def _short_conv_tile(ct_ref, w_ref, carry_ref, sequence_start):
    width = w_ref.shape[1]
    u = ct_ref[:, COL_CU:COL_CU + width].astype(F32)
    b_gate = ct_ref[:, COL_CB:COL_CB + width].astype(F32)
    c_gate = ct_ref[:, COL_CC:COL_CC + width].astype(F32)
    z = c_gate * u
    prev = jnp.where(sequence_start, 0.0, carry_ref[...])
    carry_ref[...] = z[z.shape[0] - 8:, :]
    last1, last2 = prev[7:8, :], prev[6:7, :]
    row = lax.broadcasted_iota(jnp.int32, z.shape, 0)
    z1 = jnp.where(row >= 1, pltpu.roll(z, 1, 0), last1)
    z2 = jnp.where(row >= 2, pltpu.roll(z, 2, 0), jnp.where(row == 1, last1, last2))
    w = w_ref[...]
    return (b_gate * (w[0:1, :] * z2 + w[1:2, :] * z1 + w[2:3, :] * z)).astype(BF16)


def _mlstm_kernel(q_ref, k_ref, v_ref, o_ref, small_ref, bias_ref, gain_ref, out_ref,
                  c_ref, m_ref, b_ref, u_ref, ut_ref, *, seq, chunk):
    c_ref[...] = jnp.zeros_like(c_ref)
    m_ref[...] = jnp.zeros_like(m_ref)
    g = small_ref[...] + bias_ref[...]
    b_all = _cumsum_lanes(_log_sigmoid(g), segment=chunk)
    b_ref[...] = b_all
    u_all = _row_select([g[4 + h:5 + h, :] - b_all[8 + h:9 + h, :] for h in range(MLSTM_HEADS)],
                        8, g.shape[1])
    u_ref[...] = u_all
    ut_ref[...] = jnp.concatenate([u_all, jnp.zeros((120, g.shape[1]), F32)], axis=0).T
    refs = (q_ref, k_ref, v_ref, o_ref, gain_ref, out_ref, c_ref, m_ref, b_ref, u_ref, ut_ref)
    pl.loop(0, seq // chunk)(lambda c: _mlstm_chunk(c, *refs, seq=seq, chunk=chunk))


def _mlstm_chunk(c, q_ref, k_ref, v_ref, o_ref, gain_ref, out_ref, c_ref, m_ref, b_ref, u_ref, ut_ref, *,
                 seq, chunk):
    units = range(c_ref.shape[0])
    head = lambda u: u % MLSTM_HEADS
    sl = lambda u: slice(head(u) * HEAD_DIM, (head(u) + 1) * HEAD_DIM)
    cols = lambda u: pl.ds(pl.multiple_of((u // MLSTM_HEADS) * seq + c * chunk, chunk), chunk)

    src = lax.broadcasted_iota(jnp.int32, (chunk, chunk), 0)
    tgt = lax.broadcasted_iota(jnp.int32, (chunk, chunk), 1)
    causal = src <= tgt
    ones_rows = (lax.broadcasted_iota(jnp.int32, (HEAD_DIM, chunk), 0) == 0).astype(BF16)

    def first_matmuls(u):
        qs = q_ref[sl(u), cols(u)]
        ks = k_ref[sl(u), cols(u)]
        c_prev = c_ref[u]
        return dict(ks=ks, c_prev=c_prev, scores=_dot_tn(ks, qs),
                    carried=_dot(c_prev.astype(BF16), qs))

    def gate_arithmetic(u, st):
        h = head(u)
        b_row = b_ref[8 + h:9 + h, cols(u)]
        b_last = b_row[:, chunk - 1:chunk]
        u_row = u_ref[h:h + 1, cols(u)]
        u_col = ut_ref[cols(u), h:h + 1]
        m_prev = m_ref[u][0:1, 0:1]
        dmat = jnp.where(causal, b_row + u_col, NEG)
        g_row = b_row + m_prev
        m_u = jnp.maximum(g_row, jnp.max(dmat, axis=0, keepdims=True))
        st.update(sw=(st["scores"] * jnp.exp(dmat - m_u)).astype(BF16), m_t=m_u, inter=jnp.exp(g_row - m_u))
        m_loc = jnp.max(u_row + b_last, axis=1, keepdims=True)
        m_new = jnp.maximum(b_last + m_prev, m_loc)
        w_row = jnp.exp(u_row + b_last - m_new)
        st.update(decay=jnp.exp(b_last + m_prev - m_new),
                  kw=(st["ks"].astype(F32) * w_row).astype(BF16))
        m_ref[u] = jnp.broadcast_to(m_new, m_ref.shape[1:])

    def second_matmuls(u, st):
        v_aug = jnp.concatenate([v_ref[sl(u), cols(u)], ones_rows], axis=0)
        st.update(intra=_dot(v_aug, st["sw"]), update=_dot_nt(v_aug, st["kw"]))

    def epilogue(u, st):
        c_ref[u] = st["decay"] * st["c_prev"] + st["update"]
        tot = st["intra"] + st["inter"] * st["carried"]
        num = tot[0:HEAD_DIM]
        den = tot[HEAD_DIM:HEAD_DIM + 1]
        ht = num / jnp.maximum(jnp.abs(den), jnp.exp(-st["m_t"]))
        hn = _rms_rows(ht, gain_ref[sl(u), :])
        out_ref[sl(u), cols(u)] = (_sigmoid(o_ref[sl(u), cols(u)].astype(F32)) * hn).astype(BF16)

    n = len(units)
    ahead = 2
    state = {u: first_matmuls(u) for u in range(min(ahead, n))}
    for u in units:
        gate_arithmetic(u, state[u])
        if u + ahead < n:
            state[u + ahead] = first_matmuls(u + ahead)
        second_matmuls(u, state[u])
        if u >= 1:
            epilogue(u - 1, state.pop(u - 1))
    epilogue(n - 1, state.pop(n - 1))


def _mlstm(feat, small, bias_col, gain_col, batch, seq):
    chunk = min(L_MLSTM, seq)
    group = MLSTM_GROUP if batch % MLSTM_GROUP == 0 else 1
    span = group * seq
    units = group * MLSTM_HEADS
    t = batch * seq
    width = MLSTM_HEADS * HEAD_DIM
    blk = lambda row: pl.BlockSpec((width, span), lambda b: (row // width, b))
    return pl.pallas_call(
        functools.partial(_mlstm_kernel, seq=seq, chunk=chunk),
        grid=(batch // group,),
        in_specs=[blk(ROW_MQ), blk(ROW_MK), blk(ROW_MV), blk(ROW_MO),
                  pl.BlockSpec((N_SMALL, span), lambda b: (0, b)),
                  pl.BlockSpec((N_SMALL, 1), lambda b: (0, 0)),
                  pl.BlockSpec((width, 1), lambda b: (0, 0))],
        out_specs=pl.BlockSpec((width, span), lambda b: (0, b)),
        out_shape=jax.ShapeDtypeStruct((width, t), BF16),
        scratch_shapes=[pltpu.VMEM((units, 2 * HEAD_DIM, HEAD_DIM), F32),
                        pltpu.VMEM((units, 8, 128), F32),
                        pltpu.VMEM((N_SMALL, span), F32),
                        pltpu.VMEM((8, span), F32),
                        pltpu.VMEM((span, 128), F32)],
        compiler_params=pltpu.CompilerParams(
            dimension_semantics=("parallel",), vmem_limit_bytes=VMEM_LIMIT),
        name="mlstm",
    )(feat, feat, feat, feat, small, bias_col, gain_col)


def _swa_kernel(q_ref, k_ref, v_ref, bias_ref, sink_ref, out_ref, *, nblk):
    refs = (q_ref, k_ref, v_ref, bias_ref, sink_ref, out_ref)
    pl.loop(0, q_ref.shape[1] // (nblk * WINDOW))(lambda n: _swa_span(n, *refs, nblk=nblk))


def _swa_span(n, q_ref, k_ref, v_ref, bias_ref, sink_ref, out_ref, *, nblk):
    w = WINDOW
    off = pl.multiple_of(n * (nblk * w), nblk * w)
    cur = pl.ds(off, nblk * w)
    prev = pl.ds(pl.multiple_of(jnp.maximum(off - w, 0), w), w)
    key = lax.broadcasted_iota(jnp.int32, (2 * w, SWA_GROUP * w), 0)
    first_block_pad = jnp.logical_and(n == 0, key < w)
    kvs = range(SWA_KV_HEADS)
    ksl = [slice(kv * HEAD_DIM, (kv + 1) * HEAD_DIM) for kv in kvs]
    kt = [jnp.concatenate([k_ref[s, prev], k_ref[s, cur]], axis=1) for s in ksl]
    vt = [jnp.concatenate([v_ref[s, prev], v_ref[s, cur]], axis=1) for s in ksl]

    bands = [(blk, kv) for blk in range(nblk) for kv in kvs]

    def head_rows(kv, g):
        return slice((kv * SWA_GROUP + g) * HEAD_DIM, (kv * SWA_GROUP + g + 1) * HEAD_DIM)

    def score(blk, kv):
        qt = jnp.concatenate([q_ref[head_rows(kv, g), pl.ds(off + blk * w, w)] for g in range(SWA_GROUP)], axis=1)
        s = _dot_tn(kt[kv][:, blk * w:(blk + 2) * w], qt) + bias_ref[kv]
        return jnp.where(first_block_pad, NEG, s) if blk == 0 else s

    def softmax(s, kv):
        sink = sink_ref[kv]
        m = jnp.maximum(jnp.max(s, axis=0, keepdims=True), sink)
        p = jnp.exp2(s - m)
        return p.astype(BF16), jnp.sum(p, axis=0, keepdims=True) + jnp.exp2(sink - m)

    def finish(blk, kv, o, d):
        o = o / d
        for g in range(SWA_GROUP):
            out_ref[head_rows(kv, g), pl.ds(off + blk * w, w)] = o[:, g * w:(g + 1) * w].astype(BF16)

    ahead = 2
    scores = [score(*band) for band in bands[:ahead]]
    pending = None
    for k, (blk, kv) in enumerate(bands):
        p, d = softmax(scores[k], kv)
        if k + ahead < len(bands):
            scores.append(score(*bands[k + ahead]))
        o = _dot(vt[kv][:, blk * w:(blk + 2) * w], p)
        if pending is not None:
            finish(*pending)
        pending = (blk, kv, o, d)
    finish(*pending)


def _swa(feat, bias_t, sink_rows, batch, seq):
    nblk = min(SWA_BLOCKS, seq // WINDOW)
    t = batch * seq
    qw = SWA_Q_HEADS * HEAD_DIM
    kw = SWA_KV_HEADS * HEAD_DIM
    return pl.pallas_call(
        functools.partial(_swa_kernel, nblk=nblk),
        grid=(batch,),
        in_specs=[pl.BlockSpec((qw, seq), lambda b: (ROW_SQ // qw, b)),
                  pl.BlockSpec((kw, seq), lambda b: (ROW_SK // kw, b)),
                  pl.BlockSpec((kw, seq), lambda b: (ROW_SV // kw, b)),
                  pl.BlockSpec(bias_t.shape, lambda b: (0, 0, 0)),
                  pl.BlockSpec(sink_rows.shape, lambda b: (0, 0, 0))],
        out_specs=pl.BlockSpec((qw, seq), lambda b: (0, b)),
        out_shape=jax.ShapeDtypeStruct((qw, t), BF16),
        compiler_params=pltpu.CompilerParams(
            dimension_semantics=("parallel",), vmem_limit_bytes=VMEM_LIMIT),
        name="swa_attention",
    )(feat, feat, feat, bias_t, sink_rows)


def _merge_ffn_kernel(x_ref, yf_ref, ct_ref, ym_ref, ys_ref, g_ref, cw_ref,
                      wf_ref, wc_ref, wm_ref, ws_ref, wo_ref, gain_ref, wg_ref, wu_ref, wd_ref,
                      out_ref, carry_ref, *, tiles_per_seq):
    def gated(branch, y):
        g = g_ref[:, branch * D_MODEL:(branch + 1) * D_MODEL]
        return (1.0 + jnp.tanh(g.astype(F32))) * y

    y_conv = _short_conv_tile(ct_ref, cw_ref, carry_ref, lax.rem(pl.program_id(0), tiles_per_seq) == 0)
    merged = gated(0, _dot_tn(yf_ref[...], wf_ref[...]))
    merged += gated(1, _dot(y_conv, wc_ref[...]))
    merged += gated(2, _dot_tn(ym_ref[...], wm_ref[...]))
    merged += gated(3, _dot_tn(ys_ref[...], ws_ref[...]))
    x = x_ref[...] + _dot(merged.astype(BF16), wo_ref[...])

    ms = jnp.mean(x * x, axis=-1, keepdims=True)
    hn = (x * lax.rsqrt(ms + EPS) * gain_ref[...]).astype(BF16)
    acc = x
    bounds = list(range(0, D_FF, TF_FFN)) + [D_FF]
    for lo, hi in zip(bounds[:-1], bounds[1:]):
        h = _dot(hn, wg_ref[:, lo:hi])
        act = (h * (1.0 + jnp.tanh(h)) * _dot(hn, wu_ref[:, lo:hi])).astype(BF16)
        acc = acc + _dot(act, wd_ref[lo:hi, :])
    out_ref[...] = acc


def _merge_ffn(x2, y_fox_t, conv_tok, y_mlstm_t, y_swa_t, gates, conv_w, w_fox, w_conv, w_mlstm, w_swa,
               w_out, gain, w_gate_half, w_up, w_down, layer, seq):
    t = x2.shape[0]
    tm = min(TM_MERGE, t, seq)
    row_blk = lambda width: pl.BlockSpec((tm, width), lambda i: (i, 0))
    feat_blk = lambda a: pl.BlockSpec((a.shape[0], tm), lambda i: (0, i))
    full = lambda a: _resident(a.shape[1:], layer)
    return pl.pallas_call(
        functools.partial(_merge_ffn_kernel, tiles_per_seq=seq // tm),
        grid=(t // tm,),
        in_specs=[row_blk(D_MODEL),
                  feat_blk(y_fox_t),
                  row_blk(conv_tok.shape[1]),
                  feat_blk(y_mlstm_t), feat_blk(y_swa_t),
                  row_blk(gates.shape[1]),
                  full(conv_w),
                  full(w_fox), full(w_conv), full(w_mlstm), full(w_swa), full(w_out),
                  full(gain), full(w_gate_half), full(w_up), full(w_down)],
        out_specs=row_blk(D_MODEL),
        out_shape=jax.ShapeDtypeStruct((t, D_MODEL), F32),
        scratch_shapes=[pltpu.VMEM((8, conv_w.shape[2]), F32)],
        compiler_params=pltpu.CompilerParams(
            dimension_semantics=("arbitrary",), vmem_limit_bytes=VMEM_LIMIT),
        name="merge_ffn",
    )(x2, y_fox_t, conv_tok, y_mlstm_t, y_swa_t, gates, conv_w,
      w_fox, w_conv, w_mlstm, w_swa, w_out, gain, w_gate_half, w_up, w_down)


IN_FQKV, IN_FF, IN_CONV, IN_MQ, IN_MK, IN_MV = 0, 768, 772, 1540, 1796, 2052
IN_MI, IN_MF, IN_MO, IN_SQ, IN_SK, IN_SV, IN_GATES, IN_COLS = 2308, 2312, 2316, 2572, 3084, 3212, 3340, 7436
TK_PREP = 256


def _wprep_kernel(wt_ref, tok_ref, feat_ref):
    tok_ref[:, COL_CU:COL_GATES] = wt_ref[IN_CONV:IN_MQ, :].T.astype(BF16)
    for lo in range(0, N_TOK - COL_GATES, TN_IN):
        gates = wt_ref[IN_GATES + lo:IN_GATES + lo + TN_IN, :]
        tok_ref[:, COL_GATES + lo:COL_GATES + lo + TN_IN] = (0.5 * gates).T.astype(BF16)
    for row, lo, hi in ((ROW_FQ, IN_FQKV, IN_FF), (ROW_MQ, IN_MQ, IN_MI), (ROW_SQ, IN_SQ, IN_SK),
                        (ROW_MO, IN_MO, IN_SQ), (ROW_SK, IN_SK, IN_GATES)):
        feat_ref[row:row + hi - lo, :] = wt_ref[lo:hi, :].astype(BF16)
    fox_f = wt_ref[IN_FF:IN_FF + N_SMALL, :]
    mlstm_if = wt_ref[IN_MI - 4:IN_MI - 4 + N_SMALL, :]
    row = lax.broadcasted_iota(jnp.int32, fox_f.shape, 0)
    small = jnp.where(row < 4, fox_f, jnp.where(row < 12, mlstm_if, 0.0))
    feat_ref[N_FEAT:, :] = small.astype(BF16)


def _prep_w_in(w_in):
    depth, d, cols = w_in.shape
    return pl.pallas_call(
        _wprep_kernel,
        grid=(depth, d // TK_PREP),
        in_specs=[pl.BlockSpec((None, cols, TK_PREP), lambda l, j: (l, 0, j))],
        out_specs=[pl.BlockSpec((None, TK_PREP, N_TOK), lambda l, j: (l, j, 0)),
                   pl.BlockSpec((None, N_FEAT + N_SMALL, TK_PREP), lambda l, j: (l, 0, j))],
        out_shape=[jax.ShapeDtypeStruct((depth, d, N_TOK), BF16),
                   jax.ShapeDtypeStruct((depth, N_FEAT + N_SMALL, d), BF16)],
        compiler_params=pltpu.CompilerParams(
            dimension_semantics=("parallel", "parallel"), vmem_limit_bytes=VMEM_LIMIT),
        name="w_in_relayout",
    )(jnp.transpose(w_in, (0, 2, 1)))


def _head_scale(depth, fox_q_gain, fox_k_gain, swa_q_gain, swa_k_gain):
    def tiled(gain, heads, factor=1.0):
        return jnp.tile(gain.astype(F32) * factor, (1, heads))
    ones = lambda n: jnp.ones((depth, n), F32)
    cols = [(ROW_FQ, tiled(fox_q_gain, FOX_HEADS, QK_SCALE * LOG2E)), (ROW_FK, tiled(fox_k_gain, FOX_HEADS)),
            (ROW_FV, ones(ROW_MQ - ROW_FV)), (ROW_MQ, QK_SCALE * ones(ROW_MK - ROW_MQ)),
            (ROW_MK, ones(ROW_SQ - ROW_MK)), (ROW_SQ, tiled(swa_q_gain, SWA_Q_HEADS, QK_SCALE * LOG2E)),
            (ROW_MO, ones(ROW_SK - ROW_MO)), (ROW_SK, tiled(swa_k_gain, SWA_KV_HEADS)),
            (ROW_SV, ones(N_FEAT - ROW_SV))]
    assert [r for r, _ in cols] == sorted(r for r, _ in cols)
    return jnp.concatenate([c for _, c in cols], axis=1)[:, :, None]


def _col(v, n=None):
    v = v.astype(F32).reshape(-1, 1)
    if n is not None and v.shape[0] < n:
        v = jnp.concatenate([v, jnp.zeros((n - v.shape[0], 1), F32)], axis=0)
    return v


def kernel(x, rel_bias, attn_norm, w_in, fox_f_bias, fox_q_gain, fox_k_gain, conv_w, mlstm_i_bias, mlstm_f_bias, mlstm_h_gain, swa_q_gain, swa_k_gain, swa_sinks, w_fox_out, w_conv_out, w_mlstm_out, w_swa_out, w_merge_out, ffn_norm, w_gate, w_up, w_down):
    batch, seq, _ = x.shape
    depth = w_in.shape[0]
    x2 = x.reshape(batch * seq, D_MODEL)
    bias_t = _swa_bias_table(rel_bias)

    w_tok, w_feat_t = _prep_w_in(w_in)
    attn_gain = attn_norm.reshape(depth, 1, D_MODEL)
    ffn_gain = ffn_norm.reshape(depth, 1, D_MODEL)
    w_fox_b, w_conv_b, w_mlstm_b, w_swa_b = (w.astype(BF16) for w in (w_fox_out, w_conv_out, w_mlstm_out, w_swa_out))
    w_merge_half = (0.5 * w_merge_out).astype(BF16)
    w_gate_half = (0.5 * w_gate).astype(BF16)
    w_up_b = w_up.astype(BF16)
    w_down_b = w_down.astype(BF16)

    head_scale = _head_scale(depth, fox_q_gain, fox_k_gain, swa_q_gain, swa_k_gain)

    for l in range(depth):
        tok, gates, feat, small = _inproj(x2, attn_gain, w_tok, w_feat_t, head_scale, l)

        y_fox_t = _fox(feat, small, _col(fox_f_bias[l], N_SMALL), batch, seq)
        gate_bias = jnp.concatenate([jnp.zeros((4,), F32), mlstm_i_bias[l], mlstm_f_bias[l]])
        y_mlstm_t = _mlstm(feat, small, _col(gate_bias, N_SMALL), _col(mlstm_h_gain[l]), batch, seq)
        sink_rows = jnp.broadcast_to(
            (swa_sinks[l].astype(F32) * LOG2E).reshape(SWA_KV_HEADS, 1, SWA_GROUP, 1),
            (SWA_KV_HEADS, 1, SWA_GROUP, WINDOW)).reshape(SWA_KV_HEADS, 1, SWA_GROUP * WINDOW)
        y_swa_t = _swa(feat, bias_t, sink_rows, batch, seq)

        x2 = _merge_ffn(x2, y_fox_t, tok, y_mlstm_t, y_swa_t, gates, conv_w,
                        w_fox_b, w_conv_b, w_mlstm_b, w_swa_b, w_merge_half,
                        ffn_gain, w_gate_half, w_up_b, w_down_b, l, seq)
    return x2.reshape(batch, seq, D_MODEL)
```

```python
import functools

import numpy as np
import jax
import jax.numpy as jnp
from jax import lax
from jax.experimental import pallas as pl
from jax.experimental.pallas import tpu as pltpu

F32 = jnp.float32
BF16 = jnp.bfloat16

D_MODEL = 1024
HEAD_DIM = 64
FOX_HEADS = 4
MLSTM_HEADS = 4
SWA_Q_HEADS = 8
SWA_KV_HEADS = 2
SWA_GROUP = SWA_Q_HEADS // SWA_KV_HEADS
WINDOW = 128
REL_BUCKETS = 32
REL_MAX_DIST = 128
D_FF = 2816
EPS = 1e-6
NEG = -1e30
QK_SCALE = HEAD_DIM ** -0.5
LOG2E = 1.4426950408889634

ROW_FQ, ROW_FK, ROW_FV = 0, 256, 512
ROW_MQ, ROW_MK, ROW_MV = 768, 1024, 1280
ROW_SQ, ROW_MO, ROW_SK, ROW_SV = 1536, 2048, 2304, 2432
N_FEAT = 2560
N_SMALL = 16
COL_CU, COL_CB, COL_CC, COL_GATES = 0, 256, 512, 768
N_TOK = 4864

TM_IN = 512
TN_IN = 512
FEAT_CHUNK = 640
FOX_GROUP = 2
FOX_TILES = 2
TQ_FOX = 256
TK_FOX = 256
L_MLSTM = 256
MLSTM_GROUP = 2
SWA_BLOCKS = 16
TM_MERGE = 512
Y_TILE = TM_MERGE
TF_FFN = 512
VMEM_LIMIT = 56 * 1024 * 1024


def _dot(a, b):
    return jnp.dot(a, b, preferred_element_type=F32)


def _dot_nt(a, b):
    return lax.dot_general(a, b, (((1,), (1,)), ((), ())), preferred_element_type=F32)


def _dot_tn(a, b):
    return lax.dot_general(a, b, (((0,), (0,)), ((), ())), preferred_element_type=F32)


def _sigmoid(x):
    return 0.5 * jnp.tanh(0.5 * x) + 0.5


def _log_sigmoid(x):
    return jnp.minimum(x, 0.0) - jnp.log(1.0 + jnp.exp(-jnp.abs(x)))


def _cumsum_lanes(x, segment=None):
    n = segment or x.shape[-1]
    lane = lax.broadcasted_iota(jnp.int32, x.shape, x.ndim - 1) & (n - 1)
    k = 1
    while k < n:
        x = x + jnp.where(lane >= k, pltpu.roll(x, k, x.ndim - 1), 0.0)
        k *= 2
    return x


def _row_select(rows, n_rows, width):
    rid = lax.broadcasted_iota(jnp.int32, (n_rows, width), 0)
    out = jnp.zeros((n_rows, width), F32)
    for r, v in enumerate(rows):
        out = jnp.where(rid == r, v, out)
    return out


def _split3(c):
    hi = c.astype(BF16).astype(F32)
    r = c - hi
    mid = r.astype(BF16).astype(F32)
    lo = (r - mid).astype(BF16).astype(F32)
    return hi, mid, lo


def _rms_rows(xt, gain_col):
    ms = jnp.mean(xt * xt, axis=0, keepdims=True)
    return xt * lax.rsqrt(ms + EPS) * gain_col


def _bias_kernel(rb_ref, idx_ref, out_ref):
    kv = pl.program_id(0)
    g = pl.program_id(1)
    head = kv * SWA_GROUP + g
    idx = idx_ref[...]
    acc = jnp.full(idx.shape, NEG, F32)
    for b in range(REL_BUCKETS):
        acc = jnp.where(idx == b, rb_ref[b * SWA_Q_HEADS + head] * LOG2E, acc)
    out_ref[0] = acc


def _bucket_table():
    j = np.arange(2 * WINDOW)[:, None]
    i = np.arange(WINDOW)[None, :]
    dist = i + WINDOW - j
    n = np.maximum(dist, 0).astype(np.int32)
    max_exact = REL_BUCKETS // 2
    nf = np.maximum(n, 1).astype(np.float32)
    large = max_exact + (np.log(nf / np.float32(max_exact)) / np.float32(np.log(REL_MAX_DIST / max_exact))
                         * np.float32(REL_BUCKETS - max_exact)).astype(np.int32)
    large = np.minimum(large, REL_BUCKETS - 1)
    bucket = np.where(n < max_exact, n, large)
    return np.where((dist >= 0) & (dist < WINDOW), bucket, -1).astype(np.int32)


def _swa_bias_table(rel_bias):
    idx = jnp.asarray(_bucket_table())
    return pl.pallas_call(
        _bias_kernel,
        grid=(SWA_KV_HEADS, SWA_GROUP),
        in_specs=[
            pl.BlockSpec(memory_space=pltpu.SMEM),
            pl.BlockSpec((2 * WINDOW, WINDOW), lambda kv, g: (0, 0)),
        ],
        out_specs=pl.BlockSpec((1, 2 * WINDOW, WINDOW), lambda kv, g: (kv, 0, g)),
        out_shape=jax.ShapeDtypeStruct((SWA_KV_HEADS, 2 * WINDOW, SWA_GROUP * WINDOW), F32),
        name="swa_bias_table",
    )(rel_bias.reshape(-1), idx)


QK_NORM_ROWS = ((ROW_FQ, ROW_FV), (ROW_SQ, ROW_MO), (ROW_SK, ROW_SV))


def _inproj_kernel(x_ref, g_ref, wt_ref, wf_ref, hs_ref, tok_ref, gates_ref, feat_ref, small_ref):
    x = x_ref[...]
    ms = jnp.mean(x * x, axis=-1, keepdims=True)
    xn = (x * lax.rsqrt(ms + EPS) * g_ref[...]).astype(BF16)
    for out_ref, base, width in ((tok_ref, COL_CU, COL_GATES), (gates_ref, COL_GATES, N_TOK - COL_GATES)):
        for lo in range(0, width, TN_IN):
            hi = min(lo + TN_IN, width)
            out_ref[:, lo:hi] = _dot(xn, wt_ref[:, base + lo:base + hi]).astype(BF16)
    n_chunks = N_FEAT // FEAT_CHUNK
    for c in range(n_chunks):
        lo = c * FEAT_CHUNK
        hi = lo + FEAT_CHUNK + (N_SMALL if c == n_chunks - 1 else 0)
        r = _dot_nt(wf_ref[lo:hi, :], xn)
        for row in range(lo, lo + FEAT_CHUNK, HEAD_DIM):
            head = r[row - lo:row - lo + HEAD_DIM]
            scale = hs_ref[row:row + HEAD_DIM, :]
            if any(a <= row < b for a, b in QK_NORM_ROWS):
                head = _rms_rows(head, scale)
            elif ROW_MQ <= row < ROW_MK:
                head = head * scale
            feat_ref[row:row + HEAD_DIM, :] = head.astype(BF16)
        if c == n_chunks - 1:
            small_ref[...] = r[FEAT_CHUNK:]


def _resident(shape, layer=None):
    if layer is None:
        return pl.BlockSpec(shape, lambda *_: (0,) * len(shape), pipeline_mode=pl.Buffered(1))
    return pl.BlockSpec((None,) + tuple(shape), lambda *_: (layer,) + (0,) * len(shape),
                        pipeline_mode=pl.Buffered(1))


def _inproj(x2, gain, w_tok, w_feat_t, head_scale, layer):
    t = x2.shape[0]
    tm = min(TM_IN, t)
    return pl.pallas_call(
        _inproj_kernel,
        grid=(t // tm,),
        in_specs=[
            pl.BlockSpec((tm, D_MODEL), lambda i: (i, 0)),
            _resident((1, D_MODEL), layer),
            _resident((D_MODEL, N_TOK), layer),
            _resident((N_FEAT + N_SMALL, D_MODEL), layer),
            _resident((N_FEAT, 1), layer),
        ],
        out_specs=[
            pl.BlockSpec((tm, COL_GATES), lambda i: (i, 0)),
            pl.BlockSpec((tm, N_TOK - COL_GATES), lambda i: (i, 0)),
            pl.BlockSpec((N_FEAT, tm), lambda i: (0, i)),
            pl.BlockSpec((N_SMALL, tm), lambda i: (0, i)),
        ],
        out_shape=[
            jax.ShapeDtypeStruct((t, COL_GATES), BF16),
            jax.ShapeDtypeStruct((t, N_TOK - COL_GATES), BF16),
            jax.ShapeDtypeStruct((N_FEAT, t), BF16),
            jax.ShapeDtypeStruct((N_SMALL, t), F32),
        ],
        compiler_params=pltpu.CompilerParams(
            dimension_semantics=("parallel",), vmem_limit_bytes=VMEM_LIMIT),
        name="inproj",
    )(x2, gain, w_tok, w_feat_t, head_scale)


def _fox_kernel(fq_ref, fk_ref, fv_ref, small_ref, fb_ref, out_ref,
                qa_ref, ka_ref, s_ref, p_ref, acc_ref, m_ref, l_ref, a_ref, *, seq, tq, tk):
    n_units = qa_ref.shape[0]
    c_all = _cumsum_lanes(_log_sigmoid(small_ref[...] + fb_ref[...]), segment=seq)
    for u in range(n_units):
        e, h = divmod(u, FOX_HEADS)
        sl = slice(h * HEAD_DIM, (h + 1) * HEAD_DIM)
        cols = slice(e * seq, (e + 1) * seq)
        c = c_all[h:h + 1, cols] * LOG2E
        hi, mid, lo = _split3(c)
        one = jnp.ones_like(c)
        qa_ref[u, 0:HEAD_DIM, :] = fq_ref[sl, cols]
        qa_ref[u, HEAD_DIM:, :] = _row_select([hi, mid, lo, one, one, one], HEAD_DIM, seq).astype(BF16)
        k_aug = _row_select([one, one, one, -hi, -mid, -lo], HEAD_DIM, seq)
        ka_ref[u] = jnp.concatenate([fk_ref[sl, cols].astype(F32), k_aug], axis=0).T.astype(BF16)
    refs = (fv_ref, out_ref, qa_ref, ka_ref, s_ref, p_ref, acc_ref, m_ref, l_ref, a_ref)
    pl.loop(0, seq // (FOX_TILES * tq))(lambda i: _fox_tile_group(i, *refs, seq=seq, tq=tq, tk=tk))


def _fox_tile_group(i,fv_ref, out_ref, qa_ref, ka_ref, s_ref, p_ref, acc_ref, m_ref, l_ref, a_ref, *, seq, tq, tk):
    n = FOX_TILES
    n_units = qa_ref.shape[0]
    chains = range(n * n_units)
    unit = lambda c: c // n
    from_tile = lambda r: [c for c in chains if c % n >= r]
    rows = lambda c: slice((unit(c) % FOX_HEADS) * HEAD_DIM, (unit(c) % FOX_HEADS + 1) * HEAD_DIM)
    base = lambda c: (unit(c) // FOX_HEADS) * seq
    q_off = lambda c: pl.multiple_of((n * i + c % n) * tq, tq)
    qa = [qa_ref[unit(c), :, pl.ds(q_off(c), tq)] for c in chains]

    m_ref[...] = jnp.full(m_ref.shape, NEG, F32)
    l_ref[...] = jnp.zeros_like(l_ref)
    acc_ref[...] = jnp.zeros_like(acc_ref)
    a_ref[...] = jnp.ones_like(a_ref)
    p_ref[...] = jnp.zeros_like(p_ref)

    def qk(j, slot, which):
        off = pl.multiple_of(j * tk, tk)
        for c in which:
            s_ref[slot, c] = _dot(ka_ref[unit(c), pl.ds(off, tk), :], qa[c])

    def pv(j, which):
        off = pl.multiple_of(j * tk, tk)
        for c in which:
            v = fv_ref[rows(c), pl.ds(pl.multiple_of(base(c) + off, tk), tk)]
            acc_ref[c] = a_ref[c] * acc_ref[c] + _dot(v, p_ref[c])

    def softmax(slot, which, own_block=()):
        for c in which:
            s = s_ref[slot, c]
            if c in own_block:
                key = lax.broadcasted_iota(jnp.int32, s.shape, 0)
                qry = lax.broadcasted_iota(jnp.int32, s.shape, 1)
                s = jnp.where(key <= qry, s, NEG)
            m = m_ref[c]
            m_new = jnp.maximum(m, jnp.max(s, axis=0, keepdims=True))
            p = jnp.exp2(s - m_new)
            alpha = jnp.exp2(m - m_new)
            l_ref[c] = alpha * l_ref[c] + jnp.sum(p, axis=0, keepdims=True)
            m_ref[c] = m_new
            a_ref[c] = alpha
            p_ref[c] = p.astype(BF16)

    def stage(j, slot):
        qk(j + 1, 1 - slot, chains)
        pv(jnp.maximum(j - 1, 0), chains)
        softmax(slot, chains)

    qk(0, 0, chains)
    first_own = n * i

    @pl.loop(0, first_own // 2)
    def _(k):
        stage(2 * k, 0)
        stage(2 * k + 1, 1)

    for r in range(n):
        if r + 1 < n:
            qk(first_own + r + 1, (r + 1) % 2, from_tile(r + 1))
        pv(jnp.maximum(first_own + r - 1, 0), from_tile(max(r - 1, 0)))
        softmax(r % 2, from_tile(r), own_block=[c for c in chains if c % n == r])
    pv(first_own + n - 1, from_tile(n - 1))

    per_tile = Y_TILE // tq
    for c in chains:
        tile = base(c) // Y_TILE + (n // per_tile) * i + (c % n) // per_tile
        lane = ((c % n) % per_tile) * tq
        out_ref[tile, rows(c), lane:lane + tq] = (acc_ref[c] / l_ref[c]).astype(BF16)


def _fox(feat, small, f_bias_col, batch, seq):
    tq = min(TQ_FOX, seq)
    tk = min(TK_FOX, tq)
    assert tq == tk and FOX_TILES % 2 == 0 and seq % (FOX_TILES * tq) == 0, \
        "query tiles run in even groups; a tile's own block is one key block"
    group = FOX_GROUP if batch % FOX_GROUP == 0 else 1
    span = group * seq
    units = group * FOX_HEADS
    chains = FOX_TILES * units
    t = batch * seq
    width = FOX_HEADS * HEAD_DIM
    blk = lambda row: pl.BlockSpec((width, span), lambda b: (row // width, b))
    col = lambda n: pl.BlockSpec((n, 1), lambda b: (0, 0))
    return pl.pallas_call(
        functools.partial(_fox_kernel, seq=seq, tq=tq, tk=tk),
        grid=(batch // group,),
        in_specs=[blk(ROW_FQ), blk(ROW_FK), blk(ROW_FV),
                  pl.BlockSpec((N_SMALL, span), lambda b: (0, b)),
                  col(N_SMALL)],
        out_specs=pl.BlockSpec((span // Y_TILE, width, Y_TILE), lambda b: (b, 0, 0)),
        out_shape=jax.ShapeDtypeStruct((t // Y_TILE, width, Y_TILE), BF16),
        scratch_shapes=[pltpu.VMEM((units, 2 * HEAD_DIM, seq), BF16),
                        pltpu.VMEM((units, seq, 2 * HEAD_DIM), BF16),
                        pltpu.VMEM((2, chains, tk, tq), F32),
                        pltpu.VMEM((chains, tk, tq), BF16),
                        pltpu.VMEM((chains, HEAD_DIM, tq), F32),
                        pltpu.VMEM((chains, 1, tq), F32),
                        pltpu.VMEM((chains, 1, tq), F32),
                        pltpu.VMEM((chains, 1, tq), F32)],
        compiler_params=pltpu.CompilerParams(
            dimension_semantics=("parallel",), vmem_limit_bytes=VMEM_LIMIT),
        name="fox_attention",
    )(feat, feat, feat, small, f_bias_col)


def _short_conv_tile(ct_ref, w_ref, carry_ref, sequence_start):
    width = w_ref.shape[1]
    u = ct_ref[:, COL_CU:COL_CU + width].astype(F32)
    b_gate = ct_ref[:, COL_CB:COL_CB + width].astype(F32)
    c_gate = ct_ref[:, COL_CC:COL_CC + width].astype(F32)
    z = c_gate * u
    prev = jnp.where(sequence_start, 0.0, carry_ref[...])
    carry_ref[...] = z[z.shape[0] - 8:, :]
    last1, last2 = prev[7:8, :], prev[6:7, :]
    row = lax.broadcasted_iota(jnp.int32, z.shape, 0)
    z1 = jnp.where(row >= 1, pltpu.roll(z, 1, 0), last1)
    z2 = jnp.where(row >= 2, pltpu.roll(z, 2, 0), jnp.where(row == 1, last1, last2))
    w = w_ref[...]
    return (b_gate * (w[0:1, :] * z2 + w[1:2, :] * z1 + w[2:3, :] * z)).astype(BF16)


def _mlstm_kernel(q_ref, k_ref, v_ref, o_ref, small_ref, bias_ref, gain_ref, out_ref,
                  c_ref, m_ref, b_ref, u_ref, ut_ref, *, seq, chunk):
    c_ref[...] = jnp.zeros_like(c_ref)
    m_ref[...] = jnp.zeros_like(m_ref)
    g = small_ref[...] + bias_ref[...]
    b_all = _cumsum_lanes(_log_sigmoid(g), segment=chunk)
    b_ref[...] = b_all
    u_all = _row_select([g[4 + h:5 + h, :] - b_all[8 + h:9 + h, :] for h in range(MLSTM_HEADS)],
                        8, g.shape[1])
    u_ref[...] = u_all
    ut_ref[...] = jnp.concatenate([u_all, jnp.zeros((120, g.shape[1]), F32)], axis=0).T
    refs = (q_ref, k_ref, v_ref, o_ref, gain_ref, out_ref, c_ref, m_ref, b_ref, u_ref, ut_ref)
    pl.loop(0, seq // chunk)(lambda c: _mlstm_chunk(c, *refs, seq=seq, chunk=chunk))


def _mlstm_chunk(c, q_ref, k_ref, v_ref, o_ref, gain_ref, out_ref, c_ref, m_ref, b_ref, u_ref, ut_ref, *,
                 seq, chunk):
    units = range(c_ref.shape[0])
    head = lambda u: u % MLSTM_HEADS
    sl = lambda u: slice(head(u) * HEAD_DIM, (head(u) + 1) * HEAD_DIM)
    cols = lambda u: pl.ds(pl.multiple_of((u // MLSTM_HEADS) * seq + c * chunk, chunk), chunk)

    src = lax.broadcasted_iota(jnp.int32, (chunk, chunk), 0)
    tgt = lax.broadcasted_iota(jnp.int32, (chunk, chunk), 1)
    causal = src <= tgt
    ones_rows = (lax.broadcasted_iota(jnp.int32, (HEAD_DIM, chunk), 0) == 0).astype(BF16)

    def first_matmuls(u):
        qs = q_ref[sl(u), cols(u)]
        ks = k_ref[sl(u), cols(u)]
        c_prev = c_ref[u]
        return dict(ks=ks, c_prev=c_prev, scores=_dot_tn(ks, qs),
                    carried=_dot(c_prev.astype(BF16), qs))

    def gate_arithmetic(u, st):
        h = head(u)
        b_row = b_ref[8 + h:9 + h, cols(u)]
        b_last = b_row[:, chunk - 1:chunk]
        u_row = u_ref[h:h + 1, cols(u)]
        u_col = ut_ref[cols(u), h:h + 1]
        m_prev = m_ref[u][0:1, 0:1]
        dmat = jnp.where(causal, b_row + u_col, NEG)
        g_row = b_row + m_prev
        m_u = jnp.maximum(g_row, jnp.max(dmat, axis=0, keepdims=True))
        st.update(sw=(st["scores"] * jnp.exp(dmat - m_u)).astype(BF16), m_t=m_u, inter=jnp.exp(g_row - m_u))
        m_loc = jnp.max(u_row + b_last, axis=1, keepdims=True)
        m_new = jnp.maximum(b_last + m_prev, m_loc)
        w_row = jnp.exp(u_row + b_last - m_new)
        st.update(decay=jnp.exp(b_last + m_prev - m_new),
                  kw=(st["ks"].astype(F32) * w_row).astype(BF16))
        m_ref[u] = jnp.broadcast_to(m_new, m_ref.shape[1:])

    def second_matmuls(u, st):
        v_aug = jnp.concatenate([v_ref[sl(u), cols(u)], ones_rows], axis=0)
        st.update(intra=_dot(v_aug, st["sw"]), update=_dot_nt(v_aug, st["kw"]))

    def epilogue(u, st):
        c_ref[u] = st["decay"] * st["c_prev"] + st["update"]
        tot = st["intra"] + st["inter"] * st["carried"]
        num = tot[0:HEAD_DIM]
        den = tot[HEAD_DIM:HEAD_DIM + 1]
        ht = num / jnp.maximum(jnp.abs(den), jnp.exp(-st["m_t"]))
        hn = _rms_rows(ht, gain_ref[sl(u), :])
        per_tile = Y_TILE // chunk
        tile = (u // MLSTM_HEADS) * (seq // Y_TILE) + c // per_tile
        lane = pl.ds(pl.multiple_of(lax.rem(c, per_tile) * chunk, chunk), chunk)
        out_ref[tile, sl(u), lane] = (_sigmoid(o_ref[sl(u), cols(u)].astype(F32)) * hn).astype(BF16)

    n = len(units)
    ahead = 2
    state = {u: first_matmuls(u) for u in range(min(ahead, n))}
    for u in units:
        gate_arithmetic(u, state[u])
        if u + ahead < n:
            state[u + ahead] = first_matmuls(u + ahead)
        second_matmuls(u, state[u])
        if u >= 1:
            epilogue(u - 1, state.pop(u - 1))
    epilogue(n - 1, state.pop(n - 1))


def _mlstm(feat, small, bias_col, gain_col, batch, seq):
    chunk = min(L_MLSTM, seq)
    group = MLSTM_GROUP if batch % MLSTM_GROUP == 0 else 1
    span = group * seq
    units = group * MLSTM_HEADS
    t = batch * seq
    width = MLSTM_HEADS * HEAD_DIM
    blk = lambda row: pl.BlockSpec((width, span), lambda b: (row // width, b))
    return pl.pallas_call(
        functools.partial(_mlstm_kernel, seq=seq, chunk=chunk),
        grid=(batch // group,),
        in_specs=[blk(ROW_MQ), blk(ROW_MK), blk(ROW_MV), blk(ROW_MO),
                  pl.BlockSpec((N_SMALL, span), lambda b: (0, b)),
                  pl.BlockSpec((N_SMALL, 1), lambda b: (0, 0)),
                  pl.BlockSpec((width, 1), lambda b: (0, 0))],
        out_specs=pl.BlockSpec((span // Y_TILE, width, Y_TILE), lambda b: (b, 0, 0)),
        out_shape=jax.ShapeDtypeStruct((t // Y_TILE, width, Y_TILE), BF16),
        scratch_shapes=[pltpu.VMEM((units, 2 * HEAD_DIM, HEAD_DIM), F32),
                        pltpu.VMEM((units, 8, 128), F32),
                        pltpu.VMEM((N_SMALL, span), F32),
                        pltpu.VMEM((8, span), F32),
                        pltpu.VMEM((span, 128), F32)],
        compiler_params=pltpu.CompilerParams(
            dimension_semantics=("parallel",), vmem_limit_bytes=VMEM_LIMIT),
        name="mlstm",
    )(feat, feat, feat, feat, small, bias_col, gain_col)


def _swa_kernel(q_ref, k_ref, v_ref, bias_ref, sink_ref, out_ref, *, nblk):
    refs = (q_ref, k_ref, v_ref, bias_ref, sink_ref, out_ref)
    pl.loop(0, q_ref.shape[1] // (nblk * WINDOW))(lambda n: _swa_span(n, *refs, nblk=nblk))


def _swa_span(n, q_ref, k_ref, v_ref, bias_ref, sink_ref, out_ref, *, nblk):
    w = WINDOW
    off = pl.multiple_of(n * (nblk * w), nblk * w)
    cur = pl.ds(off, nblk * w)
    prev = pl.ds(pl.multiple_of(jnp.maximum(off - w, 0), w), w)
    key = lax.broadcasted_iota(jnp.int32, (2 * w, SWA_GROUP * w), 0)
    first_block_pad = jnp.logical_and(n == 0, key < w)
    kvs = range(SWA_KV_HEADS)
    ksl = [slice(kv * HEAD_DIM, (kv + 1) * HEAD_DIM) for kv in kvs]
    kt = [jnp.concatenate([k_ref[s, prev], k_ref[s, cur]], axis=1) for s in ksl]
    vt = [jnp.concatenate([v_ref[s, prev], v_ref[s, cur]], axis=1) for s in ksl]

    bands = [(blk, kv) for blk in range(nblk) for kv in kvs]

    def head_rows(kv, g):
        return slice((kv * SWA_GROUP + g) * HEAD_DIM, (kv * SWA_GROUP + g + 1) * HEAD_DIM)

    def score(blk, kv):
        qt = jnp.concatenate([q_ref[head_rows(kv, g), pl.ds(off + blk * w, w)] for g in range(SWA_GROUP)], axis=1)
        s = _dot_tn(kt[kv][:, blk * w:(blk + 2) * w], qt) + bias_ref[kv]
        return jnp.where(first_block_pad, NEG, s) if blk == 0 else s

    def softmax(s, kv):
        sink = sink_ref[kv]
        m = jnp.maximum(jnp.max(s, axis=0, keepdims=True), sink)
        p = jnp.exp2(s - m)
        return p.astype(BF16), jnp.sum(p, axis=0, keepdims=True) + jnp.exp2(sink - m)

    def finish(blk, kv, o, d):
        o = o / d
        for g in range(SWA_GROUP):
            tile = n * (nblk * w // Y_TILE) + blk * w // Y_TILE
            lane = blk * w % Y_TILE
            out_ref[tile, head_rows(kv, g), lane:lane + w] = o[:, g * w:(g + 1) * w].astype(BF16)

    ahead = 2
    scores = [score(*band) for band in bands[:ahead]]
    pending = None
    for k, (blk, kv) in enumerate(bands):
        p, d = softmax(scores[k], kv)
        if k + ahead < len(bands):
            scores.append(score(*bands[k + ahead]))
        o = _dot(vt[kv][:, blk * w:(blk + 2) * w], p)
        if pending is not None:
            finish(*pending)
        pending = (blk, kv, o, d)
    finish(*pending)


def _swa(feat, bias_t, sink_rows, batch, seq):
    nblk = min(SWA_BLOCKS, seq // WINDOW)
    t = batch * seq
    qw = SWA_Q_HEADS * HEAD_DIM
    kw = SWA_KV_HEADS * HEAD_DIM
    return pl.pallas_call(
        functools.partial(_swa_kernel, nblk=nblk),
        grid=(batch,),
        in_specs=[pl.BlockSpec((qw, seq), lambda b: (ROW_SQ // qw, b)),
                  pl.BlockSpec((kw, seq), lambda b: (ROW_SK // kw, b)),
                  pl.BlockSpec((kw, seq), lambda b: (ROW_SV // kw, b)),
                  pl.BlockSpec(bias_t.shape, lambda b: (0, 0, 0)),
                  pl.BlockSpec(sink_rows.shape, lambda b: (0, 0, 0))],
        out_specs=pl.BlockSpec((seq // Y_TILE, qw, Y_TILE), lambda b: (b, 0, 0)),
        out_shape=jax.ShapeDtypeStruct((t // Y_TILE, qw, Y_TILE), BF16),
        compiler_params=pltpu.CompilerParams(
            dimension_semantics=("parallel",), vmem_limit_bytes=VMEM_LIMIT),
        name="swa_attention",
    )(feat, feat, feat, bias_t, sink_rows)


def _merge_ffn_kernel(x_ref, yf_ref, ct_ref, ym_ref, ys_ref, g_ref, cw_ref,
                      wf_ref, wc_ref, wm_ref, ws_ref, wo_ref, gain_ref, wg_ref, wu_ref, wd_ref,
                      out_ref, carry_ref, *, tiles_per_seq):
    def gated(branch, y):
        g = g_ref[:, branch * D_MODEL:(branch + 1) * D_MODEL]
        return (1.0 + jnp.tanh(g.astype(F32))) * y

    y_conv = _short_conv_tile(ct_ref, cw_ref, carry_ref, lax.rem(pl.program_id(0), tiles_per_seq) == 0)
    merged = gated(0, _dot_tn(yf_ref[...], wf_ref[...]))
    merged += gated(1, _dot(y_conv, wc_ref[...]))
    merged += gated(2, _dot_tn(ym_ref[...], wm_ref[...]))
    merged += gated(3, _dot_tn(ys_ref[...], ws_ref[...]))
    x = x_ref[...] + _dot(merged.astype(BF16), wo_ref[...])

    ms = jnp.mean(x * x, axis=-1, keepdims=True)
    hn = (x * lax.rsqrt(ms + EPS) * gain_ref[...]).astype(BF16)
    acc = x
    bounds = list(range(0, D_FF, TF_FFN)) + [D_FF]
    for lo, hi in zip(bounds[:-1], bounds[1:]):
        h = _dot(hn, wg_ref[:, lo:hi])
        act = (h * (1.0 + jnp.tanh(h)) * _dot(hn, wu_ref[:, lo:hi])).astype(BF16)
        acc = acc + _dot(act, wd_ref[lo:hi, :])
    out_ref[...] = acc


def _merge_ffn(x2, y_fox_t, conv_tok, y_mlstm_t, y_swa_t, gates, conv_w, w_fox, w_conv, w_mlstm, w_swa,
               w_out, gain, w_gate_half, w_up, w_down, layer, seq):
    t = x2.shape[0]
    tm = min(TM_MERGE, t, seq)
    row_blk = lambda width: pl.BlockSpec((tm, width), lambda i: (i, 0))
    assert tm == Y_TILE, "the mixers write one contiguous (features, tokens) slab per merge row tile"
    feat_blk = lambda a: pl.BlockSpec((None,) + a.shape[1:], lambda i: (i, 0, 0))
    full = lambda a: _resident(a.shape[1:], layer)
    return pl.pallas_call(
        functools.partial(_merge_ffn_kernel, tiles_per_seq=seq // tm),
        grid=(t // tm,),
        in_specs=[row_blk(D_MODEL),
                  feat_blk(y_fox_t),
                  row_blk(conv_tok.shape[1]),
                  feat_blk(y_mlstm_t), feat_blk(y_swa_t),
                  row_blk(gates.shape[1]),
                  full(conv_w),
                  full(w_fox), full(w_conv), full(w_mlstm), full(w_swa), full(w_out),
                  full(gain), full(w_gate_half), full(w_up), full(w_down)],
        out_specs=row_blk(D_MODEL),
        out_shape=jax.ShapeDtypeStruct((t, D_MODEL), F32),
        scratch_shapes=[pltpu.VMEM((8, conv_w.shape[2]), F32)],
        compiler_params=pltpu.CompilerParams(
            dimension_semantics=("arbitrary",), vmem_limit_bytes=VMEM_LIMIT),
        name="merge_ffn",
    )(x2, y_fox_t, conv_tok, y_mlstm_t, y_swa_t, gates, conv_w,
      w_fox, w_conv, w_mlstm, w_swa, w_out, gain, w_gate_half, w_up, w_down)


IN_FQKV, IN_FF, IN_CONV, IN_MQ, IN_MK, IN_MV = 0, 768, 772, 1540, 1796, 2052
IN_MI, IN_MF, IN_MO, IN_SQ, IN_SK, IN_SV, IN_GATES, IN_COLS = 2308, 2312, 2316, 2572, 3084, 3212, 3340, 7436
TK_PREP = 256


def _wprep_kernel(wt_ref, tok_ref, feat_ref):
    tok_ref[:, COL_CU:COL_GATES] = wt_ref[IN_CONV:IN_MQ, :].T.astype(BF16)
    for lo in range(0, N_TOK - COL_GATES, TN_IN):
        gates = wt_ref[IN_GATES + lo:IN_GATES + lo + TN_IN, :]
        tok_ref[:, COL_GATES + lo:COL_GATES + lo + TN_IN] = (0.5 * gates).T.astype(BF16)
    for row, lo, hi in ((ROW_FQ, IN_FQKV, IN_FF), (ROW_MQ, IN_MQ, IN_MI), (ROW_SQ, IN_SQ, IN_SK),
                        (ROW_MO, IN_MO, IN_SQ), (ROW_SK, IN_SK, IN_GATES)):
        feat_ref[row:row + hi - lo, :] = wt_ref[lo:hi, :].astype(BF16)
    fox_f = wt_ref[IN_FF:IN_FF + N_SMALL, :]
    mlstm_if = wt_ref[IN_MI - 4:IN_MI - 4 + N_SMALL, :]
    row = lax.broadcasted_iota(jnp.int32, fox_f.shape, 0)
    small = jnp.where(row < 4, fox_f, jnp.where(row < 12, mlstm_if, 0.0))
    feat_ref[N_FEAT:, :] = small.astype(BF16)


def _prep_w_in(w_in):
    depth, d, cols = w_in.shape
    return pl.pallas_call(
        _wprep_kernel,
        grid=(depth, d // TK_PREP),
        in_specs=[pl.BlockSpec((None, cols, TK_PREP), lambda l, j: (l, 0, j))],
        out_specs=[pl.BlockSpec((None, TK_PREP, N_TOK), lambda l, j: (l, j, 0)),
                   pl.BlockSpec((None, N_FEAT + N_SMALL, TK_PREP), lambda l, j: (l, 0, j))],
        out_shape=[jax.ShapeDtypeStruct((depth, d, N_TOK), BF16),
                   jax.ShapeDtypeStruct((depth, N_FEAT + N_SMALL, d), BF16)],
        compiler_params=pltpu.CompilerParams(
            dimension_semantics=("parallel", "parallel"), vmem_limit_bytes=VMEM_LIMIT),
        name="w_in_relayout",
    )(jnp.transpose(w_in, (0, 2, 1)))


def _head_scale(depth, fox_q_gain, fox_k_gain, swa_q_gain, swa_k_gain):
    def tiled(gain, heads, factor=1.0):
        return jnp.tile(gain.astype(F32) * factor, (1, heads))
    ones = lambda n: jnp.ones((depth, n), F32)
    cols = [(ROW_FQ, tiled(fox_q_gain, FOX_HEADS, QK_SCALE * LOG2E)), (ROW_FK, tiled(fox_k_gain, FOX_HEADS)),
            (ROW_FV, ones(ROW_MQ - ROW_FV)), (ROW_MQ, QK_SCALE * ones(ROW_MK - ROW_MQ)),
            (ROW_MK, ones(ROW_SQ - ROW_MK)), (ROW_SQ, tiled(swa_q_gain, SWA_Q_HEADS, QK_SCALE * LOG2E)),
            (ROW_MO, ones(ROW_SK - ROW_MO)), (ROW_SK, tiled(swa_k_gain, SWA_KV_HEADS)),
            (ROW_SV, ones(N_FEAT - ROW_SV))]
    assert [r for r, _ in cols] == sorted(r for r, _ in cols)
    return jnp.concatenate([c for _, c in cols], axis=1)[:, :, None]


def _col(v, n=None):
    v = v.astype(F32).reshape(-1, 1)
    if n is not None and v.shape[0] < n:
        v = jnp.concatenate([v, jnp.zeros((n - v.shape[0], 1), F32)], axis=0)
    return v


def kernel(x, rel_bias, attn_norm, w_in, fox_f_bias, fox_q_gain, fox_k_gain, conv_w, mlstm_i_bias, mlstm_f_bias, mlstm_h_gain, swa_q_gain, swa_k_gain, swa_sinks, w_fox_out, w_conv_out, w_mlstm_out, w_swa_out, w_merge_out, ffn_norm, w_gate, w_up, w_down):
    batch, seq, _ = x.shape
    depth = w_in.shape[0]
    x2 = x.reshape(batch * seq, D_MODEL)
    bias_t = _swa_bias_table(rel_bias)

    w_tok, w_feat_t = _prep_w_in(w_in)
    attn_gain = attn_norm.reshape(depth, 1, D_MODEL)
    ffn_gain = ffn_norm.reshape(depth, 1, D_MODEL)
    w_fox_b, w_conv_b, w_mlstm_b, w_swa_b = (w.astype(BF16) for w in (w_fox_out, w_conv_out, w_mlstm_out, w_swa_out))
    w_merge_half = (0.5 * w_merge_out).astype(BF16)
    w_gate_half = (0.5 * w_gate).astype(BF16)
    w_up_b = w_up.astype(BF16)
    w_down_b = w_down.astype(BF16)

    head_scale = _head_scale(depth, fox_q_gain, fox_k_gain, swa_q_gain, swa_k_gain)

    for l in range(depth):
        tok, gates, feat, small = _inproj(x2, attn_gain, w_tok, w_feat_t, head_scale, l)

        y_fox_t = _fox(feat, small, _col(fox_f_bias[l], N_SMALL), batch, seq)
        gate_bias = jnp.concatenate([jnp.zeros((4,), F32), mlstm_i_bias[l], mlstm_f_bias[l]])
        y_mlstm_t = _mlstm(feat, small, _col(gate_bias, N_SMALL), _col(mlstm_h_gain[l]), batch, seq)
        sink_rows = jnp.broadcast_to(
            (swa_sinks[l].astype(F32) * LOG2E).reshape(SWA_KV_HEADS, 1, SWA_GROUP, 1),
            (SWA_KV_HEADS, 1, SWA_GROUP, WINDOW)).reshape(SWA_KV_HEADS, 1, SWA_GROUP * WINDOW)
        y_swa_t = _swa(feat, bias_t, sink_rows, batch, seq)

        x2 = _merge_ffn(x2, y_fox_t, tok, y_mlstm_t, y_swa_t, gates, conv_w,
                        w_fox_b, w_conv_b, w_mlstm_b, w_swa_b, w_merge_half,
                        ffn_gain, w_gate_half, w_up_b, w_down_b, l, seq)
    return x2.reshape(batch, seq, D_MODEL)
```

```python
import functools

import numpy as np
import jax
import jax.numpy as jnp
from jax import lax
from jax.experimental import pallas as pl
from jax.experimental.pallas import tpu as pltpu

F32 = jnp.float32
BF16 = jnp.bfloat16

D_MODEL = 1024
HEAD_DIM = 64
FOX_HEADS = 4
MLSTM_HEADS = 4
SWA_Q_HEADS = 8
SWA_KV_HEADS = 2
SWA_GROUP = SWA_Q_HEADS // SWA_KV_HEADS
WINDOW = 128
REL_BUCKETS = 32
REL_MAX_DIST = 128
D_FF = 2816
EPS = 1e-6
NEG = -1e30
QK_SCALE = HEAD_DIM ** -0.5
LOG2E = 1.4426950408889634

ROW_FQ, ROW_FK, ROW_FV = 0, 256, 512
ROW_MQ, ROW_MK, ROW_MV = 768, 1024, 1280
ROW_SQ, ROW_MO, ROW_SK, ROW_SV = 1536, 2048, 2304, 2432
N_FEAT = 2560
N_SMALL = 16
COL_CU, COL_CB, COL_CC, COL_GATES = 0, 256, 512, 768
N_TOK = 4864

TM_IN = 512
TN_IN = 512
FEAT_CHUNK = 640
FOX_GROUP = 2
FOX_TILES = 2
TQ_FOX = 256
TK_FOX = 256
L_MLSTM = 256
MLSTM_GROUP = 2
SWA_BLOCKS = 16
TM_MERGE = 512
Y_TILE = TM_MERGE
TF_FFN = 512
V7X_VMEM_BYTES = 64 * 1024 * 1024
VMEM_LIMIT = V7X_VMEM_BYTES * 7 // 8


def _dot(a, b):
    return jnp.dot(a, b, preferred_element_type=F32)


def _dot_nt(a, b):
    return lax.dot_general(a, b, (((1,), (1,)), ((), ())), preferred_element_type=F32)


def _dot_tn(a, b):
    return lax.dot_general(a, b, (((0,), (0,)), ((), ())), preferred_element_type=F32)


def _sigmoid(x):
    return 0.5 * jnp.tanh(0.5 * x) + 0.5


def _log_sigmoid(x):
    return jnp.minimum(x, 0.0) - jnp.log(1.0 + jnp.exp(-jnp.abs(x)))


def _cumsum_lanes(x, segment=None):
    n = segment or x.shape[-1]
    lane = lax.broadcasted_iota(jnp.int32, x.shape, x.ndim - 1) & (n - 1)
    k = 1
    while k < n:
        x = x + jnp.where(lane >= k, pltpu.roll(x, k, x.ndim - 1), 0.0)
        k *= 2
    return x


def _row_select(rows, n_rows, width):
    rid = lax.broadcasted_iota(jnp.int32, (n_rows, width), 0)
    out = jnp.zeros((n_rows, width), F32)
    for r, v in enumerate(rows):
        out = jnp.where(rid == r, v, out)
    return out


def _split3(c):
    hi = c.astype(BF16).astype(F32)
    r = c - hi
    mid = r.astype(BF16).astype(F32)
    lo = (r - mid).astype(BF16).astype(F32)
    return hi, mid, lo


def _rms_rows(xt, gain_col):
    ms = jnp.mean(xt * xt, axis=0, keepdims=True)
    return xt * lax.rsqrt(ms + EPS) * gain_col


def _bias_kernel(rb_ref, idx_ref, out_ref):
    kv = pl.program_id(0)
    g = pl.program_id(1)
    head = kv * SWA_GROUP + g
    idx = idx_ref[...]
    acc = jnp.full(idx.shape, NEG, F32)
    for b in range(REL_BUCKETS):
        acc = jnp.where(idx == b, rb_ref[b * SWA_Q_HEADS + head] * LOG2E, acc)
    out_ref[0] = acc


def _bucket_table():
    j = np.arange(2 * WINDOW)[:, None]
    i = np.arange(WINDOW)[None, :]
    dist = i + WINDOW - j
    n = np.maximum(dist, 0).astype(np.int32)
    max_exact = REL_BUCKETS // 2
    nf = np.maximum(n, 1).astype(np.float32)
    large = max_exact + (np.log(nf / np.float32(max_exact)) / np.float32(np.log(REL_MAX_DIST / max_exact))
                         * np.float32(REL_BUCKETS - max_exact)).astype(np.int32)
    large = np.minimum(large, REL_BUCKETS - 1)
    bucket = np.where(n < max_exact, n, large)
    return np.where((dist >= 0) & (dist < WINDOW), bucket, -1).astype(np.int32)


def _swa_bias_table(rel_bias):
    idx = jnp.asarray(_bucket_table())
    return pl.pallas_call(
        _bias_kernel,
        grid=(SWA_KV_HEADS, SWA_GROUP),
        in_specs=[
            pl.BlockSpec(memory_space=pltpu.SMEM),
            pl.BlockSpec((2 * WINDOW, WINDOW), lambda kv, g: (0, 0)),
        ],
        out_specs=pl.BlockSpec((1, 2 * WINDOW, WINDOW), lambda kv, g: (kv, 0, g)),
        out_shape=jax.ShapeDtypeStruct((SWA_KV_HEADS, 2 * WINDOW, SWA_GROUP * WINDOW), F32),
        name="swa_bias_table",
    )(rel_bias.reshape(-1), idx)


QK_NORM_ROWS = ((ROW_FQ, ROW_FV), (ROW_SQ, ROW_MO), (ROW_SK, ROW_SV))


def _inproj_kernel(x_ref, g_ref, wt_ref, wf_ref, hs_ref, tok_ref, gates_ref, feat_ref, small_ref):
    x = x_ref[...]
    ms = jnp.mean(x * x, axis=-1, keepdims=True)
    xn = (x * lax.rsqrt(ms + EPS) * g_ref[...]).astype(BF16)
    for out_ref, base, width in ((tok_ref, COL_CU, COL_GATES), (gates_ref, COL_GATES, N_TOK - COL_GATES)):
        for lo in range(0, width, TN_IN):
            hi = min(lo + TN_IN, width)
            out_ref[:, lo:hi] = _dot(xn, wt_ref[:, base + lo:base + hi]).astype(BF16)
    n_chunks = N_FEAT // FEAT_CHUNK
    for c in range(n_chunks):
        lo = c * FEAT_CHUNK
        hi = lo + FEAT_CHUNK + (N_SMALL if c == n_chunks - 1 else 0)
        r = _dot_nt(wf_ref[lo:hi, :], xn)
        for row in range(lo, lo + FEAT_CHUNK, HEAD_DIM):
            head = r[row - lo:row - lo + HEAD_DIM]
            scale = hs_ref[row:row + HEAD_DIM, :]
            if any(a <= row < b for a, b in QK_NORM_ROWS):
                head = _rms_rows(head, scale)
            elif ROW_MQ <= row < ROW_MK:
                head = head * scale
            feat_ref[row:row + HEAD_DIM, :] = head.astype(BF16)
        if c == n_chunks - 1:
            small_ref[...] = r[FEAT_CHUNK:]


def _resident(shape, layer=None):
    if layer is None:
        return pl.BlockSpec(shape, lambda *_: (0,) * len(shape), pipeline_mode=pl.Buffered(1))
    return pl.BlockSpec((None,) + tuple(shape), lambda *_: (layer,) + (0,) * len(shape),
                        pipeline_mode=pl.Buffered(1))


def _inproj(x2, gain, w_tok, w_feat_t, head_scale, layer):
    t = x2.shape[0]
    tm = min(TM_IN, t)
    return pl.pallas_call(
        _inproj_kernel,
        grid=(t // tm,),
        in_specs=[
            pl.BlockSpec((tm, D_MODEL), lambda i: (i, 0)),
            _resident((1, D_MODEL), layer),
            _resident((D_MODEL, N_TOK), layer),
            _resident((N_FEAT + N_SMALL, D_MODEL), layer),
            _resident((N_FEAT, 1), layer),
        ],
        out_specs=[
            pl.BlockSpec((tm, COL_GATES), lambda i: (i, 0)),
            pl.BlockSpec((tm, N_TOK - COL_GATES), lambda i: (i, 0)),
            pl.BlockSpec((N_FEAT, tm), lambda i: (0, i)),
            pl.BlockSpec((N_SMALL, tm), lambda i: (0, i)),
        ],
        out_shape=[
            jax.ShapeDtypeStruct((t, COL_GATES), BF16),
            jax.ShapeDtypeStruct((t, N_TOK - COL_GATES), BF16),
            jax.ShapeDtypeStruct((N_FEAT, t), BF16),
            jax.ShapeDtypeStruct((N_SMALL, t), F32),
        ],
        compiler_params=pltpu.CompilerParams(
            dimension_semantics=("parallel",), vmem_limit_bytes=VMEM_LIMIT),
        name="inproj",
    )(x2, gain, w_tok, w_feat_t, head_scale)


def _fox_kernel(fq_ref, fk_ref, fv_ref, small_ref, fb_ref, out_ref,
                qa_ref, ka_ref, s_ref, p_ref, acc_ref, m_ref, l_ref, a_ref, *, seq, tq, tk):
    n_units = qa_ref.shape[0]
    c_all = _cumsum_lanes(_log_sigmoid(small_ref[...] + fb_ref[...]), segment=seq)
    for u in range(n_units):
        e, h = divmod(u, FOX_HEADS)
        sl = slice(h * HEAD_DIM, (h + 1) * HEAD_DIM)
        cols = slice(e * seq, (e + 1) * seq)
        c = c_all[h:h + 1, cols] * LOG2E
        hi, mid, lo = _split3(c)
        one = jnp.ones_like(c)
        qa_ref[u, 0:HEAD_DIM, :] = fq_ref[sl, cols]
        qa_ref[u, HEAD_DIM:, :] = _row_select([hi, mid, lo, one, one, one], HEAD_DIM, seq).astype(BF16)
        k_aug = _row_select([one, one, one, -hi, -mid, -lo], HEAD_DIM, seq)
        ka_ref[u] = jnp.concatenate([fk_ref[sl, cols].astype(F32), k_aug], axis=0).T.astype(BF16)
    refs = (fv_ref, out_ref, qa_ref, ka_ref, s_ref, p_ref, acc_ref, m_ref, l_ref, a_ref)
    pl.loop(0, seq // (FOX_TILES * tq))(lambda i: _fox_tile_group(i, *refs, seq=seq, tq=tq, tk=tk))


def _fox_tile_group(i,fv_ref, out_ref, qa_ref, ka_ref, s_ref, p_ref, acc_ref, m_ref, l_ref, a_ref, *, seq, tq, tk):
    n = FOX_TILES
    n_units = qa_ref.shape[0]
    chains = range(n * n_units)
    unit = lambda c: c // n
    from_tile = lambda r: [c for c in chains if c % n >= r]
    rows = lambda c: slice((unit(c) % FOX_HEADS) * HEAD_DIM, (unit(c) % FOX_HEADS + 1) * HEAD_DIM)
    base = lambda c: (unit(c) // FOX_HEADS) * seq
    q_off = lambda c: pl.multiple_of((n * i + c % n) * tq, tq)
    qa = [qa_ref[unit(c), :, pl.ds(q_off(c), tq)] for c in chains]

    m_ref[...] = jnp.full(m_ref.shape, NEG, F32)
    l_ref[...] = jnp.zeros_like(l_ref)
    acc_ref[...] = jnp.zeros_like(acc_ref)
    a_ref[...] = jnp.ones_like(a_ref)
    p_ref[...] = jnp.zeros_like(p_ref)

    def qk(j, slot, which):
        off = pl.multiple_of(j * tk, tk)
        for c in which:
            s_ref[slot, c] = _dot(ka_ref[unit(c), pl.ds(off, tk), :], qa[c])

    def pv(j, which):
        off = pl.multiple_of(j * tk, tk)
        for c in which:
            v = fv_ref[rows(c), pl.ds(pl.multiple_of(base(c) + off, tk), tk)]
            acc_ref[c] = a_ref[c] * acc_ref[c] + _dot(v, p_ref[c])

    def softmax(slot, which, own_block=()):
        for c in which:
            s = s_ref[slot, c]
            if c in own_block:
                key = lax.broadcasted_iota(jnp.int32, s.shape, 0)
                qry = lax.broadcasted_iota(jnp.int32, s.shape, 1)
                s = jnp.where(key <= qry, s, NEG)
            m = m_ref[c]
            m_new = jnp.maximum(m, jnp.max(s, axis=0, keepdims=True))
            p = jnp.exp2(s - m_new)
            alpha = jnp.exp2(m - m_new)
            l_ref[c] = alpha * l_ref[c] + jnp.sum(p, axis=0, keepdims=True)
            m_ref[c] = m_new
            a_ref[c] = alpha
            p_ref[c] = p.astype(BF16)

    def stage(j, slot):
        qk(j + 1, 1 - slot, chains)
        pv(jnp.maximum(j - 1, 0), chains)
        softmax(slot, chains)

    qk(0, 0, chains)
    first_own = n * i

    @pl.loop(0, first_own // 2)
    def _(k):
        stage(2 * k, 0)
        stage(2 * k + 1, 1)

    for r in range(n):
        if r + 1 < n:
            qk(first_own + r + 1, (r + 1) % 2, from_tile(r + 1))
        pv(jnp.maximum(first_own + r - 1, 0), from_tile(max(r - 1, 0)))
        softmax(r % 2, from_tile(r), own_block=[c for c in chains if c % n == r])
    pv(first_own + n - 1, from_tile(n - 1))

    per_tile = Y_TILE // tq
    for c in chains:
        tile = base(c) // Y_TILE + (n // per_tile) * i + (c % n) // per_tile
        lane = ((c % n) % per_tile) * tq
        out_ref[tile, rows(c), lane:lane + tq] = (acc_ref[c] / l_ref[c]).astype(BF16)


def _fox(feat, small, f_bias_col, batch, seq):
    tq = min(TQ_FOX, seq)
    tk = min(TK_FOX, tq)
    assert tq == tk and FOX_TILES % 2 == 0 and seq % (FOX_TILES * tq) == 0, \
        "query tiles run in even groups; a tile's own block is one key block"
    group = FOX_GROUP if batch % FOX_GROUP == 0 else 1
    span = group * seq
    units = group * FOX_HEADS
    chains = FOX_TILES * units
    t = batch * seq
    width = FOX_HEADS * HEAD_DIM
    blk = lambda row: pl.BlockSpec((width, span), lambda b: (row // width, b))
    col = lambda n: pl.BlockSpec((n, 1), lambda b: (0, 0))
    return pl.pallas_call(
        functools.partial(_fox_kernel, seq=seq, tq=tq, tk=tk),
        grid=(batch // group,),
        in_specs=[blk(ROW_FQ), blk(ROW_FK), blk(ROW_FV),
                  pl.BlockSpec((N_SMALL, span), lambda b: (0, b)),
                  col(N_SMALL)],
        out_specs=pl.BlockSpec((span // Y_TILE, width, Y_TILE), lambda b: (b, 0, 0)),
        out_shape=jax.ShapeDtypeStruct((t // Y_TILE, width, Y_TILE), BF16),
        scratch_shapes=[pltpu.VMEM((units, 2 * HEAD_DIM, seq), BF16),
                        pltpu.VMEM((units, seq, 2 * HEAD_DIM), BF16),
                        pltpu.VMEM((2, chains, tk, tq), F32),
                        pltpu.VMEM((chains, tk, tq), BF16),
                        pltpu.VMEM((chains, HEAD_DIM, tq), F32),
                        pltpu.VMEM((chains, 1, tq), F32),
                        pltpu.VMEM((chains, 1, tq), F32),
                        pltpu.VMEM((chains, 1, tq), F32)],
        compiler_params=pltpu.CompilerParams(
            dimension_semantics=("parallel",), vmem_limit_bytes=VMEM_LIMIT),
        name="fox_attention",
    )(feat, feat, feat, small, f_bias_col)


def _short_conv_tile(ct_ref, w_ref, carry_ref, sequence_start):
    width = w_ref.shape[1]
    u = ct_ref[:, COL_CU:COL_CU + width].astype(F32)
    b_gate = ct_ref[:, COL_CB:COL_CB + width].astype(F32)
    c_gate = ct_ref[:, COL_CC:COL_CC + width].astype(F32)
    z = c_gate * u
    prev = jnp.where(sequence_start, 0.0, carry_ref[...])
    carry_ref[...] = z[z.shape[0] - 8:, :]
    last1, last2 = prev[7:8, :], prev[6:7, :]
    row = lax.broadcasted_iota(jnp.int32, z.shape, 0)
    z1 = jnp.where(row >= 1, pltpu.roll(z, 1, 0), last1)
    z2 = jnp.where(row >= 2, pltpu.roll(z, 2, 0), jnp.where(row == 1, last1, last2))
    w = w_ref[...]
    return (b_gate * (w[0:1, :] * z2 + w[1:2, :] * z1 + w[2:3, :] * z)).astype(BF16)


def _mlstm_kernel(q_ref, k_ref, v_ref, o_ref, small_ref, bias_ref, gain_ref, out_ref,
                  c_ref, m_ref, b_ref, u_ref, ut_ref, *, seq, chunk):
    c_ref[...] = jnp.zeros_like(c_ref)
    m_ref[...] = jnp.zeros_like(m_ref)
    g = small_ref[...] + bias_ref[...]
    b_all = _cumsum_lanes(_log_sigmoid(g), segment=chunk)
    b_ref[...] = b_all
    u_all = _row_select([g[4 + h:5 + h, :] - b_all[8 + h:9 + h, :] for h in range(MLSTM_HEADS)],
                        8, g.shape[1])
    u_ref[...] = u_all
    ut_ref[...] = jnp.concatenate([u_all, jnp.zeros((120, g.shape[1]), F32)], axis=0).T
    refs = (q_ref, k_ref, v_ref, o_ref, gain_ref, out_ref, c_ref, m_ref, b_ref, u_ref, ut_ref)
    pl.loop(0, seq // chunk)(lambda c: _mlstm_chunk(c, *refs, seq=seq, chunk=chunk))


def _mlstm_chunk(c, q_ref, k_ref, v_ref, o_ref, gain_ref, out_ref, c_ref, m_ref, b_ref, u_ref, ut_ref, *,
                 seq, chunk):
    units = range(c_ref.shape[0])
    head = lambda u: u % MLSTM_HEADS
    sl = lambda u: slice(head(u) * HEAD_DIM, (head(u) + 1) * HEAD_DIM)
    cols = lambda u: pl.ds(pl.multiple_of((u // MLSTM_HEADS) * seq + c * chunk, chunk), chunk)

    src = lax.broadcasted_iota(jnp.int32, (chunk, chunk), 0)
    tgt = lax.broadcasted_iota(jnp.int32, (chunk, chunk), 1)
    causal = src <= tgt
    ones_rows = (lax.broadcasted_iota(jnp.int32, (HEAD_DIM, chunk), 0) == 0).astype(BF16)

    def first_matmuls(u):
        qs = q_ref[sl(u), cols(u)]
        ks = k_ref[sl(u), cols(u)]
        c_prev = c_ref[u]
        return dict(ks=ks, c_prev=c_prev, scores=_dot_tn(ks, qs),
                    carried=_dot(c_prev.astype(BF16), qs))

    def gate_arithmetic(u, st):
        h = head(u)
        b_row = b_ref[8 + h:9 + h, cols(u)]
        b_last = b_row[:, chunk - 1:chunk]
        u_row = u_ref[h:h + 1, cols(u)]
        u_col = ut_ref[cols(u), h:h + 1]
        m_prev = m_ref[u][0:1, 0:1]
        dmat = jnp.where(causal, b_row + u_col, NEG)
        g_row = b_row + m_prev
        m_u = jnp.maximum(g_row, jnp.max(dmat, axis=0, keepdims=True))
        st.update(sw=(st["scores"] * jnp.exp(dmat - m_u)).astype(BF16), m_t=m_u, inter=jnp.exp(g_row - m_u))
        m_loc = jnp.max(u_row + b_last, axis=1, keepdims=True)
        m_new = jnp.maximum(b_last + m_prev, m_loc)
        w_row = jnp.exp(u_row + b_last - m_new)
        st.update(decay=jnp.exp(b_last + m_prev - m_new),
                  kw=(st["ks"].astype(F32) * w_row).astype(BF16))
        m_ref[u] = jnp.broadcast_to(m_new, m_ref.shape[1:])

    def second_matmuls(u, st):
        v_aug = jnp.concatenate([v_ref[sl(u), cols(u)], ones_rows], axis=0)
        st.update(intra=_dot(v_aug, st["sw"]), update=_dot_nt(v_aug, st["kw"]))

    def epilogue(u, st):
        c_ref[u] = st["decay"] * st["c_prev"] + st["update"]
        tot = st["intra"] + st["inter"] * st["carried"]
        num = tot[0:HEAD_DIM]
        den = tot[HEAD_DIM:HEAD_DIM + 1]
        ht = num / jnp.maximum(jnp.abs(den), jnp.exp(-st["m_t"]))
        hn = _rms_rows(ht, gain_ref[sl(u), :])
        per_tile = Y_TILE // chunk
        tile = (u // MLSTM_HEADS) * (seq // Y_TILE) + c // per_tile
        lane = pl.ds(pl.multiple_of(lax.rem(c, per_tile) * chunk, chunk), chunk)
        out_ref[tile, sl(u), lane] = (_sigmoid(o_ref[sl(u), cols(u)].astype(F32)) * hn).astype(BF16)

    n = len(units)
    ahead = 2
    state = {u: first_matmuls(u) for u in range(min(ahead, n))}
    for u in units:
        gate_arithmetic(u, state[u])
        if u + ahead < n:
            state[u + ahead] = first_matmuls(u + ahead)
        second_matmuls(u, state[u])
        if u >= 1:
            epilogue(u - 1, state.pop(u - 1))
    epilogue(n - 1, state.pop(n - 1))


def _mlstm(feat, small, bias_col, gain_col, batch, seq):
    chunk = min(L_MLSTM, seq)
    group = MLSTM_GROUP if batch % MLSTM_GROUP == 0 else 1
    span = group * seq
    units = group * MLSTM_HEADS
    t = batch * seq
    width = MLSTM_HEADS * HEAD_DIM
    blk = lambda row: pl.BlockSpec((width, span), lambda b: (row // width, b))
    return pl.pallas_call(
        functools.partial(_mlstm_kernel, seq=seq, chunk=chunk),
        grid=(batch // group,),
        in_specs=[blk(ROW_MQ), blk(ROW_MK), blk(ROW_MV), blk(ROW_MO),
                  pl.BlockSpec((N_SMALL, span), lambda b: (0, b)),
                  pl.BlockSpec((N_SMALL, 1), lambda b: (0, 0)),
                  pl.BlockSpec((width, 1), lambda b: (0, 0))],
        out_specs=pl.BlockSpec((span // Y_TILE, width, Y_TILE), lambda b: (b, 0, 0)),
        out_shape=jax.ShapeDtypeStruct((t // Y_TILE, width, Y_TILE), BF16),
        scratch_shapes=[pltpu.VMEM((units, 2 * HEAD_DIM, HEAD_DIM), F32),
                        pltpu.VMEM((units, 8, 128), F32),
                        pltpu.VMEM((N_SMALL, span), F32),
                        pltpu.VMEM((8, span), F32),
                        pltpu.VMEM((span, 128), F32)],
        compiler_params=pltpu.CompilerParams(
            dimension_semantics=("parallel",), vmem_limit_bytes=VMEM_LIMIT),
        name="mlstm",
    )(feat, feat, feat, feat, small, bias_col, gain_col)


def _swa_kernel(q_ref, k_ref, v_ref, bias_ref, sink_ref, out_ref, *, nblk):
    refs = (q_ref, k_ref, v_ref, bias_ref, sink_ref, out_ref)
    pl.loop(0, q_ref.shape[1] // (nblk * WINDOW))(lambda n: _swa_span(n, *refs, nblk=nblk))


def _swa_span(n, q_ref, k_ref, v_ref, bias_ref, sink_ref, out_ref, *, nblk):
    w = WINDOW
    off = pl.multiple_of(n * (nblk * w), nblk * w)
    cur = pl.ds(off, nblk * w)
    prev = pl.ds(pl.multiple_of(jnp.maximum(off - w, 0), w), w)
    key = lax.broadcasted_iota(jnp.int32, (2 * w, SWA_GROUP * w), 0)
    first_block_pad = jnp.logical_and(n == 0, key < w)
    kvs = range(SWA_KV_HEADS)
    ksl = [slice(kv * HEAD_DIM, (kv + 1) * HEAD_DIM) for kv in kvs]
    kt = [jnp.concatenate([k_ref[s, prev], k_ref[s, cur]], axis=1) for s in ksl]
    vt = [jnp.concatenate([v_ref[s, prev], v_ref[s, cur]], axis=1) for s in ksl]

    bands = [(blk, kv) for blk in range(nblk) for kv in kvs]

    def head_rows(kv, g):
        return slice((kv * SWA_GROUP + g) * HEAD_DIM, (kv * SWA_GROUP + g + 1) * HEAD_DIM)

    def score(blk, kv):
        qt = jnp.concatenate([q_ref[head_rows(kv, g), pl.ds(off + blk * w, w)] for g in range(SWA_GROUP)], axis=1)
        s = _dot_tn(kt[kv][:, blk * w:(blk + 2) * w], qt) + bias_ref[kv]
        return jnp.where(first_block_pad, NEG, s) if blk == 0 else s

    def softmax(s, kv):
        sink = sink_ref[kv]
        m = jnp.maximum(jnp.max(s, axis=0, keepdims=True), sink)
        p = jnp.exp2(s - m)
        return p.astype(BF16), jnp.sum(p, axis=0, keepdims=True) + jnp.exp2(sink - m)

    def finish(blk, kv, o, d):
        o = o / d
        for g in range(SWA_GROUP):
            tile = n * (nblk * w // Y_TILE) + blk * w // Y_TILE
            lane = blk * w % Y_TILE
            out_ref[tile, head_rows(kv, g), lane:lane + w] = o[:, g * w:(g + 1) * w].astype(BF16)

    ahead = 2
    scores = [score(*band) for band in bands[:ahead]]
    pending = None
    for k, (blk, kv) in enumerate(bands):
        p, d = softmax(scores[k], kv)
        if k + ahead < len(bands):
            scores.append(score(*bands[k + ahead]))
        o = _dot(vt[kv][:, blk * w:(blk + 2) * w], p)
        if pending is not None:
            finish(*pending)
        pending = (blk, kv, o, d)
    finish(*pending)


def _swa(feat, bias_t, sink_rows, batch, seq):
    nblk = min(SWA_BLOCKS, seq // WINDOW)
    t = batch * seq
    qw = SWA_Q_HEADS * HEAD_DIM
    kw = SWA_KV_HEADS * HEAD_DIM
    return pl.pallas_call(
        functools.partial(_swa_kernel, nblk=nblk),
        grid=(batch,),
        in_specs=[pl.BlockSpec((qw, seq), lambda b: (ROW_SQ // qw, b)),
                  pl.BlockSpec((kw, seq), lambda b: (ROW_SK // kw, b)),
                  pl.BlockSpec((kw, seq), lambda b: (ROW_SV // kw, b)),
                  pl.BlockSpec(bias_t.shape, lambda b: (0, 0, 0)),
                  pl.BlockSpec(sink_rows.shape, lambda b: (0, 0, 0))],
        out_specs=pl.BlockSpec((seq // Y_TILE, qw, Y_TILE), lambda b: (b, 0, 0)),
        out_shape=jax.ShapeDtypeStruct((t // Y_TILE, qw, Y_TILE), BF16),
        compiler_params=pltpu.CompilerParams(
            dimension_semantics=("parallel",), vmem_limit_bytes=VMEM_LIMIT),
        name="swa_attention",
    )(feat, feat, feat, bias_t, sink_rows)


def _merge_ffn_kernel(x_ref, yf_ref, ct_ref, ym_ref, ys_ref, g_ref, cw_ref,
                      wf_ref, wc_ref, wm_ref, ws_ref, wo_ref, gain_ref, wg_ref, wu_ref, wd_ref,
                      out_ref, carry_ref, *, tiles_per_seq):
    def gated(branch, y):
        g = g_ref[:, branch * D_MODEL:(branch + 1) * D_MODEL]
        return (1.0 + jnp.tanh(g.astype(F32))) * y

    y_conv = _short_conv_tile(ct_ref, cw_ref, carry_ref, lax.rem(pl.program_id(0), tiles_per_seq) == 0)
    merged = gated(0, _dot_tn(yf_ref[...], wf_ref[...]))
    merged += gated(1, _dot(y_conv, wc_ref[...]))
    merged += gated(2, _dot_tn(ym_ref[...], wm_ref[...]))
    merged += gated(3, _dot_tn(ys_ref[...], ws_ref[...]))
    x = x_ref[...] + _dot(merged.astype(BF16), wo_ref[...])

    ms = jnp.mean(x * x, axis=-1, keepdims=True)
    hn = (x * lax.rsqrt(ms + EPS) * gain_ref[...]).astype(BF16)
    acc = x
    bounds = list(range(0, D_FF, TF_FFN)) + [D_FF]
    for lo, hi in zip(bounds[:-1], bounds[1:]):
        h = _dot(hn, wg_ref[:, lo:hi])
        act = (h * (1.0 + jnp.tanh(h)) * _dot(hn, wu_ref[:, lo:hi])).astype(BF16)
        acc = acc + _dot(act, wd_ref[lo:hi, :])
    out_ref[...] = acc


def _merge_ffn(x2, y_fox_t, conv_tok, y_mlstm_t, y_swa_t, gates, conv_w, w_fox, w_conv, w_mlstm, w_swa,
               w_out, gain, w_gate_half, w_up, w_down, layer, seq):
    t = x2.shape[0]
    tm = min(TM_MERGE, t, seq)
    row_blk = lambda width: pl.BlockSpec((tm, width), lambda i: (i, 0))
    assert tm == Y_TILE, "the mixers write one contiguous (features, tokens) slab per merge row tile"
    feat_blk = lambda a: pl.BlockSpec((None,) + a.shape[1:], lambda i: (i, 0, 0))
    full = lambda a: _resident(a.shape[1:], layer)
    return pl.pallas_call(
        functools.partial(_merge_ffn_kernel, tiles_per_seq=seq // tm),
        grid=(t // tm,),
        in_specs=[row_blk(D_MODEL),
                  feat_blk(y_fox_t),
                  row_blk(conv_tok.shape[1]),
                  feat_blk(y_mlstm_t), feat_blk(y_swa_t),
                  row_blk(gates.shape[1]),
                  full(conv_w),
                  full(w_fox), full(w_conv), full(w_mlstm), full(w_swa), full(w_out),
                  full(gain), full(w_gate_half), full(w_up), full(w_down)],
        out_specs=row_blk(D_MODEL),
        out_shape=jax.ShapeDtypeStruct((t, D_MODEL), F32),
        scratch_shapes=[pltpu.VMEM((8, conv_w.shape[2]), F32)],
        compiler_params=pltpu.CompilerParams(
            dimension_semantics=("arbitrary",), vmem_limit_bytes=VMEM_LIMIT),
        name="merge_ffn",
    )(x2, y_fox_t, conv_tok, y_mlstm_t, y_swa_t, gates, conv_w,
      w_fox, w_conv, w_mlstm, w_swa, w_out, gain, w_gate_half, w_up, w_down)


IN_FQKV, IN_FF, IN_CONV, IN_MQ, IN_MK, IN_MV = 0, 768, 772, 1540, 1796, 2052
IN_MI, IN_MF, IN_MO, IN_SQ, IN_SK, IN_SV, IN_GATES, IN_COLS = 2308, 2312, 2316, 2572, 3084, 3212, 3340, 7436
TK_PREP = 256


def _wprep_kernel(wt_ref, tok_ref, feat_ref):
    tok_ref[:, COL_CU:COL_GATES] = wt_ref[IN_CONV:IN_MQ, :].T.astype(BF16)
    for lo in range(0, N_TOK - COL_GATES, TN_IN):
        gates = wt_ref[IN_GATES + lo:IN_GATES + lo + TN_IN, :]
        tok_ref[:, COL_GATES + lo:COL_GATES + lo + TN_IN] = (0.5 * gates).T.astype(BF16)
    for row, lo, hi in ((ROW_FQ, IN_FQKV, IN_FF), (ROW_MQ, IN_MQ, IN_MI), (ROW_SQ, IN_SQ, IN_SK),
                        (ROW_MO, IN_MO, IN_SQ), (ROW_SK, IN_SK, IN_GATES)):
        feat_ref[row:row + hi - lo, :] = wt_ref[lo:hi, :].astype(BF16)
    fox_f = wt_ref[IN_FF:IN_FF + N_SMALL, :]
    mlstm_if = wt_ref[IN_MI - 4:IN_MI - 4 + N_SMALL, :]
    row = lax.broadcasted_iota(jnp.int32, fox_f.shape, 0)
    small = jnp.where(row < 4, fox_f, jnp.where(row < 12, mlstm_if, 0.0))
    feat_ref[N_FEAT:, :] = small.astype(BF16)


def _prep_w_in(w_in):
    depth, d, cols = w_in.shape
    return pl.pallas_call(
        _wprep_kernel,
        grid=(depth, d // TK_PREP),
        in_specs=[pl.BlockSpec((None, cols, TK_PREP), lambda l, j: (l, 0, j))],
        out_specs=[pl.BlockSpec((None, TK_PREP, N_TOK), lambda l, j: (l, j, 0)),
                   pl.BlockSpec((None, N_FEAT + N_SMALL, TK_PREP), lambda l, j: (l, 0, j))],
        out_shape=[jax.ShapeDtypeStruct((depth, d, N_TOK), BF16),
                   jax.ShapeDtypeStruct((depth, N_FEAT + N_SMALL, d), BF16)],
        compiler_params=pltpu.CompilerParams(
            dimension_semantics=("parallel", "parallel"), vmem_limit_bytes=VMEM_LIMIT),
        name="w_in_relayout",
    )(jnp.transpose(w_in, (0, 2, 1)))


def _head_scale(depth, fox_q_gain, fox_k_gain, swa_q_gain, swa_k_gain):
    def tiled(gain, heads, factor=1.0):
        return jnp.tile(gain.astype(F32) * factor, (1, heads))
    ones = lambda n: jnp.ones((depth, n), F32)
    cols = [(ROW_FQ, tiled(fox_q_gain, FOX_HEADS, QK_SCALE * LOG2E)), (ROW_FK, tiled(fox_k_gain, FOX_HEADS)),
            (ROW_FV, ones(ROW_MQ - ROW_FV)), (ROW_MQ, QK_SCALE * ones(ROW_MK - ROW_MQ)),
            (ROW_MK, ones(ROW_SQ - ROW_MK)), (ROW_SQ, tiled(swa_q_gain, SWA_Q_HEADS, QK_SCALE * LOG2E)),
            (ROW_MO, ones(ROW_SK - ROW_MO)), (ROW_SK, tiled(swa_k_gain, SWA_KV_HEADS)),
            (ROW_SV, ones(N_FEAT - ROW_SV))]
    assert [r for r, _ in cols] == sorted(r for r, _ in cols)
    return jnp.concatenate([c for _, c in cols], axis=1)[:, :, None]


def _col(v, n=None):
    v = v.astype(F32).reshape(-1, 1)
    if n is not None and v.shape[0] < n:
        v = jnp.concatenate([v, jnp.zeros((n - v.shape[0], 1), F32)], axis=0)
    return v


def kernel(x, rel_bias, attn_norm, w_in, fox_f_bias, fox_q_gain, fox_k_gain, conv_w, mlstm_i_bias, mlstm_f_bias, mlstm_h_gain, swa_q_gain, swa_k_gain, swa_sinks, w_fox_out, w_conv_out, w_mlstm_out, w_swa_out, w_merge_out, ffn_norm, w_gate, w_up, w_down):
    batch, seq, _ = x.shape
    depth = w_in.shape[0]
    x2 = x.reshape(batch * seq, D_MODEL)
    bias_t = _swa_bias_table(rel_bias)

    w_tok, w_feat_t = _prep_w_in(w_in)
    attn_gain = attn_norm.reshape(depth, 1, D_MODEL)
    ffn_gain = ffn_norm.reshape(depth, 1, D_MODEL)
    w_fox_b, w_conv_b, w_mlstm_b, w_swa_b = (w.astype(BF16) for w in (w_fox_out, w_conv_out, w_mlstm_out, w_swa_out))
    w_merge_half = (0.5 * w_merge_out).astype(BF16)
    w_gate_half = (0.5 * w_gate).astype(BF16)
    w_up_b = w_up.astype(BF16)
    w_down_b = w_down.astype(BF16)

    head_scale = _head_scale(depth, fox_q_gain, fox_k_gain, swa_q_gain, swa_k_gain)

    for l in range(depth):
        tok, gates, feat, small = _inproj(x2, attn_gain, w_tok, w_feat_t, head_scale, l)

        y_fox_t = _fox(feat, small, _col(fox_f_bias[l], N_SMALL), batch, seq)
        gate_bias = jnp.concatenate([jnp.zeros((4,), F32), mlstm_i_bias[l], mlstm_f_bias[l]])
        y_mlstm_t = _mlstm(feat, small, _col(gate_bias, N_SMALL), _col(mlstm_h_gain[l]), batch, seq)
        sink_rows = jnp.broadcast_to(
            (swa_sinks[l].astype(F32) * LOG2E).reshape(SWA_KV_HEADS, 1, SWA_GROUP, 1),
            (SWA_KV_HEADS, 1, SWA_GROUP, WINDOW)).reshape(SWA_KV_HEADS, 1, SWA_GROUP * WINDOW)
        y_swa_t = _swa(feat, bias_t, sink_rows, batch, seq)

        x2 = _merge_ffn(x2, y_fox_t, tok, y_mlstm_t, y_swa_t, gates, conv_w,
                        w_fox_b, w_conv_b, w_mlstm_b, w_swa_b, w_merge_half,
                        ffn_gain, w_gate_half, w_up_b, w_down_b, l, seq)
    return x2.reshape(batch, seq, D_MODEL)
```

```python
import functools

import numpy as np
import jax
import jax.numpy as jnp
from jax import lax
from jax.experimental import pallas as pl
from jax.experimental.pallas import tpu as pltpu

F32 = jnp.float32
BF16 = jnp.bfloat16

D_MODEL = 1024
HEAD_DIM = 64
FOX_HEADS = 4
MLSTM_HEADS = 4
SWA_Q_HEADS = 8
SWA_KV_HEADS = 2
SWA_GROUP = SWA_Q_HEADS // SWA_KV_HEADS
WINDOW = 128
REL_BUCKETS = 32
REL_MAX_DIST = 128
D_FF = 2816
EPS = 1e-6
NEG = -1e30
QK_SCALE = HEAD_DIM ** -0.5
LOG2E = 1.4426950408889634

ROW_FQ, ROW_FK, ROW_FV = 0, 256, 512
ROW_MQ, ROW_MK, ROW_MV = 768, 1024, 1280
ROW_SQ, ROW_MO, ROW_SK, ROW_SV = 1536, 2048, 2304, 2432
N_FEAT = 2560
N_SMALL = 16
COL_CU, COL_CB, COL_CC, COL_GATES = 0, 256, 512, 768
N_TOK = 4864

TM_IN = 512
TN_IN = 512
FEAT_CHUNK = 640
FOX_GROUP = 2
FOX_TILES = 2
TQ_FOX = 256
TK_FOX = 256
L_MLSTM = 256
MLSTM_GROUP = 2
SWA_BLOCKS = 16
TM_MERGE = 512
Y_TILE = TM_MERGE
TF_FFN = 512
V7X_VMEM_BYTES = 64 * 1024 * 1024
VMEM_LIMIT = V7X_VMEM_BYTES * 7 // 8


def _dot(a, b):
    return jnp.dot(a, b, preferred_element_type=F32)


def _dot_nt(a, b):
    return lax.dot_general(a, b, (((1,), (1,)), ((), ())), preferred_element_type=F32)


def _dot_tn(a, b):
    return lax.dot_general(a, b, (((0,), (0,)), ((), ())), preferred_element_type=F32)


def _sigmoid(x):
    return 0.5 * jnp.tanh(0.5 * x) + 0.5


def _log_sigmoid(x):
    return jnp.minimum(x, 0.0) - jnp.log(1.0 + jnp.exp(-jnp.abs(x)))


def _cumsum_lanes(x, segment=None):
    n = segment or x.shape[-1]
    lane = lax.broadcasted_iota(jnp.int32, x.shape, x.ndim - 1) & (n - 1)
    k = 1
    while k < n:
        x = x + jnp.where(lane >= k, pltpu.roll(x, k, x.ndim - 1), 0.0)
        k *= 2
    return x


def _row_select(rows, n_rows, width):
    rid = lax.broadcasted_iota(jnp.int32, (n_rows, width), 0)
    out = jnp.zeros((n_rows, width), F32)
    for r, v in enumerate(rows):
        out = jnp.where(rid == r, v, out)
    return out


def _split3(c):
    hi = c.astype(BF16).astype(F32)
    r = c - hi
    mid = r.astype(BF16).astype(F32)
    lo = (r - mid).astype(BF16).astype(F32)
    return hi, mid, lo


def _rms_rows(xt, gain_col):
    ms = jnp.mean(xt * xt, axis=0, keepdims=True)
    return xt * lax.rsqrt(ms + EPS) * gain_col


def _bias_kernel(rb_ref, idx_ref, out_ref):
    kv = pl.program_id(0)
    g = pl.program_id(1)
    head = kv * SWA_GROUP + g
    idx = idx_ref[...]
    acc = jnp.full(idx.shape, NEG, F32)
    for b in range(REL_BUCKETS):
        acc = jnp.where(idx == b, rb_ref[b * SWA_Q_HEADS + head] * LOG2E, acc)
    out_ref[0] = acc


def _bucket_table():
    j = np.arange(2 * WINDOW)[:, None]
    i = np.arange(WINDOW)[None, :]
    dist = i + WINDOW - j
    n = np.maximum(dist, 0).astype(np.int32)
    max_exact = REL_BUCKETS // 2
    nf = np.maximum(n, 1).astype(np.float32)
    large = max_exact + (np.log(nf / np.float32(max_exact)) / np.float32(np.log(REL_MAX_DIST / max_exact))
                         * np.float32(REL_BUCKETS - max_exact)).astype(np.int32)
    large = np.minimum(large, REL_BUCKETS - 1)
    bucket = np.where(n < max_exact, n, large)
    return np.where((dist >= 0) & (dist < WINDOW), bucket, -1).astype(np.int32)


def _swa_bias_table(rel_bias):
    idx = jnp.asarray(_bucket_table())
    return pl.pallas_call(
        _bias_kernel,
        grid=(SWA_KV_HEADS, SWA_GROUP),
        in_specs=[
            pl.BlockSpec(memory_space=pltpu.SMEM),
            pl.BlockSpec((2 * WINDOW, WINDOW), lambda kv, g: (0, 0)),
        ],
        out_specs=pl.BlockSpec((1, 2 * WINDOW, WINDOW), lambda kv, g: (kv, 0, g)),
        out_shape=jax.ShapeDtypeStruct((SWA_KV_HEADS, 2 * WINDOW, SWA_GROUP * WINDOW), F32),
        name="swa_bias_table",
    )(rel_bias.reshape(-1), idx)


QK_NORM_ROWS = ((ROW_FQ, ROW_FV), (ROW_SQ, ROW_MO), (ROW_SK, ROW_SV))


def _inproj_kernel(x_ref, g_ref, wt_ref, wf_ref, hs_ref, cw_ref, conv_ref, gates_ref, feat_ref, small_ref,
                   carry_ref, *, tiles_per_seq):
    x = x_ref[...]
    ms = jnp.mean(x * x, axis=-1, keepdims=True)
    xn = (x * lax.rsqrt(ms + EPS) * g_ref[...]).astype(BF16)
    cin = _dot(xn, wt_ref[:, COL_CU:COL_GATES])
    width = cw_ref.shape[1]
    conv_ref[...] = _short_conv_tile(cin[:, COL_CU:COL_CU + width], cin[:, COL_CB:COL_CB + width],
                                     cin[:, COL_CC:COL_CC + width], cw_ref, carry_ref,
                                     lax.rem(pl.program_id(0), tiles_per_seq) == 0)
    for lo in range(0, N_TOK - COL_GATES, TN_IN):
        gates_ref[:, lo:lo + TN_IN] = _dot(xn, wt_ref[:, COL_GATES + lo:COL_GATES + lo + TN_IN]).astype(BF16)
    n_chunks = N_FEAT // FEAT_CHUNK
    for c in range(n_chunks):
        lo = c * FEAT_CHUNK
        hi = lo + FEAT_CHUNK + (N_SMALL if c == n_chunks - 1 else 0)
        r = _dot_nt(wf_ref[lo:hi, :], xn)
        for row in range(lo, lo + FEAT_CHUNK, HEAD_DIM):
            head = r[row - lo:row - lo + HEAD_DIM]
            scale = hs_ref[row:row + HEAD_DIM, :]
            if any(a <= row < b for a, b in QK_NORM_ROWS):
                head = _rms_rows(head, scale)
            elif ROW_MQ <= row < ROW_MK:
                head = head * scale
            feat_ref[row:row + HEAD_DIM, :] = head.astype(BF16)
        if c == n_chunks - 1:
            small_ref[...] = r[FEAT_CHUNK:]


def _resident(shape, layer=None):
    if layer is None:
        return pl.BlockSpec(shape, lambda *_: (0,) * len(shape), pipeline_mode=pl.Buffered(1))
    return pl.BlockSpec((None,) + tuple(shape), lambda *_: (layer,) + (0,) * len(shape),
                        pipeline_mode=pl.Buffered(1))


def _inproj(x2, gain, w_tok, w_feat_t, head_scale, conv_w, layer, seq):
    t = x2.shape[0]
    tm = min(TM_IN, t, seq)
    conv_width = conv_w.shape[2]
    return pl.pallas_call(
        functools.partial(_inproj_kernel, tiles_per_seq=seq // tm),
        grid=(t // tm,),
        in_specs=[
            pl.BlockSpec((tm, D_MODEL), lambda i: (i, 0)),
            _resident((1, D_MODEL), layer),
            _resident((D_MODEL, N_TOK), layer),
            _resident((N_FEAT + N_SMALL, D_MODEL), layer),
            _resident((N_FEAT, 1), layer),
            _resident(conv_w.shape[1:], layer),
        ],
        out_specs=[
            pl.BlockSpec((tm, conv_width), lambda i: (i, 0)),
            pl.BlockSpec((tm, N_TOK - COL_GATES), lambda i: (i, 0)),
            pl.BlockSpec((N_FEAT, tm), lambda i: (0, i)),
            pl.BlockSpec((N_SMALL, tm), lambda i: (0, i)),
        ],
        out_shape=[
            jax.ShapeDtypeStruct((t, conv_width), BF16),
            jax.ShapeDtypeStruct((t, N_TOK - COL_GATES), BF16),
            jax.ShapeDtypeStruct((N_FEAT, t), BF16),
            jax.ShapeDtypeStruct((N_SMALL, t), F32),
        ],
        scratch_shapes=[pltpu.VMEM((8, conv_width), F32)],
        compiler_params=pltpu.CompilerParams(
            dimension_semantics=("arbitrary",), vmem_limit_bytes=VMEM_LIMIT),
        name="inproj",
    )(x2, gain, w_tok, w_feat_t, head_scale, conv_w)


def _fox_kernel(fq_ref, fk_ref, fv_ref, small_ref, fb_ref, out_ref,
                qa_ref, ka_ref, s_ref, p_ref, acc_ref, m_ref, l_ref, a_ref, *, seq, tq, tk):
    n_units = qa_ref.shape[0]
    c_all = _cumsum_lanes(_log_sigmoid(small_ref[...] + fb_ref[...]), segment=seq)
    for u in range(n_units):
        e, h = divmod(u, FOX_HEADS)
        sl = slice(h * HEAD_DIM, (h + 1) * HEAD_DIM)
        cols = slice(e * seq, (e + 1) * seq)
        c = c_all[h:h + 1, cols] * LOG2E
        hi, mid, lo = _split3(c)
        one = jnp.ones_like(c)
        qa_ref[u, 0:HEAD_DIM, :] = fq_ref[sl, cols]
        qa_ref[u, HEAD_DIM:, :] = _row_select([hi, mid, lo, one, one, one], HEAD_DIM, seq).astype(BF16)
        k_aug = _row_select([one, one, one, -hi, -mid, -lo], HEAD_DIM, seq)
        ka_ref[u] = jnp.concatenate([fk_ref[sl, cols].astype(F32), k_aug], axis=0).T.astype(BF16)
    refs = (fv_ref, out_ref, qa_ref, ka_ref, s_ref, p_ref, acc_ref, m_ref, l_ref, a_ref)
    pl.loop(0, seq // (FOX_TILES * tq))(lambda i: _fox_tile_group(i, *refs, seq=seq, tq=tq, tk=tk))


def _fox_tile_group(i,fv_ref, out_ref, qa_ref, ka_ref, s_ref, p_ref, acc_ref, m_ref, l_ref, a_ref, *, seq, tq, tk):
    n = FOX_TILES
    n_units = qa_ref.shape[0]
    chains = range(n * n_units)
    unit = lambda c: c // n
    from_tile = lambda r: [c for c in chains if c % n >= r]
    rows = lambda c: slice((unit(c) % FOX_HEADS) * HEAD_DIM, (unit(c) % FOX_HEADS + 1) * HEAD_DIM)
    base = lambda c: (unit(c) // FOX_HEADS) * seq
    q_off = lambda c: pl.multiple_of((n * i + c % n) * tq, tq)
    qa = [qa_ref[unit(c), :, pl.ds(q_off(c), tq)] for c in chains]

    m_ref[...] = jnp.full(m_ref.shape, NEG, F32)
    l_ref[...] = jnp.zeros_like(l_ref)
    acc_ref[...] = jnp.zeros_like(acc_ref)
    a_ref[...] = jnp.ones_like(a_ref)
    p_ref[...] = jnp.zeros_like(p_ref)

    def qk(j, slot, which):
        off = pl.multiple_of(j * tk, tk)
        for c in which:
            s_ref[slot, c] = _dot(ka_ref[unit(c), pl.ds(off, tk), :], qa[c])

    def pv(j, which):
        off = pl.multiple_of(j * tk, tk)
        for c in which:
            v = fv_ref[rows(c), pl.ds(pl.multiple_of(base(c) + off, tk), tk)]
            acc_ref[c] = a_ref[c] * acc_ref[c] + _dot(v, p_ref[c])

    def softmax(slot, which, own_block=()):
        for c in which:
            s = s_ref[slot, c]
            if c in own_block:
                key = lax.broadcasted_iota(jnp.int32, s.shape, 0)
                qry = lax.broadcasted_iota(jnp.int32, s.shape, 1)
                s = jnp.where(key <= qry, s, NEG)
            m = m_ref[c]
            m_new = jnp.maximum(m, jnp.max(s, axis=0, keepdims=True))
            p = jnp.exp2(s - m_new)
            alpha = jnp.exp2(m - m_new)
            l_ref[c] = alpha * l_ref[c] + jnp.sum(p, axis=0, keepdims=True)
            m_ref[c] = m_new
            a_ref[c] = alpha
            p_ref[c] = p.astype(BF16)

    def stage(j, slot):
        qk(j + 1, 1 - slot, chains)
        pv(jnp.maximum(j - 1, 0), chains)
        softmax(slot, chains)

    qk(0, 0, chains)
    first_own = n * i

    @pl.loop(0, first_own // 2)
    def _(k):
        stage(2 * k, 0)
        stage(2 * k + 1, 1)

    for r in range(n):
        if r + 1 < n:
            qk(first_own + r + 1, (r + 1) % 2, from_tile(r + 1))
        pv(jnp.maximum(first_own + r - 1, 0), from_tile(max(r - 1, 0)))
        softmax(r % 2, from_tile(r), own_block=[c for c in chains if c % n == r])
    pv(first_own + n - 1, from_tile(n - 1))

    per_tile = Y_TILE // tq
    for c in chains:
        tile = base(c) // Y_TILE + (n // per_tile) * i + (c % n) // per_tile
        lane = ((c % n) % per_tile) * tq
        out_ref[tile, rows(c), lane:lane + tq] = (acc_ref[c] / l_ref[c]).astype(BF16)


def _fox(feat, small, f_bias_col, batch, seq):
    tq = min(TQ_FOX, seq)
    tk = min(TK_FOX, tq)
    assert tq == tk and FOX_TILES % 2 == 0 and seq % (FOX_TILES * tq) == 0, \
        "query tiles run in even groups; a tile's own block is one key block"
    group = FOX_GROUP if batch % FOX_GROUP == 0 else 1
    span = group * seq
    units = group * FOX_HEADS
    chains = FOX_TILES * units
    t = batch * seq
    width = FOX_HEADS * HEAD_DIM
    blk = lambda row: pl.BlockSpec((width, span), lambda b: (row // width, b))
    col = lambda n: pl.BlockSpec((n, 1), lambda b: (0, 0))
    return pl.pallas_call(
        functools.partial(_fox_kernel, seq=seq, tq=tq, tk=tk),
        grid=(batch // group,),
        in_specs=[blk(ROW_FQ), blk(ROW_FK), blk(ROW_FV),
                  pl.BlockSpec((N_SMALL, span), lambda b: (0, b)),
                  col(N_SMALL)],
        out_specs=pl.BlockSpec((span // Y_TILE, width, Y_TILE), lambda b: (b, 0, 0)),
        out_shape=jax.ShapeDtypeStruct((t // Y_TILE, width, Y_TILE), BF16),
        scratch_shapes=[pltpu.VMEM((units, 2 * HEAD_DIM, seq), BF16),
                        pltpu.VMEM((units, seq, 2 * HEAD_DIM), BF16),
                        pltpu.VMEM((2, chains, tk, tq), F32),
                        pltpu.VMEM((chains, tk, tq), BF16),
                        pltpu.VMEM((chains, HEAD_DIM, tq), F32),
                        pltpu.VMEM((chains, 1, tq), F32),
                        pltpu.VMEM((chains, 1, tq), F32),
                        pltpu.VMEM((chains, 1, tq), F32)],
        compiler_params=pltpu.CompilerParams(
            dimension_semantics=("parallel",), vmem_limit_bytes=VMEM_LIMIT),
        name="fox_attention",
    )(feat, feat, feat, small, f_bias_col)


def _short_conv_tile(u, b_gate, c_gate, w_ref, carry_ref, sequence_start):
    z = c_gate * u
    prev = jnp.where(sequence_start, 0.0, carry_ref[...])
    carry_ref[...] = z[z.shape[0] - 8:, :]
    last1, last2 = prev[7:8, :], prev[6:7, :]
    row = lax.broadcasted_iota(jnp.int32, z.shape, 0)
    z1 = jnp.where(row >= 1, pltpu.roll(z, 1, 0), last1)
    z2 = jnp.where(row >= 2, pltpu.roll(z, 2, 0), jnp.where(row == 1, last1, last2))
    w = w_ref[...]
    return (b_gate * (w[0:1, :] * z2 + w[1:2, :] * z1 + w[2:3, :] * z)).astype(BF16)


def _mlstm_kernel(q_ref, k_ref, v_ref, o_ref, small_ref, bias_ref, gain_ref, out_ref,
                  c_ref, m_ref, b_ref, u_ref, ut_ref, *, seq, chunk):
    c_ref[...] = jnp.zeros_like(c_ref)
    m_ref[...] = jnp.zeros_like(m_ref)
    g = small_ref[...] + bias_ref[...]
    b_all = _cumsum_lanes(_log_sigmoid(g), segment=chunk)
    b_ref[...] = b_all
    u_all = _row_select([g[4 + h:5 + h, :] - b_all[8 + h:9 + h, :] for h in range(MLSTM_HEADS)],
                        8, g.shape[1])
    u_ref[...] = u_all
    ut_ref[...] = jnp.concatenate([u_all, jnp.zeros((120, g.shape[1]), F32)], axis=0).T
    refs = (q_ref, k_ref, v_ref, o_ref, gain_ref, out_ref, c_ref, m_ref, b_ref, u_ref, ut_ref)
    pl.loop(0, seq // chunk)(lambda c: _mlstm_chunk(c, *refs, seq=seq, chunk=chunk))


def _mlstm_chunk(c, q_ref, k_ref, v_ref, o_ref, gain_ref, out_ref, c_ref, m_ref, b_ref, u_ref, ut_ref, *,
                 seq, chunk):
    units = range(c_ref.shape[0])
    head = lambda u: u % MLSTM_HEADS
    sl = lambda u: slice(head(u) * HEAD_DIM, (head(u) + 1) * HEAD_DIM)
    cols = lambda u: pl.ds(pl.multiple_of((u // MLSTM_HEADS) * seq + c * chunk, chunk), chunk)

    src = lax.broadcasted_iota(jnp.int32, (chunk, chunk), 0)
    tgt = lax.broadcasted_iota(jnp.int32, (chunk, chunk), 1)
    causal = src <= tgt
    ones_rows = (lax.broadcasted_iota(jnp.int32, (HEAD_DIM, chunk), 0) == 0).astype(BF16)

    def first_matmuls(u):
        qs = q_ref[sl(u), cols(u)]
        ks = k_ref[sl(u), cols(u)]
        c_prev = c_ref[u]
        return dict(ks=ks, c_prev=c_prev, scores=_dot_tn(ks, qs),
                    carried=_dot(c_prev.astype(BF16), qs))

    def gate_arithmetic(u, st):
        h = head(u)
        b_row = b_ref[8 + h:9 + h, cols(u)]
        b_last = b_row[:, chunk - 1:chunk]
        u_row = u_ref[h:h + 1, cols(u)]
        u_col = ut_ref[cols(u), h:h + 1]
        m_prev = m_ref[u][0:1, 0:1]
        dmat = jnp.where(causal, b_row + u_col, NEG)
        g_row = b_row + m_prev
        m_u = jnp.maximum(g_row, jnp.max(dmat, axis=0, keepdims=True))
        st.update(sw=(st["scores"] * jnp.exp(dmat - m_u)).astype(BF16), m_t=m_u, inter=jnp.exp(g_row - m_u))
        m_loc = jnp.max(u_row + b_last, axis=1, keepdims=True)
        m_new = jnp.maximum(b_last + m_prev, m_loc)
        w_row = jnp.exp(u_row + b_last - m_new)
        st.update(decay=jnp.exp(b_last + m_prev - m_new),
                  kw=(st["ks"].astype(F32) * w_row).astype(BF16))
        m_ref[u] = jnp.broadcast_to(m_new, m_ref.shape[1:])

    def second_matmuls(u, st):
        v_aug = jnp.concatenate([v_ref[sl(u), cols(u)], ones_rows], axis=0)
        st.update(intra=_dot(v_aug, st["sw"]), update=_dot_nt(v_aug, st["kw"]))

    def epilogue(u, st):
        c_ref[u] = st["decay"] * st["c_prev"] + st["update"]
        tot = st["intra"] + st["inter"] * st["carried"]
        num = tot[0:HEAD_DIM]
        den = tot[HEAD_DIM:HEAD_DIM + 1]
        ht = num / jnp.maximum(jnp.abs(den), jnp.exp(-st["m_t"]))
        hn = _rms_rows(ht, gain_ref[sl(u), :])
        per_tile = Y_TILE // chunk
        tile = (u // MLSTM_HEADS) * (seq // Y_TILE) + c // per_tile
        lane = pl.ds(pl.multiple_of(lax.rem(c, per_tile) * chunk, chunk), chunk)
        out_ref[tile, sl(u), lane] = (_sigmoid(o_ref[sl(u), cols(u)].astype(F32)) * hn).astype(BF16)

    n = len(units)
    ahead = 2
    state = {u: first_matmuls(u) for u in range(min(ahead, n))}
    for u in units:
        gate_arithmetic(u, state[u])
        if u + ahead < n:
            state[u + ahead] = first_matmuls(u + ahead)
        second_matmuls(u, state[u])
        if u >= 1:
            epilogue(u - 1, state.pop(u - 1))
    epilogue(n - 1, state.pop(n - 1))


def _mlstm(feat, small, bias_col, gain_col, batch, seq):
    chunk = min(L_MLSTM, seq)
    group = MLSTM_GROUP if batch % MLSTM_GROUP == 0 else 1
    span = group * seq
    units = group * MLSTM_HEADS
    t = batch * seq
    width = MLSTM_HEADS * HEAD_DIM
    blk = lambda row: pl.BlockSpec((width, span), lambda b: (row // width, b))
    return pl.pallas_call(
        functools.partial(_mlstm_kernel, seq=seq, chunk=chunk),
        grid=(batch // group,),
        in_specs=[blk(ROW_MQ), blk(ROW_MK), blk(ROW_MV), blk(ROW_MO),
                  pl.BlockSpec((N_SMALL, span), lambda b: (0, b)),
                  pl.BlockSpec((N_SMALL, 1), lambda b: (0, 0)),
                  pl.BlockSpec((width, 1), lambda b: (0, 0))],
        out_specs=pl.BlockSpec((span // Y_TILE, width, Y_TILE), lambda b: (b, 0, 0)),
        out_shape=jax.ShapeDtypeStruct((t // Y_TILE, width, Y_TILE), BF16),
        scratch_shapes=[pltpu.VMEM((units, 2 * HEAD_DIM, HEAD_DIM), F32),
                        pltpu.VMEM((units, 8, 128), F32),
                        pltpu.VMEM((N_SMALL, span), F32),
                        pltpu.VMEM((8, span), F32),
                        pltpu.VMEM((span, 128), F32)],
        compiler_params=pltpu.CompilerParams(
            dimension_semantics=("parallel",), vmem_limit_bytes=VMEM_LIMIT),
        name="mlstm",
    )(feat, feat, feat, feat, small, bias_col, gain_col)


def _swa_kernel(q_ref, k_ref, v_ref, bias_ref, sink_ref, out_ref, *, nblk):
    refs = (q_ref, k_ref, v_ref, bias_ref, sink_ref, out_ref)
    pl.loop(0, q_ref.shape[1] // (nblk * WINDOW))(lambda n: _swa_span(n, *refs, nblk=nblk))


def _swa_span(n, q_ref, k_ref, v_ref, bias_ref, sink_ref, out_ref, *, nblk):
    w = WINDOW
    off = pl.multiple_of(n * (nblk * w), nblk * w)
    cur = pl.ds(off, nblk * w)
    prev = pl.ds(pl.multiple_of(jnp.maximum(off - w, 0), w), w)
    key = lax.broadcasted_iota(jnp.int32, (2 * w, SWA_GROUP * w), 0)
    first_block_pad = jnp.logical_and(n == 0, key < w)
    kvs = range(SWA_KV_HEADS)
    ksl = [slice(kv * HEAD_DIM, (kv + 1) * HEAD_DIM) for kv in kvs]
    kt = [jnp.concatenate([k_ref[s, prev], k_ref[s, cur]], axis=1) for s in ksl]
    vt = [jnp.concatenate([v_ref[s, prev], v_ref[s, cur]], axis=1) for s in ksl]

    bands = [(blk, kv) for blk in range(nblk) for kv in kvs]

    def head_rows(kv, g):
        return slice((kv * SWA_GROUP + g) * HEAD_DIM, (kv * SWA_GROUP + g + 1) * HEAD_DIM)

    def score(blk, kv):
        qt = jnp.concatenate([q_ref[head_rows(kv, g), pl.ds(off + blk * w, w)] for g in range(SWA_GROUP)], axis=1)
        s = _dot_tn(kt[kv][:, blk * w:(blk + 2) * w], qt) + bias_ref[kv]
        return jnp.where(first_block_pad, NEG, s) if blk == 0 else s

    def softmax(s, kv):
        sink = sink_ref[kv]
        m = jnp.maximum(jnp.max(s, axis=0, keepdims=True), sink)
        p = jnp.exp2(s - m)
        return p.astype(BF16), jnp.sum(p, axis=0, keepdims=True) + jnp.exp2(sink - m)

    def finish(blk, kv, o, d):
        o = o / d
        for g in range(SWA_GROUP):
            tile = n * (nblk * w // Y_TILE) + blk * w // Y_TILE
            lane = blk * w % Y_TILE
            out_ref[tile, head_rows(kv, g), lane:lane + w] = o[:, g * w:(g + 1) * w].astype(BF16)

    ahead = 2
    scores = [score(*band) for band in bands[:ahead]]
    pending = None
    for k, (blk, kv) in enumerate(bands):
        p, d = softmax(scores[k], kv)
        if k + ahead < len(bands):
            scores.append(score(*bands[k + ahead]))
        o = _dot(vt[kv][:, blk * w:(blk + 2) * w], p)
        if pending is not None:
            finish(*pending)
        pending = (blk, kv, o, d)
    finish(*pending)


def _swa(feat, bias_t, sink_rows, batch, seq):
    nblk = min(SWA_BLOCKS, seq // WINDOW)
    t = batch * seq
    qw = SWA_Q_HEADS * HEAD_DIM
    kw = SWA_KV_HEADS * HEAD_DIM
    return pl.pallas_call(
        functools.partial(_swa_kernel, nblk=nblk),
        grid=(batch,),
        in_specs=[pl.BlockSpec((qw, seq), lambda b: (ROW_SQ // qw, b)),
                  pl.BlockSpec((kw, seq), lambda b: (ROW_SK // kw, b)),
                  pl.BlockSpec((kw, seq), lambda b: (ROW_SV // kw, b)),
                  pl.BlockSpec(bias_t.shape, lambda b: (0, 0, 0)),
                  pl.BlockSpec(sink_rows.shape, lambda b: (0, 0, 0))],
        out_specs=pl.BlockSpec((seq // Y_TILE, qw, Y_TILE), lambda b: (b, 0, 0)),
        out_shape=jax.ShapeDtypeStruct((t // Y_TILE, qw, Y_TILE), BF16),
        compiler_params=pltpu.CompilerParams(
            dimension_semantics=("parallel",), vmem_limit_bytes=VMEM_LIMIT),
        name="swa_attention",
    )(feat, feat, feat, bias_t, sink_rows)


def _merge_ffn_kernel(x_ref, yf_ref, yc_ref, ym_ref, ys_ref, g_ref,
                      wf_ref, wc_ref, wm_ref, ws_ref, wo_ref, gain_ref, wg_ref, wu_ref, wd_ref, out_ref):
    def gated(branch, y):
        g = g_ref[:, branch * D_MODEL:(branch + 1) * D_MODEL]
        return (1.0 + jnp.tanh(g.astype(F32))) * y

    merged = gated(0, _dot_tn(yf_ref[...], wf_ref[...]))
    merged += gated(1, _dot(yc_ref[...], wc_ref[...]))
    merged += gated(2, _dot_tn(ym_ref[...], wm_ref[...]))
    merged += gated(3, _dot_tn(ys_ref[...], ws_ref[...]))
    x = x_ref[...] + _dot(merged.astype(BF16), wo_ref[...])

    ms = jnp.mean(x * x, axis=-1, keepdims=True)
    hn = (x * lax.rsqrt(ms + EPS) * gain_ref[...]).astype(BF16)
    acc = x
    bounds = list(range(0, D_FF, TF_FFN)) + [D_FF]
    for lo, hi in zip(bounds[:-1], bounds[1:]):
        h = _dot(hn, wg_ref[:, lo:hi])
        act = (h * (1.0 + jnp.tanh(h)) * _dot(hn, wu_ref[:, lo:hi])).astype(BF16)
        acc = acc + _dot(act, wd_ref[lo:hi, :])
    out_ref[...] = acc


def _merge_ffn(x2, y_fox_t, y_conv, y_mlstm_t, y_swa_t, gates, w_fox, w_conv, w_mlstm, w_swa,
               w_out, gain, w_gate_half, w_up, w_down, layer):
    t = x2.shape[0]
    tm = min(TM_MERGE, t)
    row_blk = lambda width: pl.BlockSpec((tm, width), lambda i: (i, 0))
    assert tm == Y_TILE, "the mixers write one contiguous (features, tokens) slab per merge row tile"
    feat_blk = lambda a: pl.BlockSpec((None,) + a.shape[1:], lambda i: (i, 0, 0))
    full = lambda a: _resident(a.shape[1:], layer)
    return pl.pallas_call(
        _merge_ffn_kernel,
        grid=(t // tm,),
        in_specs=[row_blk(D_MODEL),
                  feat_blk(y_fox_t),
                  row_blk(y_conv.shape[1]),
                  feat_blk(y_mlstm_t), feat_blk(y_swa_t),
                  row_blk(gates.shape[1]),
                  full(w_fox), full(w_conv), full(w_mlstm), full(w_swa), full(w_out),
                  full(gain), full(w_gate_half), full(w_up), full(w_down)],
        out_specs=row_blk(D_MODEL),
        out_shape=jax.ShapeDtypeStruct((t, D_MODEL), F32),
        compiler_params=pltpu.CompilerParams(
            dimension_semantics=("parallel",), vmem_limit_bytes=VMEM_LIMIT),
        name="merge_ffn",
    )(x2, y_fox_t, y_conv, y_mlstm_t, y_swa_t, gates,
      w_fox, w_conv, w_mlstm, w_swa, w_out, gain, w_gate_half, w_up, w_down)


IN_FQKV, IN_FF, IN_CONV, IN_MQ, IN_MK, IN_MV = 0, 768, 772, 1540, 1796, 2052
IN_MI, IN_MF, IN_MO, IN_SQ, IN_SK, IN_SV, IN_GATES, IN_COLS = 2308, 2312, 2316, 2572, 3084, 3212, 3340, 7436
TK_PREP = 256


def _wprep_kernel(wt_ref, tok_ref, feat_ref):
    tok_ref[:, COL_CU:COL_GATES] = wt_ref[IN_CONV:IN_MQ, :].T.astype(BF16)
    for lo in range(0, N_TOK - COL_GATES, TN_IN):
        gates = wt_ref[IN_GATES + lo:IN_GATES + lo + TN_IN, :]
        tok_ref[:, COL_GATES + lo:COL_GATES + lo + TN_IN] = (0.5 * gates).T.astype(BF16)
    for row, lo, hi in ((ROW_FQ, IN_FQKV, IN_FF), (ROW_MQ, IN_MQ, IN_MI), (ROW_SQ, IN_SQ, IN_SK),
                        (ROW_MO, IN_MO, IN_SQ), (ROW_SK, IN_SK, IN_GATES)):
        feat_ref[row:row + hi - lo, :] = wt_ref[lo:hi, :].astype(BF16)
    fox_f = wt_ref[IN_FF:IN_FF + N_SMALL, :]
    mlstm_if = wt_ref[IN_MI - 4:IN_MI - 4 + N_SMALL, :]
    row = lax.broadcasted_iota(jnp.int32, fox_f.shape, 0)
    small = jnp.where(row < 4, fox_f, jnp.where(row < 12, mlstm_if, 0.0))
    feat_ref[N_FEAT:, :] = small.astype(BF16)


def _prep_w_in(w_in):
    depth, d, cols = w_in.shape
    return pl.pallas_call(
        _wprep_kernel,
        grid=(depth, d // TK_PREP),
        in_specs=[pl.BlockSpec((None, cols, TK_PREP), lambda l, j: (l, 0, j))],
        out_specs=[pl.BlockSpec((None, TK_PREP, N_TOK), lambda l, j: (l, j, 0)),
                   pl.BlockSpec((None, N_FEAT + N_SMALL, TK_PREP), lambda l, j: (l, 0, j))],
        out_shape=[jax.ShapeDtypeStruct((depth, d, N_TOK), BF16),
                   jax.ShapeDtypeStruct((depth, N_FEAT + N_SMALL, d), BF16)],
        compiler_params=pltpu.CompilerParams(
            dimension_semantics=("parallel", "parallel"), vmem_limit_bytes=VMEM_LIMIT),
        name="w_in_relayout",
    )(jnp.transpose(w_in, (0, 2, 1)))


def _head_scale(depth, fox_q_gain, fox_k_gain, swa_q_gain, swa_k_gain):
    def tiled(gain, heads, factor=1.0):
        return jnp.tile(gain.astype(F32) * factor, (1, heads))
    ones = lambda n: jnp.ones((depth, n), F32)
    cols = [(ROW_FQ, tiled(fox_q_gain, FOX_HEADS, QK_SCALE * LOG2E)), (ROW_FK, tiled(fox_k_gain, FOX_HEADS)),
            (ROW_FV, ones(ROW_MQ - ROW_FV)), (ROW_MQ, QK_SCALE * ones(ROW_MK - ROW_MQ)),
            (ROW_MK, ones(ROW_SQ - ROW_MK)), (ROW_SQ, tiled(swa_q_gain, SWA_Q_HEADS, QK_SCALE * LOG2E)),
            (ROW_MO, ones(ROW_SK - ROW_MO)), (ROW_SK, tiled(swa_k_gain, SWA_KV_HEADS)),
            (ROW_SV, ones(N_FEAT - ROW_SV))]
    assert [r for r, _ in cols] == sorted(r for r, _ in cols)
    return jnp.concatenate([c for _, c in cols], axis=1)[:, :, None]


def _col(v, n=None):
    v = v.astype(F32).reshape(-1, 1)
    if n is not None and v.shape[0] < n:
        v = jnp.concatenate([v, jnp.zeros((n - v.shape[0], 1), F32)], axis=0)
    return v


def kernel(x, rel_bias, attn_norm, w_in, fox_f_bias, fox_q_gain, fox_k_gain, conv_w, mlstm_i_bias, mlstm_f_bias, mlstm_h_gain, swa_q_gain, swa_k_gain, swa_sinks, w_fox_out, w_conv_out, w_mlstm_out, w_swa_out, w_merge_out, ffn_norm, w_gate, w_up, w_down):
    batch, seq, _ = x.shape
    depth = w_in.shape[0]
    x2 = x.reshape(batch * seq, D_MODEL)
    bias_t = _swa_bias_table(rel_bias)

    w_tok, w_feat_t = _prep_w_in(w_in)
    attn_gain = attn_norm.reshape(depth, 1, D_MODEL)
    ffn_gain = ffn_norm.reshape(depth, 1, D_MODEL)
    w_fox_b, w_conv_b, w_mlstm_b, w_swa_b = (w.astype(BF16) for w in (w_fox_out, w_conv_out, w_mlstm_out, w_swa_out))
    w_merge_half = (0.5 * w_merge_out).astype(BF16)
    w_gate_half = (0.5 * w_gate).astype(BF16)
    w_up_b = w_up.astype(BF16)
    w_down_b = w_down.astype(BF16)

    head_scale = _head_scale(depth, fox_q_gain, fox_k_gain, swa_q_gain, swa_k_gain)

    for l in range(depth):
        y_conv, gates, feat, small = _inproj(x2, attn_gain, w_tok, w_feat_t, head_scale, conv_w, l, seq)

        y_fox_t = _fox(feat, small, _col(fox_f_bias[l], N_SMALL), batch, seq)
        gate_bias = jnp.concatenate([jnp.zeros((4,), F32), mlstm_i_bias[l], mlstm_f_bias[l]])
        y_mlstm_t = _mlstm(feat, small, _col(gate_bias, N_SMALL), _col(mlstm_h_gain[l]), batch, seq)
        sink_rows = jnp.broadcast_to(
            (swa_sinks[l].astype(F32) * LOG2E).reshape(SWA_KV_HEADS, 1, SWA_GROUP, 1),
            (SWA_KV_HEADS, 1, SWA_GROUP, WINDOW)).reshape(SWA_KV_HEADS, 1, SWA_GROUP * WINDOW)
        y_swa_t = _swa(feat, bias_t, sink_rows, batch, seq)

        x2 = _merge_ffn(x2, y_fox_t, y_conv, y_mlstm_t, y_swa_t, gates,
                        w_fox_b, w_conv_b, w_mlstm_b, w_swa_b, w_merge_half,
                        ffn_gain, w_gate_half, w_up_b, w_down_b, l)
    return x2.reshape(batch, seq, D_MODEL)
```

```python
import functools

import numpy as np
import jax
import jax.numpy as jnp
from jax import lax
from jax.experimental import pallas as pl
from jax.experimental.pallas import tpu as pltpu

F32 = jnp.float32
BF16 = jnp.bfloat16

D_MODEL = 1024
HEAD_DIM = 64
FOX_HEADS = 4
MLSTM_HEADS = 4
SWA_Q_HEADS = 8
SWA_KV_HEADS = 2
SWA_GROUP = SWA_Q_HEADS // SWA_KV_HEADS
WINDOW = 128
REL_BUCKETS = 32
REL_MAX_DIST = 128
D_FF = 2816
EPS = 1e-6
NEG = -1e30
QK_SCALE = HEAD_DIM ** -0.5
LOG2E = 1.4426950408889634

ROW_FQ, ROW_FK, ROW_FV = 0, 256, 512
ROW_MQ, ROW_MK, ROW_MV = 768, 1024, 1280
ROW_SQ, ROW_MO, ROW_SK, ROW_SV = 1536, 2048, 2304, 2432
N_FEAT = 2560
N_SMALL = 16
COL_CU, COL_CB, COL_CC, COL_GATES = 0, 256, 512, 768
N_TOK = 4864

TM_IN = 512
TN_IN = 512
FEAT_CHUNK = 640
FOX_GROUP = 2
FOX_TILES = 2
TQ_FOX = 256
TK_FOX = 256
L_MLSTM = 256
MLSTM_GROUP = 2
SWA_BLOCKS = 16
TM_MERGE = 512
Y_TILE = TM_MERGE
TF_FFN = 512
V7X_VMEM_BYTES = 64 * 1024 * 1024
VMEM_LIMIT = V7X_VMEM_BYTES * 7 // 8


def _dot(a, b):
    return jnp.dot(a, b, preferred_element_type=F32)


def _dot_nt(a, b):
    return lax.dot_general(a, b, (((1,), (1,)), ((), ())), preferred_element_type=F32)


def _dot_tn(a, b):
    return lax.dot_general(a, b, (((0,), (0,)), ((), ())), preferred_element_type=F32)


def _sigmoid(x):
    return 0.5 * jnp.tanh(0.5 * x) + 0.5


def _log_sigmoid(x):
    return jnp.minimum(x, 0.0) - jnp.log(1.0 + jnp.exp(-jnp.abs(x)))


def _cumsum_lanes(x, segment=None):
    n = segment or x.shape[-1]
    lane = lax.broadcasted_iota(jnp.int32, x.shape, x.ndim - 1) & (n - 1)
    k = 1
    while k < n:
        x = x + jnp.where(lane >= k, pltpu.roll(x, k, x.ndim - 1), 0.0)
        k *= 2
    return x


def _row_select(rows, n_rows, width):
    rid = lax.broadcasted_iota(jnp.int32, (n_rows, width), 0)
    out = jnp.zeros((n_rows, width), F32)
    for r, v in enumerate(rows):
        out = jnp.where(rid == r, v, out)
    return out


def _split3(c):
    hi = c.astype(BF16).astype(F32)
    r = c - hi
    mid = r.astype(BF16).astype(F32)
    lo = (r - mid).astype(BF16).astype(F32)
    return hi, mid, lo


def _rms_rows(xt, gain_col):
    ms = jnp.mean(xt * xt, axis=0, keepdims=True)
    return xt * lax.rsqrt(ms + EPS) * gain_col


def _bias_kernel(rb_ref, idx_ref, out_ref):
    kv = pl.program_id(0)
    g = pl.program_id(1)
    head = kv * SWA_GROUP + g
    idx = idx_ref[...]
    acc = jnp.full(idx.shape, NEG, F32)
    for b in range(REL_BUCKETS):
        acc = jnp.where(idx == b, rb_ref[b * SWA_Q_HEADS + head] * LOG2E, acc)
    out_ref[0] = acc


def _bucket_table():
    j = np.arange(2 * WINDOW)[:, None]
    i = np.arange(WINDOW)[None, :]
    dist = i + WINDOW - j
    n = np.maximum(dist, 0).astype(np.int32)
    max_exact = REL_BUCKETS // 2
    nf = np.maximum(n, 1).astype(np.float32)
    large = max_exact + (np.log(nf / np.float32(max_exact)) / np.float32(np.log(REL_MAX_DIST / max_exact))
                         * np.float32(REL_BUCKETS - max_exact)).astype(np.int32)
    large = np.minimum(large, REL_BUCKETS - 1)
    bucket = np.where(n < max_exact, n, large)
    return np.where((dist >= 0) & (dist < WINDOW), bucket, -1).astype(np.int32)


def _swa_bias_table(rel_bias):
    idx = jnp.asarray(_bucket_table())
    return pl.pallas_call(
        _bias_kernel,
        grid=(SWA_KV_HEADS, SWA_GROUP),
        in_specs=[
            pl.BlockSpec(memory_space=pltpu.SMEM),
            pl.BlockSpec((2 * WINDOW, WINDOW), lambda kv, g: (0, 0)),
        ],
        out_specs=pl.BlockSpec((1, 2 * WINDOW, WINDOW), lambda kv, g: (kv, 0, g)),
        out_shape=jax.ShapeDtypeStruct((SWA_KV_HEADS, 2 * WINDOW, SWA_GROUP * WINDOW), F32),
        name="swa_bias_table",
    )(rel_bias.reshape(-1), idx)


QK_NORM_ROWS = ((ROW_FQ, ROW_FV), (ROW_SQ, ROW_MO), (ROW_SK, ROW_SV))


def _inproj_kernel(x_ref, g_ref, wt_ref, wf_ref, hs_ref, cw_ref, conv_ref, gates_ref, feat_ref, small_ref,
                   carry_ref, *, tiles_per_seq):
    x = x_ref[...]
    ms = jnp.mean(x * x, axis=-1, keepdims=True)
    xn = (x * lax.rsqrt(ms + EPS) * g_ref[...]).astype(BF16)
    cin = _dot(xn, wt_ref[:, COL_CU:COL_GATES])
    width = cw_ref.shape[1]
    conv_ref[...] = _short_conv_tile(cin[:, COL_CU:COL_CU + width], cin[:, COL_CB:COL_CB + width],
                                     cin[:, COL_CC:COL_CC + width], cw_ref, carry_ref,
                                     lax.rem(pl.program_id(0), tiles_per_seq) == 0)
    for lo in range(0, N_TOK - COL_GATES, TN_IN):
        gates_ref[:, lo:lo + TN_IN] = _dot(xn, wt_ref[:, COL_GATES + lo:COL_GATES + lo + TN_IN]).astype(BF16)
    n_chunks = N_FEAT // FEAT_CHUNK
    for c in range(n_chunks):
        lo = c * FEAT_CHUNK
        hi = lo + FEAT_CHUNK + (N_SMALL if c == n_chunks - 1 else 0)
        r = _dot_nt(wf_ref[lo:hi, :], xn)
        for row in range(lo, lo + FEAT_CHUNK, HEAD_DIM):
            head = r[row - lo:row - lo + HEAD_DIM]
            scale = hs_ref[row:row + HEAD_DIM, :]
            if any(a <= row < b for a, b in QK_NORM_ROWS):
                head = _rms_rows(head, scale)
            elif ROW_MQ <= row < ROW_MK:
                head = head * scale
            feat_ref[row:row + HEAD_DIM, :] = head.astype(BF16)
        if c == n_chunks - 1:
            small_ref[...] = r[FEAT_CHUNK:]


def _resident(shape, layer=None):
    if layer is None:
        return pl.BlockSpec(shape, lambda *_: (0,) * len(shape), pipeline_mode=pl.Buffered(1))
    return pl.BlockSpec((None,) + tuple(shape), lambda *_: (layer,) + (0,) * len(shape),
                        pipeline_mode=pl.Buffered(1))


def _inproj(x2, gain, w_tok, w_feat_t, head_scale, conv_w, layer, seq):
    t = x2.shape[0]
    tm = min(TM_IN, t, seq)
    conv_width = conv_w.shape[2]
    return pl.pallas_call(
        functools.partial(_inproj_kernel, tiles_per_seq=seq // tm),
        grid=(t // tm,),
        in_specs=[
            pl.BlockSpec((tm, D_MODEL), lambda i: (i, 0)),
            _resident((1, D_MODEL), layer),
            _resident((D_MODEL, N_TOK), layer),
            _resident((N_FEAT + N_SMALL, D_MODEL), layer),
            _resident((N_FEAT, 1), layer),
            _resident(conv_w.shape[1:], layer),
        ],
        out_specs=[
            pl.BlockSpec((tm, conv_width), lambda i: (i, 0)),
            pl.BlockSpec((tm, N_TOK - COL_GATES), lambda i: (i, 0)),
            pl.BlockSpec((N_FEAT, tm), lambda i: (0, i)),
            pl.BlockSpec((N_SMALL, tm), lambda i: (0, i)),
        ],
        out_shape=[
            jax.ShapeDtypeStruct((t, conv_width), BF16),
            jax.ShapeDtypeStruct((t, N_TOK - COL_GATES), BF16),
            jax.ShapeDtypeStruct((N_FEAT, t), BF16),
            jax.ShapeDtypeStruct((N_SMALL, t), F32),
        ],
        scratch_shapes=[pltpu.VMEM((8, conv_width), F32)],
        compiler_params=pltpu.CompilerParams(
            dimension_semantics=("arbitrary",), vmem_limit_bytes=VMEM_LIMIT),
        name="inproj",
    )(x2, gain, w_tok, w_feat_t, head_scale, conv_w)


def _fox_kernel(fq_ref, fk_ref, fv_ref, small_ref, fb_ref, out_ref,
                qa_ref, ka_ref, s_ref, p_ref, acc_ref, m_ref, l_ref, a_ref, *, seq, tq, tk):
    n_units = qa_ref.shape[0]
    c_all = _cumsum_lanes(_log_sigmoid(small_ref[...] + fb_ref[...]), segment=seq)
    for u in range(n_units):
        e, h = divmod(u, FOX_HEADS)
        sl = slice(h * HEAD_DIM, (h + 1) * HEAD_DIM)
        cols = slice(e * seq, (e + 1) * seq)
        c = c_all[h:h + 1, cols] * LOG2E
        hi, mid, lo = _split3(c)
        one = jnp.ones_like(c)
        qa_ref[u, 0:HEAD_DIM, :] = fq_ref[sl, cols]
        qa_ref[u, HEAD_DIM:, :] = _row_select([hi, mid, lo, one, one, one], HEAD_DIM, seq).astype(BF16)
        k_aug = _row_select([one, one, one, -hi, -mid, -lo], HEAD_DIM, seq)
        ka_ref[u] = jnp.concatenate([fk_ref[sl, cols].astype(F32), k_aug], axis=0).T.astype(BF16)
    refs = (fv_ref, out_ref, qa_ref, ka_ref, s_ref, p_ref, acc_ref, m_ref, l_ref, a_ref)
    pl.loop(0, seq // (FOX_TILES * tq))(lambda i: _fox_tile_group(i, *refs, seq=seq, tq=tq, tk=tk))


def _fox_tile_group(i,fv_ref, out_ref, qa_ref, ka_ref, s_ref, p_ref, acc_ref, m_ref, l_ref, a_ref, *, seq, tq, tk):
    n = FOX_TILES
    n_units = qa_ref.shape[0]
    chains = range(n * n_units)
    unit = lambda c: c // n
    from_tile = lambda r: [c for c in chains if c % n >= r]
    rows = lambda c: slice((unit(c) % FOX_HEADS) * HEAD_DIM, (unit(c) % FOX_HEADS + 1) * HEAD_DIM)
    base = lambda c: (unit(c) // FOX_HEADS) * seq
    q_off = lambda c: pl.multiple_of((n * i + c % n) * tq, tq)
    qa = [qa_ref[unit(c), :, pl.ds(q_off(c), tq)] for c in chains]

    m_ref[...] = jnp.full(m_ref.shape, NEG, F32)
    l_ref[...] = jnp.zeros_like(l_ref)
    acc_ref[...] = jnp.zeros_like(acc_ref)
    a_ref[...] = jnp.ones_like(a_ref)
    p_ref[...] = jnp.zeros_like(p_ref)

    def qk(j, slot, which):
        off = pl.multiple_of(j * tk, tk)
        for c in which:
            s_ref[slot, c] = _dot(ka_ref[unit(c), pl.ds(off, tk), :], qa[c])

    def pv(j, which):
        off = pl.multiple_of(j * tk, tk)
        for c in which:
            v = fv_ref[rows(c), pl.ds(pl.multiple_of(base(c) + off, tk), tk)]
            acc_ref[c] = a_ref[c] * acc_ref[c] + _dot(v, p_ref[c])

    def softmax(slot, which, own_block=()):
        for c in which:
            s = s_ref[slot, c]
            if c in own_block:
                key = lax.broadcasted_iota(jnp.int32, s.shape, 0)
                qry = lax.broadcasted_iota(jnp.int32, s.shape, 1)
                s = jnp.where(key <= qry, s, NEG)
            m = m_ref[c]
            m_new = jnp.maximum(m, jnp.max(s, axis=0, keepdims=True))
            p = jnp.exp2(s - m_new)
            alpha = jnp.exp2(m - m_new)
            l_ref[c] = alpha * l_ref[c] + jnp.sum(p, axis=0, keepdims=True)
            m_ref[c] = m_new
            a_ref[c] = alpha
            p_ref[c] = p.astype(BF16)

    def stage(j, slot):
        qk(j + 1, 1 - slot, chains)
        pv(jnp.maximum(j - 1, 0), chains)
        softmax(slot, chains)

    qk(0, 0, chains)
    first_own = n * i

    @pl.loop(0, first_own // 2)
    def _(k):
        stage(2 * k, 0)
        stage(2 * k + 1, 1)

    for r in range(n):
        if r + 1 < n:
            qk(first_own + r + 1, (r + 1) % 2, from_tile(r + 1))
        pv(jnp.maximum(first_own + r - 1, 0), from_tile(max(r - 1, 0)))
        softmax(r % 2, from_tile(r), own_block=[c for c in chains if c % n == r])
    pv(first_own + n - 1, from_tile(n - 1))

    per_tile = Y_TILE // tq
    for c in chains:
        tile = base(c) // Y_TILE + (n // per_tile) * i + (c % n) // per_tile
        lane = ((c % n) % per_tile) * tq
        out_ref[tile, rows(c), lane:lane + tq] = (acc_ref[c] / l_ref[c]).astype(BF16)


def _fox(feat, small, f_bias_col, batch, seq):
    tq = min(TQ_FOX, seq)
    tk = min(TK_FOX, tq)
    assert tq == tk and FOX_TILES % 2 == 0 and seq % (FOX_TILES * tq) == 0, \
        "query tiles run in even groups; a tile's own block is one key block"
    group = FOX_GROUP if batch % FOX_GROUP == 0 else 1
    span = group * seq
    units = group * FOX_HEADS
    chains = FOX_TILES * units
    t = batch * seq
    width = FOX_HEADS * HEAD_DIM
    blk = lambda row: pl.BlockSpec((width, span), lambda b: (row // width, b))
    col = lambda n: pl.BlockSpec((n, 1), lambda b: (0, 0))
    return pl.pallas_call(
        functools.partial(_fox_kernel, seq=seq, tq=tq, tk=tk),
        grid=(batch // group,),
        in_specs=[blk(ROW_FQ), blk(ROW_FK), blk(ROW_FV),
                  pl.BlockSpec((N_SMALL, span), lambda b: (0, b)),
                  col(N_SMALL)],
        out_specs=pl.BlockSpec((span // Y_TILE, width, Y_TILE), lambda b: (b, 0, 0)),
        out_shape=jax.ShapeDtypeStruct((t // Y_TILE, width, Y_TILE), BF16),
        scratch_shapes=[pltpu.VMEM((units, 2 * HEAD_DIM, seq), BF16),
                        pltpu.VMEM((units, seq, 2 * HEAD_DIM), BF16),
                        pltpu.VMEM((2, chains, tk, tq), F32),
                        pltpu.VMEM((chains, tk, tq), BF16),
                        pltpu.VMEM((chains, HEAD_DIM, tq), F32),
                        pltpu.VMEM((chains, 1, tq), F32),
                        pltpu.VMEM((chains, 1, tq), F32),
                        pltpu.VMEM((chains, 1, tq), F32)],
        compiler_params=pltpu.CompilerParams(
            dimension_semantics=("parallel",), vmem_limit_bytes=VMEM_LIMIT),
        name="fox_attention",
    )(feat, feat, feat, small, f_bias_col)


def _short_conv_tile(u, b_gate, c_gate, w_ref, carry_ref, sequence_start):
    z = c_gate * u
    prev = jnp.where(sequence_start, 0.0, carry_ref[...])
    carry_ref[...] = z[z.shape[0] - 8:, :]
    last1, last2 = prev[7:8, :], prev[6:7, :]
    row = lax.broadcasted_iota(jnp.int32, z.shape, 0)
    z1 = jnp.where(row >= 1, pltpu.roll(z, 1, 0), last1)
    z2 = jnp.where(row >= 2, pltpu.roll(z, 2, 0), jnp.where(row == 1, last1, last2))
    w = w_ref[...]
    return (b_gate * (w[0:1, :] * z2 + w[1:2, :] * z1 + w[2:3, :] * z)).astype(BF16)


def _mlstm_kernel(q_ref, k_ref, v_ref, o_ref, small_ref, bias_ref, gain_ref, out_ref,
                  c_ref, m_ref, b_ref, u_ref, ut_ref, *, seq, chunk):
    c_ref[...] = jnp.zeros_like(c_ref)
    m_ref[...] = jnp.zeros_like(m_ref)
    g = small_ref[...] + bias_ref[...]
    b_all = _cumsum_lanes(_log_sigmoid(g), segment=chunk)
    b_ref[...] = b_all
    u_all = _row_select([g[4 + h:5 + h, :] - b_all[8 + h:9 + h, :] for h in range(MLSTM_HEADS)],
                        8, g.shape[1])
    u_ref[...] = u_all
    ut_ref[...] = jnp.concatenate([u_all, jnp.zeros((120, g.shape[1]), F32)], axis=0).T
    refs = (q_ref, k_ref, v_ref, o_ref, gain_ref, out_ref, c_ref, m_ref, b_ref, u_ref, ut_ref)
    pl.loop(0, seq // chunk)(lambda c: _mlstm_chunk(c, *refs, seq=seq, chunk=chunk))


def _mlstm_chunk(c, q_ref, k_ref, v_ref, o_ref, gain_ref, out_ref, c_ref, m_ref, b_ref, u_ref, ut_ref, *,
                 seq, chunk):
    units = range(c_ref.shape[0])
    head = lambda u: u % MLSTM_HEADS
    sl = lambda u: slice(head(u) * HEAD_DIM, (head(u) + 1) * HEAD_DIM)
    cols = lambda u: pl.ds(pl.multiple_of((u // MLSTM_HEADS) * seq + c * chunk, chunk), chunk)

    src = lax.broadcasted_iota(jnp.int32, (chunk, chunk), 0)
    tgt = lax.broadcasted_iota(jnp.int32, (chunk, chunk), 1)
    causal = src <= tgt
    ones_rows = (lax.broadcasted_iota(jnp.int32, (HEAD_DIM, chunk), 0) == 0).astype(BF16)

    def first_matmuls(u):
        qs = q_ref[sl(u), cols(u)]
        ks = k_ref[sl(u), cols(u)]
        c_prev = c_ref[u]
        return dict(ks=ks, c_prev=c_prev, scores=_dot_tn(ks, qs),
                    carried=_dot(c_prev.astype(BF16), qs))

    def gate_arithmetic(u, st):
        h = head(u)
        b_row = b_ref[8 + h:9 + h, cols(u)]
        b_last = b_row[:, chunk - 1:chunk]
        u_row = u_ref[h:h + 1, cols(u)]
        u_col = ut_ref[cols(u), h:h + 1]
        m_prev = m_ref[u][0:1, 0:1]
        dmat = jnp.where(causal, b_row + u_col, NEG)
        g_row = b_row + m_prev
        m_u = jnp.maximum(g_row, jnp.max(dmat, axis=0, keepdims=True))
        st.update(sw=(st["scores"] * jnp.exp(dmat - m_u)).astype(BF16), m_t=m_u, inter=jnp.exp(g_row - m_u))
        m_loc = jnp.max(u_row + b_last, axis=1, keepdims=True)
        m_new = jnp.maximum(b_last + m_prev, m_loc)
        w_row = jnp.exp(u_row + b_last - m_new)
        st.update(decay=jnp.exp(b_last + m_prev - m_new),
                  kw=(st["ks"].astype(F32) * w_row).astype(BF16))
        m_ref[u] = jnp.broadcast_to(m_new, m_ref.shape[1:])

    def second_matmuls(u, st):
        v_aug = jnp.concatenate([v_ref[sl(u), cols(u)], ones_rows], axis=0)
        st.update(intra=_dot(v_aug, st["sw"]), update=_dot_nt(v_aug, st["kw"]))

    def epilogue(u, st):
        c_ref[u] = st["decay"] * st["c_prev"] + st["update"]
        tot = st["intra"] + st["inter"] * st["carried"]
        num = tot[0:HEAD_DIM]
        den = tot[HEAD_DIM:HEAD_DIM + 1]
        ht = num / jnp.maximum(jnp.abs(den), jnp.exp(-st["m_t"]))
        hn = _rms_rows(ht, gain_ref[sl(u), :])
        per_tile = Y_TILE // chunk
        tile = (u // MLSTM_HEADS) * (seq // Y_TILE) + c // per_tile
        lane = pl.ds(pl.multiple_of(lax.rem(c, per_tile) * chunk, chunk), chunk)
        out_ref[tile, sl(u), lane] = (_sigmoid(o_ref[sl(u), cols(u)].astype(F32)) * hn).astype(BF16)

    n = len(units)
    ahead = 2
    state = {u: first_matmuls(u) for u in range(min(ahead, n))}
    for u in units:
        gate_arithmetic(u, state[u])
        if u + ahead < n:
            state[u + ahead] = first_matmuls(u + ahead)
        second_matmuls(u, state[u])
        if u >= 1:
            epilogue(u - 1, state.pop(u - 1))
    epilogue(n - 1, state.pop(n - 1))


def _mlstm(feat, small, bias_col, gain_col, batch, seq):
    chunk = min(L_MLSTM, seq)
    group = MLSTM_GROUP if batch % MLSTM_GROUP == 0 else 1
    span = group * seq
    units = group * MLSTM_HEADS
    t = batch * seq
    width = MLSTM_HEADS * HEAD_DIM
    blk = lambda row: pl.BlockSpec((width, span), lambda b: (row // width, b))
    return pl.pallas_call(
        functools.partial(_mlstm_kernel, seq=seq, chunk=chunk),
        grid=(batch // group,),
        in_specs=[blk(ROW_MQ), blk(ROW_MK), blk(ROW_MV), blk(ROW_MO),
                  pl.BlockSpec((N_SMALL, span), lambda b: (0, b)),
                  pl.BlockSpec((N_SMALL, 1), lambda b: (0, 0)),
                  pl.BlockSpec((width, 1), lambda b: (0, 0))],
        out_specs=pl.BlockSpec((span // Y_TILE, width, Y_TILE), lambda b: (b, 0, 0)),
        out_shape=jax.ShapeDtypeStruct((t // Y_TILE, width, Y_TILE), BF16),
        scratch_shapes=[pltpu.VMEM((units, 2 * HEAD_DIM, HEAD_DIM), F32),
                        pltpu.VMEM((units, 8, 128), F32),
                        pltpu.VMEM((N_SMALL, span), F32),
                        pltpu.VMEM((8, span), F32),
                        pltpu.VMEM((span, 128), F32)],
        compiler_params=pltpu.CompilerParams(
            dimension_semantics=("parallel",), vmem_limit_bytes=VMEM_LIMIT),
        name="mlstm",
    )(feat, feat, feat, feat, small, bias_col, gain_col)


def _swa_kernel(q_ref, k_ref, v_ref, bias_ref, sink_ref, out_ref, *, nblk):
    refs = (q_ref, k_ref, v_ref, bias_ref, sink_ref, out_ref)
    pl.loop(0, q_ref.shape[1] // (nblk * WINDOW))(lambda n: _swa_span(n, *refs, nblk=nblk))


def _swa_span(n, q_ref, k_ref, v_ref, bias_ref, sink_ref, out_ref, *, nblk):
    w = WINDOW
    off = pl.multiple_of(n * (nblk * w), nblk * w)
    cur = pl.ds(off, nblk * w)
    prev = pl.ds(pl.multiple_of(jnp.maximum(off - w, 0), w), w)
    key = lax.broadcasted_iota(jnp.int32, (2 * w, SWA_GROUP * w), 0)
    first_block_pad = jnp.logical_and(n == 0, key < w)
    kvs = range(SWA_KV_HEADS)
    ksl = [slice(kv * HEAD_DIM, (kv + 1) * HEAD_DIM) for kv in kvs]
    kt = [jnp.concatenate([k_ref[s, prev], k_ref[s, cur]], axis=1) for s in ksl]
    vt = [jnp.concatenate([v_ref[s, prev], v_ref[s, cur]], axis=1) for s in ksl]

    bands = [(blk, kv) for blk in range(nblk) for kv in kvs]

    def head_rows(kv, g):
        return slice((kv * SWA_GROUP + g) * HEAD_DIM, (kv * SWA_GROUP + g + 1) * HEAD_DIM)

    def score(blk, kv):
        qt = jnp.concatenate([q_ref[head_rows(kv, g), pl.ds(off + blk * w, w)] for g in range(SWA_GROUP)], axis=1)
        s = _dot_tn(kt[kv][:, blk * w:(blk + 2) * w], qt) + bias_ref[kv]
        return jnp.where(first_block_pad, NEG, s) if blk == 0 else s

    def softmax(s, kv):
        sink = sink_ref[kv]
        m = jnp.maximum(jnp.max(s, axis=0, keepdims=True), sink)
        p = jnp.exp2(s - m)
        return p.astype(BF16), jnp.sum(p, axis=0, keepdims=True) + jnp.exp2(sink - m)

    def finish(blk, kv, o, d):
        o = o / d
        for g in range(SWA_GROUP):
            tile = n * (nblk * w // Y_TILE) + blk * w // Y_TILE
            lane = blk * w % Y_TILE
            out_ref[tile, head_rows(kv, g), lane:lane + w] = o[:, g * w:(g + 1) * w].astype(BF16)

    ahead = 1
    scores = [score(*band) for band in bands[:ahead]]
    pending = None
    for k, (blk, kv) in enumerate(bands):
        p, d = softmax(scores[k], kv)
        if k + ahead < len(bands):
            scores.append(score(*bands[k + ahead]))
        o = _dot(vt[kv][:, blk * w:(blk + 2) * w], p)
        if pending is not None:
            finish(*pending)
        pending = (blk, kv, o, d)
    finish(*pending)


def _swa(feat, bias_t, sink_rows, batch, seq):
    nblk = min(SWA_BLOCKS, seq // WINDOW)
    t = batch * seq
    qw = SWA_Q_HEADS * HEAD_DIM
    kw = SWA_KV_HEADS * HEAD_DIM
    return pl.pallas_call(
        functools.partial(_swa_kernel, nblk=nblk),
        grid=(batch,),
        in_specs=[pl.BlockSpec((qw, seq), lambda b: (ROW_SQ // qw, b)),
                  pl.BlockSpec((kw, seq), lambda b: (ROW_SK // kw, b)),
                  pl.BlockSpec((kw, seq), lambda b: (ROW_SV // kw, b)),
                  pl.BlockSpec(bias_t.shape, lambda b: (0, 0, 0)),
                  pl.BlockSpec(sink_rows.shape, lambda b: (0, 0, 0))],
        out_specs=pl.BlockSpec((seq // Y_TILE, qw, Y_TILE), lambda b: (b, 0, 0)),
        out_shape=jax.ShapeDtypeStruct((t // Y_TILE, qw, Y_TILE), BF16),
        compiler_params=pltpu.CompilerParams(
            dimension_semantics=("parallel",), vmem_limit_bytes=VMEM_LIMIT),
        name="swa_attention",
    )(feat, feat, feat, bias_t, sink_rows)


def _merge_ffn_kernel(x_ref, yf_ref, yc_ref, ym_ref, ys_ref, g_ref,
                      wf_ref, wc_ref, wm_ref, ws_ref, wo_ref, gain_ref, wg_ref, wu_ref, wd_ref, out_ref):
    def gated(branch, y):
        g = g_ref[:, branch * D_MODEL:(branch + 1) * D_MODEL]
        return (1.0 + jnp.tanh(g.astype(F32))) * y

    merged = gated(0, _dot_tn(yf_ref[...], wf_ref[...]))
    merged += gated(1, _dot(yc_ref[...], wc_ref[...]))
    merged += gated(2, _dot_tn(ym_ref[...], wm_ref[...]))
    merged += gated(3, _dot_tn(ys_ref[...], ws_ref[...]))
    x = x_ref[...] + _dot(merged.astype(BF16), wo_ref[...])

    ms = jnp.mean(x * x, axis=-1, keepdims=True)
    hn = (x * lax.rsqrt(ms + EPS) * gain_ref[...]).astype(BF16)
    acc = x
    bounds = list(range(0, D_FF, TF_FFN)) + [D_FF]
    for lo, hi in zip(bounds[:-1], bounds[1:]):
        h = _dot(hn, wg_ref[:, lo:hi])
        act = (h * (1.0 + jnp.tanh(h)) * _dot(hn, wu_ref[:, lo:hi])).astype(BF16)
        acc = acc + _dot(act, wd_ref[lo:hi, :])
    out_ref[...] = acc


def _merge_ffn(x2, y_fox_t, y_conv, y_mlstm_t, y_swa_t, gates, w_fox, w_conv, w_mlstm, w_swa,
               w_out, gain, w_gate_half, w_up, w_down, layer):
    t = x2.shape[0]
    tm = min(TM_MERGE, t)
    row_blk = lambda width: pl.BlockSpec((tm, width), lambda i: (i, 0))
    assert tm == Y_TILE, "the mixers write one contiguous (features, tokens) slab per merge row tile"
    feat_blk = lambda a: pl.BlockSpec((None,) + a.shape[1:], lambda i: (i, 0, 0))
    full = lambda a: _resident(a.shape[1:], layer)
    return pl.pallas_call(
        _merge_ffn_kernel,
        grid=(t // tm,),
        in_specs=[row_blk(D_MODEL),
                  feat_blk(y_fox_t),
                  row_blk(y_conv.shape[1]),
                  feat_blk(y_mlstm_t), feat_blk(y_swa_t),
                  row_blk(gates.shape[1]),
                  full(w_fox), full(w_conv), full(w_mlstm), full(w_swa), full(w_out),
                  full(gain), full(w_gate_half), full(w_up), full(w_down)],
        out_specs=row_blk(D_MODEL),
        out_shape=jax.ShapeDtypeStruct((t, D_MODEL), F32),
        compiler_params=pltpu.CompilerParams(
            dimension_semantics=("parallel",), vmem_limit_bytes=VMEM_LIMIT),
        name="merge_ffn",
    )(x2, y_fox_t, y_conv, y_mlstm_t, y_swa_t, gates,
      w_fox, w_conv, w_mlstm, w_swa, w_out, gain, w_gate_half, w_up, w_down)


IN_FQKV, IN_FF, IN_CONV, IN_MQ, IN_MK, IN_MV = 0, 768, 772, 1540, 1796, 2052
IN_MI, IN_MF, IN_MO, IN_SQ, IN_SK, IN_SV, IN_GATES, IN_COLS = 2308, 2312, 2316, 2572, 3084, 3212, 3340, 7436
TK_PREP = 256


def _wprep_kernel(wt_ref, tok_ref, feat_ref):
    tok_ref[:, COL_CU:COL_GATES] = wt_ref[IN_CONV:IN_MQ, :].T.astype(BF16)
    for lo in range(0, N_TOK - COL_GATES, TN_IN):
        gates = wt_ref[IN_GATES + lo:IN_GATES + lo + TN_IN, :]
        tok_ref[:, COL_GATES + lo:COL_GATES + lo + TN_IN] = (0.5 * gates).T.astype(BF16)
    for row, lo, hi in ((ROW_FQ, IN_FQKV, IN_FF), (ROW_MQ, IN_MQ, IN_MI), (ROW_SQ, IN_SQ, IN_SK),
                        (ROW_MO, IN_MO, IN_SQ), (ROW_SK, IN_SK, IN_GATES)):
        feat_ref[row:row + hi - lo, :] = wt_ref[lo:hi, :].astype(BF16)
    fox_f = wt_ref[IN_FF:IN_FF + N_SMALL, :]
    mlstm_if = wt_ref[IN_MI - 4:IN_MI - 4 + N_SMALL, :]
    row = lax.broadcasted_iota(jnp.int32, fox_f.shape, 0)
    small = jnp.where(row < 4, fox_f, jnp.where(row < 12, mlstm_if, 0.0))
    feat_ref[N_FEAT:, :] = small.astype(BF16)


def _prep_w_in(w_in):
    depth, d, cols = w_in.shape
    return pl.pallas_call(
        _wprep_kernel,
        grid=(depth, d // TK_PREP),
        in_specs=[pl.BlockSpec((None, cols, TK_PREP), lambda l, j: (l, 0, j))],
        out_specs=[pl.BlockSpec((None, TK_PREP, N_TOK), lambda l, j: (l, j, 0)),
                   pl.BlockSpec((None, N_FEAT + N_SMALL, TK_PREP), lambda l, j: (l, 0, j))],
        out_shape=[jax.ShapeDtypeStruct((depth, d, N_TOK), BF16),
                   jax.ShapeDtypeStruct((depth, N_FEAT + N_SMALL, d), BF16)],
        compiler_params=pltpu.CompilerParams(
            dimension_semantics=("parallel", "parallel"), vmem_limit_bytes=VMEM_LIMIT),
        name="w_in_relayout",
    )(jnp.transpose(w_in, (0, 2, 1)))


def _head_scale(depth, fox_q_gain, fox_k_gain, swa_q_gain, swa_k_gain):
    def tiled(gain, heads, factor=1.0):
        return jnp.tile(gain.astype(F32) * factor, (1, heads))
    ones = lambda n: jnp.ones((depth, n), F32)
    cols = [(ROW_FQ, tiled(fox_q_gain, FOX_HEADS, QK_SCALE * LOG2E)), (ROW_FK, tiled(fox_k_gain, FOX_HEADS)),
            (ROW_FV, ones(ROW_MQ - ROW_FV)), (ROW_MQ, QK_SCALE * ones(ROW_MK - ROW_MQ)),
            (ROW_MK, ones(ROW_SQ - ROW_MK)), (ROW_SQ, tiled(swa_q_gain, SWA_Q_HEADS, QK_SCALE * LOG2E)),
            (ROW_MO, ones(ROW_SK - ROW_MO)), (ROW_SK, tiled(swa_k_gain, SWA_KV_HEADS)),
            (ROW_SV, ones(N_FEAT - ROW_SV))]
    assert [r for r, _ in cols] == sorted(r for r, _ in cols)
    return jnp.concatenate([c for _, c in cols], axis=1)[:, :, None]


def _col(v, n=None):
    v = v.astype(F32).reshape(-1, 1)
    if n is not None and v.shape[0] < n:
        v = jnp.concatenate([v, jnp.zeros((n - v.shape[0], 1), F32)], axis=0)
    return v


def kernel(x, rel_bias, attn_norm, w_in, fox_f_bias, fox_q_gain, fox_k_gain, conv_w, mlstm_i_bias, mlstm_f_bias, mlstm_h_gain, swa_q_gain, swa_k_gain, swa_sinks, w_fox_out, w_conv_out, w_mlstm_out, w_swa_out, w_merge_out, ffn_norm, w_gate, w_up, w_down):
    batch, seq, _ = x.shape
    depth = w_in.shape[0]
    x2 = x.reshape(batch * seq, D_MODEL)
    bias_t = _swa_bias_table(rel_bias)

    w_tok, w_feat_t = _prep_w_in(w_in)
    attn_gain = attn_norm.reshape(depth, 1, D_MODEL)
    ffn_gain = ffn_norm.reshape(depth, 1, D_MODEL)
    w_fox_b, w_conv_b, w_mlstm_b, w_swa_b = (w.astype(BF16) for w in (w_fox_out, w_conv_out, w_mlstm_out, w_swa_out))
    w_merge_half = (0.5 * w_merge_out).astype(BF16)
    w_gate_half = (0.5 * w_gate).astype(BF16)
    w_up_b = w_up.astype(BF16)
    w_down_b = w_down.astype(BF16)

    head_scale = _head_scale(depth, fox_q_gain, fox_k_gain, swa_q_gain, swa_k_gain)

    for l in range(depth):
        y_conv, gates, feat, small = _inproj(x2, attn_gain, w_tok, w_feat_t, head_scale, conv_w, l, seq)

        y_fox_t = _fox(feat, small, _col(fox_f_bias[l], N_SMALL), batch, seq)
        gate_bias = jnp.concatenate([jnp.zeros((4,), F32), mlstm_i_bias[l], mlstm_f_bias[l]])
        y_mlstm_t = _mlstm(feat, small, _col(gate_bias, N_SMALL), _col(mlstm_h_gain[l]), batch, seq)
        sink_rows = jnp.broadcast_to(
            (swa_sinks[l].astype(F32) * LOG2E).reshape(SWA_KV_HEADS, 1, SWA_GROUP, 1),
            (SWA_KV_HEADS, 1, SWA_GROUP, WINDOW)).reshape(SWA_KV_HEADS, 1, SWA_GROUP * WINDOW)
        y_swa_t = _swa(feat, bias_t, sink_rows, batch, seq)

        x2 = _merge_ffn(x2, y_fox_t, y_conv, y_mlstm_t, y_swa_t, gates,
                        w_fox_b, w_conv_b, w_mlstm_b, w_swa_b, w_merge_half,
                        ffn_gain, w_gate_half, w_up_b, w_down_b, l)
    return x2.reshape(batch, seq, D_MODEL)
```
